```python
import jax, jax.numpy as jnp
from jax import lax
import numpy as np

D_MODEL = 1024
BATCH = 4
SEQ = 4096
DEPTH = 2
DEC_BATCH = 16
DEC_SEQ = 32
PAST_LEN = 1024

CHUNK = 64
BAND_CHUNKS = 8
HEAD_DIM = 64
A_HEADS = 8
B_HEADS = 8
B_KEY_DIM = 64
B_VAL_DIM = 64
D_A = A_HEADS * HEAD_DIM
D_B = B_HEADS * B_KEY_DIM
AB_IN = 3 * D_A + 4 * D_B
REL_CLIP = 128
HGRN_BLOCK = 16
CONV_WIDTH = 3
N_GROUPS = 4
EXPERTS_PER_GROUP = 8
N_EXPERTS = N_GROUPS * EXPERTS_PER_GROUP
TOP_K_IN_GROUP = 2
D_EXPERT = 256
N_AB = (DEPTH + 1) // 2
N_C = DEPTH // 2
RMS_EPS = 1e-6
NEG_INF = -1e30

kernel_name = "chunk_band_hgrn2_shortconv_hmoe_step"


def rmsnorm(x, gain):
    xf = x.astype(jnp.float32)
    y = xf * lax.rsqrt(jnp.mean(xf * xf, axis=-1, keepdims=True) + RMS_EPS)
    return (y * gain.astype(jnp.float32)).astype(x.dtype)


def ab_project(xn, w_in, q_gain, k_gain, lb):
    B, T, _ = xn.shape
    z = xn @ w_in
    idx = [D_A, 2 * D_A, 3 * D_A, 3 * D_A + D_B, 3 * D_A + 2 * D_B, 3 * D_A + 3 * D_B]
    qa, ka, va, fb, ib, qb, gb = jnp.split(z, idx, axis=-1)
    qa = rmsnorm(qa.reshape(B, T, A_HEADS, HEAD_DIM), q_gain)
    ka = rmsnorm(ka.reshape(B, T, A_HEADS, HEAD_DIM), k_gain)
    va = va.reshape(B, T, A_HEADS, HEAD_DIM)
    lbf = lb.astype(jnp.float32)
    f = lbf + (1.0 - lbf) * jax.nn.sigmoid(fb.astype(jnp.float32))
    logf = jnp.log(f).reshape(B, T, B_HEADS, B_KEY_DIM)
    kb = (1.0 - f).reshape(B, T, B_HEADS, B_KEY_DIM)
    qb = jax.nn.silu(qb.astype(jnp.float32)).reshape(B, T, B_HEADS, B_KEY_DIM)
    vb = ib.astype(jnp.float32).reshape(B, T, B_HEADS, B_VAL_DIM)
    return qa, ka, va, qb, kb, vb, logf, gb


def rel_bias_lookup(rel_bias, rel):
    idx = jnp.clip(rel, -REL_CLIP, REL_CLIP) + REL_CLIP
    return jnp.take(rel_bias, idx, axis=1)


def band_attention_prompt(q, k, v, rel_bias):
    B, T, H, Dh = q.shape
    nc = T // CHUNK
    nb = BAND_CHUNKS + 1
    pad = ((0, 0), (BAND_CHUNKS * CHUNK, 0), (0, 0), (0, 0))
    kp = jnp.pad(k, pad).reshape(B, nc + BAND_CHUNKS, CHUNK, H, Dh)
    vp = jnp.pad(v, pad).reshape(B, nc + BAND_CHUNKS, CHUNK, H, Dh)
    band = jnp.arange(nc)[:, None] + jnp.arange(nb)[None, :]
    kb = kp[:, band].reshape(B, nc, nb * CHUNK, H, Dh)
    vb = vp[:, band].reshape(B, nc, nb * CHUNK, H, Dh)
    qc = q.reshape(B, nc, CHUNK, H, Dh)
    r = jnp.arange(CHUNK)
    j = jnp.arange(nb)
    rel = (BAND_CHUNKS - j)[None, :, None] * CHUNK + r[:, None, None] - r[None, None, :]
    bias = rel_bias_lookup(rel_bias, rel.reshape(CHUNK, nb * CHUNK)).astype(jnp.float32)
    valid = (jnp.arange(nc)[:, None] - BAND_CHUNKS + j[None, :]) >= 0
    valid = jnp.repeat(valid, CHUNK, axis=1)
    s = jnp.einsum('bcqhd,bckhd->bhcqk', qc, kb).astype(jnp.float32) * (Dh ** -0.5) + bias[:, None]
    s = jnp.where(valid[:, None, :], s, NEG_INF)
    p = jax.nn.softmax(s, axis=-1).astype(v.dtype)
    o = jnp.einsum('bhcqk,bckhd->bcqhd', p, vb)
    return o.reshape(B, T, H, Dh)


def band_attention_sample(q, k_all, v_all, rel_bias):
    B, Tn, H, Dh = q.shape
    K = k_all.shape[1]
    L = K - Tn
    rel = (L + jnp.arange(Tn))[:, None] - jnp.arange(K)[None, :]
    bias = rel_bias_lookup(rel_bias, rel).astype(jnp.float32)
    s = jnp.einsum('bqhd,bkhd->bhqk', q, k_all).astype(jnp.float32) * (Dh ** -0.5) + bias
    p = jax.nn.softmax(s, axis=-1).astype(v_all.dtype)
    return jnp.einsum('bhqk,bkhd->bqhd', p, v_all)


def hgrn2_blocked(q, k, v, logf, s0):
    B, T, H, Dk = q.shape
    Dv = v.shape[-1]
    nb = -(-T // HGRN_BLOCK)
    pad = nb * HGRN_BLOCK - T
    pw = ((0, 0), (0, pad), (0, 0), (0, 0))
    rs = lambda a: jnp.pad(a, pw).reshape(B, nb, HGRN_BLOCK, H, a.shape[-1]).astype(jnp.float32)
    q, k, v, logf = rs(q), rs(k), rs(v), rs(logf)
    b = jnp.cumsum(logf, axis=2)
    b_last = b[:, :, -1:]
    q_dec = q * jnp.exp(b)
    k_in = k * jnp.exp(-b)
    k_out = k * jnp.exp(b_last - b)
    mask = jnp.tril(jnp.ones((HGRN_BLOCK, HGRN_BLOCK), jnp.float32))
    attn = jnp.einsum('bnthk,bnshk->bnhts', q_dec, k_in) * mask
    o_intra = jnp.einsum('bnhts,bnshv->bnthv', attn, v)
    u = jnp.einsum('bnshk,bnshv->bnhkv', k_out, v)
    decay = jnp.exp(b_last[:, :, 0])

    def step(s, inp):
        d, uu = inp
        return d[..., None] * s + uu, s

    s_final, s_prev = lax.scan(step, s0.astype(jnp.float32),
                               (jnp.moveaxis(decay, 1, 0), jnp.moveaxis(u, 1, 0)))
    s_prev = jnp.moveaxis(s_prev, 0, 1)
    o_inter = jnp.einsum('bnthk,bnhkv->bnthv', q_dec, s_prev)
    o = (o_intra + o_inter).reshape(B, nb * HGRN_BLOCK, H, Dv)[:, :T]
    return o, s_final


def ab_output(oa, ob, gb, out_gain, w_out):
    B, T = oa.shape[:2]
    ob = rmsnorm(ob, out_gain.reshape(B_HEADS, B_VAL_DIM)).reshape(B, T, D_B)
    ob = ob * jax.nn.silu(gb.astype(jnp.float32))
    z = jnp.concatenate([oa.reshape(B, T, D_A), ob.astype(oa.dtype)], axis=-1)
    return z @ w_out


def short_conv(xn, w_in, conv_w, w_out, prev):
    T = xn.shape[1]
    z = xn @ w_in
    bg, cg, h = jnp.split(z, 3, axis=-1)
    u = cg * h
    up = jnp.concatenate([prev.astype(u.dtype), u], axis=1)
    conv = up[:, 0:T] * conv_w[0]
    for j in range(1, CONV_WIDTH):
        conv = conv + up[:, j:j + T] * conv_w[j]
    y = (bg * conv) @ w_out
    return y, up[:, -(CONV_WIDTH - 1):]


def hier_moe(x, w_group, b_group, w_expert, b_expert, w1, w3, w2):
    B, T, D = x.shape
    xt = x.reshape(B * T, D)
    g_logits = (xt @ w_group).astype(jnp.float32) + b_group.astype(jnp.float32)
    g_prob = jax.nn.softmax(g_logits, axis=-1)
    grp = jnp.argmax(g_logits, axis=-1)
    p_grp = jnp.take_along_axis(g_prob, grp[:, None], axis=-1)
    e_logits = ((xt @ w_expert).astype(jnp.float32) + b_expert.astype(jnp.float32)).reshape(-1, N_GROUPS, EXPERTS_PER_GROUP)
    sel = jnp.take_along_axis(e_logits, grp[:, None, None], axis=1)[:, 0]
    topv, topi = lax.top_k(sel, TOP_K_IN_GROUP)
    w_top = jax.nn.softmax(topv, axis=-1)
    gate_in = jnp.sum(jax.nn.one_hot(topi, EXPERTS_PER_GROUP, dtype=jnp.float32) * w_top[..., None], axis=1)
    gates = (jax.nn.one_hot(grp, N_GROUPS, dtype=jnp.float32)[:, :, None] * gate_in[:, None, :] * p_grp[:, :, None])
    gates = gates.reshape(-1, N_EXPERTS).astype(x.dtype)
    h = jax.nn.silu(jnp.einsum('nd,edf->nef', xt, w1)) * jnp.einsum('nd,edf->nef', xt, w3) * gates[..., None]
    out = jnp.einsum('nef,efd->nd', h, w2)
    return out.reshape(B, T, D)


def setup_inputs(seed: int = 0) -> dict:
    key = jax.random.key(seed)
    ks = jax.random.split(key, 32)
    f32 = jnp.float32
    nrm = lambda k, shape, scale: jax.random.normal(k, shape, f32) * scale
    L_A = min(BAND_CHUNKS * CHUNK, PAST_LEN)
    return {
        "x_prompt": nrm(ks[0], (BATCH, SEQ, D_MODEL), 1.0),
        "x_sample": nrm(ks[1], (DEC_BATCH, DEC_SEQ, D_MODEL), 1.0),
        "cache_a_k": nrm(ks[2], (N_AB, DEC_BATCH, L_A, A_HEADS, HEAD_DIM), 1.0),
        "cache_a_v": nrm(ks[3], (N_AB, DEC_BATCH, L_A, A_HEADS, HEAD_DIM), 1.0),
        "state_hgrn": nrm(ks[4], (N_AB, DEC_BATCH, B_HEADS, B_KEY_DIM, B_VAL_DIM), 0.3),
        "state_conv": nrm(ks[5], (N_C, DEC_BATCH, CONV_WIDTH - 1, D_MODEL), 0.5),
        "norm_mix": 1.0 + nrm(ks[6], (DEPTH, D_MODEL), 0.01),
        "norm_ffn": 1.0 + nrm(ks[7], (DEPTH, D_MODEL), 0.01),
        "w_in_ab": nrm(ks[8], (N_AB, D_MODEL, AB_IN), D_MODEL ** -0.5),
        "w_out_ab": nrm(ks[9], (N_AB, D_A + D_B, D_MODEL), (D_A + D_B) ** -0.5),
        "q_norm": 1.0 + nrm(ks[10], (N_AB, HEAD_DIM), 0.01),
        "k_norm": 1.0 + nrm(ks[11], (N_AB, HEAD_DIM), 0.01),
        "rel_bias": nrm(ks[12], (N_AB, A_HEADS, 2 * REL_CLIP + 1), 0.1),
        "hgrn_lb_logits": nrm(ks[13], (N_AB + 1, D_B), 0.5),
        "hgrn_out_norm": 1.0 + nrm(ks[14], (N_AB, D_B), 0.01),
        "w_in_c": nrm(ks[15], (N_C, D_MODEL, 3 * D_MODEL), D_MODEL ** -0.5),
        "conv_w": nrm(ks[16], (N_C, CONV_WIDTH, D_MODEL), CONV_WIDTH ** -0.5),
        "w_out_c": nrm(ks[17], (N_C, D_MODEL, D_MODEL), D_MODEL ** -0.5),
        "w_group": nrm(ks[18], (DEPTH, D_MODEL, N_GROUPS), D_MODEL ** -0.5),
        "b_group": nrm(ks[19], (DEPTH, N_GROUPS), 0.01),
        "w_expert": nrm(ks[20], (DEPTH, D_MODEL, N_EXPERTS), D_MODEL ** -0.5),
        "b_expert": nrm(ks[21], (DEPTH, N_EXPERTS), 0.01),
        "w1": nrm(ks[22], (DEPTH, N_EXPERTS, D_MODEL, D_EXPERT), D_MODEL ** -0.5),
        "w3": nrm(ks[23], (DEPTH, N_EXPERTS, D_MODEL, D_EXPERT), D_MODEL ** -0.5),
        "w2": nrm(ks[24], (DEPTH, N_EXPERTS, D_EXPERT, D_MODEL), D_EXPERT ** -0.5),
    }


def reference(x_prompt, x_sample, cache_a_k, cache_a_v, state_hgrn, state_conv,
              norm_mix, norm_ffn, w_in_ab, w_out_ab, q_norm, k_norm, rel_bias,
              hgrn_lb_logits, hgrn_out_norm, w_in_c, conv_w, w_out_c,
              w_group, b_group, w_expert, b_expert, w1, w3, w2):
    hp, hs = x_prompt, x_sample
    lb_all = jnp.cumsum(jax.nn.softmax(hgrn_lb_logits.astype(jnp.float32), axis=0), axis=0)
    L_A = cache_a_k.shape[2]
    nk_p, nv_p, nk_s, nv_s, nh_p, nh_s, nc_p, nc_s = [], [], [], [], [], [], [], []
    for layer in range(DEPTH):
        l = layer // 2
        if layer % 2 == 0:
            qa, ka, va, qb, kb, vb, logf, gb = ab_project(rmsnorm(hp, norm_mix[layer]), w_in_ab[l], q_norm[l], k_norm[l], lb_all[l])
            oa = band_attention_prompt(qa, ka, va, rel_bias[l])
            s0 = jnp.zeros((hp.shape[0], B_HEADS, B_KEY_DIM, B_VAL_DIM), jnp.float32)
            ob, sp = hgrn2_blocked(qb, kb, vb, logf, s0)
            mix_p = ab_output(oa, ob, gb, hgrn_out_norm[l], w_out_ab[l])
            keep = min(BAND_CHUNKS * CHUNK, hp.shape[1])
            nk_p.append(ka[:, -keep:])
            nv_p.append(va[:, -keep:])
            nh_p.append(sp)
            qa, ka, va, qb, kb, vb, logf, gb = ab_project(rmsnorm(hs, norm_mix[layer]), w_in_ab[l], q_norm[l], k_norm[l], lb_all[l])
            k_all = jnp.concatenate([cache_a_k[l].astype(ka.dtype), ka], axis=1)
            v_all = jnp.concatenate([cache_a_v[l].astype(va.dtype), va], axis=1)
            oa = band_attention_sample(qa, k_all, v_all, rel_bias[l])
            ob, ss = hgrn2_blocked(qb, kb, vb, logf, state_hgrn[l])
            mix_s = ab_output(oa, ob, gb, hgrn_out_norm[l], w_out_ab[l])
            nk_s.append(k_all[:, -L_A:])
            nv_s.append(v_all[:, -L_A:])
            nh_s.append(ss)
        else:
            zeros_prev = jnp.zeros((hp.shape[0], CONV_WIDTH - 1, D_MODEL), hp.dtype)
            mix_p, cp = short_conv(rmsnorm(hp, norm_mix[layer]), w_in_c[l], conv_w[l], w_out_c[l], zeros_prev)
            mix_s, cs = short_conv(rmsnorm(hs, norm_mix[layer]), w_in_c[l], conv_w[l], w_out_c[l], state_conv[l])
            nc_p.append(cp)
            nc_s.append(cs)
        hp = hp + mix_p
        hs = hs + mix_s
        hp = hp + hier_moe(rmsnorm(hp, norm_ffn[layer]), w_group[layer], b_group[layer], w_expert[layer], b_expert[layer], w1[layer], w3[layer], w2[layer])
        hs = hs + hier_moe(rmsnorm(hs, norm_ffn[layer]), w_group[layer], b_group[layer], w_expert[layer], b_expert[layer], w1[layer], w3[layer], w2[layer])
    return (hp, hs, jnp.stack(nk_p), jnp.stack(nv_p), jnp.stack(nk_s), jnp.stack(nv_s),
            jnp.stack(nh_p), jnp.stack(nh_s), jnp.stack(nc_p), jnp.stack(nc_s))
```

```python
import functools

import jax
import jax.numpy as jnp
from jax import lax
from jax.experimental import pallas as pl
from jax.experimental.pallas import tpu as pltpu

F32 = jnp.float32
BF16 = jnp.bfloat16
I32 = jnp.int32

D_MODEL = 1024
CHUNK = 64
BAND_CHUNKS = 8
HEAD_DIM = 64
N_HEADS = 8
D_HALF = N_HEADS * HEAD_DIM
REL_CLIP = 128
HGRN_BLOCK = 16
N_GROUPS = 4
EXPERTS_PER_GROUP = 8
N_EXPERTS = N_GROUPS * EXPERTS_PER_GROUP
D_EXPERT = 256
RMS_EPS = 1e-6
NEG_INF = -1e30

LANES = 128
TOKEN_TILE = 256
ATTN_TILE = 256
ATTN_WINDOW = ATTN_TILE + BAND_CHUNKS * CHUNK
MOE_TILE = 256
VMEM_LIMIT = 48 * 1024 * 1024


def _cparams(*sem):
    return pltpu.CompilerParams(dimension_semantics=sem, vmem_limit_bytes=VMEM_LIMIT)


def _const_spec(shape):
    nd = len(shape)
    return pl.BlockSpec(shape, lambda *_: (0,) * nd)


def _rms(x, gain):
    ms = jnp.mean(x * x, axis=-1, keepdims=True)
    return (x * lax.rsqrt(ms + RMS_EPS)) * gain


def _split2(x):
    hi = x.astype(BF16)
    lo = (x - hi.astype(F32)).astype(BF16)
    return hi, lo


def _split3(x):
    p0 = x.astype(BF16)
    r = x - p0.astype(F32)
    p1 = r.astype(BF16)
    p2 = (r - p1.astype(F32)).astype(BF16)
    return p0, p1, p2


def _dot(a, b):
    return jnp.dot(a, b, preferred_element_type=F32)


def _dot_nt(a, b):
    return lax.dot_general(a, b, (((1,), (1,)), ((), ())), preferred_element_type=F32)


def _dot_tn(a, b):
    return lax.dot_general(a, b, (((0,), (0,)), ((), ())), preferred_element_type=F32)


def _head_mean_sq(v, bd):
    hi, lo = _split2(v * v)
    return _dot(hi, bd) + _dot(lo, bd)


def _proj_ab_body(x_ref, g_ref, w_ref, qg_ref, kg_ref, lb_ref, bd_ref,
                  q_o, kf_o, vf_o, k16_o, v16_o, lf_o, kb_o, vb_o, qb_o, gb_o):
    xb = _rms(x_ref[...], g_ref[...]).astype(BF16)
    bd = bd_ref[...]

    def seg(j):
        return _dot(xb, w_ref[:, j * D_HALF:(j + 1) * D_HALF])

    qa = seg(0)
    qn = qa * lax.rsqrt(_head_mean_sq(qa, bd) + RMS_EPS) * qg_ref[...]
    q_o[...] = (qn * (HEAD_DIM ** -0.5)).astype(BF16)
    ka = seg(1)
    kn = ka * lax.rsqrt(_head_mean_sq(ka, bd) + RMS_EPS) * kg_ref[...]
    kf_o[...] = kn
    k16_o[...] = kn.astype(BF16)
    va = seg(2)
    vf_o[...] = va
    v16_o[...] = va.astype(BF16)
    lb = lb_ref[...]
    f = lb + (1.0 - lb) * jax.nn.sigmoid(seg(3))
    lf_o[...] = jnp.log(f)
    kb_o[...] = 1.0 - f
    vb_o[...] = seg(4)
    qb_o[...] = jax.nn.silu(seg(5))
    gb_o[...] = seg(6)


def _proj_ab(x, gain, w16, qg, kg, lb, bd):
    n = x.shape[0]
    tm = TOKEN_TILE
    row = lambda w: pl.BlockSpec((tm, w), lambda i: (i, 0))
    outs = [jax.ShapeDtypeStruct((n, D_HALF), dt)
            for dt in (BF16, F32, F32, BF16, BF16, F32, F32, F32, F32, F32)]
    return pl.pallas_call(
        _proj_ab_body,
        out_shape=outs,
        grid=(n // tm,),
        in_specs=[row(D_MODEL), _const_spec((1, D_MODEL)), _const_spec(w16.shape),
                  _const_spec((1, D_HALF)), _const_spec((1, D_HALF)), _const_spec((1, D_HALF)),
                  _const_spec((D_HALF, D_HALF))],
        out_specs=[row(D_HALF)] * 10,
        compiler_params=_cparams("parallel"),
        name="proj_ab",
    )(x, gain, w16, qg, kg, lb, bd)


def _attn_prompt_body(q_ref, k0, k1, k2, v0, v1, v2, bias_ref, o_ref):
    i = pl.program_id(1)
    lim = jnp.maximum(2 - i, 0) * ATTN_TILE
    col = lax.broadcasted_iota(I32, (ATTN_TILE, ATTN_WINDOW), 1)
    dead = col < lim
    for h in range(N_HEADS):
        hs = slice(h * HEAD_DIM, (h + 1) * HEAD_DIM)
        qh = q_ref[:, hs]
        s = jnp.concatenate([_dot_nt(qh, k0[:, hs]), _dot_nt(qh, k1[:, hs]),
                             _dot_nt(qh, k2[:, hs])], axis=1)
        s = jnp.where(dead, NEG_INF, s + bias_ref[h])
        m = jnp.max(s, axis=-1, keepdims=True)
        p = jnp.exp(s - m)
        l = jnp.sum(p, axis=-1, keepdims=True)
        p16 = p.astype(BF16)
        o = (_dot(p16[:, 0:ATTN_TILE], v0[:, hs])
             + _dot(p16[:, ATTN_TILE:2 * ATTN_TILE], v1[:, hs])
             + _dot(p16[:, 2 * ATTN_TILE:], v2[:, hs]))
        o_ref[:, hs] = (o / l).astype(BF16)


def _attn_prompt(q16, k16, v16, bias_full, batch, seq):
    tiles = seq // ATTN_TILE
    qspec = pl.BlockSpec((ATTN_TILE, D_HALF), lambda b, i: (b * tiles + i, 0))

    def kv(back):
        return pl.BlockSpec((ATTN_TILE, D_HALF),
                            lambda b, i: (b * tiles + jnp.maximum(i - back, 0), 0))

    return pl.pallas_call(
        _attn_prompt_body,
        out_shape=jax.ShapeDtypeStruct((batch * seq, D_HALF), BF16),
        grid=(batch, tiles),
        in_specs=[qspec, kv(2), kv(1), kv(0), kv(2), kv(1), kv(0),
                  _const_spec(bias_full.shape)],
        out_specs=qspec,
        compiler_params=_cparams("parallel", "parallel"),
        name="attn_prompt",
    )(q16, k16, k16, k16, v16, v16, v16, bias_full)


def _attn_sample_body(q_ref, kc_ref, vc_ref, kn_ref, vn_ref, bc_ref, bn_ref, o_ref):
    kc = kc_ref[0].astype(BF16)
    vc = vc_ref[0].astype(BF16)
    for h in range(N_HEADS):
        hs = slice(h * HEAD_DIM, (h + 1) * HEAD_DIM)
        qh = q_ref[:, hs]
        sc = _dot_nt(qh, kc[:, hs]) + bc_ref[h]
        sn = _dot_nt(qh, kn_ref[:, hs]) + bn_ref[h]
        m = jnp.maximum(jnp.max(sc, axis=-1, keepdims=True), jnp.max(sn, axis=-1, keepdims=True))
        pc = jnp.exp(sc - m)
        pn = jnp.exp(sn - m)
        l = jnp.sum(pc, axis=-1, keepdims=True) + jnp.sum(pn, axis=-1, keepdims=True)
        o = _dot(pc.astype(BF16), vc[:, hs]) + _dot(pn.astype(BF16), vn_ref[:, hs])
        o_ref[:, hs] = (o / l).astype(BF16)


def _attn_sample(q16, k16, v16, cache_k, cache_v, bias_c, bias_n, row0, nseq, tn):
    blk0 = row0 // tn
    la = cache_k.shape[1]
    new = pl.BlockSpec((tn, D_HALF), lambda b: (blk0 + b, 0))
    cache = pl.BlockSpec((1, la, D_HALF), lambda b: (b, 0, 0))
    return pl.pallas_call(
        _attn_sample_body,
        out_shape=jax.ShapeDtypeStruct((nseq * tn, D_HALF), BF16),
        grid=(nseq,),
        in_specs=[new, cache, cache, new, new, _const_spec(bias_c.shape), _const_spec(bias_n.shape)],
        out_specs=pl.BlockSpec((tn, D_HALF), lambda b: (b, 0)),
        compiler_params=_cparams("parallel"),
        name="attn_sample",
    )(q16, cache_k, cache_v, k16, v16, bias_c, bias_n)


def _hgrn_body(q_ref, k_ref, v_ref, lf_ref, s0_ref, tblk_ref, tfull_ref, o_ref, sout_ref, st_scr,
               *, chunk, nchunks):
    nblk = chunk // HGRN_BLOCK

    @pl.when(pl.program_id(1) == 0)
    def _():
        st_scr[...] = s0_ref[0]

    row = lax.broadcasted_iota(I32, (chunk, D_HALF), 0)
    r2 = lax.broadcasted_iota(I32, (chunk, chunk), 0)
    c2 = lax.broadcasted_iota(I32, (chunk, chunk), 1)
    same_blk_causal = jnp.logical_and(r2 // HGRN_BLOCK == c2 // HGRN_BLOCK, c2 <= r2)

    def one_chunk(c, carry):
        sl = pl.ds(pl.multiple_of(c * chunk, chunk), chunk)
        q = q_ref[sl, :]
        k = k_ref[sl, :]
        v16 = v_ref[sl, :].astype(BF16)
        l0, l1, l2 = _split3(lf_ref[sl, :])
        tb = tblk_ref[...]
        tf = tfull_ref[...]
        b_in = _dot(tb, l0) + _dot(tb, l1) + _dot(tb, l2)
        b_ch = _dot(tf, l0) + _dot(tf, l1) + _dot(tf, l2)
        ld = (q * jnp.exp(b_in)).astype(BF16)
        rd = (k * jnp.exp(-b_in)).astype(BF16)
        lj, rj = [], []
        for j in range(nblk - 1):
            e_j = b_ch[(j + 1) * HGRN_BLOCK - 1:(j + 1) * HGRN_BLOCK, :]
            later = row >= (j + 1) * HGRN_BLOCK
            inside = jnp.logical_and(row >= j * HGRN_BLOCK, row < (j + 1) * HGRN_BLOCK)
            lj.append(jnp.where(later, q * jnp.exp(jnp.minimum(b_ch - e_j, 0.0)), 0.0).astype(BF16))
            rj.append(jnp.where(inside, k * jnp.exp(jnp.minimum(e_j - b_ch, 0.0)), 0.0).astype(BF16))
        e_end = b_ch[chunk - 1:chunk, :]
        qc = (q * jnp.exp(b_ch)).astype(BF16)
        kc = (k * jnp.exp(e_end - b_ch)).astype(BF16)
        dec = jnp.exp(e_end)
        for h in range(N_HEADS):
            hs = slice(h * HEAD_DIM, (h + 1) * HEAD_DIM)
            sc = jnp.where(same_blk_causal, _dot_nt(ld[:, hs], rd[:, hs]), 0.0)
            for j in range(nblk - 1):
                sc = sc + _dot_nt(lj[j][:, hs], rj[j][:, hs])
            st = st_scr[h]
            o = _dot(sc.astype(BF16), v16[:, hs]) + _dot_nt(qc[:, hs], st.astype(BF16))
            o_ref[sl, hs] = o
            st_scr[h] = st * dec[:, hs] + _dot_tn(v16[:, hs], kc[:, hs])
        return carry

    lax.fori_loop(0, nchunks, one_chunk, 0)
    sout_ref[0] = st_scr[...]


def _hgrn(qb, kb, vb, lf, s0t, row0, nseq, seq, chunk, nchunks):
    tt = chunk * nchunks
    steps = seq // tt
    blk0 = row0 // tt
    t = jnp.arange(chunk)
    lower = t[None, :] <= t[:, None]
    tfull = lower.astype(BF16)
    tblk = jnp.logical_and(lower, (t[None, :] // HGRN_BLOCK) == (t[:, None] // HGRN_BLOCK)).astype(BF16)
    tok = pl.BlockSpec((tt, D_HALF), lambda b, j: (blk0 + b * steps + j, 0))
    state = pl.BlockSpec((1, N_HEADS, HEAD_DIM, HEAD_DIM), lambda b, j: (b, 0, 0, 0))
    return pl.pallas_call(
        functools.partial(_hgrn_body, chunk=chunk, nchunks=nchunks),
        out_shape=[jax.ShapeDtypeStruct((nseq * seq, D_HALF), F32),
                   jax.ShapeDtypeStruct(s0t.shape, F32)],
        grid=(nseq, steps),
        in_specs=[tok, tok, tok, tok, state, _const_spec((chunk, chunk)), _const_spec((chunk, chunk))],
        out_specs=[pl.BlockSpec((tt, D_HALF), lambda b, j: (b * steps + j, 0)), state],
        scratch_shapes=[pltpu.VMEM((N_HEADS, HEAD_DIM, HEAD_DIM), F32)],
        compiler_params=_cparams("parallel", "arbitrary"),
        name="hgrn",
    )(qb, kb, vb, lf, s0t, tblk, tfull)


def _route(xn, wr_hi, wr_lo, br, tstrict, cnt_scr, route_o, cnt_o):
    rows = xn.shape[0]
    x_hi, x_lo = _split2(xn)
    logits = _dot(x_hi, wr_hi) + _dot(x_lo, wr_hi) + _dot(x_hi, wr_lo) + br
    lane = lax.broadcasted_iota(I32, (rows, LANES), 1)
    lane_f = lane.astype(F32)

    def first_argmax(vals):
        m = jnp.max(vals, axis=-1, keepdims=True)
        idx = jnp.min(jnp.where(vals == m, lane_f, float(LANES)), axis=-1, keepdims=True)
        return m, idx.astype(I32)

    is_grp = jnp.logical_and(lane >= N_EXPERTS, lane < N_EXPERTS + N_GROUPS)
    gl = jnp.where(is_grp, logits, -jnp.inf)
    gmax, gidx = first_argmax(gl)
    p_grp = 1.0 / jnp.sum(jnp.exp(gl - gmax), axis=-1, keepdims=True)
    grp = gidx - N_EXPERTS
    in_grp = jnp.logical_and(lane < N_EXPERTS, lane // EXPERTS_PER_GROUP == grp)
    el = jnp.where(in_grp, logits, -jnp.inf)
    v1, e1 = first_argmax(el)
    v2, e2 = first_argmax(jnp.where(lane == e1, -jnp.inf, el))
    t2 = jnp.exp(v2 - v1)
    den = 1.0 + t2
    g1 = (1.0 / den) * p_grp
    g2 = (t2 / den) * p_grp

    oh1 = lane == e1
    oh2 = lane == e2
    oh = jnp.where(jnp.logical_or(oh1, oh2), 1.0, 0.0)
    before = _dot(tstrict, oh.astype(BF16)) + cnt_scr[...]
    rank1 = jnp.sum(jnp.where(oh1, before, 0.0), axis=-1, keepdims=True)
    rank2 = jnp.sum(jnp.where(oh2, before, 0.0), axis=-1, keepdims=True)
    cnt_scr[...] = cnt_scr[...] + jnp.sum(oh, axis=0, keepdims=True)
    cnt_o[...] = cnt_scr[...]

    slab = jnp.zeros((rows, LANES), F32)
    for idx, val in enumerate((e1.astype(F32), e2.astype(F32), g1, g2, rank1, rank2)):
        slab = jnp.where(lane == idx, val, slab)
    route_o[...] = slab


def _out_ab_body(h_ref, oa_ref, ob_ref, gb_ref, og_ref, bd_ref, w_ref, gf_ref,
                 wrh_ref, wrl_ref, br_ref, ts_ref, h_o, xn_o, route_o, cnt_o, cnt_scr):
    @pl.when(pl.program_id(0) == 0)
    def _():
        cnt_scr[...] = jnp.zeros_like(cnt_scr)

    ob = ob_ref[...]
    obn = ob * lax.rsqrt(_head_mean_sq(ob, bd_ref[...]) + RMS_EPS) * og_ref[...]
    obg = (obn * jax.nn.silu(gb_ref[...])).astype(BF16)
    mix = _dot(oa_ref[...], w_ref[0:D_HALF, :]) + _dot(obg, w_ref[D_HALF:, :])
    h1 = h_ref[...] + mix
    h_o[...] = h1
    xn = _rms(h1, gf_ref[...])
    xn_o[...] = xn
    _route(xn, wrh_ref[...], wrl_ref[...], br_ref[...], ts_ref[...], cnt_scr, route_o, cnt_o)


def _out_ab(h, oa16, ob, gb, og, bd, w16, gf, wrh, wrl, br, tstrict):
    n = h.shape[0]
    tm = TOKEN_TILE
    row = lambda w: pl.BlockSpec((tm, w), lambda i: (i, 0))
    return pl.pallas_call(
        _out_ab_body,
        out_shape=[jax.ShapeDtypeStruct((n, D_MODEL), F32), jax.ShapeDtypeStruct((n, D_MODEL), F32),
                   jax.ShapeDtypeStruct((n, LANES), F32), jax.ShapeDtypeStruct((1, LANES), F32)],
        grid=(n // tm,),
        in_specs=[row(D_MODEL), row(D_HALF), row(D_HALF), row(D_HALF), _const_spec((1, D_HALF)),
                  _const_spec((D_HALF, D_HALF)), _const_spec((D_MODEL, D_MODEL)),
                  _const_spec((1, D_MODEL)), _const_spec((D_MODEL, LANES)),
                  _const_spec((D_MODEL, LANES)), _const_spec((1, LANES)), _const_spec((tm, tm))],
        out_specs=[row(D_MODEL), row(D_MODEL), row(LANES), _const_spec((1, LANES))],
        scratch_shapes=[pltpu.VMEM((1, LANES), F32)],
        compiler_params=_cparams("arbitrary"),
        name="out_ab",
    )(h, oa16, ob, gb, og, bd, w16, gf, wrh, wrl, br, tstrict)


def _row_copy(src, src_row, dst, dst_row, sem):
    return pltpu.make_async_copy(src.at[pl.ds(src_row, 1)], dst.at[pl.ds(dst_row, 1)], sem)


def _dispatch_body(pos_ref, x_hbm, init_hbm, xs_hbm, sem, *, n_tok, tile):
    del init_hbm
    base = pl.program_id(0) * tile

    def issue(t, carry):
        tok = base + t
        _row_copy(x_hbm, tok, xs_hbm, pos_ref[tok], sem).start()
        _row_copy(x_hbm, tok, xs_hbm, pos_ref[n_tok + tok], sem).start()
        return carry

    def drain(t, carry):
        _row_copy(x_hbm, 0, xs_hbm, 0, sem).wait()
        _row_copy(x_hbm, 0, xs_hbm, 0, sem).wait()
        return carry

    lax.fori_loop(0, tile, issue, 0)
    lax.fori_loop(0, tile, drain, 0)


def _dispatch(pos, xn, n_rows):
    n = xn.shape[0]
    init = jnp.zeros((n_rows, D_MODEL), F32)
    return pl.pallas_call(
        functools.partial(_dispatch_body, n_tok=n, tile=TOKEN_TILE),
        out_shape=jax.ShapeDtypeStruct((n_rows, D_MODEL), F32),
        grid_spec=pltpu.PrefetchScalarGridSpec(
            num_scalar_prefetch=1,
            grid=(n // TOKEN_TILE,),
            in_specs=[pl.BlockSpec(memory_space=pl.ANY), pl.BlockSpec(memory_space=pl.ANY)],
            out_specs=pl.BlockSpec(memory_space=pl.ANY),
            scratch_shapes=[pltpu.SemaphoreType.DMA(())],
        ),
        input_output_aliases={2: 0},
        compiler_params=pltpu.CompilerParams(dimension_semantics=("arbitrary",)),
        name="moe_dispatch",
    )(pos, xn, init)


def _collect_body(pos_ref, y_hbm, out_hbm, sem, *, n_tok, tile):
    base = pl.program_id(0) * tile

    def issue(t, carry):
        tok = base + t
        _row_copy(y_hbm, pos_ref[tok], out_hbm.at[0], tok, sem).start()
        _row_copy(y_hbm, pos_ref[n_tok + tok], out_hbm.at[1], tok, sem).start()
        return carry

    def drain(t, carry):
        _row_copy(y_hbm, 0, out_hbm.at[0], 0, sem).wait()
        _row_copy(y_hbm, 0, out_hbm.at[1], 0, sem).wait()
        return carry

    lax.fori_loop(0, tile, issue, 0)
    lax.fori_loop(0, tile, drain, 0)


def _collect(pos, y, n):
    return pl.pallas_call(
        functools.partial(_collect_body, n_tok=n, tile=TOKEN_TILE),
        out_shape=jax.ShapeDtypeStruct((2, n, D_MODEL), F32),
        grid_spec=pltpu.PrefetchScalarGridSpec(
            num_scalar_prefetch=1,
            grid=(n // TOKEN_TILE,),
            in_specs=[pl.BlockSpec(memory_space=pl.ANY)],
            out_specs=pl.BlockSpec(memory_space=pl.ANY),
            scratch_shapes=[pltpu.SemaphoreType.DMA(())],
        ),
        compiler_params=pltpu.CompilerParams(dimension_semantics=("arbitrary",)),
        name="moe_collect",
    )(pos, y)


def _experts_body(te_ref, nt_ref, x_ref, w1_ref, w3_ref, w2_ref, y_ref, w1_s, w3_s, w2_s):
    i = pl.program_id(0)
    prev = te_ref[jnp.maximum(i - 1, 0)]
    fresh = jnp.logical_or(i == 0, te_ref[i] != prev)

    @pl.when(fresh)
    def _():
        w1_s[...] = w1_ref[0].astype(BF16)
        w3_s[...] = w3_ref[0].astype(BF16)
        w2_s[...] = w2_ref[0].astype(BF16)

    @pl.when(i < nt_ref[0])
    def _():
        x = x_ref[...].astype(BF16)
        a = _dot(x, w1_s[...])
        b = _dot(x, w3_s[...])
        y_ref[...] = _dot((jax.nn.silu(a) * b).astype(BF16), w2_s[...])

    @pl.when(i >= nt_ref[0])
    def _():
        y_ref[...] = jnp.zeros_like(y_ref)


def _experts(tile_expert, n_tiles, xs, w1, w3, w2):
    rows = xs.shape[0]
    tm = MOE_TILE
    return pl.pallas_call(
        _experts_body,
        out_shape=jax.ShapeDtypeStruct((rows, D_MODEL), F32),
        grid_spec=pltpu.PrefetchScalarGridSpec(
            num_scalar_prefetch=2,
            grid=(rows // tm,),
            in_specs=[pl.BlockSpec((tm, D_MODEL), lambda i, te, nt: (i, 0)),
                      pl.BlockSpec((1, D_MODEL, D_EXPERT), lambda i, te, nt: (te[i], 0, 0)),
                      pl.BlockSpec((1, D_MODEL, D_EXPERT), lambda i, te, nt: (te[i], 0, 0)),
                      pl.BlockSpec((1, D_EXPERT, D_MODEL), lambda i, te, nt: (te[i], 0, 0))],
            out_specs=pl.BlockSpec((tm, D_MODEL), lambda i, te, nt: (i, 0)),
            scratch_shapes=[pltpu.VMEM((D_MODEL, D_EXPERT), BF16), pltpu.VMEM((D_MODEL, D_EXPERT), BF16),
                            pltpu.VMEM((D_EXPERT, D_MODEL), BF16)],
        ),
        compiler_params=_cparams("arbitrary"),
        name="moe_experts",
    )(tile_expert, n_tiles, xs, w1, w3, w2)


def _moe_plan(route, counts_per_segment, seg_rows):
    e = route[:, 0:2].astype(I32)
    rank = route[:, 4:6].astype(I32)
    seg_counts = [c[0, :N_EXPERTS].astype(I32) for c in counts_per_segment]
    base = jnp.zeros((N_EXPERTS,), I32)
    seg_base = []
    for c in seg_counts:
        seg_base.append(base)
        base = base + c
    counts = base
    row_base = jnp.concatenate([jnp.broadcast_to(b[None], (r, N_EXPERTS))
                                for b, r in zip(seg_base, seg_rows)], axis=0)
    padded = ((counts + MOE_TILE - 1) // MOE_TILE) * MOE_TILE
    ends = jnp.cumsum(padded)
    offs = ends - padded
    pos = jnp.take(offs, e) + rank + jnp.take_along_axis(row_base, e, axis=1)
    pos_flat = jnp.concatenate([pos[:, 0], pos[:, 1]])
    n = route.shape[0]
    max_tiles = (2 * n + N_EXPERTS * (MOE_TILE - 1) + MOE_TILE - 1) // MOE_TILE
    tile_start = jnp.arange(max_tiles, dtype=I32) * MOE_TILE
    tile_expert = jnp.minimum(jnp.sum((tile_start[:, None] >= ends[None, :]).astype(I32), axis=1),
                              N_EXPERTS - 1)
    n_tiles = (ends[-1] // MOE_TILE).reshape(1)
    return pos_flat, tile_expert, n_tiles, max_tiles * MOE_TILE


def _moe(xn, route, counts_per_segment, seg_rows, w1, w3, w2):
    pos, tile_expert, n_tiles, n_rows = _moe_plan(route, counts_per_segment, seg_rows)
    xs = _dispatch(pos, xn, n_rows)
    ys = _experts(tile_expert, n_tiles, xs, w1, w3, w2)
    return _collect(pos, ys, xn.shape[0])


def _proj_c_body(h_ref, ya_ref, yb_ref, r_ref, g_ref, w_ref, h_o, u_o, bg_o):
    r = r_ref[...]
    h2 = h_ref[...] + (r[:, 2:3] * ya_ref[0] + r[:, 3:4] * yb_ref[0])
    h_o[...] = h2
    xb = _rms(h2, g_ref[...]).astype(BF16)
    bg_o[...] = _dot(xb, w_ref[:, 0:D_MODEL])
    u_o[...] = _dot(xb, w_ref[:, D_MODEL:2 * D_MODEL]) * _dot(xb, w_ref[:, 2 * D_MODEL:])


def _proj_c(h, y2, route, gain, w16):
    n = h.shape[0]
    tm = TOKEN_TILE
    row = lambda w: pl.BlockSpec((tm, w), lambda i: (i, 0))
    ysp = lambda k: pl.BlockSpec((1, tm, D_MODEL), lambda i: (k, i, 0))
    return pl.pallas_call(
        _proj_c_body,
        out_shape=[jax.ShapeDtypeStruct((n, D_MODEL), F32)] * 3,
        grid=(n // tm,),
        in_specs=[row(D_MODEL), ysp(0), ysp(1), row(LANES), _const_spec((1, D_MODEL)),
                  _const_spec(w16.shape)],
        out_specs=[row(D_MODEL)] * 3,
        compiler_params=_cparams("parallel"),
        name="proj_c",
    )(h, y2, y2, route, gain, w16)


def _conv_out_body(u_ref, up_ref, st_ref, bg_ref, h_ref, cw_ref, w_ref, gf_ref,
                   wrh_ref, wrl_ref, br_ref, ts_ref, h_o, xn_o, route_o, cnt_o, cnt_scr,
                   *, tile, seq):
    i = pl.program_id(0)

    @pl.when(i == 0)
    def _():
        cnt_scr[...] = jnp.zeros_like(cnt_scr)

    u = u_ref[...]
    rowi = lax.broadcasted_iota(I32, (tile, D_MODEL), 0)
    if seq >= tile:
        at_start = (i * tile) % seq == 0
        st = st_ref[0]
        prev = up_ref[...]
        m2 = jnp.where(at_start, st[0:1, :], prev[6:7, :])
        m1 = jnp.where(at_start, st[1:2, :], prev[7:8, :])
        pos = rowi
    else:
        per = tile // seq
        st = st_ref[...]
        m2 = jnp.broadcast_to(st[:, 0:1, :], (per, seq, D_MODEL)).reshape(tile, D_MODEL)
        m1 = jnp.broadcast_to(st[:, 1:2, :], (per, seq, D_MODEL)).reshape(tile, D_MODEL)
        pos = rowi % seq
    u1 = jnp.where(pos == 0, m1, pltpu.roll(u, 1, axis=0))
    u2 = jnp.where(pos == 0, m2, jnp.where(pos == 1, m1, pltpu.roll(u, 2, axis=0)))
    cw = cw_ref[...]
    conv = u2 * cw[0:1, :] + u1 * cw[1:2, :] + u * cw[2:3, :]
    mix = _dot((bg_ref[...] * conv).astype(BF16), w_ref[...])
    h3 = h_ref[...] + mix
    h_o[...] = h3
    xn = _rms(h3, gf_ref[...])
    xn_o[...] = xn
    _route(xn, wrh_ref[...], wrl_ref[...], br_ref[...], ts_ref[...], cnt_scr, route_o, cnt_o)


def _conv_out(u, bg, h, state, cw, w16, gf, wrh, wrl, br, tstrict, row0, nseq, seq):
    tm = TOKEN_TILE
    n_g = nseq * seq
    blk0 = row0 // tm
    row = lambda w: pl.BlockSpec((tm, w), lambda i: (blk0 + i, 0))
    orow = lambda w: pl.BlockSpec((tm, w), lambda i: (i, 0))
    prev = pl.BlockSpec((8, D_MODEL), lambda i: (jnp.maximum((blk0 + i) * (tm // 8) - 1, 0), 0))
    if seq >= tm:
        st_spec = pl.BlockSpec((1, 2, D_MODEL), lambda i: ((i * tm) // seq, 0, 0))
    else:
        st_spec = pl.BlockSpec((tm // seq, 2, D_MODEL), lambda i: (i, 0, 0))
    return pl.pallas_call(
        functools.partial(_conv_out_body, tile=tm, seq=seq),
        out_shape=[jax.ShapeDtypeStruct((n_g, D_MODEL), F32), jax.ShapeDtypeStruct((n_g, D_MODEL), F32),
                   jax.ShapeDtypeStruct((n_g, LANES), F32), jax.ShapeDtypeStruct((1, LANES), F32)],
        grid=(n_g // tm,),
        in_specs=[row(D_MODEL), prev, st_spec, row(D_MODEL), row(D_MODEL), _const_spec((3, D_MODEL)),
                  _const_spec((D_MODEL, D_MODEL)), _const_spec((1, D_MODEL)),
                  _const_spec((D_MODEL, LANES)), _const_spec((D_MODEL, LANES)),
                  _const_spec((1, LANES)), _const_spec((tm, tm))],
        out_specs=[orow(D_MODEL), orow(D_MODEL), orow(LANES), _const_spec((1, LANES))],
        scratch_shapes=[pltpu.VMEM((1, LANES), F32)],
        compiler_params=_cparams("arbitrary"),
        name="conv_out",
    )(u, u, state, bg, h, cw, w16, gf, wrh, wrl, br, tstrict)


def _combine_body(h_ref, ya_ref, yb_ref, r_ref, o_ref):
    r = r_ref[...]
    o_ref[...] = h_ref[...] + (r[:, 2:3] * ya_ref[0] + r[:, 3:4] * yb_ref[0])


def _combine(h, y2, route):
    n = h.shape[0]
    tm = TOKEN_TILE
    row = lambda w: pl.BlockSpec((tm, w), lambda i: (i, 0))
    ysp = lambda k: pl.BlockSpec((1, tm, D_MODEL), lambda i: (k, i, 0))
    return pl.pallas_call(
        _combine_body,
        out_shape=jax.ShapeDtypeStruct((n, D_MODEL), F32),
        grid=(n // tm,),
        in_specs=[row(D_MODEL), ysp(0), ysp(1), row(LANES)],
        out_specs=row(D_MODEL),
        compiler_params=_cparams("parallel"),
        name="moe_combine",
    )(h, y2, y2, route)


def _bias_prompt(rel_bias):
    r = jnp.arange(ATTN_TILE)[:, None]
    c = jnp.arange(ATTN_WINDOW)[None, :]
    rel = r + BAND_CHUNKS * CHUNK - c
    j = c // CHUNK - r // CHUNK
    band = jnp.logical_and(j >= 0, j <= BAND_CHUNKS)
    idx = jnp.clip(rel, -REL_CLIP, REL_CLIP) + REL_CLIP
    bias = jnp.take(rel_bias.astype(F32), idx, axis=1)
    return jnp.where(band[None], bias, NEG_INF)


def _bias_sample(rel_bias, la, tn):
    rel = (la + jnp.arange(tn))[:, None] - jnp.arange(la + tn)[None, :]
    idx = jnp.clip(rel, -REL_CLIP, REL_CLIP) + REL_CLIP
    bias = jnp.take(rel_bias.astype(F32), idx, axis=1)
    return bias[:, :, :la], bias[:, :, la:]


def _router_weights(w_group, b_group, w_expert, b_expert):
    pad = LANES - N_EXPERTS - N_GROUPS
    w = jnp.concatenate([w_expert, w_group, jnp.zeros((D_MODEL, pad), F32)], axis=1)
    b = jnp.concatenate([b_expert, b_group, jnp.zeros((pad,), F32)])[None, :].astype(F32)
    hi, lo = _split2(w.astype(F32))
    return hi, lo, b


def kernel(x_prompt, x_sample, cache_a_k, cache_a_v, state_hgrn, state_conv, norm_mix, norm_ffn,
           w_in_ab, w_out_ab, q_norm, k_norm, rel_bias, hgrn_lb_logits, hgrn_out_norm, w_in_c,
           conv_w, w_out_c, w_group, b_group, w_expert, b_expert, w1, w3, w2):
    batch, seq, d = x_prompt.shape
    nseq_s, tn, _ = x_sample.shape
    la = cache_a_k.shape[2]
    n_p = batch * seq
    n_s = nseq_s * tn
    n = n_p + n_s
    keep = min(BAND_CHUNKS * CHUNK, seq)

    x = jnp.concatenate([x_prompt.reshape(n_p, d), x_sample.reshape(n_s, d)], axis=0)
    lb_all = jnp.cumsum(jax.nn.softmax(hgrn_lb_logits.astype(F32), axis=0), axis=0)
    head_avg = jnp.kron(jnp.eye(N_HEADS, dtype=F32),
                        jnp.full((HEAD_DIM, HEAD_DIM), 1.0 / HEAD_DIM, F32)).astype(BF16)
    t = jnp.arange(TOKEN_TILE)
    tstrict = (t[None, :] < t[:, None]).astype(BF16)
    row1 = lambda v: v.astype(F32).reshape(1, -1)
    tile8 = lambda v: jnp.tile(v.astype(F32), N_HEADS).reshape(1, -1)

    l = 0
    (q16, kf, vf, k16, v16, lf, kb, vb, qb, gb) = _proj_ab(
        x, row1(norm_mix[0]), w_in_ab[l].astype(BF16), tile8(q_norm[l]), tile8(k_norm[l]),
        row1(lb_all[l]), head_avg)

    oa_p = _attn_prompt(q16, k16, v16, _bias_prompt(rel_bias[l]), batch, seq)
    bias_c, bias_n = _bias_sample(rel_bias[l], la, tn)
    oa_s = _attn_sample(q16, k16, v16, cache_a_k[l].reshape(nseq_s, la, D_HALF),
                        cache_a_v[l].reshape(nseq_s, la, D_HALF), bias_c, bias_n, n_p, nseq_s, tn)
    oa = jnp.concatenate([oa_p, oa_s], axis=0)

    zeros_state = jnp.zeros((batch, N_HEADS, HEAD_DIM, HEAD_DIM), F32)
    ob_p, st_p = _hgrn(qb, kb, vb, lf, zeros_state, 0, batch, seq, 64, 4)
    ob_s, st_s = _hgrn(qb, kb, vb, lf, jnp.swapaxes(state_hgrn[l].astype(F32), -1, -2),
                       n_p, nseq_s, tn, tn, 1)
    ob = jnp.concatenate([ob_p, ob_s], axis=0)

    wrh, wrl, br = _router_weights(w_group[0], b_group[0], w_expert[0], b_expert[0])
    h1, xn1, route1, cnt1 = _out_ab(x, oa, ob, gb, row1(hgrn_out_norm[l]), head_avg,
                                    w_out_ab[l].astype(BF16), row1(norm_ffn[0]), wrh, wrl, br, tstrict)
    y1 = _moe(xn1, route1, [cnt1], [n], w1[0], w3[0], w2[0])

    h2, u, bg = _proj_c(h1, y1, route1, row1(norm_mix[1]), w_in_c[0].astype(BF16))
    wrh, wrl, br = _router_weights(w_group[1], b_group[1], w_expert[1], b_expert[1])
    conv_args = (conv_w[0].astype(F32), w_out_c[0].astype(BF16), row1(norm_ffn[1]), wrh, wrl, br, tstrict)
    h3_p, xn2_p, route2_p, cnt2_p = _conv_out(u, bg, h2, jnp.zeros((batch, 2, d), F32), *conv_args,
                                              0, batch, seq)
    h3_s, xn2_s, route2_s, cnt2_s = _conv_out(u, bg, h2, state_conv[0].astype(F32), *conv_args,
                                              n_p, nseq_s, tn)
    h3 = jnp.concatenate([h3_p, h3_s], axis=0)
    xn2 = jnp.concatenate([xn2_p, xn2_s], axis=0)
    route2 = jnp.concatenate([route2_p, route2_s], axis=0)
    y2 = _moe(xn2, route2, [cnt2_p, cnt2_s], [n_p, n_s], w1[1], w3[1], w2[1])
    out = _combine(h3, y2, route2)

    y_prompt = out[:n_p].reshape(batch, seq, d)
    y_sample = out[n_p:].reshape(nseq_s, tn, d)
    heads = lambda a, b_, t_: a.reshape(b_, t_, N_HEADS, HEAD_DIM)
    kf_p = heads(kf[:n_p], batch, seq)
    vf_p = heads(vf[:n_p], batch, seq)
    kf_s = heads(kf[n_p:], nseq_s, tn)
    vf_s = heads(vf[n_p:], nseq_s, tn)
    nk_p = kf_p[:, -keep:][None]
    nv_p = vf_p[:, -keep:][None]
    nk_s = jnp.concatenate([cache_a_k[l].astype(F32), kf_s], axis=1)[:, -la:][None]
    nv_s = jnp.concatenate([cache_a_v[l].astype(F32), vf_s], axis=1)[:, -la:][None]
    nh_p = jnp.swapaxes(st_p, -1, -2)[None]
    nh_s = jnp.swapaxes(st_s, -1, -2)[None]
    u_p = u[:n_p].reshape(batch, seq, d)
    u_s = u[n_p:].reshape(nseq_s, tn, d)
    nc_p = u_p[:, -2:][None]
    nc_s = jnp.concatenate([state_conv[0].astype(F32), u_s], axis=1)[:, -2:][None]
    return (y_prompt, y_sample, nk_p, nv_p, nk_s, nv_s, nh_p, nh_s, nc_p, nc_s)
```

```python
import functools

import jax
import jax.numpy as jnp
from jax import lax
from jax.experimental import pallas as pl
from jax.experimental.pallas import tpu as pltpu

F32 = jnp.float32
BF16 = jnp.bfloat16
I32 = jnp.int32

D_MODEL = 1024
CHUNK = 64
BAND_CHUNKS = 8
HEAD_DIM = 64
N_HEADS = 8
D_HALF = N_HEADS * HEAD_DIM
REL_CLIP = 128
HGRN_BLOCK = 16
N_GROUPS = 4
EXPERTS_PER_GROUP = 8
N_EXPERTS = N_GROUPS * EXPERTS_PER_GROUP
D_EXPERT = 256
RMS_EPS = 1e-6
NEG_INF = -1e30

LANES = 128
ROW_TILE = D_MODEL // LANES
TOKEN_TILE = 256
ATTN_TILE = 256
ATTN_WINDOW = ATTN_TILE + BAND_CHUNKS * CHUNK
MOE_TILE = 256
VMEM_LIMIT = 48 * 1024 * 1024


def _cparams(*sem):
    return pltpu.CompilerParams(dimension_semantics=sem, vmem_limit_bytes=VMEM_LIMIT)


def _const_spec(shape):
    nd = len(shape)
    return pl.BlockSpec(shape, lambda *_: (0,) * nd)


def _store_row_tiles(ref, val):
    rows = val.shape[0]
    for c in range(ROW_TILE):
        ref[pl.ds(c, rows, stride=ROW_TILE), :] = val[:, c * LANES:(c + 1) * LANES]


def _load_row_tiles(ref, rows):
    return jnp.concatenate([ref[pl.ds(c, rows, stride=ROW_TILE), :] for c in range(ROW_TILE)], axis=1)


def _rms(x, gain):
    ms = jnp.mean(x * x, axis=-1, keepdims=True)
    return (x * lax.rsqrt(ms + RMS_EPS)) * gain


def _split2(x):
    hi = x.astype(BF16)
    lo = (x - hi.astype(F32)).astype(BF16)
    return hi, lo


def _split3(x):
    p0 = x.astype(BF16)
    r = x - p0.astype(F32)
    p1 = r.astype(BF16)
    p2 = (r - p1.astype(F32)).astype(BF16)
    return p0, p1, p2


def _dot(a, b):
    return jnp.dot(a, b, preferred_element_type=F32)


def _dot_nt(a, b):
    return lax.dot_general(a, b, (((1,), (1,)), ((), ())), preferred_element_type=F32)


def _dot_tn(a, b):
    return lax.dot_general(a, b, (((0,), (0,)), ((), ())), preferred_element_type=F32)


def _head_mean_sq(v, bd):
    hi, lo = _split2(v * v)
    return _dot(hi, bd) + _dot(lo, bd)


def _proj_ab_body(x_ref, g_ref, w_ref, qg_ref, kg_ref, lb_ref, bd_ref,
                  q_o, kf_o, vf_o, k16_o, v16_o, lf_o, kb_o, vb_o, qb_o, gb_o):
    xb = _rms(x_ref[...], g_ref[...]).astype(BF16)
    bd = bd_ref[...]

    def seg(j):
        return _dot(xb, w_ref[:, j * D_HALF:(j + 1) * D_HALF])

    qa = seg(0)
    qn = qa * lax.rsqrt(_head_mean_sq(qa, bd) + RMS_EPS) * qg_ref[...]
    q_o[...] = (qn * (HEAD_DIM ** -0.5)).astype(BF16)
    ka = seg(1)
    kn = ka * lax.rsqrt(_head_mean_sq(ka, bd) + RMS_EPS) * kg_ref[...]
    kf_o[...] = kn
    k16_o[...] = kn.astype(BF16)
    va = seg(2)
    vf_o[...] = va
    v16_o[...] = va.astype(BF16)
    lb = lb_ref[...]
    f = lb + (1.0 - lb) * jax.nn.sigmoid(seg(3))
    lf_o[...] = jnp.log(f)
    kb_o[...] = 1.0 - f
    vb_o[...] = seg(4)
    qb_o[...] = jax.nn.silu(seg(5))
    gb_o[...] = seg(6)


def _proj_ab(x, gain, w16, qg, kg, lb, bd):
    n = x.shape[0]
    tm = TOKEN_TILE
    row = lambda w: pl.BlockSpec((tm, w), lambda i: (i, 0))
    outs = [jax.ShapeDtypeStruct((n, D_HALF), dt)
            for dt in (BF16, F32, F32, BF16, BF16, F32, F32, F32, F32, F32)]
    return pl.pallas_call(
        _proj_ab_body,
        out_shape=outs,
        grid=(n // tm,),
        in_specs=[row(D_MODEL), _const_spec((1, D_MODEL)), _const_spec(w16.shape),
                  _const_spec((1, D_HALF)), _const_spec((1, D_HALF)), _const_spec((1, D_HALF)),
                  _const_spec((D_HALF, D_HALF))],
        out_specs=[row(D_HALF)] * 10,
        compiler_params=_cparams("parallel"),
        name="proj_ab",
    )(x, gain, w16, qg, kg, lb, bd)


def _attn_prompt_body(q_ref, k0, k1, k2, v0, v1, v2, bias_ref, o_ref):
    i = pl.program_id(1)
    lim = jnp.maximum(2 - i, 0) * ATTN_TILE
    col = lax.broadcasted_iota(I32, (ATTN_TILE, ATTN_WINDOW), 1)
    dead = col < lim
    for h in range(N_HEADS):
        hs = slice(h * HEAD_DIM, (h + 1) * HEAD_DIM)
        qh = q_ref[:, hs]
        s = jnp.concatenate([_dot_nt(qh, k0[:, hs]), _dot_nt(qh, k1[:, hs]),
                             _dot_nt(qh, k2[:, hs])], axis=1)
        s = jnp.where(dead, NEG_INF, s + bias_ref[h])
        m = jnp.max(s, axis=-1, keepdims=True)
        p = jnp.exp(s - m)
        l = jnp.sum(p, axis=-1, keepdims=True)
        p16 = p.astype(BF16)
        o = (_dot(p16[:, 0:ATTN_TILE], v0[:, hs])
             + _dot(p16[:, ATTN_TILE:2 * ATTN_TILE], v1[:, hs])
             + _dot(p16[:, 2 * ATTN_TILE:], v2[:, hs]))
        o_ref[:, hs] = (o / l).astype(BF16)


def _attn_prompt(q16, k16, v16, bias_full, batch, seq):
    tiles = seq // ATTN_TILE
    qspec = pl.BlockSpec((ATTN_TILE, D_HALF), lambda b, i: (b * tiles + i, 0))

    def kv(back):
        return pl.BlockSpec((ATTN_TILE, D_HALF),
                            lambda b, i: (b * tiles + jnp.maximum(i - back, 0), 0))

    return pl.pallas_call(
        _attn_prompt_body,
        out_shape=jax.ShapeDtypeStruct((batch * seq, D_HALF), BF16),
        grid=(batch, tiles),
        in_specs=[qspec, kv(2), kv(1), kv(0), kv(2), kv(1), kv(0),
                  _const_spec(bias_full.shape)],
        out_specs=qspec,
        compiler_params=_cparams("parallel", "parallel"),
        name="attn_prompt",
    )(q16, k16, k16, k16, v16, v16, v16, bias_full)


def _attn_sample_body(q_ref, kc_ref, vc_ref, kn_ref, vn_ref, bc_ref, bn_ref, o_ref):
    kc = kc_ref[0].astype(BF16)
    vc = vc_ref[0].astype(BF16)
    for h in range(N_HEADS):
        hs = slice(h * HEAD_DIM, (h + 1) * HEAD_DIM)
        qh = q_ref[:, hs]
        sc = _dot_nt(qh, kc[:, hs]) + bc_ref[h]
        sn = _dot_nt(qh, kn_ref[:, hs]) + bn_ref[h]
        m = jnp.maximum(jnp.max(sc, axis=-1, keepdims=True), jnp.max(sn, axis=-1, keepdims=True))
        pc = jnp.exp(sc - m)
        pn = jnp.exp(sn - m)
        l = jnp.sum(pc, axis=-1, keepdims=True) + jnp.sum(pn, axis=-1, keepdims=True)
        o = _dot(pc.astype(BF16), vc[:, hs]) + _dot(pn.astype(BF16), vn_ref[:, hs])
        o_ref[:, hs] = (o / l).astype(BF16)


def _attn_sample(q16, k16, v16, cache_k, cache_v, bias_c, bias_n, row0, nseq, tn):
    blk0 = row0 // tn
    la = cache_k.shape[1]
    new = pl.BlockSpec((tn, D_HALF), lambda b: (blk0 + b, 0))
    cache = pl.BlockSpec((1, la, D_HALF), lambda b: (b, 0, 0))
    return pl.pallas_call(
        _attn_sample_body,
        out_shape=jax.ShapeDtypeStruct((nseq * tn, D_HALF), BF16),
        grid=(nseq,),
        in_specs=[new, cache, cache, new, new, _const_spec(bias_c.shape), _const_spec(bias_n.shape)],
        out_specs=pl.BlockSpec((tn, D_HALF), lambda b: (b, 0)),
        compiler_params=_cparams("parallel"),
        name="attn_sample",
    )(q16, cache_k, cache_v, k16, v16, bias_c, bias_n)


def _hgrn_body(q_ref, k_ref, v_ref, lf_ref, s0_ref, tblk_ref, tfull_ref, o_ref, sout_ref, st_scr,
               *, chunk, nchunks):
    nblk = chunk // HGRN_BLOCK

    @pl.when(pl.program_id(1) == 0)
    def _():
        st_scr[...] = s0_ref[0]

    row = lax.broadcasted_iota(I32, (chunk, D_HALF), 0)
    r2 = lax.broadcasted_iota(I32, (chunk, chunk), 0)
    c2 = lax.broadcasted_iota(I32, (chunk, chunk), 1)
    same_blk_causal = jnp.logical_and(r2 // HGRN_BLOCK == c2 // HGRN_BLOCK, c2 <= r2)

    def one_chunk(c, carry):
        sl = pl.ds(pl.multiple_of(c * chunk, chunk), chunk)
        q = q_ref[sl, :]
        k = k_ref[sl, :]
        v16 = v_ref[sl, :].astype(BF16)
        l0, l1, l2 = _split3(lf_ref[sl, :])
        tb = tblk_ref[...]
        tf = tfull_ref[...]
        b_in = _dot(tb, l0) + _dot(tb, l1) + _dot(tb, l2)
        b_ch = _dot(tf, l0) + _dot(tf, l1) + _dot(tf, l2)
        ld = (q * jnp.exp(b_in)).astype(BF16)
        rd = (k * jnp.exp(-b_in)).astype(BF16)
        lj, rj = [], []
        for j in range(nblk - 1):
            e_j = b_ch[(j + 1) * HGRN_BLOCK - 1:(j + 1) * HGRN_BLOCK, :]
            later = row >= (j + 1) * HGRN_BLOCK
            inside = jnp.logical_and(row >= j * HGRN_BLOCK, row < (j + 1) * HGRN_BLOCK)
            lj.append(jnp.where(later, q * jnp.exp(jnp.minimum(b_ch - e_j, 0.0)), 0.0).astype(BF16))
            rj.append(jnp.where(inside, k * jnp.exp(jnp.minimum(e_j - b_ch, 0.0)), 0.0).astype(BF16))
        e_end = b_ch[chunk - 1:chunk, :]
        qc = (q * jnp.exp(b_ch)).astype(BF16)
        kc = (k * jnp.exp(e_end - b_ch)).astype(BF16)
        dec = jnp.exp(e_end)
        for h in range(N_HEADS):
            hs = slice(h * HEAD_DIM, (h + 1) * HEAD_DIM)
            sc = jnp.where(same_blk_causal, _dot_nt(ld[:, hs], rd[:, hs]), 0.0)
            for j in range(nblk - 1):
                sc = sc + _dot_nt(lj[j][:, hs], rj[j][:, hs])
            st = st_scr[h]
            o = _dot(sc.astype(BF16), v16[:, hs]) + _dot_nt(qc[:, hs], st.astype(BF16))
            o_ref[sl, hs] = o
            st_scr[h] = st * dec[:, hs] + _dot_tn(v16[:, hs], kc[:, hs])
        return carry

    lax.fori_loop(0, nchunks, one_chunk, 0)
    sout_ref[0] = st_scr[...]


def _hgrn(qb, kb, vb, lf, s0t, row0, nseq, seq, chunk, nchunks):
    tt = chunk * nchunks
    steps = seq // tt
    blk0 = row0 // tt
    t = jnp.arange(chunk)
    lower = t[None, :] <= t[:, None]
    tfull = lower.astype(BF16)
    tblk = jnp.logical_and(lower, (t[None, :] // HGRN_BLOCK) == (t[:, None] // HGRN_BLOCK)).astype(BF16)
    tok = pl.BlockSpec((tt, D_HALF), lambda b, j: (blk0 + b * steps + j, 0))
    state = pl.BlockSpec((1, N_HEADS, HEAD_DIM, HEAD_DIM), lambda b, j: (b, 0, 0, 0))
    return pl.pallas_call(
        functools.partial(_hgrn_body, chunk=chunk, nchunks=nchunks),
        out_shape=[jax.ShapeDtypeStruct((nseq * seq, D_HALF), F32),
                   jax.ShapeDtypeStruct(s0t.shape, F32)],
        grid=(nseq, steps),
        in_specs=[tok, tok, tok, tok, state, _const_spec((chunk, chunk)), _const_spec((chunk, chunk))],
        out_specs=[pl.BlockSpec((tt, D_HALF), lambda b, j: (b * steps + j, 0)), state],
        scratch_shapes=[pltpu.VMEM((N_HEADS, HEAD_DIM, HEAD_DIM), F32)],
        compiler_params=_cparams("parallel", "arbitrary"),
        name="hgrn",
    )(qb, kb, vb, lf, s0t, tblk, tfull)


def _route(xn, wr_hi, wr_lo, br, tstrict, cnt_scr, route_o, cnt_o):
    rows = xn.shape[0]
    x_hi, x_lo = _split2(xn)
    logits = _dot(x_hi, wr_hi) + _dot(x_lo, wr_hi) + _dot(x_hi, wr_lo) + br
    lane = lax.broadcasted_iota(I32, (rows, LANES), 1)
    lane_f = lane.astype(F32)

    def first_argmax(vals):
        m = jnp.max(vals, axis=-1, keepdims=True)
        idx = jnp.min(jnp.where(vals == m, lane_f, float(LANES)), axis=-1, keepdims=True)
        return m, idx.astype(I32)

    is_grp = jnp.logical_and(lane >= N_EXPERTS, lane < N_EXPERTS + N_GROUPS)
    gl = jnp.where(is_grp, logits, -jnp.inf)
    gmax, gidx = first_argmax(gl)
    p_grp = 1.0 / jnp.sum(jnp.exp(gl - gmax), axis=-1, keepdims=True)
    grp = gidx - N_EXPERTS
    in_grp = jnp.logical_and(lane < N_EXPERTS, lane // EXPERTS_PER_GROUP == grp)
    el = jnp.where(in_grp, logits, -jnp.inf)
    v1, e1 = first_argmax(el)
    v2, e2 = first_argmax(jnp.where(lane == e1, -jnp.inf, el))
    t2 = jnp.exp(v2 - v1)
    den = 1.0 + t2
    g1 = (1.0 / den) * p_grp
    g2 = (t2 / den) * p_grp

    oh1 = lane == e1
    oh2 = lane == e2
    oh = jnp.where(jnp.logical_or(oh1, oh2), 1.0, 0.0)
    before = _dot(tstrict, oh.astype(BF16)) + cnt_scr[...]
    rank1 = jnp.sum(jnp.where(oh1, before, 0.0), axis=-1, keepdims=True)
    rank2 = jnp.sum(jnp.where(oh2, before, 0.0), axis=-1, keepdims=True)
    cnt_scr[...] = cnt_scr[...] + jnp.sum(oh, axis=0, keepdims=True)
    cnt_o[...] = cnt_scr[...]

    slab = jnp.zeros((rows, LANES), F32)
    for idx, val in enumerate((e1.astype(F32), e2.astype(F32), g1, g2, rank1, rank2)):
        slab = jnp.where(lane == idx, val, slab)
    route_o[...] = slab


def _out_ab_body(h_ref, oa_ref, ob_ref, gb_ref, og_ref, bd_ref, w_ref, gf_ref,
                 wrh_ref, wrl_ref, br_ref, ts_ref, h_o, xn_o, route_o, cnt_o, cnt_scr):
    @pl.when(pl.program_id(0) == 0)
    def _():
        cnt_scr[...] = jnp.zeros_like(cnt_scr)

    ob = ob_ref[...]
    obn = ob * lax.rsqrt(_head_mean_sq(ob, bd_ref[...]) + RMS_EPS) * og_ref[...]
    obg = (obn * jax.nn.silu(gb_ref[...])).astype(BF16)
    mix = _dot(oa_ref[...], w_ref[0:D_HALF, :]) + _dot(obg, w_ref[D_HALF:, :])
    h1 = h_ref[...] + mix
    h_o[...] = h1
    xn = _rms(h1, gf_ref[...])
    _store_row_tiles(xn_o, xn)
    _route(xn, wrh_ref[...], wrl_ref[...], br_ref[...], ts_ref[...], cnt_scr, route_o, cnt_o)


def _out_ab(h, oa16, ob, gb, og, bd, w16, gf, wrh, wrl, br, tstrict):
    n = h.shape[0]
    tm = TOKEN_TILE
    row = lambda w: pl.BlockSpec((tm, w), lambda i: (i, 0))
    return pl.pallas_call(
        _out_ab_body,
        out_shape=[jax.ShapeDtypeStruct((n, D_MODEL), F32), jax.ShapeDtypeStruct((n * ROW_TILE, LANES), F32),
                   jax.ShapeDtypeStruct((n, LANES), F32), jax.ShapeDtypeStruct((1, LANES), F32)],
        grid=(n // tm,),
        in_specs=[row(D_MODEL), row(D_HALF), row(D_HALF), row(D_HALF), _const_spec((1, D_HALF)),
                  _const_spec((D_HALF, D_HALF)), _const_spec((D_MODEL, D_MODEL)),
                  _const_spec((1, D_MODEL)), _const_spec((D_MODEL, LANES)),
                  _const_spec((D_MODEL, LANES)), _const_spec((1, LANES)), _const_spec((tm, tm))],
        out_specs=[row(D_MODEL), pl.BlockSpec((tm * ROW_TILE, LANES), lambda i: (i, 0)), row(LANES),
                   _const_spec((1, LANES))],
        scratch_shapes=[pltpu.VMEM((1, LANES), F32)],
        compiler_params=_cparams("arbitrary"),
        name="out_ab",
    )(h, oa16, ob, gb, og, bd, w16, gf, wrh, wrl, br, tstrict)


def _row_copy(src, src_row, dst, dst_row, sem):
    s0 = pl.multiple_of(src_row * ROW_TILE, ROW_TILE)
    d0 = pl.multiple_of(dst_row * ROW_TILE, ROW_TILE)
    return pltpu.make_async_copy(src.at[pl.ds(s0, ROW_TILE)], dst.at[pl.ds(d0, ROW_TILE)], sem)


def _dispatch_body(pos_ref, x_hbm, init_hbm, xs_hbm, sem, *, n_tok, tile):
    del init_hbm
    base = pl.program_id(0) * tile

    def issue(t, carry):
        tok = base + t
        _row_copy(x_hbm, tok, xs_hbm, pos_ref[tok], sem).start()
        _row_copy(x_hbm, tok, xs_hbm, pos_ref[n_tok + tok], sem).start()
        return carry

    def drain(t, carry):
        _row_copy(x_hbm, 0, xs_hbm, 0, sem).wait()
        _row_copy(x_hbm, 0, xs_hbm, 0, sem).wait()
        return carry

    lax.fori_loop(0, tile, issue, 0)
    lax.fori_loop(0, tile, drain, 0)


def _dispatch(pos, xn, n_rows):
    n = xn.shape[0] // ROW_TILE
    init = jnp.zeros((n_rows * ROW_TILE, LANES), F32)
    return pl.pallas_call(
        functools.partial(_dispatch_body, n_tok=n, tile=TOKEN_TILE),
        out_shape=jax.ShapeDtypeStruct((n_rows * ROW_TILE, LANES), F32),
        grid_spec=pltpu.PrefetchScalarGridSpec(
            num_scalar_prefetch=1,
            grid=(n // TOKEN_TILE,),
            in_specs=[pl.BlockSpec(memory_space=pl.ANY), pl.BlockSpec(memory_space=pl.ANY)],
            out_specs=pl.BlockSpec(memory_space=pl.ANY),
            scratch_shapes=[pltpu.SemaphoreType.DMA(())],
        ),
        input_output_aliases={2: 0},
        compiler_params=pltpu.CompilerParams(dimension_semantics=("arbitrary",)),
        name="moe_dispatch",
    )(pos, xn, init)


def _collect_body(pos_ref, y_hbm, out_hbm, sem, *, n_tok, tile):
    base = pl.program_id(0) * tile

    def issue(t, carry):
        tok = base + t
        _row_copy(y_hbm, pos_ref[tok], out_hbm.at[0], tok, sem).start()
        _row_copy(y_hbm, pos_ref[n_tok + tok], out_hbm.at[1], tok, sem).start()
        return carry

    def drain(t, carry):
        _row_copy(y_hbm, 0, out_hbm.at[0], 0, sem).wait()
        _row_copy(y_hbm, 0, out_hbm.at[1], 0, sem).wait()
        return carry

    lax.fori_loop(0, tile, issue, 0)
    lax.fori_loop(0, tile, drain, 0)


def _collect(pos, y, n):
    return pl.pallas_call(
        functools.partial(_collect_body, n_tok=n, tile=TOKEN_TILE),
        out_shape=jax.ShapeDtypeStruct((2, n * ROW_TILE, LANES), F32),
        grid_spec=pltpu.PrefetchScalarGridSpec(
            num_scalar_prefetch=1,
            grid=(n // TOKEN_TILE,),
            in_specs=[pl.BlockSpec(memory_space=pl.ANY)],
            out_specs=pl.BlockSpec(memory_space=pl.ANY),
            scratch_shapes=[pltpu.SemaphoreType.DMA(())],
        ),
        compiler_params=pltpu.CompilerParams(dimension_semantics=("arbitrary",)),
        name="moe_collect",
    )(pos, y)


def _experts_body(te_ref, nt_ref, x_ref, w1_ref, w3_ref, w2_ref, y_ref, w1_s, w3_s, w2_s):
    i = pl.program_id(0)
    prev = te_ref[jnp.maximum(i - 1, 0)]
    fresh = jnp.logical_or(i == 0, te_ref[i] != prev)

    @pl.when(fresh)
    def _():
        w1_s[...] = w1_ref[0].astype(BF16)
        w3_s[...] = w3_ref[0].astype(BF16)
        w2_s[...] = w2_ref[0].astype(BF16)

    @pl.when(i < nt_ref[0])
    def _():
        x = _load_row_tiles(x_ref, MOE_TILE).astype(BF16)
        a = _dot(x, w1_s[...])
        b = _dot(x, w3_s[...])
        _store_row_tiles(y_ref, _dot((jax.nn.silu(a) * b).astype(BF16), w2_s[...]))

    @pl.when(i >= nt_ref[0])
    def _():
        y_ref[...] = jnp.zeros_like(y_ref)


def _experts(tile_expert, n_tiles, xs, w1, w3, w2):
    rows = xs.shape[0] // ROW_TILE
    tm = MOE_TILE
    return pl.pallas_call(
        _experts_body,
        out_shape=jax.ShapeDtypeStruct((rows * ROW_TILE, LANES), F32),
        grid_spec=pltpu.PrefetchScalarGridSpec(
            num_scalar_prefetch=2,
            grid=(rows // tm,),
            in_specs=[pl.BlockSpec((tm * ROW_TILE, LANES), lambda i, te, nt: (i, 0)),
                      pl.BlockSpec((1, D_MODEL, D_EXPERT), lambda i, te, nt: (te[i], 0, 0)),
                      pl.BlockSpec((1, D_MODEL, D_EXPERT), lambda i, te, nt: (te[i], 0, 0)),
                      pl.BlockSpec((1, D_EXPERT, D_MODEL), lambda i, te, nt: (te[i], 0, 0))],
            out_specs=pl.BlockSpec((tm * ROW_TILE, LANES), lambda i, te, nt: (i, 0)),
            scratch_shapes=[pltpu.VMEM((D_MODEL, D_EXPERT), BF16), pltpu.VMEM((D_MODEL, D_EXPERT), BF16),
                            pltpu.VMEM((D_EXPERT, D_MODEL), BF16)],
        ),
        compiler_params=_cparams("arbitrary"),
        name="moe_experts",
    )(tile_expert, n_tiles, xs, w1, w3, w2)


def _moe_plan(route, counts_per_segment, seg_rows):
    e = route[:, 0:2].astype(I32)
    rank = route[:, 4:6].astype(I32)
    seg_counts = [c[0, :N_EXPERTS].astype(I32) for c in counts_per_segment]
    base = jnp.zeros((N_EXPERTS,), I32)
    seg_base = []
    for c in seg_counts:
        seg_base.append(base)
        base = base + c
    counts = base
    row_base = jnp.concatenate([jnp.broadcast_to(b[None], (r, N_EXPERTS))
                                for b, r in zip(seg_base, seg_rows)], axis=0)
    padded = ((counts + MOE_TILE - 1) // MOE_TILE) * MOE_TILE
    ends = jnp.cumsum(padded)
    offs = ends - padded
    pos = jnp.take(offs, e) + rank + jnp.take_along_axis(row_base, e, axis=1)
    pos_flat = jnp.concatenate([pos[:, 0], pos[:, 1]])
    n = route.shape[0]
    max_tiles = (2 * n + N_EXPERTS * (MOE_TILE - 1) + MOE_TILE - 1) // MOE_TILE
    tile_start = jnp.arange(max_tiles, dtype=I32) * MOE_TILE
    tile_expert = jnp.minimum(jnp.sum((tile_start[:, None] >= ends[None, :]).astype(I32), axis=1),
                              N_EXPERTS - 1)
    n_tiles = (ends[-1] // MOE_TILE).reshape(1)
    return pos_flat, tile_expert, n_tiles, max_tiles * MOE_TILE


def _moe(xn, route, counts_per_segment, seg_rows, w1, w3, w2):
    pos, tile_expert, n_tiles, n_rows = _moe_plan(route, counts_per_segment, seg_rows)
    xs = _dispatch(pos, xn, n_rows)
    ys = _experts(tile_expert, n_tiles, xs, w1, w3, w2)
    return _collect(pos, ys, xn.shape[0] // ROW_TILE)


def _add_moe(h, ya_ref, yb_ref, route):
    rows = h.shape[0]
    ya = _load_row_tiles(ya_ref.at[0], rows)
    yb = _load_row_tiles(yb_ref.at[0], rows)
    return h + (route[:, 2:3] * ya + route[:, 3:4] * yb)


def _moe_out_spec(k, tm):
    return pl.BlockSpec((1, tm * ROW_TILE, LANES), lambda i: (k, i, 0))


def _proj_c_body(h_ref, ya_ref, yb_ref, r_ref, g_ref, w_ref, h_o, u_o, bg_o):
    h2 = _add_moe(h_ref[...], ya_ref, yb_ref, r_ref[...])
    h_o[...] = h2
    xb = _rms(h2, g_ref[...]).astype(BF16)
    bg_o[...] = _dot(xb, w_ref[:, 0:D_MODEL])
    u_o[...] = _dot(xb, w_ref[:, D_MODEL:2 * D_MODEL]) * _dot(xb, w_ref[:, 2 * D_MODEL:])


def _proj_c(h, y2, route, gain, w16):
    n = h.shape[0]
    tm = TOKEN_TILE
    row = lambda w: pl.BlockSpec((tm, w), lambda i: (i, 0))
    ysp = lambda k: _moe_out_spec(k, tm)
    return pl.pallas_call(
        _proj_c_body,
        out_shape=[jax.ShapeDtypeStruct((n, D_MODEL), F32)] * 3,
        grid=(n // tm,),
        in_specs=[row(D_MODEL), ysp(0), ysp(1), row(LANES), _const_spec((1, D_MODEL)),
                  _const_spec(w16.shape)],
        out_specs=[row(D_MODEL)] * 3,
        compiler_params=_cparams("parallel"),
        name="proj_c",
    )(h, y2, y2, route, gain, w16)


def _conv_out_body(u_ref, up_ref, st_ref, bg_ref, h_ref, cw_ref, w_ref, gf_ref,
                   wrh_ref, wrl_ref, br_ref, ts_ref, h_o, xn_o, route_o, cnt_o, cnt_scr,
                   *, tile, seq):
    i = pl.program_id(0)

    @pl.when(i == 0)
    def _():
        cnt_scr[...] = jnp.zeros_like(cnt_scr)

    u = u_ref[...]
    rowi = lax.broadcasted_iota(I32, (tile, D_MODEL), 0)
    if seq >= tile:
        at_start = (i * tile) % seq == 0
        st = st_ref[0]
        prev = up_ref[...]
        m2 = jnp.where(at_start, st[0:1, :], prev[6:7, :])
        m1 = jnp.where(at_start, st[1:2, :], prev[7:8, :])
        pos = rowi
    else:
        per = tile // seq
        st = st_ref[...]
        m2 = jnp.broadcast_to(st[:, 0:1, :], (per, seq, D_MODEL)).reshape(tile, D_MODEL)
        m1 = jnp.broadcast_to(st[:, 1:2, :], (per, seq, D_MODEL)).reshape(tile, D_MODEL)
        pos = rowi % seq
    u1 = jnp.where(pos == 0, m1, pltpu.roll(u, 1, axis=0))
    u2 = jnp.where(pos == 0, m2, jnp.where(pos == 1, m1, pltpu.roll(u, 2, axis=0)))
    cw = cw_ref[...]
    conv = u2 * cw[0:1, :] + u1 * cw[1:2, :] + u * cw[2:3, :]
    mix = _dot((bg_ref[...] * conv).astype(BF16), w_ref[...])
    h3 = h_ref[...] + mix
    h_o[...] = h3
    xn = _rms(h3, gf_ref[...])
    _store_row_tiles(xn_o, xn)
    _route(xn, wrh_ref[...], wrl_ref[...], br_ref[...], ts_ref[...], cnt_scr, route_o, cnt_o)


def _conv_out(u, bg, h, state, cw, w16, gf, wrh, wrl, br, tstrict, row0, nseq, seq):
    tm = TOKEN_TILE
    n_g = nseq * seq
    blk0 = row0 // tm
    row = lambda w: pl.BlockSpec((tm, w), lambda i: (blk0 + i, 0))
    orow = lambda w: pl.BlockSpec((tm, w), lambda i: (i, 0))
    prev = pl.BlockSpec((8, D_MODEL), lambda i: (jnp.maximum((blk0 + i) * (tm // 8) - 1, 0), 0))
    if seq >= tm:
        st_spec = pl.BlockSpec((1, 2, D_MODEL), lambda i: ((i * tm) // seq, 0, 0))
    else:
        st_spec = pl.BlockSpec((tm // seq, 2, D_MODEL), lambda i: (i, 0, 0))
    return pl.pallas_call(
        functools.partial(_conv_out_body, tile=tm, seq=seq),
        out_shape=[jax.ShapeDtypeStruct((n_g, D_MODEL), F32),
                   jax.ShapeDtypeStruct((n_g * ROW_TILE, LANES), F32),
                   jax.ShapeDtypeStruct((n_g, LANES), F32), jax.ShapeDtypeStruct((1, LANES), F32)],
        grid=(n_g // tm,),
        in_specs=[row(D_MODEL), prev, st_spec, row(D_MODEL), row(D_MODEL), _const_spec((3, D_MODEL)),
                  _const_spec((D_MODEL, D_MODEL)), _const_spec((1, D_MODEL)),
                  _const_spec((D_MODEL, LANES)), _const_spec((D_MODEL, LANES)),
                  _const_spec((1, LANES)), _const_spec((tm, tm))],
        out_specs=[orow(D_MODEL), pl.BlockSpec((tm * ROW_TILE, LANES), lambda i: (i, 0)), orow(LANES),
                   _const_spec((1, LANES))],
        scratch_shapes=[pltpu.VMEM((1, LANES), F32)],
        compiler_params=_cparams("arbitrary"),
        name="conv_out",
    )(u, u, state, bg, h, cw, w16, gf, wrh, wrl, br, tstrict)


def _combine_body(h_ref, ya_ref, yb_ref, r_ref, o_ref):
    o_ref[...] = _add_moe(h_ref[...], ya_ref, yb_ref, r_ref[...])


def _combine(h, y2, route):
    n = h.shape[0]
    tm = TOKEN_TILE
    row = lambda w: pl.BlockSpec((tm, w), lambda i: (i, 0))
    ysp = lambda k: _moe_out_spec(k, tm)
    return pl.pallas_call(
        _combine_body,
        out_shape=jax.ShapeDtypeStruct((n, D_MODEL), F32),
        grid=(n // tm,),
        in_specs=[row(D_MODEL), ysp(0), ysp(1), row(LANES)],
        out_specs=row(D_MODEL),
        compiler_params=_cparams("parallel"),
        name="moe_combine",
    )(h, y2, y2, route)


def _rel_bias_toeplitz(rel_bias, rows, cols, lead):
    period = rows + cols
    k = jnp.arange(period)
    d = jnp.where(k < cols, k, k - period)
    idx = jnp.clip(lead - d, -REL_CLIP, REL_CLIP) + REL_CLIP
    v = jnp.take(rel_bias.astype(F32), idx, axis=1)
    heads = v.shape[0]
    skew = jnp.tile(v, (1, rows))[:, :rows * (period - 1)].reshape(heads, rows, period - 1)
    return skew[:, :, :cols]


def _bias_prompt(rel_bias):
    r = jnp.arange(ATTN_TILE)[:, None]
    c = jnp.arange(ATTN_WINDOW)[None, :]
    j = c // CHUNK - r // CHUNK
    band = jnp.logical_and(j >= 0, j <= BAND_CHUNKS)
    bias = _rel_bias_toeplitz(rel_bias, ATTN_TILE, ATTN_WINDOW, BAND_CHUNKS * CHUNK)
    return jnp.where(band[None], bias, NEG_INF)


def _bias_sample(rel_bias, la, tn):
    bias = _rel_bias_toeplitz(rel_bias, tn, la + tn, la)
    return bias[:, :, :la], bias[:, :, la:]


def _router_weights(w_group, b_group, w_expert, b_expert):
    pad = LANES - N_EXPERTS - N_GROUPS
    w = jnp.concatenate([w_expert, w_group, jnp.zeros((D_MODEL, pad), F32)], axis=1)
    b = jnp.concatenate([b_expert, b_group, jnp.zeros((pad,), F32)])[None, :].astype(F32)
    hi, lo = _split2(w.astype(F32))
    return hi, lo, b


def kernel(x_prompt, x_sample, cache_a_k, cache_a_v, state_hgrn, state_conv, norm_mix, norm_ffn,
           w_in_ab, w_out_ab, q_norm, k_norm, rel_bias, hgrn_lb_logits, hgrn_out_norm, w_in_c,
           conv_w, w_out_c, w_group, b_group, w_expert, b_expert, w1, w3, w2):
    batch, seq, d = x_prompt.shape
    nseq_s, tn, _ = x_sample.shape
    la = cache_a_k.shape[2]
    n_p = batch * seq
    n_s = nseq_s * tn
    n = n_p + n_s
    keep = min(BAND_CHUNKS * CHUNK, seq)

    x = jnp.concatenate([x_prompt.reshape(n_p, d), x_sample.reshape(n_s, d)], axis=0)
    lb_all = jnp.cumsum(jax.nn.softmax(hgrn_lb_logits.astype(F32), axis=0), axis=0)
    head_avg = jnp.kron(jnp.eye(N_HEADS, dtype=F32),
                        jnp.full((HEAD_DIM, HEAD_DIM), 1.0 / HEAD_DIM, F32)).astype(BF16)
    t = jnp.arange(TOKEN_TILE)
    tstrict = (t[None, :] < t[:, None]).astype(BF16)
    row1 = lambda v: v.astype(F32).reshape(1, -1)
    tile8 = lambda v: jnp.tile(v.astype(F32), N_HEADS).reshape(1, -1)

    l = 0
    (q16, kf, vf, k16, v16, lf, kb, vb, qb, gb) = _proj_ab(
        x, row1(norm_mix[0]), w_in_ab[l].astype(BF16), tile8(q_norm[l]), tile8(k_norm[l]),
        row1(lb_all[l]), head_avg)

    oa_p = _attn_prompt(q16, k16, v16, _bias_prompt(rel_bias[l]), batch, seq)
    bias_c, bias_n = _bias_sample(rel_bias[l], la, tn)
    oa_s = _attn_sample(q16, k16, v16, cache_a_k[l].reshape(nseq_s, la, D_HALF),
                        cache_a_v[l].reshape(nseq_s, la, D_HALF), bias_c, bias_n, n_p, nseq_s, tn)
    oa = jnp.concatenate([oa_p, oa_s], axis=0)

    zeros_state = jnp.zeros((batch, N_HEADS, HEAD_DIM, HEAD_DIM), F32)
    ob_p, st_p = _hgrn(qb, kb, vb, lf, zeros_state, 0, batch, seq, 64, 4)
    ob_s, st_s = _hgrn(qb, kb, vb, lf, jnp.swapaxes(state_hgrn[l].astype(F32), -1, -2),
                       n_p, nseq_s, tn, tn, 1)
    ob = jnp.concatenate([ob_p, ob_s], axis=0)

    wrh, wrl, br = _router_weights(w_group[0], b_group[0], w_expert[0], b_expert[0])
    h1, xn1, route1, cnt1 = _out_ab(x, oa, ob, gb, row1(hgrn_out_norm[l]), head_avg,
                                    w_out_ab[l].astype(BF16), row1(norm_ffn[0]), wrh, wrl, br, tstrict)
    y1 = _moe(xn1, route1, [cnt1], [n], w1[0], w3[0], w2[0])

    h2, u, bg = _proj_c(h1, y1, route1, row1(norm_mix[1]), w_in_c[0].astype(BF16))
    wrh, wrl, br = _router_weights(w_group[1], b_group[1], w_expert[1], b_expert[1])
    conv_args = (conv_w[0].astype(F32), w_out_c[0].astype(BF16), row1(norm_ffn[1]), wrh, wrl, br, tstrict)
    h3_p, xn2_p, route2_p, cnt2_p = _conv_out(u, bg, h2, jnp.zeros((batch, 2, d), F32), *conv_args,
                                              0, batch, seq)
    h3_s, xn2_s, route2_s, cnt2_s = _conv_out(u, bg, h2, state_conv[0].astype(F32), *conv_args,
                                              n_p, nseq_s, tn)
    h3 = jnp.concatenate([h3_p, h3_s], axis=0)
    xn2 = jnp.concatenate([xn2_p, xn2_s], axis=0)
    route2 = jnp.concatenate([route2_p, route2_s], axis=0)
    y2 = _moe(xn2, route2, [cnt2_p, cnt2_s], [n_p, n_s], w1[1], w3[1], w2[1])
    out = _combine(h3, y2, route2)

    y_prompt = out[:n_p].reshape(batch, seq, d)
    y_sample = out[n_p:].reshape(nseq_s, tn, d)
    heads = lambda a, b_, t_: a.reshape(b_, t_, N_HEADS, HEAD_DIM)
    kf_p = heads(kf[:n_p], batch, seq)
    vf_p = heads(vf[:n_p], batch, seq)
    kf_s = heads(kf[n_p:], nseq_s, tn)
    vf_s = heads(vf[n_p:], nseq_s, tn)
    nk_p = kf_p[:, -keep:][None]
    nv_p = vf_p[:, -keep:][None]
    nk_s = jnp.concatenate([cache_a_k[l].astype(F32), kf_s], axis=1)[:, -la:][None]
    nv_s = jnp.concatenate([cache_a_v[l].astype(F32), vf_s], axis=1)[:, -la:][None]
    nh_p = jnp.swapaxes(st_p, -1, -2)[None]
    nh_s = jnp.swapaxes(st_s, -1, -2)[None]
    u_p = u[:n_p].reshape(batch, seq, d)
    u_s = u[n_p:].reshape(nseq_s, tn, d)
    nc_p = u_p[:, -2:][None]
    nc_s = jnp.concatenate([state_conv[0].astype(F32), u_s], axis=1)[:, -2:][None]
    return (y_prompt, y_sample, nk_p, nv_p, nk_s, nv_s, nh_p, nh_s, nc_p, nc_s)
```

```python
import functools

import jax
import jax.numpy as jnp
from jax import lax
from jax.experimental import pallas as pl
from jax.experimental.pallas import tpu as pltpu

F32 = jnp.float32
BF16 = jnp.bfloat16
I32 = jnp.int32

D_MODEL = 1024
CHUNK = 64
BAND_CHUNKS = 8
HEAD_DIM = 64
N_HEADS = 8
D_HALF = N_HEADS * HEAD_DIM
REL_CLIP = 128
HGRN_BLOCK = 16
N_GROUPS = 4
EXPERTS_PER_GROUP = 8
N_EXPERTS = N_GROUPS * EXPERTS_PER_GROUP
D_EXPERT = 256
RMS_EPS = 1e-6
NEG_INF = -1e30

LANES = 128
ROW_TILE = D_MODEL // LANES
TOKEN_TILE = 256
ATTN_TILE = 256
ATTN_WINDOW = ATTN_TILE + BAND_CHUNKS * CHUNK
MOE_TILE = 256
GATHER_UNROLL = 8
VMEM_LIMIT = 48 * 1024 * 1024


def _cparams(*sem):
    return pltpu.CompilerParams(dimension_semantics=sem, vmem_limit_bytes=VMEM_LIMIT)


def _const_spec(shape):
    nd = len(shape)
    return pl.BlockSpec(shape, lambda *_: (0,) * nd)


def _store_row_tiles(ref, val):
    rows = val.shape[0]
    for c in range(ROW_TILE):
        ref[pl.ds(c, rows, stride=ROW_TILE), :] = val[:, c * LANES:(c + 1) * LANES]


def _load_row_tiles(ref, rows):
    return jnp.concatenate([ref[pl.ds(c, rows, stride=ROW_TILE), :] for c in range(ROW_TILE)], axis=1)


def _rms(x, gain):
    ms = jnp.mean(x * x, axis=-1, keepdims=True)
    return (x * lax.rsqrt(ms + RMS_EPS)) * gain


def _split2(x):
    hi = x.astype(BF16)
    lo = (x - hi.astype(F32)).astype(BF16)
    return hi, lo


def _split3(x):
    p0 = x.astype(BF16)
    r = x - p0.astype(F32)
    p1 = r.astype(BF16)
    p2 = (r - p1.astype(F32)).astype(BF16)
    return p0, p1, p2


def _dot(a, b):
    return jnp.dot(a, b, preferred_element_type=F32)


def _dot_nt(a, b):
    return lax.dot_general(a, b, (((1,), (1,)), ((), ())), preferred_element_type=F32)


def _dot_tn(a, b):
    return lax.dot_general(a, b, (((0,), (0,)), ((), ())), preferred_element_type=F32)


def _head_mean_sq(v, bd):
    hi, lo = _split2(v * v)
    return _dot(hi, bd) + _dot(lo, bd)


def _proj_ab_body(x_ref, g_ref, w_ref, qg_ref, kg_ref, lb_ref, bd_ref,
                  q_o, kf_o, vf_o, k16_o, v16_o, lf_o, kb_o, vb_o, qb_o, gb_o):
    xb = _rms(x_ref[...], g_ref[...]).astype(BF16)
    bd = bd_ref[...]

    def seg(j):
        return _dot(xb, w_ref[:, j * D_HALF:(j + 1) * D_HALF])

    qa = seg(0)
    qn = qa * lax.rsqrt(_head_mean_sq(qa, bd) + RMS_EPS) * qg_ref[...]
    q_o[...] = (qn * (HEAD_DIM ** -0.5)).astype(BF16)
    ka = seg(1)
    kn = ka * lax.rsqrt(_head_mean_sq(ka, bd) + RMS_EPS) * kg_ref[...]
    kf_o[...] = kn
    k16_o[...] = kn.astype(BF16)
    va = seg(2)
    vf_o[...] = va
    v16_o[...] = va.astype(BF16)
    lb = lb_ref[...]
    f = lb + (1.0 - lb) * jax.nn.sigmoid(seg(3))
    lf_o[...] = jnp.log(f)
    kb_o[...] = 1.0 - f
    vb_o[...] = seg(4)
    qb_o[...] = jax.nn.silu(seg(5))
    gb_o[...] = seg(6)


def _proj_ab(x, gain, w16, qg, kg, lb, bd):
    n = x.shape[0]
    tm = TOKEN_TILE
    row = lambda w: pl.BlockSpec((tm, w), lambda i: (i, 0))
    outs = [jax.ShapeDtypeStruct((n, D_HALF), dt)
            for dt in (BF16, F32, F32, BF16, BF16, F32, F32, F32, F32, F32)]
    return pl.pallas_call(
        _proj_ab_body,
        out_shape=outs,
        grid=(n // tm,),
        in_specs=[row(D_MODEL), _const_spec((1, D_MODEL)), _const_spec(w16.shape),
                  _const_spec((1, D_HALF)), _const_spec((1, D_HALF)), _const_spec((1, D_HALF)),
                  _const_spec((D_HALF, D_HALF))],
        out_specs=[row(D_HALF)] * 10,
        compiler_params=_cparams("parallel"),
        name="proj_ab",
    )(x, gain, w16, qg, kg, lb, bd)


def _attn_prompt_body(q_ref, k0, k1, k2, v0, v1, v2, bias_ref, o_ref):
    i = pl.program_id(1)
    lim = jnp.maximum(2 - i, 0) * ATTN_TILE
    col = lax.broadcasted_iota(I32, (ATTN_TILE, ATTN_WINDOW), 1)
    dead = col < lim
    for h in range(N_HEADS):
        hs = slice(h * HEAD_DIM, (h + 1) * HEAD_DIM)
        qh = q_ref[:, hs]
        s = jnp.concatenate([_dot_nt(qh, k0[:, hs]), _dot_nt(qh, k1[:, hs]),
                             _dot_nt(qh, k2[:, hs])], axis=1)
        s = jnp.where(dead, NEG_INF, s + bias_ref[h])
        m = jnp.max(s, axis=-1, keepdims=True)
        p = jnp.exp(s - m)
        l = jnp.sum(p, axis=-1, keepdims=True)
        p16 = p.astype(BF16)
        o = (_dot(p16[:, 0:ATTN_TILE], v0[:, hs])
             + _dot(p16[:, ATTN_TILE:2 * ATTN_TILE], v1[:, hs])
             + _dot(p16[:, 2 * ATTN_TILE:], v2[:, hs]))
        o_ref[:, hs] = (o / l).astype(BF16)


def _attn_prompt(q16, k16, v16, bias_full, batch, seq):
    tiles = seq // ATTN_TILE
    qspec = pl.BlockSpec((ATTN_TILE, D_HALF), lambda b, i: (b * tiles + i, 0))

    def kv(back):
        return pl.BlockSpec((ATTN_TILE, D_HALF),
                            lambda b, i: (b * tiles + jnp.maximum(i - back, 0), 0))

    return pl.pallas_call(
        _attn_prompt_body,
        out_shape=jax.ShapeDtypeStruct((batch * seq, D_HALF), BF16),
        grid=(batch, tiles),
        in_specs=[qspec, kv(2), kv(1), kv(0), kv(2), kv(1), kv(0),
                  _const_spec(bias_full.shape)],
        out_specs=qspec,
        compiler_params=_cparams("parallel", "parallel"),
        name="attn_prompt",
    )(q16, k16, k16, k16, v16, v16, v16, bias_full)


def _attn_sample_body(q_ref, kc_ref, vc_ref, kn_ref, vn_ref, bc_ref, bn_ref, o_ref):
    kc = kc_ref[0].astype(BF16)
    vc = vc_ref[0].astype(BF16)
    for h in range(N_HEADS):
        hs = slice(h * HEAD_DIM, (h + 1) * HEAD_DIM)
        qh = q_ref[:, hs]
        sc = _dot_nt(qh, kc[:, hs]) + bc_ref[h]
        sn = _dot_nt(qh, kn_ref[:, hs]) + bn_ref[h]
        m = jnp.maximum(jnp.max(sc, axis=-1, keepdims=True), jnp.max(sn, axis=-1, keepdims=True))
        pc = jnp.exp(sc - m)
        pn = jnp.exp(sn - m)
        l = jnp.sum(pc, axis=-1, keepdims=True) + jnp.sum(pn, axis=-1, keepdims=True)
        o = _dot(pc.astype(BF16), vc[:, hs]) + _dot(pn.astype(BF16), vn_ref[:, hs])
        o_ref[:, hs] = (o / l).astype(BF16)


def _attn_sample(q16, k16, v16, cache_k, cache_v, bias_c, bias_n, row0, nseq, tn):
    blk0 = row0 // tn
    la = cache_k.shape[1]
    new = pl.BlockSpec((tn, D_HALF), lambda b: (blk0 + b, 0))
    cache = pl.BlockSpec((1, la, D_HALF), lambda b: (b, 0, 0))
    return pl.pallas_call(
        _attn_sample_body,
        out_shape=jax.ShapeDtypeStruct((nseq * tn, D_HALF), BF16),
        grid=(nseq,),
        in_specs=[new, cache, cache, new, new, _const_spec(bias_c.shape), _const_spec(bias_n.shape)],
        out_specs=pl.BlockSpec((tn, D_HALF), lambda b: (b, 0)),
        compiler_params=_cparams("parallel"),
        name="attn_sample",
    )(q16, cache_k, cache_v, k16, v16, bias_c, bias_n)


def _hgrn_body(q_ref, k_ref, v_ref, lf_ref, s0_ref, tblk_ref, tfull_ref, o_ref, sout_ref, st_scr,
               *, chunk, nchunks):
    nblk = chunk // HGRN_BLOCK

    @pl.when(pl.program_id(1) == 0)
    def _():
        st_scr[...] = s0_ref[0]

    row = lax.broadcasted_iota(I32, (chunk, D_HALF), 0)
    r2 = lax.broadcasted_iota(I32, (chunk, chunk), 0)
    c2 = lax.broadcasted_iota(I32, (chunk, chunk), 1)
    same_blk_causal = jnp.logical_and(r2 // HGRN_BLOCK == c2 // HGRN_BLOCK, c2 <= r2)

    def one_chunk(c, carry):
        sl = pl.ds(pl.multiple_of(c * chunk, chunk), chunk)
        q = q_ref[sl, :]
        k = k_ref[sl, :]
        v16 = v_ref[sl, :].astype(BF16)
        l0, l1, l2 = _split3(lf_ref[sl, :])
        tb = tblk_ref[...]
        tf = tfull_ref[...]
        b_in = _dot(tb, l0) + _dot(tb, l1) + _dot(tb, l2)
        b_ch = _dot(tf, l0) + _dot(tf, l1) + _dot(tf, l2)
        ld = (q * jnp.exp(b_in)).astype(BF16)
        rd = (k * jnp.exp(-b_in)).astype(BF16)
        lj, rj = [], []
        for j in range(nblk - 1):
            e_j = b_ch[(j + 1) * HGRN_BLOCK - 1:(j + 1) * HGRN_BLOCK, :]
            later = row >= (j + 1) * HGRN_BLOCK
            inside = jnp.logical_and(row >= j * HGRN_BLOCK, row < (j + 1) * HGRN_BLOCK)
            lj.append(jnp.where(later, q * jnp.exp(jnp.minimum(b_ch - e_j, 0.0)), 0.0).astype(BF16))
            rj.append(jnp.where(inside, k * jnp.exp(jnp.minimum(e_j - b_ch, 0.0)), 0.0).astype(BF16))
        e_end = b_ch[chunk - 1:chunk, :]
        qc = (q * jnp.exp(b_ch)).astype(BF16)
        kc = (k * jnp.exp(e_end - b_ch)).astype(BF16)
        dec = jnp.exp(e_end)
        for h in range(N_HEADS):
            hs = slice(h * HEAD_DIM, (h + 1) * HEAD_DIM)
            sc = jnp.where(same_blk_causal, _dot_nt(ld[:, hs], rd[:, hs]), 0.0)
            for j in range(nblk - 1):
                sc = sc + _dot_nt(lj[j][:, hs], rj[j][:, hs])
            st = st_scr[h]
            o = _dot(sc.astype(BF16), v16[:, hs]) + _dot_nt(qc[:, hs], st.astype(BF16))
            o_ref[sl, hs] = o
            st_scr[h] = st * dec[:, hs] + _dot_tn(v16[:, hs], kc[:, hs])
        return carry

    lax.fori_loop(0, nchunks, one_chunk, 0)
    sout_ref[0] = st_scr[...]


def _hgrn(qb, kb, vb, lf, s0t, row0, nseq, seq, chunk, nchunks):
    tt = chunk * nchunks
    steps = seq // tt
    blk0 = row0 // tt
    t = jnp.arange(chunk)
    lower = t[None, :] <= t[:, None]
    tfull = lower.astype(BF16)
    tblk = jnp.logical_and(lower, (t[None, :] // HGRN_BLOCK) == (t[:, None] // HGRN_BLOCK)).astype(BF16)
    tok = pl.BlockSpec((tt, D_HALF), lambda b, j: (blk0 + b * steps + j, 0))
    state = pl.BlockSpec((1, N_HEADS, HEAD_DIM, HEAD_DIM), lambda b, j: (b, 0, 0, 0))
    return pl.pallas_call(
        functools.partial(_hgrn_body, chunk=chunk, nchunks=nchunks),
        out_shape=[jax.ShapeDtypeStruct((nseq * seq, D_HALF), F32),
                   jax.ShapeDtypeStruct(s0t.shape, F32)],
        grid=(nseq, steps),
        in_specs=[tok, tok, tok, tok, state, _const_spec((chunk, chunk)), _const_spec((chunk, chunk))],
        out_specs=[pl.BlockSpec((tt, D_HALF), lambda b, j: (b * steps + j, 0)), state],
        scratch_shapes=[pltpu.VMEM((N_HEADS, HEAD_DIM, HEAD_DIM), F32)],
        compiler_params=_cparams("parallel", "arbitrary"),
        name="hgrn",
    )(qb, kb, vb, lf, s0t, tblk, tfull)


def _route(xn, wr_hi, wr_lo, br, tstrict, cnt_scr, route_o, cnt_o):
    rows = xn.shape[0]
    x_hi, x_lo = _split2(xn)
    logits = _dot(x_hi, wr_hi) + _dot(x_lo, wr_hi) + _dot(x_hi, wr_lo) + br
    lane = lax.broadcasted_iota(I32, (rows, LANES), 1)
    lane_f = lane.astype(F32)

    def first_argmax(vals):
        m = jnp.max(vals, axis=-1, keepdims=True)
        idx = jnp.min(jnp.where(vals == m, lane_f, float(LANES)), axis=-1, keepdims=True)
        return m, idx.astype(I32)

    is_grp = jnp.logical_and(lane >= N_EXPERTS, lane < N_EXPERTS + N_GROUPS)
    gl = jnp.where(is_grp, logits, -jnp.inf)
    gmax, gidx = first_argmax(gl)
    p_grp = 1.0 / jnp.sum(jnp.exp(gl - gmax), axis=-1, keepdims=True)
    grp = gidx - N_EXPERTS
    in_grp = jnp.logical_and(lane < N_EXPERTS, lane // EXPERTS_PER_GROUP == grp)
    el = jnp.where(in_grp, logits, -jnp.inf)
    v1, e1 = first_argmax(el)
    v2, e2 = first_argmax(jnp.where(lane == e1, -jnp.inf, el))
    t2 = jnp.exp(v2 - v1)
    den = 1.0 + t2
    g1 = (1.0 / den) * p_grp
    g2 = (t2 / den) * p_grp

    oh1 = lane == e1
    oh2 = lane == e2
    oh = jnp.where(jnp.logical_or(oh1, oh2), 1.0, 0.0)
    before = _dot(tstrict, oh.astype(BF16)) + cnt_scr[...]
    rank1 = jnp.sum(jnp.where(oh1, before, 0.0), axis=-1, keepdims=True)
    rank2 = jnp.sum(jnp.where(oh2, before, 0.0), axis=-1, keepdims=True)
    cnt_scr[...] = cnt_scr[...] + jnp.sum(oh, axis=0, keepdims=True)
    cnt_o[...] = cnt_scr[...]

    slab = jnp.zeros((rows, LANES), F32)
    for idx, val in enumerate((e1.astype(F32), e2.astype(F32), g1, g2, rank1, rank2)):
        slab = jnp.where(lane == idx, val, slab)
    route_o[...] = slab


def _out_ab_body(h_ref, oa_ref, ob_ref, gb_ref, og_ref, bd_ref, w_ref, gf_ref,
                 wrh_ref, wrl_ref, br_ref, ts_ref, h_o, xn_o, route_o, cnt_o, cnt_scr):
    @pl.when(pl.program_id(0) == 0)
    def _():
        cnt_scr[...] = jnp.zeros_like(cnt_scr)

    ob = ob_ref[...]
    obn = ob * lax.rsqrt(_head_mean_sq(ob, bd_ref[...]) + RMS_EPS) * og_ref[...]
    obg = (obn * jax.nn.silu(gb_ref[...])).astype(BF16)
    mix = _dot(oa_ref[...], w_ref[0:D_HALF, :]) + _dot(obg, w_ref[D_HALF:, :])
    h1 = h_ref[...] + mix
    h_o[...] = h1
    xn = _rms(h1, gf_ref[...])
    _store_row_tiles(xn_o, xn)
    _route(xn, wrh_ref[...], wrl_ref[...], br_ref[...], ts_ref[...], cnt_scr, route_o, cnt_o)


def _out_ab(h, oa16, ob, gb, og, bd, w16, gf, wrh, wrl, br, tstrict):
    n = h.shape[0]
    tm = TOKEN_TILE
    row = lambda w: pl.BlockSpec((tm, w), lambda i: (i, 0))
    return pl.pallas_call(
        _out_ab_body,
        out_shape=[jax.ShapeDtypeStruct((n, D_MODEL), F32), jax.ShapeDtypeStruct((n * ROW_TILE, LANES), F32),
                   jax.ShapeDtypeStruct((n, LANES), F32), jax.ShapeDtypeStruct((1, LANES), F32)],
        grid=(n // tm,),
        in_specs=[row(D_MODEL), row(D_HALF), row(D_HALF), row(D_HALF), _const_spec((1, D_HALF)),
                  _const_spec((D_HALF, D_HALF)), _const_spec((D_MODEL, D_MODEL)),
                  _const_spec((1, D_MODEL)), _const_spec((D_MODEL, LANES)),
                  _const_spec((D_MODEL, LANES)), _const_spec((1, LANES)), _const_spec((tm, tm))],
        out_specs=[row(D_MODEL), pl.BlockSpec((tm * ROW_TILE, LANES), lambda i: (i, 0)), row(LANES),
                   _const_spec((1, LANES))],
        scratch_shapes=[pltpu.VMEM((1, LANES), F32)],
        compiler_params=_cparams("arbitrary"),
        name="out_ab",
    )(h, oa16, ob, gb, og, bd, w16, gf, wrh, wrl, br, tstrict)


def _row_copy(src, src_row, dst, dst_row, sem):
    s0 = pl.multiple_of(src_row * ROW_TILE, ROW_TILE)
    d0 = pl.multiple_of(dst_row * ROW_TILE, ROW_TILE)
    return pltpu.make_async_copy(src.at[pl.ds(s0, ROW_TILE)], dst.at[pl.ds(d0, ROW_TILE)], sem)


def _gather_rows_body(idx_ref, nt_ref, src_hbm, o_ref, sem, *, tile):
    i = pl.program_id(0)
    base = i * tile

    @pl.when(i < nt_ref[0])
    def _():
        def issue(r, carry):
            _row_copy(src_hbm, idx_ref[base + r], o_ref, r, sem).start()
            return carry

        def drain(r, carry):
            _row_copy(src_hbm, 0, o_ref, 0, sem).wait()
            return carry

        lax.fori_loop(0, tile, issue, 0, unroll=GATHER_UNROLL)
        lax.fori_loop(0, tile, drain, 0, unroll=GATHER_UNROLL)

    @pl.when(i >= nt_ref[0])
    def _():
        o_ref[...] = jnp.zeros_like(o_ref)


def _gather_rows(idx, n_tiles, src, n_out):
    tm = MOE_TILE
    return pl.pallas_call(
        functools.partial(_gather_rows_body, tile=tm),
        out_shape=jax.ShapeDtypeStruct((n_out * ROW_TILE, LANES), F32),
        grid_spec=pltpu.PrefetchScalarGridSpec(
            num_scalar_prefetch=2,
            grid=(n_out // tm,),
            in_specs=[pl.BlockSpec(memory_space=pl.ANY)],
            out_specs=pl.BlockSpec((tm * ROW_TILE, LANES), lambda i, idx, nt: (i, 0)),
            scratch_shapes=[pltpu.SemaphoreType.DMA(())],
        ),
        compiler_params=pltpu.CompilerParams(dimension_semantics=("arbitrary",)),
        name="moe_gather",
    )(idx, n_tiles, src)


def _experts_body(te_ref, nt_ref, x_ref, w1_ref, w3_ref, w2_ref, y_ref, w1_s, w3_s, w2_s):
    i = pl.program_id(0)
    prev = te_ref[jnp.maximum(i - 1, 0)]
    fresh = jnp.logical_or(i == 0, te_ref[i] != prev)

    @pl.when(fresh)
    def _():
        w1_s[...] = w1_ref[0].astype(BF16)
        w3_s[...] = w3_ref[0].astype(BF16)
        w2_s[...] = w2_ref[0].astype(BF16)

    @pl.when(i < nt_ref[0])
    def _():
        x = _load_row_tiles(x_ref, MOE_TILE).astype(BF16)
        a = _dot(x, w1_s[...])
        b = _dot(x, w3_s[...])
        _store_row_tiles(y_ref, _dot((jax.nn.silu(a) * b).astype(BF16), w2_s[...]))

    @pl.when(i >= nt_ref[0])
    def _():
        y_ref[...] = jnp.zeros_like(y_ref)


def _experts(tile_expert, n_tiles, xs, w1, w3, w2):
    rows = xs.shape[0] // ROW_TILE
    tm = MOE_TILE
    return pl.pallas_call(
        _experts_body,
        out_shape=jax.ShapeDtypeStruct((rows * ROW_TILE, LANES), F32),
        grid_spec=pltpu.PrefetchScalarGridSpec(
            num_scalar_prefetch=2,
            grid=(rows // tm,),
            in_specs=[pl.BlockSpec((tm * ROW_TILE, LANES), lambda i, te, nt: (i, 0)),
                      pl.BlockSpec((1, D_MODEL, D_EXPERT), lambda i, te, nt: (te[i], 0, 0)),
                      pl.BlockSpec((1, D_MODEL, D_EXPERT), lambda i, te, nt: (te[i], 0, 0)),
                      pl.BlockSpec((1, D_EXPERT, D_MODEL), lambda i, te, nt: (te[i], 0, 0))],
            out_specs=pl.BlockSpec((tm * ROW_TILE, LANES), lambda i, te, nt: (i, 0)),
            scratch_shapes=[pltpu.VMEM((D_MODEL, D_EXPERT), BF16), pltpu.VMEM((D_MODEL, D_EXPERT), BF16),
                            pltpu.VMEM((D_EXPERT, D_MODEL), BF16)],
        ),
        compiler_params=_cparams("arbitrary"),
        name="moe_experts",
    )(tile_expert, n_tiles, xs, w1, w3, w2)


def _moe_plan(route, counts_per_segment, seg_rows):
    e = route[:, 0:2].astype(I32)
    rank = route[:, 4:6].astype(I32)
    seg_counts = [c[0, :N_EXPERTS].astype(I32) for c in counts_per_segment]
    base = jnp.zeros((N_EXPERTS,), I32)
    seg_base = []
    for c in seg_counts:
        seg_base.append(base)
        base = base + c
    counts = base
    row_base = jnp.concatenate([jnp.broadcast_to(b[None], (r, N_EXPERTS))
                                for b, r in zip(seg_base, seg_rows)], axis=0)
    padded = ((counts + MOE_TILE - 1) // MOE_TILE) * MOE_TILE
    ends = jnp.cumsum(padded)
    offs = ends - padded
    pos = jnp.take(offs, e) + rank + jnp.take_along_axis(row_base, e, axis=1)
    pos_flat = jnp.concatenate([pos[:, 0], pos[:, 1]])
    n = route.shape[0]
    max_tiles = (2 * n + N_EXPERTS * (MOE_TILE - 1) + MOE_TILE - 1) // MOE_TILE
    tile_start = jnp.arange(max_tiles, dtype=I32) * MOE_TILE
    tile_expert = jnp.minimum(jnp.sum((tile_start[:, None] >= ends[None, :]).astype(I32), axis=1),
                              N_EXPERTS - 1)
    n_tiles = (ends[-1] // MOE_TILE).reshape(1)
    n_rows = max_tiles * MOE_TILE
    row_src = jnp.zeros((n_rows,), I32).at[pos_flat].set(jnp.arange(2 * n, dtype=I32) % n)
    return pos_flat, row_src, tile_expert, n_tiles, n_rows


def _moe(xn, route, counts_per_segment, seg_rows, w1, w3, w2):
    n = xn.shape[0] // ROW_TILE
    pos, row_src, tile_expert, n_tiles, n_rows = _moe_plan(route, counts_per_segment, seg_rows)
    xs = _gather_rows(row_src, n_tiles, xn, n_rows)
    ys = _experts(tile_expert, n_tiles, xs, w1, w3, w2)
    all_tiles = jnp.full((1,), 2 * n // MOE_TILE, I32)
    return _gather_rows(pos, all_tiles, ys, 2 * n).reshape(2, n * ROW_TILE, LANES)


def _add_moe(h, ya_ref, yb_ref, route):
    rows = h.shape[0]
    ya = _load_row_tiles(ya_ref.at[0], rows)
    yb = _load_row_tiles(yb_ref.at[0], rows)
    return h + (route[:, 2:3] * ya + route[:, 3:4] * yb)


def _moe_out_spec(k, tm):
    return pl.BlockSpec((1, tm * ROW_TILE, LANES), lambda i: (k, i, 0))


def _proj_c_body(h_ref, ya_ref, yb_ref, r_ref, g_ref, w_ref, h_o, u_o, bg_o):
    h2 = _add_moe(h_ref[...], ya_ref, yb_ref, r_ref[...])
    h_o[...] = h2
    xb = _rms(h2, g_ref[...]).astype(BF16)
    bg_o[...] = _dot(xb, w_ref[:, 0:D_MODEL])
    u_o[...] = _dot(xb, w_ref[:, D_MODEL:2 * D_MODEL]) * _dot(xb, w_ref[:, 2 * D_MODEL:])


def _proj_c(h, y2, route, gain, w16):
    n = h.shape[0]
    tm = TOKEN_TILE
    row = lambda w: pl.BlockSpec((tm, w), lambda i: (i, 0))
    ysp = lambda k: _moe_out_spec(k, tm)
    return pl.pallas_call(
        _proj_c_body,
        out_shape=[jax.ShapeDtypeStruct((n, D_MODEL), F32)] * 3,
        grid=(n // tm,),
        in_specs=[row(D_MODEL), ysp(0), ysp(1), row(LANES), _const_spec((1, D_MODEL)),
                  _const_spec(w16.shape)],
        out_specs=[row(D_MODEL)] * 3,
        compiler_params=_cparams("parallel"),
        name="proj_c",
    )(h, y2, y2, route, gain, w16)


def _conv_out_body(u_ref, up_ref, st_ref, bg_ref, h_ref, cw_ref, w_ref, gf_ref,
                   wrh_ref, wrl_ref, br_ref, ts_ref, h_o, xn_o, route_o, cnt_o, cnt_scr,
                   *, tile, seq):
    i = pl.program_id(0)

    @pl.when(i == 0)
    def _():
        cnt_scr[...] = jnp.zeros_like(cnt_scr)

    u = u_ref[...]
    rowi = lax.broadcasted_iota(I32, (tile, D_MODEL), 0)
    if seq >= tile:
        at_start = (i * tile) % seq == 0
        st = st_ref[0]
        prev = up_ref[...]
        m2 = jnp.where(at_start, st[0:1, :], prev[6:7, :])
        m1 = jnp.where(at_start, st[1:2, :], prev[7:8, :])
        pos = rowi
    else:
        per = tile // seq
        st = st_ref[...]
        m2 = jnp.broadcast_to(st[:, 0:1, :], (per, seq, D_MODEL)).reshape(tile, D_MODEL)
        m1 = jnp.broadcast_to(st[:, 1:2, :], (per, seq, D_MODEL)).reshape(tile, D_MODEL)
        pos = rowi % seq
    u1 = jnp.where(pos == 0, m1, pltpu.roll(u, 1, axis=0))
    u2 = jnp.where(pos == 0, m2, jnp.where(pos == 1, m1, pltpu.roll(u, 2, axis=0)))
    cw = cw_ref[...]
    conv = u2 * cw[0:1, :] + u1 * cw[1:2, :] + u * cw[2:3, :]
    mix = _dot((bg_ref[...] * conv).astype(BF16), w_ref[...])
    h3 = h_ref[...] + mix
    h_o[...] = h3
    xn = _rms(h3, gf_ref[...])
    _store_row_tiles(xn_o, xn)
    _route(xn, wrh_ref[...], wrl_ref[...], br_ref[...], ts_ref[...], cnt_scr, route_o, cnt_o)


def _conv_out(u, bg, h, state, cw, w16, gf, wrh, wrl, br, tstrict, row0, nseq, seq):
    tm = TOKEN_TILE
    n_g = nseq * seq
    blk0 = row0 // tm
    row = lambda w: pl.BlockSpec((tm, w), lambda i: (blk0 + i, 0))
    orow = lambda w: pl.BlockSpec((tm, w), lambda i: (i, 0))
    prev = pl.BlockSpec((8, D_MODEL), lambda i: (jnp.maximum((blk0 + i) * (tm // 8) - 1, 0), 0))
    if seq >= tm:
        st_spec = pl.BlockSpec((1, 2, D_MODEL), lambda i: ((i * tm) // seq, 0, 0))
    else:
        st_spec = pl.BlockSpec((tm // seq, 2, D_MODEL), lambda i: (i, 0, 0))
    return pl.pallas_call(
        functools.partial(_conv_out_body, tile=tm, seq=seq),
        out_shape=[jax.ShapeDtypeStruct((n_g, D_MODEL), F32),
                   jax.ShapeDtypeStruct((n_g * ROW_TILE, LANES), F32),
                   jax.ShapeDtypeStruct((n_g, LANES), F32), jax.ShapeDtypeStruct((1, LANES), F32)],
        grid=(n_g // tm,),
        in_specs=[row(D_MODEL), prev, st_spec, row(D_MODEL), row(D_MODEL), _const_spec((3, D_MODEL)),
                  _const_spec((D_MODEL, D_MODEL)), _const_spec((1, D_MODEL)),
                  _const_spec((D_MODEL, LANES)), _const_spec((D_MODEL, LANES)),
                  _const_spec((1, LANES)), _const_spec((tm, tm))],
        out_specs=[orow(D_MODEL), pl.BlockSpec((tm * ROW_TILE, LANES), lambda i: (i, 0)), orow(LANES),
                   _const_spec((1, LANES))],
        scratch_shapes=[pltpu.VMEM((1, LANES), F32)],
        compiler_params=_cparams("arbitrary"),
        name="conv_out",
    )(u, u, state, bg, h, cw, w16, gf, wrh, wrl, br, tstrict)


def _combine_body(h_ref, ya_ref, yb_ref, r_ref, o_ref):
    o_ref[...] = _add_moe(h_ref[...], ya_ref, yb_ref, r_ref[...])


def _combine(h, y2, route):
    n = h.shape[0]
    tm = TOKEN_TILE
    row = lambda w: pl.BlockSpec((tm, w), lambda i: (i, 0))
    ysp = lambda k: _moe_out_spec(k, tm)
    return pl.pallas_call(
        _combine_body,
        out_shape=jax.ShapeDtypeStruct((n, D_MODEL), F32),
        grid=(n // tm,),
        in_specs=[row(D_MODEL), ysp(0), ysp(1), row(LANES)],
        out_specs=row(D_MODEL),
        compiler_params=_cparams("parallel"),
        name="moe_combine",
    )(h, y2, y2, route)


def _rel_bias_toeplitz(rel_bias, rows, cols, lead):
    period = rows + cols
    k = jnp.arange(period)
    d = jnp.where(k < cols, k, k - period)
    idx = jnp.clip(lead - d, -REL_CLIP, REL_CLIP) + REL_CLIP
    v = jnp.take(rel_bias.astype(F32), idx, axis=1)
    heads = v.shape[0]
    skew = jnp.tile(v, (1, rows))[:, :rows * (period - 1)].reshape(heads, rows, period - 1)
    return skew[:, :, :cols]


def _bias_prompt(rel_bias):
    r = jnp.arange(ATTN_TILE)[:, None]
    c = jnp.arange(ATTN_WINDOW)[None, :]
    j = c // CHUNK - r // CHUNK
    band = jnp.logical_and(j >= 0, j <= BAND_CHUNKS)
    bias = _rel_bias_toeplitz(rel_bias, ATTN_TILE, ATTN_WINDOW, BAND_CHUNKS * CHUNK)
    return jnp.where(band[None], bias, NEG_INF)


def _bias_sample(rel_bias, la, tn):
    bias = _rel_bias_toeplitz(rel_bias, tn, la + tn, la)
    return bias[:, :, :la], bias[:, :, la:]


def _router_weights(w_group, b_group, w_expert, b_expert):
    pad = LANES - N_EXPERTS - N_GROUPS
    w = jnp.concatenate([w_expert, w_group, jnp.zeros((D_MODEL, pad), F32)], axis=1)
    b = jnp.concatenate([b_expert, b_group, jnp.zeros((pad,), F32)])[None, :].astype(F32)
    hi, lo = _split2(w.astype(F32))
    return hi, lo, b


def kernel(x_prompt, x_sample, cache_a_k, cache_a_v, state_hgrn, state_conv, norm_mix, norm_ffn,
           w_in_ab, w_out_ab, q_norm, k_norm, rel_bias, hgrn_lb_logits, hgrn_out_norm, w_in_c,
           conv_w, w_out_c, w_group, b_group, w_expert, b_expert, w1, w3, w2):
    batch, seq, d = x_prompt.shape
    nseq_s, tn, _ = x_sample.shape
    la = cache_a_k.shape[2]
    n_p = batch * seq
    n_s = nseq_s * tn
    n = n_p + n_s
    keep = min(BAND_CHUNKS * CHUNK, seq)

    x = jnp.concatenate([x_prompt.reshape(n_p, d), x_sample.reshape(n_s, d)], axis=0)
    lb_all = jnp.cumsum(jax.nn.softmax(hgrn_lb_logits.astype(F32), axis=0), axis=0)
    head_avg = jnp.kron(jnp.eye(N_HEADS, dtype=F32),
                        jnp.full((HEAD_DIM, HEAD_DIM), 1.0 / HEAD_DIM, F32)).astype(BF16)
    t = jnp.arange(TOKEN_TILE)
    tstrict = (t[None, :] < t[:, None]).astype(BF16)
    row1 = lambda v: v.astype(F32).reshape(1, -1)
    tile8 = lambda v: jnp.tile(v.astype(F32), N_HEADS).reshape(1, -1)

    l = 0
    (q16, kf, vf, k16, v16, lf, kb, vb, qb, gb) = _proj_ab(
        x, row1(norm_mix[0]), w_in_ab[l].astype(BF16), tile8(q_norm[l]), tile8(k_norm[l]),
        row1(lb_all[l]), head_avg)

    oa_p = _attn_prompt(q16, k16, v16, _bias_prompt(rel_bias[l]), batch, seq)
    bias_c, bias_n = _bias_sample(rel_bias[l], la, tn)
    oa_s = _attn_sample(q16, k16, v16, cache_a_k[l].reshape(nseq_s, la, D_HALF),
                        cache_a_v[l].reshape(nseq_s, la, D_HALF), bias_c, bias_n, n_p, nseq_s, tn)
    oa = jnp.concatenate([oa_p, oa_s], axis=0)

    zeros_state = jnp.zeros((batch, N_HEADS, HEAD_DIM, HEAD_DIM), F32)
    ob_p, st_p = _hgrn(qb, kb, vb, lf, zeros_state, 0, batch, seq, 64, 4)
    ob_s, st_s = _hgrn(qb, kb, vb, lf, jnp.swapaxes(state_hgrn[l].astype(F32), -1, -2),
                       n_p, nseq_s, tn, tn, 1)
    ob = jnp.concatenate([ob_p, ob_s], axis=0)

    wrh, wrl, br = _router_weights(w_group[0], b_group[0], w_expert[0], b_expert[0])
    h1, xn1, route1, cnt1 = _out_ab(x, oa, ob, gb, row1(hgrn_out_norm[l]), head_avg,
                                    w_out_ab[l].astype(BF16), row1(norm_ffn[0]), wrh, wrl, br, tstrict)
    y1 = _moe(xn1, route1, [cnt1], [n], w1[0], w3[0], w2[0])

    h2, u, bg = _proj_c(h1, y1, route1, row1(norm_mix[1]), w_in_c[0].astype(BF16))
    wrh, wrl, br = _router_weights(w_group[1], b_group[1], w_expert[1], b_expert[1])
    conv_args = (conv_w[0].astype(F32), w_out_c[0].astype(BF16), row1(norm_ffn[1]), wrh, wrl, br, tstrict)
    h3_p, xn2_p, route2_p, cnt2_p = _conv_out(u, bg, h2, jnp.zeros((batch, 2, d), F32), *conv_args,
                                              0, batch, seq)
    h3_s, xn2_s, route2_s, cnt2_s = _conv_out(u, bg, h2, state_conv[0].astype(F32), *conv_args,
                                              n_p, nseq_s, tn)
    h3 = jnp.concatenate([h3_p, h3_s], axis=0)
    xn2 = jnp.concatenate([xn2_p, xn2_s], axis=0)
    route2 = jnp.concatenate([route2_p, route2_s], axis=0)
    y2 = _moe(xn2, route2, [cnt2_p, cnt2_s], [n_p, n_s], w1[1], w3[1], w2[1])
    out = _combine(h3, y2, route2)

    y_prompt = out[:n_p].reshape(batch, seq, d)
    y_sample = out[n_p:].reshape(nseq_s, tn, d)
    heads = lambda a, b_, t_: a.reshape(b_, t_, N_HEADS, HEAD_DIM)
    kf_p = heads(kf[:n_p], batch, seq)
    vf_p = heads(vf[:n_p], batch, seq)
    kf_s = heads(kf[n_p:], nseq_s, tn)
    vf_s = heads(vf[n_p:], nseq_s, tn)
    nk_p = kf_p[:, -keep:][None]
    nv_p = vf_p[:, -keep:][None]
    nk_s = jnp.concatenate([cache_a_k[l].astype(F32), kf_s], axis=1)[:, -la:][None]
    nv_s = jnp.concatenate([cache_a_v[l].astype(F32), vf_s], axis=1)[:, -la:][None]
    nh_p = jnp.swapaxes(st_p, -1, -2)[None]
    nh_s = jnp.swapaxes(st_s, -1, -2)[None]
    u_p = u[:n_p].reshape(batch, seq, d)
    u_s = u[n_p:].reshape(nseq_s, tn, d)
    nc_p = u_p[:, -2:][None]
    nc_s = jnp.concatenate([state_conv[0].astype(F32), u_s], axis=1)[:, -2:][None]
    return (y_prompt, y_sample, nk_p, nv_p, nk_s, nv_s, nh_p, nh_s, nc_p, nc_s)
```

```python
import functools

import jax
import jax.numpy as jnp
from jax import lax
from jax.experimental import pallas as pl
from jax.experimental.pallas import tpu as pltpu

F32 = jnp.float32
BF16 = jnp.bfloat16
I32 = jnp.int32

D_MODEL = 1024
CHUNK = 64
BAND_CHUNKS = 8
HEAD_DIM = 64
N_HEADS = 8
D_HALF = N_HEADS * HEAD_DIM
REL_CLIP = 128
HGRN_BLOCK = 16
N_GROUPS = 4
EXPERTS_PER_GROUP = 8
N_EXPERTS = N_GROUPS * EXPERTS_PER_GROUP
D_EXPERT = 256
RMS_EPS = 1e-6
NEG_INF = -1e30

LANES = 128
ROW_TILE = D_MODEL // LANES
TOKEN_TILE = 256
ATTN_TILE = 256
ATTN_WINDOW = ATTN_TILE + BAND_CHUNKS * CHUNK
MOE_TILE = 256
GATHER_UNROLL = 8
VMEM_LIMIT = 48 * 1024 * 1024


def _cparams(*sem):
    return pltpu.CompilerParams(dimension_semantics=sem, vmem_limit_bytes=VMEM_LIMIT)


def _const_spec(shape):
    nd = len(shape)
    return pl.BlockSpec(shape, lambda *_: (0,) * nd)


def _store_row_tiles(ref, val):
    rows = val.shape[0]
    for c in range(ROW_TILE):
        ref[pl.ds(c, rows, stride=ROW_TILE), :] = val[:, c * LANES:(c + 1) * LANES]


def _load_row_tiles(ref, rows):
    return jnp.concatenate([ref[pl.ds(c, rows, stride=ROW_TILE), :] for c in range(ROW_TILE)], axis=1)


def _rms(x, gain):
    ms = jnp.mean(x * x, axis=-1, keepdims=True)
    return (x * lax.rsqrt(ms + RMS_EPS)) * gain


def _split2(x):
    hi = x.astype(BF16)
    lo = (x - hi.astype(F32)).astype(BF16)
    return hi, lo


def _split3(x):
    p0 = x.astype(BF16)
    r = x - p0.astype(F32)
    p1 = r.astype(BF16)
    p2 = (r - p1.astype(F32)).astype(BF16)
    return p0, p1, p2


def _dot(a, b):
    return jnp.dot(a, b, preferred_element_type=F32)


def _dot_nt(a, b):
    return lax.dot_general(a, b, (((1,), (1,)), ((), ())), preferred_element_type=F32)


def _dot_tn(a, b):
    return lax.dot_general(a, b, (((0,), (0,)), ((), ())), preferred_element_type=F32)


def _head_mean_sq(v, bd):
    hi, lo = _split2(v * v)
    return _dot(hi, bd) + _dot(lo, bd)


def _group_specs(width, tiles_p, tm=TOKEN_TILE):
    return [pl.BlockSpec((tm, width), lambda i: (jnp.minimum(i, tiles_p - 1), 0)),
            pl.BlockSpec((tm, width), lambda i: (jnp.maximum(i - tiles_p, 0), 0))]


def _group_pick(p_ref, s_ref, tiles_p):
    return jnp.where(pl.program_id(0) < tiles_p, p_ref[...], s_ref[...])


def _proj_ab_body(xp_ref, xs_ref, g_ref, w_ref, qg_ref, kg_ref, lb_ref, bd_ref,
                  q_o, kf_o, vf_o, k16_o, v16_o, lf_o, kb_o, vb_o, qb_o, gb_o, *, tiles_p):
    xb = _rms(_group_pick(xp_ref, xs_ref, tiles_p), g_ref[...]).astype(BF16)
    bd = bd_ref[...]

    def seg(j):
        return _dot(xb, w_ref[:, j * D_HALF:(j + 1) * D_HALF])

    qa = seg(0)
    qn = qa * lax.rsqrt(_head_mean_sq(qa, bd) + RMS_EPS) * qg_ref[...]
    q_o[...] = (qn * (HEAD_DIM ** -0.5)).astype(BF16)
    ka = seg(1)
    kn = ka * lax.rsqrt(_head_mean_sq(ka, bd) + RMS_EPS) * kg_ref[...]
    kf_o[...] = kn
    k16_o[...] = kn.astype(BF16)
    va = seg(2)
    vf_o[...] = va
    v16_o[...] = va.astype(BF16)
    lb = lb_ref[...]
    f = lb + (1.0 - lb) * jax.nn.sigmoid(seg(3))
    lf_o[...] = jnp.log(f)
    kb_o[...] = 1.0 - f
    vb_o[...] = seg(4)
    qb_o[...] = jax.nn.silu(seg(5))
    gb_o[...] = seg(6)


def _proj_ab(x_p, x_s, gain, w16, qg, kg, lb, bd, seq, keep):
    tm = TOKEN_TILE
    n_p, n_s = x_p.shape[0], x_s.shape[0]
    n = n_p + n_s
    tiles_p, tiles_seq, tiles_keep = n_p // tm, seq // tm, keep // tm
    kept_tiles = (n_p // seq) * tiles_keep + n_s // tm

    def keep_map(i):
        b, j = i // tiles_seq, i % tiles_seq
        prompt_slot = b * tiles_keep + jnp.maximum(j - (tiles_seq - tiles_keep), 0)
        return (jnp.where(i >= tiles_p, kept_tiles - n_s // tm + (i - tiles_p), prompt_slot), 0)

    row = pl.BlockSpec((tm, D_HALF), lambda i: (i, 0))
    kept = pl.BlockSpec((tm, D_HALF), keep_map)
    full = lambda dt: jax.ShapeDtypeStruct((n, D_HALF), dt)
    kept_shape = jax.ShapeDtypeStruct((kept_tiles * tm, D_HALF), F32)
    return pl.pallas_call(
        functools.partial(_proj_ab_body, tiles_p=tiles_p),
        out_shape=[full(BF16), kept_shape, kept_shape, full(BF16), full(BF16),
                   full(F32), full(F32), full(F32), full(F32), full(F32)],
        grid=(n // tm,),
        in_specs=_group_specs(D_MODEL, tiles_p) + [
            _const_spec((1, D_MODEL)), _const_spec(w16.shape),
            _const_spec((1, D_HALF)), _const_spec((1, D_HALF)), _const_spec((1, D_HALF)),
            _const_spec((D_HALF, D_HALF))],
        out_specs=[row, kept, kept] + [row] * 7,
        compiler_params=_cparams("arbitrary"),
        name="proj_ab",
    )(x_p, x_s, gain, w16, qg, kg, lb, bd)


def _attn_prompt_body(q_ref, k0, k1, k2, v0, v1, v2, bias_ref, o_ref):
    i = pl.program_id(1)
    lim = jnp.maximum(2 - i, 0) * ATTN_TILE
    col = lax.broadcasted_iota(I32, (ATTN_TILE, ATTN_WINDOW), 1)
    dead = col < lim
    for h in range(N_HEADS):
        hs = slice(h * HEAD_DIM, (h + 1) * HEAD_DIM)
        qh = q_ref[:, hs]
        s = jnp.concatenate([_dot_nt(qh, k0[:, hs]), _dot_nt(qh, k1[:, hs]),
                             _dot_nt(qh, k2[:, hs])], axis=1)
        s = jnp.where(dead, NEG_INF, s + bias_ref[h])
        m = jnp.max(s, axis=-1, keepdims=True)
        p = jnp.exp(s - m)
        l = jnp.sum(p, axis=-1, keepdims=True)
        p16 = p.astype(BF16)
        o = (_dot(p16[:, 0:ATTN_TILE], v0[:, hs])
             + _dot(p16[:, ATTN_TILE:2 * ATTN_TILE], v1[:, hs])
             + _dot(p16[:, 2 * ATTN_TILE:], v2[:, hs]))
        o_ref[:, hs] = (o / l).astype(BF16)


def _attn_prompt(q16, k16, v16, bias_full, batch, seq):
    tiles = seq // ATTN_TILE
    qspec = pl.BlockSpec((ATTN_TILE, D_HALF), lambda b, i: (b * tiles + i, 0))

    def kv(back):
        return pl.BlockSpec((ATTN_TILE, D_HALF),
                            lambda b, i: (b * tiles + jnp.maximum(i - back, 0), 0))

    return pl.pallas_call(
        _attn_prompt_body,
        out_shape=jax.ShapeDtypeStruct((batch * seq, D_HALF), BF16),
        grid=(batch, tiles),
        in_specs=[qspec, kv(2), kv(1), kv(0), kv(2), kv(1), kv(0),
                  _const_spec(bias_full.shape)],
        out_specs=qspec,
        compiler_params=_cparams("parallel", "parallel"),
        name="attn_prompt",
    )(q16, k16, k16, k16, v16, v16, v16, bias_full)


def _attn_sample_body(q_ref, kc_ref, vc_ref, kn_ref, vn_ref, bc_ref, bn_ref, o_ref):
    kc = kc_ref[0].astype(BF16)
    vc = vc_ref[0].astype(BF16)
    for h in range(N_HEADS):
        hs = slice(h * HEAD_DIM, (h + 1) * HEAD_DIM)
        qh = q_ref[:, hs]
        sc = _dot_nt(qh, kc[:, hs]) + bc_ref[h]
        sn = _dot_nt(qh, kn_ref[:, hs]) + bn_ref[h]
        m = jnp.maximum(jnp.max(sc, axis=-1, keepdims=True), jnp.max(sn, axis=-1, keepdims=True))
        pc = jnp.exp(sc - m)
        pn = jnp.exp(sn - m)
        l = jnp.sum(pc, axis=-1, keepdims=True) + jnp.sum(pn, axis=-1, keepdims=True)
        o = _dot(pc.astype(BF16), vc[:, hs]) + _dot(pn.astype(BF16), vn_ref[:, hs])
        o_ref[:, hs] = (o / l).astype(BF16)


def _attn_sample(q16, k16, v16, cache_k, cache_v, bias_c, bias_n, row0, nseq, tn):
    blk0 = row0 // tn
    la = cache_k.shape[1]
    new = pl.BlockSpec((tn, D_HALF), lambda b: (blk0 + b, 0))
    cache = pl.BlockSpec((1, la, D_HALF), lambda b: (b, 0, 0))
    return pl.pallas_call(
        _attn_sample_body,
        out_shape=jax.ShapeDtypeStruct((nseq * tn, D_HALF), BF16),
        grid=(nseq,),
        in_specs=[new, cache, cache, new, new, _const_spec(bias_c.shape), _const_spec(bias_n.shape)],
        out_specs=pl.BlockSpec((tn, D_HALF), lambda b: (b, 0)),
        compiler_params=_cparams("parallel"),
        name="attn_sample",
    )(q16, cache_k, cache_v, k16, v16, bias_c, bias_n)


def _hgrn_body(q_ref, k_ref, v_ref, lf_ref, s0_ref, tblk_ref, tfull_ref, o_ref, sout_ref, st_scr,
               *, chunk, nchunks):
    nblk = chunk // HGRN_BLOCK

    @pl.when(pl.program_id(1) == 0)
    def _():
        st_scr[...] = s0_ref[0]

    row = lax.broadcasted_iota(I32, (chunk, D_HALF), 0)
    r2 = lax.broadcasted_iota(I32, (chunk, chunk), 0)
    c2 = lax.broadcasted_iota(I32, (chunk, chunk), 1)
    same_blk_causal = jnp.logical_and(r2 // HGRN_BLOCK == c2 // HGRN_BLOCK, c2 <= r2)

    def one_chunk(c, carry):
        sl = pl.ds(pl.multiple_of(c * chunk, chunk), chunk)
        q = q_ref[sl, :]
        k = k_ref[sl, :]
        v16 = v_ref[sl, :].astype(BF16)
        l0, l1, l2 = _split3(lf_ref[sl, :])
        tb = tblk_ref[...]
        tf = tfull_ref[...]
        b_in = _dot(tb, l0) + _dot(tb, l1) + _dot(tb, l2)
        b_ch = _dot(tf, l0) + _dot(tf, l1) + _dot(tf, l2)
        ld = (q * jnp.exp(b_in)).astype(BF16)
        rd = (k * jnp.exp(-b_in)).astype(BF16)
        lj, rj = [], []
        for j in range(nblk - 1):
            e_j = b_ch[(j + 1) * HGRN_BLOCK - 1:(j + 1) * HGRN_BLOCK, :]
            later = row >= (j + 1) * HGRN_BLOCK
            inside = jnp.logical_and(row >= j * HGRN_BLOCK, row < (j + 1) * HGRN_BLOCK)
            lj.append(jnp.where(later, q * jnp.exp(jnp.minimum(b_ch - e_j, 0.0)), 0.0).astype(BF16))
            rj.append(jnp.where(inside, k * jnp.exp(jnp.minimum(e_j - b_ch, 0.0)), 0.0).astype(BF16))
        e_end = b_ch[chunk - 1:chunk, :]
        qc = (q * jnp.exp(b_ch)).astype(BF16)
        kc = (k * jnp.exp(e_end - b_ch)).astype(BF16)
        dec = jnp.exp(e_end)
        for h in range(N_HEADS):
            hs = slice(h * HEAD_DIM, (h + 1) * HEAD_DIM)
            sc = jnp.where(same_blk_causal, _dot_nt(ld[:, hs], rd[:, hs]), 0.0)
            for j in range(nblk - 1):
                sc = sc + _dot_nt(lj[j][:, hs], rj[j][:, hs])
            st = st_scr[h]
            o = _dot(sc.astype(BF16), v16[:, hs]) + _dot_nt(qc[:, hs], st.astype(BF16))
            o_ref[sl, hs] = o
            st_scr[h] = st * dec[:, hs] + _dot_tn(v16[:, hs], kc[:, hs])
        return carry

    lax.fori_loop(0, nchunks, one_chunk, 0)
    sout_ref[0] = st_scr[...]


def _hgrn(qb, kb, vb, lf, s0t, row0, nseq, seq, chunk, nchunks):
    tt = chunk * nchunks
    steps = seq // tt
    blk0 = row0 // tt
    t = jnp.arange(chunk)
    lower = t[None, :] <= t[:, None]
    tfull = lower.astype(BF16)
    tblk = jnp.logical_and(lower, (t[None, :] // HGRN_BLOCK) == (t[:, None] // HGRN_BLOCK)).astype(BF16)
    tok = pl.BlockSpec((tt, D_HALF), lambda b, j: (blk0 + b * steps + j, 0))
    state = pl.BlockSpec((1, N_HEADS, HEAD_DIM, HEAD_DIM), lambda b, j: (b, 0, 0, 0))
    return pl.pallas_call(
        functools.partial(_hgrn_body, chunk=chunk, nchunks=nchunks),
        out_shape=[jax.ShapeDtypeStruct((nseq * seq, D_HALF), F32),
                   jax.ShapeDtypeStruct(s0t.shape, F32)],
        grid=(nseq, steps),
        in_specs=[tok, tok, tok, tok, state, _const_spec((chunk, chunk)), _const_spec((chunk, chunk))],
        out_specs=[pl.BlockSpec((tt, D_HALF), lambda b, j: (b * steps + j, 0)), state],
        scratch_shapes=[pltpu.VMEM((N_HEADS, HEAD_DIM, HEAD_DIM), F32)],
        compiler_params=_cparams("parallel", "arbitrary"),
        name="hgrn",
    )(qb, kb, vb, lf, s0t, tblk, tfull)


def _route(xn, wr_hi, wr_lo, br, tstrict, cnt_scr, route_o, cnt_o):
    rows = xn.shape[0]
    x_hi, x_lo = _split2(xn)
    logits = _dot(x_hi, wr_hi) + _dot(x_lo, wr_hi) + _dot(x_hi, wr_lo) + br
    lane = lax.broadcasted_iota(I32, (rows, LANES), 1)
    lane_f = lane.astype(F32)

    def first_argmax(vals):
        m = jnp.max(vals, axis=-1, keepdims=True)
        idx = jnp.min(jnp.where(vals == m, lane_f, float(LANES)), axis=-1, keepdims=True)
        return m, idx.astype(I32)

    is_grp = jnp.logical_and(lane >= N_EXPERTS, lane < N_EXPERTS + N_GROUPS)
    gl = jnp.where(is_grp, logits, -jnp.inf)
    gmax, gidx = first_argmax(gl)
    p_grp = 1.0 / jnp.sum(jnp.exp(gl - gmax), axis=-1, keepdims=True)
    grp = gidx - N_EXPERTS
    in_grp = jnp.logical_and(lane < N_EXPERTS, lane // EXPERTS_PER_GROUP == grp)
    el = jnp.where(in_grp, logits, -jnp.inf)
    v1, e1 = first_argmax(el)
    v2, e2 = first_argmax(jnp.where(lane == e1, -jnp.inf, el))
    t2 = jnp.exp(v2 - v1)
    den = 1.0 + t2
    g1 = (1.0 / den) * p_grp
    g2 = (t2 / den) * p_grp

    oh1 = lane == e1
    oh2 = lane == e2
    oh = jnp.where(jnp.logical_or(oh1, oh2), 1.0, 0.0)
    before = _dot(tstrict, oh.astype(BF16)) + cnt_scr[...]
    rank1 = jnp.sum(jnp.where(oh1, before, 0.0), axis=-1, keepdims=True)
    rank2 = jnp.sum(jnp.where(oh2, before, 0.0), axis=-1, keepdims=True)
    cnt_scr[...] = cnt_scr[...] + jnp.sum(oh, axis=0, keepdims=True)
    cnt_o[...] = cnt_scr[...]

    slab = jnp.zeros((rows, LANES), F32)
    for idx, val in enumerate((e1.astype(F32), e2.astype(F32), g1, g2, rank1, rank2)):
        slab = jnp.where(lane == idx, val, slab)
    route_o[...] = slab


def _out_ab_body(hp_ref, hs_ref, oap_ref, oas_ref, obp_ref, obs_ref, gb_ref, og_ref, bd_ref, w_ref,
                 gf_ref, wrh_ref, wrl_ref, br_ref, ts_ref, h_o, xn_o, route_o, cnt_o, cnt_scr,
                 *, tiles_p):
    @pl.when(pl.program_id(0) == 0)
    def _():
        cnt_scr[...] = jnp.zeros_like(cnt_scr)

    ob = _group_pick(obp_ref, obs_ref, tiles_p)
    obn = ob * lax.rsqrt(_head_mean_sq(ob, bd_ref[...]) + RMS_EPS) * og_ref[...]
    obg = (obn * jax.nn.silu(gb_ref[...])).astype(BF16)
    oa = _group_pick(oap_ref, oas_ref, tiles_p)
    mix = _dot(oa, w_ref[0:D_HALF, :]) + _dot(obg, w_ref[D_HALF:, :])
    h1 = _group_pick(hp_ref, hs_ref, tiles_p) + mix
    h_o[...] = h1
    xn = _rms(h1, gf_ref[...])
    _store_row_tiles(xn_o, xn)
    _route(xn, wrh_ref[...], wrl_ref[...], br_ref[...], ts_ref[...], cnt_scr, route_o, cnt_o)


def _out_ab(h_p, h_s, oa_p, oa_s, ob_p, ob_s, gb, og, bd, w16, gf, wrh, wrl, br, tstrict):
    tm = TOKEN_TILE
    n = h_p.shape[0] + h_s.shape[0]
    tiles_p = h_p.shape[0] // tm
    row = lambda w: pl.BlockSpec((tm, w), lambda i: (i, 0))
    return pl.pallas_call(
        functools.partial(_out_ab_body, tiles_p=tiles_p),
        out_shape=[jax.ShapeDtypeStruct((n, D_MODEL), F32), jax.ShapeDtypeStruct((n * ROW_TILE, LANES), F32),
                   jax.ShapeDtypeStruct((n, LANES), F32), jax.ShapeDtypeStruct((1, LANES), F32)],
        grid=(n // tm,),
        in_specs=_group_specs(D_MODEL, tiles_p) + _group_specs(D_HALF, tiles_p)
        + _group_specs(D_HALF, tiles_p) + [
            row(D_HALF), _const_spec((1, D_HALF)),
            _const_spec((D_HALF, D_HALF)), _const_spec((D_MODEL, D_MODEL)),
            _const_spec((1, D_MODEL)), _const_spec((D_MODEL, LANES)),
            _const_spec((D_MODEL, LANES)), _const_spec((1, LANES)), _const_spec((tm, tm))],
        out_specs=[row(D_MODEL), pl.BlockSpec((tm * ROW_TILE, LANES), lambda i: (i, 0)), row(LANES),
                   _const_spec((1, LANES))],
        scratch_shapes=[pltpu.VMEM((1, LANES), F32)],
        compiler_params=_cparams("arbitrary"),
        name="out_ab",
    )(h_p, h_s, oa_p, oa_s, ob_p, ob_s, gb, og, bd, w16, gf, wrh, wrl, br, tstrict)


def _row_copy(src, src_row, dst, dst_row, sem):
    s0 = pl.multiple_of(src_row * ROW_TILE, ROW_TILE)
    d0 = pl.multiple_of(dst_row * ROW_TILE, ROW_TILE)
    return pltpu.make_async_copy(src.at[pl.ds(s0, ROW_TILE)], dst.at[pl.ds(d0, ROW_TILE)], sem)


def _gather_rows_body(idx_ref, nt_ref, src_hbm, o_ref, buf, sems, *, tile):
    i = pl.program_id(0)
    nt = nt_ref[0]

    def issue(t):
        slot = t % 2
        base = t * tile

        def one(r, carry):
            _row_copy(src_hbm, idx_ref[base + r], buf.at[slot], r, sems.at[slot]).start()
            return carry

        lax.fori_loop(0, tile, one, 0, unroll=GATHER_UNROLL)

    @pl.when(jnp.logical_and(i == 0, nt > 0))
    def _():
        issue(0)

    @pl.when(i + 1 < nt)
    def _():
        issue(i + 1)

    @pl.when(i < nt)
    def _():
        slot = i % 2
        for _ in range(tile):
            _row_copy(src_hbm, 0, buf.at[slot], 0, sems.at[slot]).wait()
        o_ref[...] = buf[slot]

    @pl.when(i >= nt)
    def _():
        o_ref[...] = jnp.zeros_like(o_ref)


def _gather_rows(idx, n_tiles, src, n_out):
    tm = MOE_TILE
    return pl.pallas_call(
        functools.partial(_gather_rows_body, tile=tm),
        out_shape=jax.ShapeDtypeStruct((n_out * ROW_TILE, LANES), F32),
        grid_spec=pltpu.PrefetchScalarGridSpec(
            num_scalar_prefetch=2,
            grid=(n_out // tm,),
            in_specs=[pl.BlockSpec(memory_space=pl.ANY)],
            out_specs=pl.BlockSpec((tm * ROW_TILE, LANES), lambda i, idx, nt: (i, 0)),
            scratch_shapes=[pltpu.VMEM((2, tm * ROW_TILE, LANES), F32), pltpu.SemaphoreType.DMA((2,))],
        ),
        compiler_params=pltpu.CompilerParams(dimension_semantics=("arbitrary",)),
        name="moe_gather",
    )(idx, n_tiles, src)


def _experts_body(te_ref, nt_ref, x_ref, w1_ref, w3_ref, w2_ref, y_ref, w1_s, w3_s, w2_s):
    i = pl.program_id(0)
    prev = te_ref[jnp.maximum(i - 1, 0)]
    fresh = jnp.logical_or(i == 0, te_ref[i] != prev)

    @pl.when(fresh)
    def _():
        w1_s[...] = w1_ref[0].astype(BF16)
        w3_s[...] = w3_ref[0].astype(BF16)
        w2_s[...] = w2_ref[0].astype(BF16)

    @pl.when(i < nt_ref[0])
    def _():
        x = _load_row_tiles(x_ref, MOE_TILE).astype(BF16)
        a = _dot(x, w1_s[...])
        b = _dot(x, w3_s[...])
        _store_row_tiles(y_ref, _dot((jax.nn.silu(a) * b).astype(BF16), w2_s[...]))

    @pl.when(i >= nt_ref[0])
    def _():
        y_ref[...] = jnp.zeros_like(y_ref)


def _experts(tile_expert, n_tiles, xs, w1, w3, w2):
    rows = xs.shape[0] // ROW_TILE
    tm = MOE_TILE
    return pl.pallas_call(
        _experts_body,
        out_shape=jax.ShapeDtypeStruct((rows * ROW_TILE, LANES), F32),
        grid_spec=pltpu.PrefetchScalarGridSpec(
            num_scalar_prefetch=2,
            grid=(rows // tm,),
            in_specs=[pl.BlockSpec((tm * ROW_TILE, LANES), lambda i, te, nt: (i, 0)),
                      pl.BlockSpec((1, D_MODEL, D_EXPERT), lambda i, te, nt: (te[i], 0, 0)),
                      pl.BlockSpec((1, D_MODEL, D_EXPERT), lambda i, te, nt: (te[i], 0, 0)),
                      pl.BlockSpec((1, D_EXPERT, D_MODEL), lambda i, te, nt: (te[i], 0, 0))],
            out_specs=pl.BlockSpec((tm * ROW_TILE, LANES), lambda i, te, nt: (i, 0)),
            scratch_shapes=[pltpu.VMEM((D_MODEL, D_EXPERT), BF16), pltpu.VMEM((D_MODEL, D_EXPERT), BF16),
                            pltpu.VMEM((D_EXPERT, D_MODEL), BF16)],
        ),
        compiler_params=_cparams("arbitrary"),
        name="moe_experts",
    )(tile_expert, n_tiles, xs, w1, w3, w2)


def _moe_plan(route, counts_per_segment, seg_rows):
    e = route[:, 0:2].astype(I32)
    rank = route[:, 4:6].astype(I32)
    seg_counts = [c[0, :N_EXPERTS].astype(I32) for c in counts_per_segment]
    base = jnp.zeros((N_EXPERTS,), I32)
    seg_base = []
    for c in seg_counts:
        seg_base.append(base)
        base = base + c
    counts = base
    row_base = jnp.concatenate([jnp.broadcast_to(b[None], (r, N_EXPERTS))
                                for b, r in zip(seg_base, seg_rows)], axis=0)
    padded = ((counts + MOE_TILE - 1) // MOE_TILE) * MOE_TILE
    ends = jnp.cumsum(padded)
    offs = ends - padded
    pos = jnp.take(offs, e) + rank + jnp.take_along_axis(row_base, e, axis=1)
    pos_flat = jnp.concatenate([pos[:, 0], pos[:, 1]])
    n = route.shape[0]
    max_tiles = (2 * n + N_EXPERTS * (MOE_TILE - 1) + MOE_TILE - 1) // MOE_TILE
    tile_start = jnp.arange(max_tiles, dtype=I32) * MOE_TILE
    tile_expert = jnp.minimum(jnp.sum((tile_start[:, None] >= ends[None, :]).astype(I32), axis=1),
                              N_EXPERTS - 1)
    n_tiles = (ends[-1] // MOE_TILE).reshape(1)
    n_rows = max_tiles * MOE_TILE
    row_src = jnp.zeros((n_rows,), I32).at[pos_flat].set(jnp.arange(2 * n, dtype=I32) % n)
    return pos_flat, row_src, tile_expert, n_tiles, n_rows


def _moe(xn, route, counts_per_segment, seg_rows, w1, w3, w2):
    n = xn.shape[0] // ROW_TILE
    pos, row_src, tile_expert, n_tiles, n_rows = _moe_plan(route, counts_per_segment, seg_rows)
    xs = _gather_rows(row_src, n_tiles, xn, n_rows)
    ys = _experts(tile_expert, n_tiles, xs, w1, w3, w2)
    all_tiles = jnp.full((1,), 2 * n // MOE_TILE, I32)
    return _gather_rows(pos, all_tiles, ys, 2 * n).reshape(2, n * ROW_TILE, LANES)


def _add_moe(h, ya_ref, yb_ref, route):
    rows = h.shape[0]
    ya = _load_row_tiles(ya_ref.at[0], rows)
    yb = _load_row_tiles(yb_ref.at[0], rows)
    return h + (route[:, 2:3] * ya + route[:, 3:4] * yb)


def _moe_out_spec(k, tm):
    return pl.BlockSpec((1, tm * ROW_TILE, LANES), lambda i: (k, i, 0))


def _proj_c_body(h_ref, ya_ref, yb_ref, r_ref, g_ref, w_ref, h_o, u_o, bg_o):
    h2 = _add_moe(h_ref[...], ya_ref, yb_ref, r_ref[...])
    h_o[...] = h2
    xb = _rms(h2, g_ref[...]).astype(BF16)
    bg_o[...] = _dot(xb, w_ref[:, 0:D_MODEL])
    u_o[...] = _dot(xb, w_ref[:, D_MODEL:2 * D_MODEL]) * _dot(xb, w_ref[:, 2 * D_MODEL:])


def _proj_c(h, y2, route, gain, w16):
    n = h.shape[0]
    tm = TOKEN_TILE
    row = lambda w: pl.BlockSpec((tm, w), lambda i: (i, 0))
    ysp = lambda k: _moe_out_spec(k, tm)
    return pl.pallas_call(
        _proj_c_body,
        out_shape=[jax.ShapeDtypeStruct((n, D_MODEL), F32)] * 3,
        grid=(n // tm,),
        in_specs=[row(D_MODEL), ysp(0), ysp(1), row(LANES), _const_spec((1, D_MODEL)),
                  _const_spec(w16.shape)],
        out_specs=[row(D_MODEL)] * 3,
        compiler_params=_cparams("parallel"),
        name="proj_c",
    )(h, y2, y2, route, gain, w16)


def _conv_out_body(u_ref, up_ref, stp_ref, sts_ref, bg_ref, h_ref, cw_ref, w_ref, gf_ref,
                   wrh_ref, wrl_ref, br_ref, ts_ref, h_o, xn_o, route_o, cnt_o, cnt_scr,
                   *, tile, tiles_p, seq_p, seq_s):
    i = pl.program_id(0)

    @pl.when(i == 0)
    def _():
        cnt_scr[...] = jnp.zeros_like(cnt_scr)

    u = u_ref[...]
    rowi = lax.broadcasted_iota(I32, (tile, D_MODEL), 0)
    is_p = i < tiles_p
    at_start = (i * tile) % seq_p == 0
    stp = stp_ref[0]
    prev = up_ref[...]
    m2_p = jnp.where(at_start, stp[0:1, :], prev[6:7, :])
    m1_p = jnp.where(at_start, stp[1:2, :], prev[7:8, :])
    per = tile // seq_s
    sts = sts_ref[...]
    m2_s = jnp.broadcast_to(sts[:, 0:1, :], (per, seq_s, D_MODEL)).reshape(tile, D_MODEL)
    m1_s = jnp.broadcast_to(sts[:, 1:2, :], (per, seq_s, D_MODEL)).reshape(tile, D_MODEL)
    m2 = jnp.where(is_p, m2_p, m2_s)
    m1 = jnp.where(is_p, m1_p, m1_s)
    pos = jnp.where(is_p, rowi, rowi % seq_s)
    u1 = jnp.where(pos == 0, m1, pltpu.roll(u, 1, axis=0))
    u2 = jnp.where(pos == 0, m2, jnp.where(pos == 1, m1, pltpu.roll(u, 2, axis=0)))
    cw = cw_ref[...]
    conv = u2 * cw[0:1, :] + u1 * cw[1:2, :] + u * cw[2:3, :]
    mix = _dot((bg_ref[...] * conv).astype(BF16), w_ref[...])
    h3 = h_ref[...] + mix
    h_o[...] = h3
    xn = _rms(h3, gf_ref[...])
    _store_row_tiles(xn_o, xn)
    _route(xn, wrh_ref[...], wrl_ref[...], br_ref[...], ts_ref[...], cnt_scr, route_o, cnt_o)


def _conv_out(u, bg, h, state_p, state_s, cw, w16, gf, wrh, wrl, br, tstrict, seq_p, seq_s):
    tm = TOKEN_TILE
    n = u.shape[0]
    tiles_p = state_p.shape[0] * seq_p // tm
    per = tm // seq_s
    row = lambda w: pl.BlockSpec((tm, w), lambda i: (i, 0))
    prev = pl.BlockSpec((8, D_MODEL), lambda i: (jnp.maximum(i * (tm // 8) - 1, 0), 0))
    stp_spec = pl.BlockSpec((1, 2, D_MODEL),
                            lambda i: (jnp.minimum(i, tiles_p - 1) * tm // seq_p, 0, 0))
    sts_spec = pl.BlockSpec((per, 2, D_MODEL), lambda i: (jnp.maximum(i - tiles_p, 0), 0, 0))
    return pl.pallas_call(
        functools.partial(_conv_out_body, tile=tm, tiles_p=tiles_p, seq_p=seq_p, seq_s=seq_s),
        out_shape=[jax.ShapeDtypeStruct((n, D_MODEL), F32),
                   jax.ShapeDtypeStruct((n * ROW_TILE, LANES), F32),
                   jax.ShapeDtypeStruct((n, LANES), F32), jax.ShapeDtypeStruct((1, LANES), F32)],
        grid=(n // tm,),
        in_specs=[row(D_MODEL), prev, stp_spec, sts_spec, row(D_MODEL), row(D_MODEL),
                  _const_spec((3, D_MODEL)),
                  _const_spec((D_MODEL, D_MODEL)), _const_spec((1, D_MODEL)),
                  _const_spec((D_MODEL, LANES)), _const_spec((D_MODEL, LANES)),
                  _const_spec((1, LANES)), _const_spec((tm, tm))],
        out_specs=[row(D_MODEL), pl.BlockSpec((tm * ROW_TILE, LANES), lambda i: (i, 0)), row(LANES),
                   _const_spec((1, LANES))],
        scratch_shapes=[pltpu.VMEM((1, LANES), F32)],
        compiler_params=_cparams("arbitrary"),
        name="conv_out",
    )(u, u, state_p, state_s, bg, h, cw, w16, gf, wrh, wrl, br, tstrict)


def _combine_body(h_ref, ya_ref, yb_ref, r_ref, op_ref, os_ref, *, tiles_p):
    out = _add_moe(h_ref[...], ya_ref, yb_ref, r_ref[...])

    @pl.when(pl.program_id(0) < tiles_p)
    def _():
        op_ref[...] = out

    @pl.when(pl.program_id(0) >= tiles_p)
    def _():
        os_ref[...] = out


def _combine(h, y2, route, n_p):
    n = h.shape[0]
    tm = TOKEN_TILE
    tiles_p = n_p // tm
    row = lambda w: pl.BlockSpec((tm, w), lambda i: (i, 0))
    ysp = lambda k: _moe_out_spec(k, tm)
    return pl.pallas_call(
        functools.partial(_combine_body, tiles_p=tiles_p),
        out_shape=[jax.ShapeDtypeStruct((n_p, D_MODEL), F32), jax.ShapeDtypeStruct((n - n_p, D_MODEL), F32)],
        grid=(n // tm,),
        in_specs=[row(D_MODEL), ysp(0), ysp(1), row(LANES)],
        out_specs=_group_specs(D_MODEL, tiles_p),
        compiler_params=_cparams("arbitrary"),
        name="moe_combine",
    )(h, y2, y2, route)


def _rel_bias_toeplitz(rel_bias, rows, cols, lead):
    period = rows + cols
    k = jnp.arange(period)
    d = jnp.where(k < cols, k, k - period)
    idx = jnp.clip(lead - d, -REL_CLIP, REL_CLIP) + REL_CLIP
    v = jnp.take(rel_bias.astype(F32), idx, axis=1)
    heads = v.shape[0]
    skew = jnp.tile(v, (1, rows))[:, :rows * (period - 1)].reshape(heads, rows, period - 1)
    return skew[:, :, :cols]


def _bias_prompt(rel_bias):
    r = jnp.arange(ATTN_TILE)[:, None]
    c = jnp.arange(ATTN_WINDOW)[None, :]
    j = c // CHUNK - r // CHUNK
    band = jnp.logical_and(j >= 0, j <= BAND_CHUNKS)
    bias = _rel_bias_toeplitz(rel_bias, ATTN_TILE, ATTN_WINDOW, BAND_CHUNKS * CHUNK)
    return jnp.where(band[None], bias, NEG_INF)


def _bias_sample(rel_bias, la, tn):
    bias = _rel_bias_toeplitz(rel_bias, tn, la + tn, la)
    return bias[:, :, :la], bias[:, :, la:]


def _router_weights(w_group, b_group, w_expert, b_expert):
    pad = LANES - N_EXPERTS - N_GROUPS
    w = jnp.concatenate([w_expert, w_group, jnp.zeros((D_MODEL, pad), F32)], axis=1)
    b = jnp.concatenate([b_expert, b_group, jnp.zeros((pad,), F32)])[None, :].astype(F32)
    hi, lo = _split2(w.astype(F32))
    return hi, lo, b


def kernel(x_prompt, x_sample, cache_a_k, cache_a_v, state_hgrn, state_conv, norm_mix, norm_ffn,
           w_in_ab, w_out_ab, q_norm, k_norm, rel_bias, hgrn_lb_logits, hgrn_out_norm, w_in_c,
           conv_w, w_out_c, w_group, b_group, w_expert, b_expert, w1, w3, w2):
    batch, seq, d = x_prompt.shape
    nseq_s, tn, _ = x_sample.shape
    la = cache_a_k.shape[2]
    n_p = batch * seq
    n_s = nseq_s * tn
    n = n_p + n_s
    keep = min(BAND_CHUNKS * CHUNK, seq)

    x_p = x_prompt.reshape(n_p, d)
    x_s = x_sample.reshape(n_s, d)
    lb_all = jnp.cumsum(jax.nn.softmax(hgrn_lb_logits.astype(F32), axis=0), axis=0)
    head_avg = jnp.kron(jnp.eye(N_HEADS, dtype=F32),
                        jnp.full((HEAD_DIM, HEAD_DIM), 1.0 / HEAD_DIM, F32)).astype(BF16)
    t = jnp.arange(TOKEN_TILE)
    tstrict = (t[None, :] < t[:, None]).astype(BF16)
    row1 = lambda v: v.astype(F32).reshape(1, -1)
    tile8 = lambda v: jnp.tile(v.astype(F32), N_HEADS).reshape(1, -1)

    l = 0
    (q16, kf, vf, k16, v16, lf, kb, vb, qb, gb) = _proj_ab(
        x_p, x_s, row1(norm_mix[0]), w_in_ab[l].astype(BF16), tile8(q_norm[l]), tile8(k_norm[l]),
        row1(lb_all[l]), head_avg, seq, keep)

    oa_p = _attn_prompt(q16, k16, v16, _bias_prompt(rel_bias[l]), batch, seq)
    bias_c, bias_n = _bias_sample(rel_bias[l], la, tn)
    oa_s = _attn_sample(q16, k16, v16, cache_a_k[l].reshape(nseq_s, la, D_HALF),
                        cache_a_v[l].reshape(nseq_s, la, D_HALF), bias_c, bias_n, n_p, nseq_s, tn)

    zeros_state = jnp.zeros((batch, N_HEADS, HEAD_DIM, HEAD_DIM), F32)
    ob_p, st_p = _hgrn(qb, kb, vb, lf, zeros_state, 0, batch, seq, 64, 4)
    ob_s, st_s = _hgrn(qb, kb, vb, lf, jnp.swapaxes(state_hgrn[l].astype(F32), -1, -2),
                       n_p, nseq_s, tn, tn, 1)

    wrh, wrl, br = _router_weights(w_group[0], b_group[0], w_expert[0], b_expert[0])
    h1, xn1, route1, cnt1 = _out_ab(x_p, x_s, oa_p, oa_s, ob_p, ob_s, gb, row1(hgrn_out_norm[l]),
                                    head_avg, w_out_ab[l].astype(BF16), row1(norm_ffn[0]),
                                    wrh, wrl, br, tstrict)
    y1 = _moe(xn1, route1, [cnt1], [n], w1[0], w3[0], w2[0])

    h2, u, bg = _proj_c(h1, y1, route1, row1(norm_mix[1]), w_in_c[0].astype(BF16))
    wrh, wrl, br = _router_weights(w_group[1], b_group[1], w_expert[1], b_expert[1])
    h3, xn2, route2, cnt2 = _conv_out(u, bg, h2, jnp.zeros((batch, 2, d), F32),
                                      state_conv[0].astype(F32), conv_w[0].astype(F32),
                                      w_out_c[0].astype(BF16), row1(norm_ffn[1]), wrh, wrl, br,
                                      tstrict, seq, tn)
    y2 = _moe(xn2, route2, [cnt2], [n], w1[1], w3[1], w2[1])
    out_p, out_s = _combine(h3, y2, route2, n_p)

    y_prompt = out_p.reshape(batch, seq, d)
    y_sample = out_s.reshape(nseq_s, tn, d)
    n_kp = batch * keep
    heads = lambda a, b_, t_: a.reshape(b_, t_, N_HEADS, HEAD_DIM)
    nk_p = heads(kf[:n_kp], batch, keep)[None]
    nv_p = heads(vf[:n_kp], batch, keep)[None]
    kf_s = heads(kf[n_kp:n_kp + n_s], nseq_s, tn)
    vf_s = heads(vf[n_kp:n_kp + n_s], nseq_s, tn)
    nk_s = jnp.concatenate([cache_a_k[l].astype(F32), kf_s], axis=1)[:, -la:][None]
    nv_s = jnp.concatenate([cache_a_v[l].astype(F32), vf_s], axis=1)[:, -la:][None]
    nh_p = jnp.swapaxes(st_p, -1, -2)[None]
    nh_s = jnp.swapaxes(st_s, -1, -2)[None]
    nc_p = jnp.stack([u[(b + 1) * seq - 2:(b + 1) * seq] for b in range(batch)])[None]
    u_s = u[n_p:].reshape(nseq_s, tn, d)
    nc_s = jnp.concatenate([state_conv[0].astype(F32), u_s], axis=1)[:, -2:][None]
    return (y_prompt, y_sample, nk_p, nv_p, nk_s, nv_s, nh_p, nh_s, nc_p, nc_s)
```

```python
import functools

import jax
import jax.numpy as jnp
from jax import lax
from jax.experimental import pallas as pl
from jax.experimental.pallas import tpu as pltpu

F32 = jnp.float32
BF16 = jnp.bfloat16
I32 = jnp.int32

D_MODEL = 1024
CHUNK = 64
BAND_CHUNKS = 8
HEAD_DIM = 64
N_HEADS = 8
D_HALF = N_HEADS * HEAD_DIM
REL_CLIP = 128
HGRN_BLOCK = 16
N_GROUPS = 4
EXPERTS_PER_GROUP = 8
N_EXPERTS = N_GROUPS * EXPERTS_PER_GROUP
D_EXPERT = 256
RMS_EPS = 1e-6
NEG_INF = -1e30

LANES = 128
ROW_TILE = D_MODEL // LANES
TOKEN_TILE = 256
ATTN_TILE = 256
ATTN_WINDOW = ATTN_TILE + BAND_CHUNKS * CHUNK
MOE_TILE = 256
GATHER_UNROLL = 8
VMEM_LIMIT = 48 * 1024 * 1024


def _cparams(*sem):
    return pltpu.CompilerParams(dimension_semantics=sem, vmem_limit_bytes=VMEM_LIMIT)


def _const_spec(shape):
    nd = len(shape)
    return pl.BlockSpec(shape, lambda *_: (0,) * nd)


def _store_row_tiles(ref, val):
    rows = val.shape[0]
    for c in range(ROW_TILE):
        ref[pl.ds(c, rows, stride=ROW_TILE), :] = val[:, c * LANES:(c + 1) * LANES]


def _load_row_tiles(ref, rows):
    return jnp.concatenate([ref[pl.ds(c, rows, stride=ROW_TILE), :] for c in range(ROW_TILE)], axis=1)


def _rms(x, gain):
    ms = jnp.mean(x * x, axis=-1, keepdims=True)
    return (x * lax.rsqrt(ms + RMS_EPS)) * gain


def _split2(x):
    hi = x.astype(BF16)
    lo = (x - hi.astype(F32)).astype(BF16)
    return hi, lo


def _split3(x):
    p0 = x.astype(BF16)
    r = x - p0.astype(F32)
    p1 = r.astype(BF16)
    p2 = (r - p1.astype(F32)).astype(BF16)
    return p0, p1, p2


def _dot(a, b):
    return jnp.dot(a, b, preferred_element_type=F32)


def _dot_nt(a, b):
    return lax.dot_general(a, b, (((1,), (1,)), ((), ())), preferred_element_type=F32)


def _dot_tn(a, b):
    return lax.dot_general(a, b, (((0,), (0,)), ((), ())), preferred_element_type=F32)


def _head_mean_sq(v, bd):
    hi, lo = _split2(v * v)
    return _dot(hi, bd) + _dot(lo, bd)


def _group_specs(width, tiles_p, tm=TOKEN_TILE):
    return [pl.BlockSpec((tm, width), lambda i: (jnp.minimum(i, tiles_p - 1), 0)),
            pl.BlockSpec((tm, width), lambda i: (jnp.maximum(i - tiles_p, 0), 0))]


def _group_pick(p_ref, s_ref, tiles_p):
    return jnp.where(pl.program_id(0) < tiles_p, p_ref[...], s_ref[...])


def _proj_ab_body(xp_ref, xs_ref, g_ref, w_ref, qg_ref, kg_ref, lb_ref, bd_ref,
                  q_o, kf_o, vf_o, k16_o, v16_o, lf_o, kb_o, vb_o, qb_o, gb_o, *, tiles_p):
    xb = _rms(_group_pick(xp_ref, xs_ref, tiles_p), g_ref[...]).astype(BF16)
    bd = bd_ref[...]

    def seg(j):
        return _dot(xb, w_ref[:, j * D_HALF:(j + 1) * D_HALF])

    qa = seg(0)
    qn = qa * lax.rsqrt(_head_mean_sq(qa, bd) + RMS_EPS) * qg_ref[...]
    q_o[...] = (qn * (HEAD_DIM ** -0.5)).astype(BF16)
    ka = seg(1)
    kn = ka * lax.rsqrt(_head_mean_sq(ka, bd) + RMS_EPS) * kg_ref[...]
    kf_o[...] = kn
    k16_o[...] = kn.astype(BF16)
    va = seg(2)
    vf_o[...] = va
    v16_o[...] = va.astype(BF16)
    lb = lb_ref[...]
    f = lb + (1.0 - lb) * jax.nn.sigmoid(seg(3))
    lf_o[...] = jnp.log(f)
    kb_o[...] = 1.0 - f
    vb_o[...] = seg(4)
    qb_o[...] = jax.nn.silu(seg(5))
    gb_o[...] = seg(6)


def _proj_ab(x_p, x_s, gain, w16, qg, kg, lb, bd, seq, keep):
    tm = TOKEN_TILE
    n_p, n_s = x_p.shape[0], x_s.shape[0]
    n = n_p + n_s
    tiles_p, tiles_seq, tiles_keep = n_p // tm, seq // tm, keep // tm
    kept_tiles = (n_p // seq) * tiles_keep + n_s // tm

    def keep_map(i):
        b, j = i // tiles_seq, i % tiles_seq
        prompt_slot = b * tiles_keep + jnp.maximum(j - (tiles_seq - tiles_keep), 0)
        return (jnp.where(i >= tiles_p, kept_tiles - n_s // tm + (i - tiles_p), prompt_slot), 0)

    row = pl.BlockSpec((tm, D_HALF), lambda i: (i, 0))
    kept = pl.BlockSpec((tm, D_HALF), keep_map)
    full = lambda dt: jax.ShapeDtypeStruct((n, D_HALF), dt)
    kept_shape = jax.ShapeDtypeStruct((kept_tiles * tm, D_HALF), F32)
    return pl.pallas_call(
        functools.partial(_proj_ab_body, tiles_p=tiles_p),
        out_shape=[full(BF16), kept_shape, kept_shape, full(BF16), full(BF16),
                   full(F32), full(F32), full(F32), full(F32), full(F32)],
        grid=(n // tm,),
        in_specs=_group_specs(D_MODEL, tiles_p) + [
            _const_spec((1, D_MODEL)), _const_spec(w16.shape),
            _const_spec((1, D_HALF)), _const_spec((1, D_HALF)), _const_spec((1, D_HALF)),
            _const_spec((D_HALF, D_HALF))],
        out_specs=[row, kept, kept] + [row] * 7,
        compiler_params=_cparams("arbitrary"),
        name="proj_ab",
    )(x_p, x_s, gain, w16, qg, kg, lb, bd)


def _attn_prompt_body(q_ref, k0, k1, k2, v0, v1, v2, bias_ref, o_ref):
    i = pl.program_id(1)
    lim = jnp.maximum(2 - i, 0) * ATTN_TILE
    col = lax.broadcasted_iota(I32, (ATTN_TILE, ATTN_WINDOW), 1)
    dead = col < lim
    for h in range(N_HEADS):
        hs = slice(h * HEAD_DIM, (h + 1) * HEAD_DIM)
        qh = q_ref[:, hs]
        s = jnp.concatenate([_dot_nt(qh, k0[:, hs]), _dot_nt(qh, k1[:, hs]),
                             _dot_nt(qh, k2[:, hs])], axis=1)
        s = jnp.where(dead, NEG_INF, s + bias_ref[h])
        m = jnp.max(s, axis=-1, keepdims=True)
        p = jnp.exp(s - m)
        l = jnp.sum(p, axis=-1, keepdims=True)
        p16 = p.astype(BF16)
        o = (_dot(p16[:, 0:ATTN_TILE], v0[:, hs])
             + _dot(p16[:, ATTN_TILE:2 * ATTN_TILE], v1[:, hs])
             + _dot(p16[:, 2 * ATTN_TILE:], v2[:, hs]))
        o_ref[:, hs] = (o / l).astype(BF16)


def _attn_prompt(q16, k16, v16, bias_full, batch, seq):
    tiles = seq // ATTN_TILE
    qspec = pl.BlockSpec((ATTN_TILE, D_HALF), lambda b, i: (b * tiles + i, 0))

    def kv(back):
        return pl.BlockSpec((ATTN_TILE, D_HALF),
                            lambda b, i: (b * tiles + jnp.maximum(i - back, 0), 0))

    return pl.pallas_call(
        _attn_prompt_body,
        out_shape=jax.ShapeDtypeStruct((batch * seq, D_HALF), BF16),
        grid=(batch, tiles),
        in_specs=[qspec, kv(2), kv(1), kv(0), kv(2), kv(1), kv(0),
                  _const_spec(bias_full.shape)],
        out_specs=qspec,
        compiler_params=_cparams("parallel", "parallel"),
        name="attn_prompt",
    )(q16, k16, k16, k16, v16, v16, v16, bias_full)


def _attn_sample_body(q_ref, kc_ref, vc_ref, kn_ref, vn_ref, bc_ref, bn_ref, o_ref):
    kc = kc_ref[0].astype(BF16)
    vc = vc_ref[0].astype(BF16)
    for h in range(N_HEADS):
        hs = slice(h * HEAD_DIM, (h + 1) * HEAD_DIM)
        qh = q_ref[:, hs]
        sc = _dot_nt(qh, kc[:, hs]) + bc_ref[h]
        sn = _dot_nt(qh, kn_ref[:, hs]) + bn_ref[h]
        m = jnp.maximum(jnp.max(sc, axis=-1, keepdims=True), jnp.max(sn, axis=-1, keepdims=True))
        pc = jnp.exp(sc - m)
        pn = jnp.exp(sn - m)
        l = jnp.sum(pc, axis=-1, keepdims=True) + jnp.sum(pn, axis=-1, keepdims=True)
        o = _dot(pc.astype(BF16), vc[:, hs]) + _dot(pn.astype(BF16), vn_ref[:, hs])
        o_ref[:, hs] = (o / l).astype(BF16)


def _attn_sample(q16, k16, v16, cache_k, cache_v, bias_c, bias_n, row0, nseq, tn):
    blk0 = row0 // tn
    la = cache_k.shape[1]
    new = pl.BlockSpec((tn, D_HALF), lambda b: (blk0 + b, 0))
    cache = pl.BlockSpec((1, la, D_HALF), lambda b: (b, 0, 0))
    return pl.pallas_call(
        _attn_sample_body,
        out_shape=jax.ShapeDtypeStruct((nseq * tn, D_HALF), BF16),
        grid=(nseq,),
        in_specs=[new, cache, cache, new, new, _const_spec(bias_c.shape), _const_spec(bias_n.shape)],
        out_specs=pl.BlockSpec((tn, D_HALF), lambda b: (b, 0)),
        compiler_params=_cparams("parallel"),
        name="attn_sample",
    )(q16, cache_k, cache_v, k16, v16, bias_c, bias_n)


def _hgrn_body(q_ref, k_ref, v_ref, lf_ref, s0_ref, tblk_ref, tfull_ref, o_ref, sout_ref, st_scr,
               *, chunk, nchunks):
    nblk = chunk // HGRN_BLOCK

    @pl.when(pl.program_id(1) == 0)
    def _():
        st_scr[...] = s0_ref[0]

    row = lax.broadcasted_iota(I32, (chunk, D_HALF), 0)
    r2 = lax.broadcasted_iota(I32, (chunk, chunk), 0)
    c2 = lax.broadcasted_iota(I32, (chunk, chunk), 1)
    same_blk_causal = jnp.logical_and(r2 // HGRN_BLOCK == c2 // HGRN_BLOCK, c2 <= r2)

    def one_chunk(c, carry):
        sl = pl.ds(pl.multiple_of(c * chunk, chunk), chunk)
        q = q_ref[sl, :]
        k = k_ref[sl, :]
        v16 = v_ref[sl, :].astype(BF16)
        l0, l1, l2 = _split3(lf_ref[sl, :])
        tb = tblk_ref[...]
        tf = tfull_ref[...]
        b_in = _dot(tb, l0) + _dot(tb, l1) + _dot(tb, l2)
        b_ch = _dot(tf, l0) + _dot(tf, l1) + _dot(tf, l2)
        ld = (q * jnp.exp(b_in)).astype(BF16)
        rd = (k * jnp.exp(-b_in)).astype(BF16)
        lj, rj = [], []
        for j in range(nblk - 1):
            e_j = b_ch[(j + 1) * HGRN_BLOCK - 1:(j + 1) * HGRN_BLOCK, :]
            later = row >= (j + 1) * HGRN_BLOCK
            inside = jnp.logical_and(row >= j * HGRN_BLOCK, row < (j + 1) * HGRN_BLOCK)
            lj.append(jnp.where(later, q * jnp.exp(jnp.minimum(b_ch - e_j, 0.0)), 0.0).astype(BF16))
            rj.append(jnp.where(inside, k * jnp.exp(jnp.minimum(e_j - b_ch, 0.0)), 0.0).astype(BF16))
        e_end = b_ch[chunk - 1:chunk, :]
        qc = (q * jnp.exp(b_ch)).astype(BF16)
        kc = (k * jnp.exp(e_end - b_ch)).astype(BF16)
        dec = jnp.exp(e_end)
        for h in range(N_HEADS):
            hs = slice(h * HEAD_DIM, (h + 1) * HEAD_DIM)
            sc = jnp.where(same_blk_causal, _dot_nt(ld[:, hs], rd[:, hs]), 0.0)
            for j in range(nblk - 1):
                sc = sc + _dot_nt(lj[j][:, hs], rj[j][:, hs])
            st = st_scr[h]
            o = _dot(sc.astype(BF16), v16[:, hs]) + _dot_nt(qc[:, hs], st.astype(BF16))
            o_ref[sl, hs] = o
            st_scr[h] = st * dec[:, hs] + _dot_tn(v16[:, hs], kc[:, hs])
        return carry

    lax.fori_loop(0, nchunks, one_chunk, 0)
    sout_ref[0] = st_scr[...]


def _hgrn(qb, kb, vb, lf, s0t, row0, nseq, seq, chunk, nchunks):
    tt = chunk * nchunks
    steps = seq // tt
    blk0 = row0 // tt
    t = jnp.arange(chunk)
    lower = t[None, :] <= t[:, None]
    tfull = lower.astype(BF16)
    tblk = jnp.logical_and(lower, (t[None, :] // HGRN_BLOCK) == (t[:, None] // HGRN_BLOCK)).astype(BF16)
    tok = pl.BlockSpec((tt, D_HALF), lambda b, j: (blk0 + b * steps + j, 0))
    state = pl.BlockSpec((1, N_HEADS, HEAD_DIM, HEAD_DIM), lambda b, j: (b, 0, 0, 0))
    return pl.pallas_call(
        functools.partial(_hgrn_body, chunk=chunk, nchunks=nchunks),
        out_shape=[jax.ShapeDtypeStruct((nseq * seq, D_HALF), F32),
                   jax.ShapeDtypeStruct(s0t.shape, F32)],
        grid=(nseq, steps),
        in_specs=[tok, tok, tok, tok, state, _const_spec((chunk, chunk)), _const_spec((chunk, chunk))],
        out_specs=[pl.BlockSpec((tt, D_HALF), lambda b, j: (b * steps + j, 0)), state],
        scratch_shapes=[pltpu.VMEM((N_HEADS, HEAD_DIM, HEAD_DIM), F32)],
        compiler_params=_cparams("parallel", "arbitrary"),
        name="hgrn",
    )(qb, kb, vb, lf, s0t, tblk, tfull)


def _route(xn, wr_hi, wr_lo, br, tstrict, cnt_scr, route_o, cnt_o):
    rows = xn.shape[0]
    x_hi, x_lo = _split2(xn)
    logits = _dot(x_hi, wr_hi) + _dot(x_lo, wr_hi) + _dot(x_hi, wr_lo) + br
    lane = lax.broadcasted_iota(I32, (rows, LANES), 1)
    lane_f = lane.astype(F32)

    def first_argmax(vals):
        m = jnp.max(vals, axis=-1, keepdims=True)
        idx = jnp.min(jnp.where(vals == m, lane_f, float(LANES)), axis=-1, keepdims=True)
        return m, idx.astype(I32)

    is_grp = jnp.logical_and(lane >= N_EXPERTS, lane < N_EXPERTS + N_GROUPS)
    gl = jnp.where(is_grp, logits, -jnp.inf)
    gmax, gidx = first_argmax(gl)
    p_grp = 1.0 / jnp.sum(jnp.exp(gl - gmax), axis=-1, keepdims=True)
    grp = gidx - N_EXPERTS
    in_grp = jnp.logical_and(lane < N_EXPERTS, lane // EXPERTS_PER_GROUP == grp)
    el = jnp.where(in_grp, logits, -jnp.inf)
    v1, e1 = first_argmax(el)
    v2, e2 = first_argmax(jnp.where(lane == e1, -jnp.inf, el))
    t2 = jnp.exp(v2 - v1)
    den = 1.0 + t2
    g1 = (1.0 / den) * p_grp
    g2 = (t2 / den) * p_grp

    oh1 = lane == e1
    oh2 = lane == e2
    oh = jnp.where(jnp.logical_or(oh1, oh2), 1.0, 0.0)
    before = _dot(tstrict, oh.astype(BF16)) + cnt_scr[...]
    rank1 = jnp.sum(jnp.where(oh1, before, 0.0), axis=-1, keepdims=True)
    rank2 = jnp.sum(jnp.where(oh2, before, 0.0), axis=-1, keepdims=True)
    cnt_scr[...] = cnt_scr[...] + jnp.sum(oh, axis=0, keepdims=True)
    cnt_o[...] = cnt_scr[...]

    slab = jnp.zeros((rows, LANES), F32)
    for idx, val in enumerate((e1.astype(F32), e2.astype(F32), g1, g2, rank1, rank2)):
        slab = jnp.where(lane == idx, val, slab)
    route_o[...] = slab


def _out_ab_body(hp_ref, hs_ref, oap_ref, oas_ref, obp_ref, obs_ref, gb_ref, og_ref, bd_ref, w_ref,
                 gf_ref, wrh_ref, wrl_ref, br_ref, ts_ref, h_o, xn_o, route_o, cnt_o, cnt_scr,
                 *, tiles_p):
    @pl.when(pl.program_id(0) == 0)
    def _():
        cnt_scr[...] = jnp.zeros_like(cnt_scr)

    ob = _group_pick(obp_ref, obs_ref, tiles_p)
    obn = ob * lax.rsqrt(_head_mean_sq(ob, bd_ref[...]) + RMS_EPS) * og_ref[...]
    obg = (obn * jax.nn.silu(gb_ref[...])).astype(BF16)
    oa = _group_pick(oap_ref, oas_ref, tiles_p)
    mix = _dot(oa, w_ref[0:D_HALF, :]) + _dot(obg, w_ref[D_HALF:, :])
    h1 = _group_pick(hp_ref, hs_ref, tiles_p) + mix
    h_o[...] = h1
    xn = _rms(h1, gf_ref[...])
    _store_row_tiles(xn_o, xn)
    _route(xn, wrh_ref[...], wrl_ref[...], br_ref[...], ts_ref[...], cnt_scr, route_o, cnt_o)


def _out_ab(h_p, h_s, oa_p, oa_s, ob_p, ob_s, gb, og, bd, w16, gf, wrh, wrl, br, tstrict):
    tm = TOKEN_TILE
    n = h_p.shape[0] + h_s.shape[0]
    tiles_p = h_p.shape[0] // tm
    row = lambda w: pl.BlockSpec((tm, w), lambda i: (i, 0))
    return pl.pallas_call(
        functools.partial(_out_ab_body, tiles_p=tiles_p),
        out_shape=[jax.ShapeDtypeStruct((n, D_MODEL), F32), jax.ShapeDtypeStruct((n * ROW_TILE, LANES), F32),
                   jax.ShapeDtypeStruct((n, LANES), F32), jax.ShapeDtypeStruct((1, LANES), F32)],
        grid=(n // tm,),
        in_specs=_group_specs(D_MODEL, tiles_p) + _group_specs(D_HALF, tiles_p)
        + _group_specs(D_HALF, tiles_p) + [
            row(D_HALF), _const_spec((1, D_HALF)),
            _const_spec((D_HALF, D_HALF)), _const_spec((D_MODEL, D_MODEL)),
            _const_spec((1, D_MODEL)), _const_spec((D_MODEL, LANES)),
            _const_spec((D_MODEL, LANES)), _const_spec((1, LANES)), _const_spec((tm, tm))],
        out_specs=[row(D_MODEL), pl.BlockSpec((tm * ROW_TILE, LANES), lambda i: (i, 0)), row(LANES),
                   _const_spec((1, LANES))],
        scratch_shapes=[pltpu.VMEM((1, LANES), F32)],
        compiler_params=_cparams("arbitrary"),
        name="out_ab",
    )(h_p, h_s, oa_p, oa_s, ob_p, ob_s, gb, og, bd, w16, gf, wrh, wrl, br, tstrict)


def _row_copy(src, src_row, dst, dst_row, sem):
    s0 = pl.multiple_of(src_row * ROW_TILE, ROW_TILE)
    d0 = pl.multiple_of(dst_row * ROW_TILE, ROW_TILE)
    return pltpu.make_async_copy(src.at[pl.ds(s0, ROW_TILE)], dst.at[pl.ds(d0, ROW_TILE)], sem)


def _gather_rows_body(idx_ref, nt_ref, src_hbm, o_ref, buf, sems, *, tile):
    i = pl.program_id(0)
    nt = nt_ref[0]

    def issue(t):
        slot = t % 2
        base = t * tile

        def one(r, carry):
            _row_copy(src_hbm, idx_ref[base + r], buf.at[slot], r, sems.at[slot]).start()
            return carry

        lax.fori_loop(0, tile, one, 0, unroll=GATHER_UNROLL)

    @pl.when(jnp.logical_and(i == 0, nt > 0))
    def _():
        issue(0)

    @pl.when(i + 1 < nt)
    def _():
        issue(i + 1)

    @pl.when(i < nt)
    def _():
        slot = i % 2
        for _ in range(tile):
            _row_copy(src_hbm, 0, buf.at[slot], 0, sems.at[slot]).wait()
        o_ref[...] = buf[slot]

    @pl.when(i >= nt)
    def _():
        o_ref[...] = jnp.zeros_like(o_ref)


def _gather_rows(idx, n_tiles, src, n_out):
    tm = MOE_TILE
    return pl.pallas_call(
        functools.partial(_gather_rows_body, tile=tm),
        out_shape=jax.ShapeDtypeStruct((n_out * ROW_TILE, LANES), F32),
        grid_spec=pltpu.PrefetchScalarGridSpec(
            num_scalar_prefetch=2,
            grid=(n_out // tm,),
            in_specs=[pl.BlockSpec(memory_space=pl.ANY)],
            out_specs=pl.BlockSpec((tm * ROW_TILE, LANES), lambda i, idx, nt: (i, 0)),
            scratch_shapes=[pltpu.VMEM((2, tm * ROW_TILE, LANES), F32), pltpu.SemaphoreType.DMA((2,))],
        ),
        compiler_params=pltpu.CompilerParams(dimension_semantics=("arbitrary",)),
        name="moe_gather",
    )(idx, n_tiles, src)


def _scatter_rows_body(pos_ref, pad0_ref, padn_ref, nt_ref, x_ref, xs_hbm, zero_scr, sem, pad_sem,
                       *, tile, n_tok, out_tiles):
    i = pl.program_id(0)
    base = i * tile

    def issue(r, carry):
        tok = base + r
        _row_copy(x_ref, r, xs_hbm, pos_ref[tok], sem).start()
        _row_copy(x_ref, r, xs_hbm, pos_ref[n_tok + tok], sem).start()
        return carry

    lax.fori_loop(0, tile, issue, 0, unroll=GATHER_UNROLL)

    @pl.when(i == pl.num_programs(0) - 1)
    def _():
        zero_scr[...] = jnp.zeros_like(zero_scr)

        def fill(lo, count):
            def one(r, carry):
                _row_copy(zero_scr, 0, xs_hbm, lo + r, pad_sem).start()
                return carry

            def done(r, carry):
                _row_copy(zero_scr, 0, xs_hbm, 0, pad_sem).wait()
                return carry

            lax.fori_loop(0, count, one, 0)
            lax.fori_loop(0, count, done, 0)

        for e in range(N_EXPERTS):
            fill(pad0_ref[e], padn_ref[e])

        def tail_copy(t):
            rows = MOE_TILE * ROW_TILE
            return pltpu.make_async_copy(
                zero_scr, xs_hbm.at[pl.ds(pl.multiple_of(t * rows, rows), rows)], pad_sem)

        def tail_start(t, carry):
            tail_copy(t).start()
            return carry

        def tail_done(t, carry):
            tail_copy(t).wait()
            return carry

        lax.fori_loop(nt_ref[0], out_tiles, tail_start, 0)
        lax.fori_loop(nt_ref[0], out_tiles, tail_done, 0)

    for _ in range(2 * tile):
        _row_copy(x_ref, 0, xs_hbm, 0, sem).wait()


def _scatter_rows(pos, pad0, padn, n_tiles, xn, n_rows):
    tm = TOKEN_TILE
    n = xn.shape[0] // ROW_TILE
    return pl.pallas_call(
        functools.partial(_scatter_rows_body, tile=tm, n_tok=n, out_tiles=n_rows // MOE_TILE),
        out_shape=jax.ShapeDtypeStruct((n_rows * ROW_TILE, LANES), F32),
        grid_spec=pltpu.PrefetchScalarGridSpec(
            num_scalar_prefetch=4,
            grid=(n // tm,),
            in_specs=[pl.BlockSpec((tm * ROW_TILE, LANES), lambda i, p, a, b, t: (i, 0))],
            out_specs=pl.BlockSpec(memory_space=pl.ANY),
            scratch_shapes=[pltpu.VMEM((MOE_TILE * ROW_TILE, LANES), F32),
                            pltpu.SemaphoreType.DMA(()), pltpu.SemaphoreType.DMA(())],
        ),
        compiler_params=pltpu.CompilerParams(dimension_semantics=("arbitrary",)),
        name="moe_scatter",
    )(pos, pad0, padn, n_tiles, xn)


def _experts_body(te_ref, nt_ref, x_ref, w1_ref, w3_ref, w2_ref, y_ref, w1_s, w3_s, w2_s):
    i = pl.program_id(0)
    prev = te_ref[jnp.maximum(i - 1, 0)]
    fresh = jnp.logical_or(i == 0, te_ref[i] != prev)

    @pl.when(fresh)
    def _():
        w1_s[...] = w1_ref[0].astype(BF16)
        w3_s[...] = w3_ref[0].astype(BF16)
        w2_s[...] = w2_ref[0].astype(BF16)

    @pl.when(i < nt_ref[0])
    def _():
        x = _load_row_tiles(x_ref, MOE_TILE).astype(BF16)
        a = _dot(x, w1_s[...])
        b = _dot(x, w3_s[...])
        _store_row_tiles(y_ref, _dot((jax.nn.silu(a) * b).astype(BF16), w2_s[...]))

    @pl.when(i >= nt_ref[0])
    def _():
        y_ref[...] = jnp.zeros_like(y_ref)


def _experts(tile_expert, n_tiles, xs, w1, w3, w2):
    rows = xs.shape[0] // ROW_TILE
    tm = MOE_TILE
    return pl.pallas_call(
        _experts_body,
        out_shape=jax.ShapeDtypeStruct((rows * ROW_TILE, LANES), F32),
        grid_spec=pltpu.PrefetchScalarGridSpec(
            num_scalar_prefetch=2,
            grid=(rows // tm,),
            in_specs=[pl.BlockSpec((tm * ROW_TILE, LANES),
                                   lambda i, te, nt: (jnp.minimum(i, nt[0] - 1), 0)),
                      pl.BlockSpec((1, D_MODEL, D_EXPERT), lambda i, te, nt: (te[i], 0, 0)),
                      pl.BlockSpec((1, D_MODEL, D_EXPERT), lambda i, te, nt: (te[i], 0, 0)),
                      pl.BlockSpec((1, D_EXPERT, D_MODEL), lambda i, te, nt: (te[i], 0, 0))],
            out_specs=pl.BlockSpec((tm * ROW_TILE, LANES), lambda i, te, nt: (i, 0)),
            scratch_shapes=[pltpu.VMEM((D_MODEL, D_EXPERT), BF16), pltpu.VMEM((D_MODEL, D_EXPERT), BF16),
                            pltpu.VMEM((D_EXPERT, D_MODEL), BF16)],
        ),
        compiler_params=_cparams("arbitrary"),
        name="moe_experts",
    )(tile_expert, n_tiles, xs, w1, w3, w2)


def _moe_plan(route, counts):
    e = route[:, 0:2].astype(I32)
    rank = route[:, 4:6].astype(I32)
    counts = counts[0, :N_EXPERTS].astype(I32)
    padded = ((counts + MOE_TILE - 1) // MOE_TILE) * MOE_TILE
    ends = jnp.cumsum(padded)
    offs = ends - padded
    pos = jnp.take(offs, e) + rank
    pos_flat = jnp.concatenate([pos[:, 0], pos[:, 1]])
    n = route.shape[0]
    max_tiles = (2 * n + N_EXPERTS * (MOE_TILE - 1) + MOE_TILE - 1) // MOE_TILE
    tile_start = jnp.arange(max_tiles, dtype=I32) * MOE_TILE
    tile_expert = jnp.minimum(jnp.sum((tile_start[:, None] >= ends[None, :]).astype(I32), axis=1),
                              N_EXPERTS - 1)
    n_tiles = (ends[-1] // MOE_TILE).reshape(1)
    return pos_flat, offs + counts, padded - counts, tile_expert, n_tiles, max_tiles * MOE_TILE


def _moe(xn, route, counts, w1, w3, w2):
    n = xn.shape[0] // ROW_TILE
    pos, pad0, padn, tile_expert, n_tiles, n_rows = _moe_plan(route, counts)
    xs = _scatter_rows(pos, pad0, padn, n_tiles, xn, n_rows)
    ys = _experts(tile_expert, n_tiles, xs, w1, w3, w2)
    all_tiles = jnp.full((1,), 2 * n // MOE_TILE, I32)
    return _gather_rows(pos, all_tiles, ys, 2 * n).reshape(2, n * ROW_TILE, LANES)


def _add_moe(h, ya_ref, yb_ref, route):
    rows = h.shape[0]
    ya = _load_row_tiles(ya_ref.at[0], rows)
    yb = _load_row_tiles(yb_ref.at[0], rows)
    return h + (route[:, 2:3] * ya + route[:, 3:4] * yb)


def _moe_out_spec(k, tm):
    return pl.BlockSpec((1, tm * ROW_TILE, LANES), lambda i: (k, i, 0))


def _proj_c_body(h_ref, ya_ref, yb_ref, r_ref, g_ref, w_ref, h_o, u_o, bg_o):
    h2 = _add_moe(h_ref[...], ya_ref, yb_ref, r_ref[...])
    h_o[...] = h2
    xb = _rms(h2, g_ref[...]).astype(BF16)
    bg_o[...] = _dot(xb, w_ref[:, 0:D_MODEL])
    u_o[...] = _dot(xb, w_ref[:, D_MODEL:2 * D_MODEL]) * _dot(xb, w_ref[:, 2 * D_MODEL:])


def _proj_c(h, y2, route, gain, w16):
    n = h.shape[0]
    tm = TOKEN_TILE
    row = lambda w: pl.BlockSpec((tm, w), lambda i: (i, 0))
    ysp = lambda k: _moe_out_spec(k, tm)
    return pl.pallas_call(
        _proj_c_body,
        out_shape=[jax.ShapeDtypeStruct((n, D_MODEL), F32)] * 3,
        grid=(n // tm,),
        in_specs=[row(D_MODEL), ysp(0), ysp(1), row(LANES), _const_spec((1, D_MODEL)),
                  _const_spec(w16.shape)],
        out_specs=[row(D_MODEL)] * 3,
        compiler_params=_cparams("parallel"),
        name="proj_c",
    )(h, y2, y2, route, gain, w16)


def _conv_out_body(u_ref, up_ref, stp_ref, sts_ref, bg_ref, h_ref, cw_ref, w_ref, gf_ref,
                   wrh_ref, wrl_ref, br_ref, ts_ref, h_o, xn_o, route_o, cnt_o, cnt_scr,
                   *, tile, tiles_p, seq_p, seq_s):
    i = pl.program_id(0)

    @pl.when(i == 0)
    def _():
        cnt_scr[...] = jnp.zeros_like(cnt_scr)

    u = u_ref[...]
    rowi = lax.broadcasted_iota(I32, (tile, D_MODEL), 0)
    is_p = i < tiles_p
    at_start = (i * tile) % seq_p == 0
    stp = stp_ref[0]
    prev = up_ref[...]
    m2_p = jnp.where(at_start, stp[0:1, :], prev[6:7, :])
    m1_p = jnp.where(at_start, stp[1:2, :], prev[7:8, :])
    per = tile // seq_s
    sts = sts_ref[...]
    m2_s = jnp.broadcast_to(sts[:, 0:1, :], (per, seq_s, D_MODEL)).reshape(tile, D_MODEL)
    m1_s = jnp.broadcast_to(sts[:, 1:2, :], (per, seq_s, D_MODEL)).reshape(tile, D_MODEL)
    m2 = jnp.where(is_p, m2_p, m2_s)
    m1 = jnp.where(is_p, m1_p, m1_s)
    pos = jnp.where(is_p, rowi, rowi % seq_s)
    u1 = jnp.where(pos == 0, m1, pltpu.roll(u, 1, axis=0))
    u2 = jnp.where(pos == 0, m2, jnp.where(pos == 1, m1, pltpu.roll(u, 2, axis=0)))
    cw = cw_ref[...]
    conv = u2 * cw[0:1, :] + u1 * cw[1:2, :] + u * cw[2:3, :]
    mix = _dot((bg_ref[...] * conv).astype(BF16), w_ref[...])
    h3 = h_ref[...] + mix
    h_o[...] = h3
    xn = _rms(h3, gf_ref[...])
    _store_row_tiles(xn_o, xn)
    _route(xn, wrh_ref[...], wrl_ref[...], br_ref[...], ts_ref[...], cnt_scr, route_o, cnt_o)


def _conv_out(u, bg, h, state_p, state_s, cw, w16, gf, wrh, wrl, br, tstrict, seq_p, seq_s):
    tm = TOKEN_TILE
    n = u.shape[0]
    tiles_p = state_p.shape[0] * seq_p // tm
    per = tm // seq_s
    row = lambda w: pl.BlockSpec((tm, w), lambda i: (i, 0))
    prev = pl.BlockSpec((8, D_MODEL), lambda i: (jnp.maximum(i * (tm // 8) - 1, 0), 0))
    stp_spec = pl.BlockSpec((1, 2, D_MODEL),
                            lambda i: (jnp.minimum(i, tiles_p - 1) * tm // seq_p, 0, 0))
    sts_spec = pl.BlockSpec((per, 2, D_MODEL), lambda i: (jnp.maximum(i - tiles_p, 0), 0, 0))
    return pl.pallas_call(
        functools.partial(_conv_out_body, tile=tm, tiles_p=tiles_p, seq_p=seq_p, seq_s=seq_s),
        out_shape=[jax.ShapeDtypeStruct((n, D_MODEL), F32),
                   jax.ShapeDtypeStruct((n * ROW_TILE, LANES), F32),
                   jax.ShapeDtypeStruct((n, LANES), F32), jax.ShapeDtypeStruct((1, LANES), F32)],
        grid=(n // tm,),
        in_specs=[row(D_MODEL), prev, stp_spec, sts_spec, row(D_MODEL), row(D_MODEL),
                  _const_spec((3, D_MODEL)),
                  _const_spec((D_MODEL, D_MODEL)), _const_spec((1, D_MODEL)),
                  _const_spec((D_MODEL, LANES)), _const_spec((D_MODEL, LANES)),
                  _const_spec((1, LANES)), _const_spec((tm, tm))],
        out_specs=[row(D_MODEL), pl.BlockSpec((tm * ROW_TILE, LANES), lambda i: (i, 0)), row(LANES),
                   _const_spec((1, LANES))],
        scratch_shapes=[pltpu.VMEM((1, LANES), F32)],
        compiler_params=_cparams("arbitrary"),
        name="conv_out",
    )(u, u, state_p, state_s, bg, h, cw, w16, gf, wrh, wrl, br, tstrict)


def _combine_body(h_ref, ya_ref, yb_ref, r_ref, op_ref, os_ref, *, tiles_p):
    out = _add_moe(h_ref[...], ya_ref, yb_ref, r_ref[...])

    @pl.when(pl.program_id(0) < tiles_p)
    def _():
        op_ref[...] = out

    @pl.when(pl.program_id(0) >= tiles_p)
    def _():
        os_ref[...] = out


def _combine(h, y2, route, n_p):
    n = h.shape[0]
    tm = TOKEN_TILE
    tiles_p = n_p // tm
    row = lambda w: pl.BlockSpec((tm, w), lambda i: (i, 0))
    ysp = lambda k: _moe_out_spec(k, tm)
    return pl.pallas_call(
        functools.partial(_combine_body, tiles_p=tiles_p),
        out_shape=[jax.ShapeDtypeStruct((n_p, D_MODEL), F32), jax.ShapeDtypeStruct((n - n_p, D_MODEL), F32)],
        grid=(n // tm,),
        in_specs=[row(D_MODEL), ysp(0), ysp(1), row(LANES)],
        out_specs=_group_specs(D_MODEL, tiles_p),
        compiler_params=_cparams("arbitrary"),
        name="moe_combine",
    )(h, y2, y2, route)


def _rel_bias_toeplitz(rel_bias, rows, cols, lead):
    period = rows + cols
    k = jnp.arange(period)
    d = jnp.where(k < cols, k, k - period)
    idx = jnp.clip(lead - d, -REL_CLIP, REL_CLIP) + REL_CLIP
    v = jnp.take(rel_bias.astype(F32), idx, axis=1)
    heads = v.shape[0]
    skew = jnp.tile(v, (1, rows))[:, :rows * (period - 1)].reshape(heads, rows, period - 1)
    return skew[:, :, :cols]


def _bias_prompt(rel_bias):
    r = jnp.arange(ATTN_TILE)[:, None]
    c = jnp.arange(ATTN_WINDOW)[None, :]
    j = c // CHUNK - r // CHUNK
    band = jnp.logical_and(j >= 0, j <= BAND_CHUNKS)
    bias = _rel_bias_toeplitz(rel_bias, ATTN_TILE, ATTN_WINDOW, BAND_CHUNKS * CHUNK)
    return jnp.where(band[None], bias, NEG_INF)


def _bias_sample(rel_bias, la, tn):
    bias = _rel_bias_toeplitz(rel_bias, tn, la + tn, la)
    return bias[:, :, :la], bias[:, :, la:]


def _router_weights(w_group, b_group, w_expert, b_expert):
    pad = LANES - N_EXPERTS - N_GROUPS
    w = jnp.concatenate([w_expert, w_group, jnp.zeros((D_MODEL, pad), F32)], axis=1)
    b = jnp.concatenate([b_expert, b_group, jnp.zeros((pad,), F32)])[None, :].astype(F32)
    hi, lo = _split2(w.astype(F32))
    return hi, lo, b


def kernel(x_prompt, x_sample, cache_a_k, cache_a_v, state_hgrn, state_conv, norm_mix, norm_ffn,
           w_in_ab, w_out_ab, q_norm, k_norm, rel_bias, hgrn_lb_logits, hgrn_out_norm, w_in_c,
           conv_w, w_out_c, w_group, b_group, w_expert, b_expert, w1, w3, w2):
    batch, seq, d = x_prompt.shape
    nseq_s, tn, _ = x_sample.shape
    la = cache_a_k.shape[2]
    n_p = batch * seq
    n_s = nseq_s * tn
    n = n_p + n_s
    keep = min(BAND_CHUNKS * CHUNK, seq)

    x_p = x_prompt.reshape(n_p, d)
    x_s = x_sample.reshape(n_s, d)
    lb_all = jnp.cumsum(jax.nn.softmax(hgrn_lb_logits.astype(F32), axis=0), axis=0)
    head_avg = jnp.kron(jnp.eye(N_HEADS, dtype=F32),
                        jnp.full((HEAD_DIM, HEAD_DIM), 1.0 / HEAD_DIM, F32)).astype(BF16)
    t = jnp.arange(TOKEN_TILE)
    tstrict = (t[None, :] < t[:, None]).astype(BF16)
    row1 = lambda v: v.astype(F32).reshape(1, -1)
    tile8 = lambda v: jnp.tile(v.astype(F32), N_HEADS).reshape(1, -1)

    l = 0
    (q16, kf, vf, k16, v16, lf, kb, vb, qb, gb) = _proj_ab(
        x_p, x_s, row1(norm_mix[0]), w_in_ab[l].astype(BF16), tile8(q_norm[l]), tile8(k_norm[l]),
        row1(lb_all[l]), head_avg, seq, keep)

    oa_p = _attn_prompt(q16, k16, v16, _bias_prompt(rel_bias[l]), batch, seq)
    bias_c, bias_n = _bias_sample(rel_bias[l], la, tn)
    oa_s = _attn_sample(q16, k16, v16, cache_a_k[l].reshape(nseq_s, la, D_HALF),
                        cache_a_v[l].reshape(nseq_s, la, D_HALF), bias_c, bias_n, n_p, nseq_s, tn)

    zeros_state = jnp.zeros((batch, N_HEADS, HEAD_DIM, HEAD_DIM), F32)
    ob_p, st_p = _hgrn(qb, kb, vb, lf, zeros_state, 0, batch, seq, 64, 4)
    ob_s, st_s = _hgrn(qb, kb, vb, lf, jnp.swapaxes(state_hgrn[l].astype(F32), -1, -2),
                       n_p, nseq_s, tn, tn, 1)

    wrh, wrl, br = _router_weights(w_group[0], b_group[0], w_expert[0], b_expert[0])
    h1, xn1, route1, cnt1 = _out_ab(x_p, x_s, oa_p, oa_s, ob_p, ob_s, gb, row1(hgrn_out_norm[l]),
                                    head_avg, w_out_ab[l].astype(BF16), row1(norm_ffn[0]),
                                    wrh, wrl, br, tstrict)
    y1 = _moe(xn1, route1, cnt1, w1[0], w3[0], w2[0])

    h2, u, bg = _proj_c(h1, y1, route1, row1(norm_mix[1]), w_in_c[0].astype(BF16))
    wrh, wrl, br = _router_weights(w_group[1], b_group[1], w_expert[1], b_expert[1])
    h3, xn2, route2, cnt2 = _conv_out(u, bg, h2, jnp.zeros((batch, 2, d), F32),
                                      state_conv[0].astype(F32), conv_w[0].astype(F32),
                                      w_out_c[0].astype(BF16), row1(norm_ffn[1]), wrh, wrl, br,
                                      tstrict, seq, tn)
    y2 = _moe(xn2, route2, cnt2, w1[1], w3[1], w2[1])
    out_p, out_s = _combine(h3, y2, route2, n_p)

    y_prompt = out_p.reshape(batch, seq, d)
    y_sample = out_s.reshape(nseq_s, tn, d)
    n_kp = batch * keep
    heads = lambda a, b_, t_: a.reshape(b_, t_, N_HEADS, HEAD_DIM)
    nk_p = heads(kf[:n_kp], batch, keep)[None]
    nv_p = heads(vf[:n_kp], batch, keep)[None]
    kf_s = heads(kf[n_kp:n_kp + n_s], nseq_s, tn)
    vf_s = heads(vf[n_kp:n_kp + n_s], nseq_s, tn)
    nk_s = jnp.concatenate([cache_a_k[l].astype(F32), kf_s], axis=1)[:, -la:][None]
    nv_s = jnp.concatenate([cache_a_v[l].astype(F32), vf_s], axis=1)[:, -la:][None]
    nh_p = jnp.swapaxes(st_p, -1, -2)[None]
    nh_s = jnp.swapaxes(st_s, -1, -2)[None]
    nc_p = jnp.stack([u[(b + 1) * seq - 2:(b + 1) * seq] for b in range(batch)])[None]
    u_s = u[n_p:].reshape(nseq_s, tn, d)
    nc_s = jnp.concatenate([state_conv[0].astype(F32), u_s], axis=1)[:, -2:][None]
    return (y_prompt, y_sample, nk_p, nv_p, nk_s, nv_s, nh_p, nh_s, nc_p, nc_s)
```

```python
import functools

import jax
import jax.numpy as jnp
from jax import lax
from jax.experimental import pallas as pl
from jax.experimental.pallas import tpu as pltpu

F32 = jnp.float32
BF16 = jnp.bfloat16
I32 = jnp.int32

D_MODEL = 1024
CHUNK = 64
BAND_CHUNKS = 8
HEAD_DIM = 64
N_HEADS = 8
D_HALF = N_HEADS * HEAD_DIM
REL_CLIP = 128
HGRN_BLOCK = 16
N_GROUPS = 4
EXPERTS_PER_GROUP = 8
N_EXPERTS = N_GROUPS * EXPERTS_PER_GROUP
D_EXPERT = 256
RMS_EPS = 1e-6
NEG_INF = -1e30

LANES = 128
ROW_TILE = D_MODEL // LANES
TOKEN_TILE = 256
ATTN_TILE = 256
ATTN_WINDOW = ATTN_TILE + BAND_CHUNKS * CHUNK
ATTN_SPAN = -(-(BAND_CHUNKS + 1) * CHUNK // LANES) * LANES
ATTN_HEAD_GROUP = 4
MOE_TILE = 256
GATHER_UNROLL = 8
VMEM_LIMIT = 48 * 1024 * 1024


def _cparams(*sem):
    return pltpu.CompilerParams(dimension_semantics=sem, vmem_limit_bytes=VMEM_LIMIT)


def _const_spec(shape):
    nd = len(shape)
    return pl.BlockSpec(shape, lambda *_: (0,) * nd)


def _store_row_tiles(ref, val):
    rows = val.shape[0]
    for c in range(ROW_TILE):
        ref[pl.ds(c, rows, stride=ROW_TILE), :] = val[:, c * LANES:(c + 1) * LANES]


def _load_row_tiles(ref, rows):
    return jnp.concatenate([ref[pl.ds(c, rows, stride=ROW_TILE), :] for c in range(ROW_TILE)], axis=1)


def _rms(x, gain):
    ms = jnp.mean(x * x, axis=-1, keepdims=True)
    return (x * lax.rsqrt(ms + RMS_EPS)) * gain


def _split2(x):
    hi = x.astype(BF16)
    lo = (x - hi.astype(F32)).astype(BF16)
    return hi, lo


def _split3(x):
    p0 = x.astype(BF16)
    r = x - p0.astype(F32)
    p1 = r.astype(BF16)
    p2 = (r - p1.astype(F32)).astype(BF16)
    return p0, p1, p2


def _dot(a, b):
    return jnp.dot(a, b, preferred_element_type=F32)


def _dot_nt(a, b):
    return lax.dot_general(a, b, (((1,), (1,)), ((), ())), preferred_element_type=F32)


def _dot_tn(a, b):
    return lax.dot_general(a, b, (((0,), (0,)), ((), ())), preferred_element_type=F32)


def _head_mean_sq(v, bd):
    hi, lo = _split2(v * v)
    return _dot(hi, bd) + _dot(lo, bd)


def _group_specs(width, tiles_p, tm=TOKEN_TILE):
    return [pl.BlockSpec((tm, width), lambda i: (jnp.minimum(i, tiles_p - 1), 0)),
            pl.BlockSpec((tm, width), lambda i: (jnp.maximum(i - tiles_p, 0), 0))]


def _group_pick(p_ref, s_ref, tiles_p):
    return jnp.where(pl.program_id(0) < tiles_p, p_ref[...], s_ref[...])


def _proj_ab_body(xp_ref, xs_ref, g_ref, w_ref, qg_ref, kg_ref, lb_ref, bd_ref,
                  q_o, kf_o, vf_o, k16_o, v16_o, lf_o, kb_o, vb_o, qb_o, gb_o, *, tiles_p):
    xb = _rms(_group_pick(xp_ref, xs_ref, tiles_p), g_ref[...]).astype(BF16)
    bd = bd_ref[...]

    def seg(j):
        return _dot(xb, w_ref[:, j * D_HALF:(j + 1) * D_HALF])

    qa = seg(0)
    qn = qa * lax.rsqrt(_head_mean_sq(qa, bd) + RMS_EPS) * qg_ref[...]
    q_o[...] = (qn * (HEAD_DIM ** -0.5)).astype(BF16)
    ka = seg(1)
    kn = ka * lax.rsqrt(_head_mean_sq(ka, bd) + RMS_EPS) * kg_ref[...]
    kf_o[...] = kn
    k16_o[...] = kn.astype(BF16)
    va = seg(2)
    vf_o[...] = va
    v16_o[...] = va.astype(BF16)
    lb = lb_ref[...]
    f = lb + (1.0 - lb) * jax.nn.sigmoid(seg(3))
    lf_o[...] = jnp.log(f)
    kb_o[...] = 1.0 - f
    vb_o[...] = seg(4)
    qb_o[...] = jax.nn.silu(seg(5))
    gb_o[...] = seg(6)


def _proj_ab(x_p, x_s, gain, w16, qg, kg, lb, bd, seq, keep):
    tm = TOKEN_TILE
    n_p, n_s = x_p.shape[0], x_s.shape[0]
    n = n_p + n_s
    tiles_p, tiles_seq, tiles_keep = n_p // tm, seq // tm, keep // tm
    kept_tiles = (n_p // seq) * tiles_keep + n_s // tm

    def keep_map(i):
        b, j = i // tiles_seq, i % tiles_seq
        prompt_slot = b * tiles_keep + jnp.maximum(j - (tiles_seq - tiles_keep), 0)
        return (jnp.where(i >= tiles_p, kept_tiles - n_s // tm + (i - tiles_p), prompt_slot), 0)

    row = pl.BlockSpec((tm, D_HALF), lambda i: (i, 0))
    kept = pl.BlockSpec((tm, D_HALF), keep_map)
    full = lambda dt: jax.ShapeDtypeStruct((n, D_HALF), dt)
    kept_shape = jax.ShapeDtypeStruct((kept_tiles * tm, D_HALF), F32)
    return pl.pallas_call(
        functools.partial(_proj_ab_body, tiles_p=tiles_p),
        out_shape=[full(BF16), kept_shape, kept_shape, full(BF16), full(BF16),
                   full(F32), full(F32), full(F32), full(F32), full(F32)],
        grid=(n // tm,),
        in_specs=_group_specs(D_MODEL, tiles_p) + [
            _const_spec((1, D_MODEL)), _const_spec(w16.shape),
            _const_spec((1, D_HALF)), _const_spec((1, D_HALF)), _const_spec((1, D_HALF)),
            _const_spec((D_HALF, D_HALF))],
        out_specs=[row, kept, kept] + [row] * 7,
        compiler_params=_cparams("arbitrary"),
        name="proj_ab",
    )(x_p, x_s, gain, w16, qg, kg, lb, bd)


def _attn_prompt_body(q_ref, k0, k1, k2, v0, v1, v2, bias_ref, o_ref):
    i = pl.program_id(1)
    lim = jnp.maximum(2 - i, 0) * ATTN_TILE
    col = lax.broadcasted_iota(I32, (CHUNK, ATTN_SPAN), 1)
    cut = ATTN_SPAN - 2 * ATTN_TILE

    def span(parts, start):
        if start == 0:
            return (parts[0], parts[1], parts[2][:cut])
        return (parts[0][start:], parts[1], parts[2])

    for h0 in range(0, N_HEADS, ATTN_HEAD_GROUP):
        units = []
        for h in range(h0, h0 + ATTN_HEAD_GROUP):
            hs = slice(h * HEAD_DIM, (h + 1) * HEAD_DIM)
            kh = (k0[:, hs], k1[:, hs], k2[:, hs])
            vh = (v0[:, hs], v1[:, hs], v2[:, hs])
            for r in range(ATTN_TILE // CHUNK):
                start = (r * CHUNK // LANES) * LANES
                units.append((h, hs, slice(r * CHUNK, (r + 1) * CHUNK), start,
                              span(kh, start), span(vh, start)))
        s = [jnp.concatenate([_dot_nt(q_ref[rows, hs], kp) for kp in ks], axis=1)
             for (_, hs, rows, _, ks, _) in units]
        s = [jnp.where(col + start < lim, NEG_INF,
                       s_u + bias_ref[h, rows, start:start + ATTN_SPAN])
             for s_u, (h, _, rows, start, _, _) in zip(s, units)]
        m = [jnp.max(s_u, axis=-1, keepdims=True) for s_u in s]
        p = [jnp.exp(s_u - m_u) for s_u, m_u in zip(s, m)]
        l = [jnp.sum(p_u, axis=-1, keepdims=True) for p_u in p]
        outs = []
        for p_u, (_, _, _, _, _, vs) in zip(p, units):
            p16 = p_u.astype(BF16)
            o, at = 0.0, 0
            for vp in vs:
                o = o + _dot(p16[:, at:at + vp.shape[0]], vp)
                at += vp.shape[0]
            outs.append(o)
        for o_u, l_u, (_, hs, rows, _, _, _) in zip(outs, l, units):
            o_ref[rows, hs] = (o_u / l_u).astype(BF16)


def _attn_prompt(q16, k16, v16, bias_full, batch, seq):
    tiles = seq // ATTN_TILE
    qspec = pl.BlockSpec((ATTN_TILE, D_HALF), lambda b, i: (b * tiles + i, 0))

    def kv(back):
        return pl.BlockSpec((ATTN_TILE, D_HALF),
                            lambda b, i: (b * tiles + jnp.maximum(i - back, 0), 0))

    return pl.pallas_call(
        _attn_prompt_body,
        out_shape=jax.ShapeDtypeStruct((batch * seq, D_HALF), BF16),
        grid=(batch, tiles),
        in_specs=[qspec, kv(2), kv(1), kv(0), kv(2), kv(1), kv(0),
                  _const_spec(bias_full.shape)],
        out_specs=qspec,
        compiler_params=_cparams("parallel", "parallel"),
        name="attn_prompt",
    )(q16, k16, k16, k16, v16, v16, v16, bias_full)


def _attn_sample_body(q_ref, kc_ref, vc_ref, kn_ref, vn_ref, bc_ref, bn_ref, o_ref):
    kc = kc_ref[0].astype(BF16)
    vc = vc_ref[0].astype(BF16)
    for h in range(N_HEADS):
        hs = slice(h * HEAD_DIM, (h + 1) * HEAD_DIM)
        qh = q_ref[:, hs]
        sc = _dot_nt(qh, kc[:, hs]) + bc_ref[h]
        sn = _dot_nt(qh, kn_ref[:, hs]) + bn_ref[h]
        m = jnp.maximum(jnp.max(sc, axis=-1, keepdims=True), jnp.max(sn, axis=-1, keepdims=True))
        pc = jnp.exp(sc - m)
        pn = jnp.exp(sn - m)
        l = jnp.sum(pc, axis=-1, keepdims=True) + jnp.sum(pn, axis=-1, keepdims=True)
        o = _dot(pc.astype(BF16), vc[:, hs]) + _dot(pn.astype(BF16), vn_ref[:, hs])
        o_ref[:, hs] = (o / l).astype(BF16)


def _attn_sample(q16, k16, v16, cache_k, cache_v, bias_c, bias_n, row0, nseq, tn):
    blk0 = row0 // tn
    la = cache_k.shape[1]
    new = pl.BlockSpec((tn, D_HALF), lambda b: (blk0 + b, 0))
    cache = pl.BlockSpec((1, la, D_HALF), lambda b: (b, 0, 0))
    return pl.pallas_call(
        _attn_sample_body,
        out_shape=jax.ShapeDtypeStruct((nseq * tn, D_HALF), BF16),
        grid=(nseq,),
        in_specs=[new, cache, cache, new, new, _const_spec(bias_c.shape), _const_spec(bias_n.shape)],
        out_specs=pl.BlockSpec((tn, D_HALF), lambda b: (b, 0)),
        compiler_params=_cparams("parallel"),
        name="attn_sample",
    )(q16, cache_k, cache_v, k16, v16, bias_c, bias_n)


def _hgrn_body(q_ref, k_ref, v_ref, lf_ref, s0_ref, tblk_ref, tfull_ref, o_ref, sout_ref, st_scr,
               *, chunk, nchunks):
    nblk = chunk // HGRN_BLOCK

    @pl.when(pl.program_id(1) == 0)
    def _():
        st_scr[...] = s0_ref[0]

    row = lax.broadcasted_iota(I32, (chunk, D_HALF), 0)
    r2 = lax.broadcasted_iota(I32, (chunk, chunk), 0)
    c2 = lax.broadcasted_iota(I32, (chunk, chunk), 1)
    same_blk_causal = jnp.logical_and(r2 // HGRN_BLOCK == c2 // HGRN_BLOCK, c2 <= r2)

    def one_chunk(c, carry):
        sl = pl.ds(pl.multiple_of(c * chunk, chunk), chunk)
        q = q_ref[sl, :]
        k = k_ref[sl, :]
        v16 = v_ref[sl, :].astype(BF16)
        l0, l1, l2 = _split3(lf_ref[sl, :])
        tb = tblk_ref[...]
        tf = tfull_ref[...]
        b_in = _dot(tb, l0) + _dot(tb, l1) + _dot(tb, l2)
        b_ch = _dot(tf, l0) + _dot(tf, l1) + _dot(tf, l2)
        ld = (q * jnp.exp(b_in)).astype(BF16)
        rd = (k * jnp.exp(-b_in)).astype(BF16)
        lj, rj = [], []
        for j in range(nblk - 1):
            e_j = b_ch[(j + 1) * HGRN_BLOCK - 1:(j + 1) * HGRN_BLOCK, :]
            later = row >= (j + 1) * HGRN_BLOCK
            inside = jnp.logical_and(row >= j * HGRN_BLOCK, row < (j + 1) * HGRN_BLOCK)
            lj.append(jnp.where(later, q * jnp.exp(jnp.minimum(b_ch - e_j, 0.0)), 0.0).astype(BF16))
            rj.append(jnp.where(inside, k * jnp.exp(jnp.minimum(e_j - b_ch, 0.0)), 0.0).astype(BF16))
        e_end = b_ch[chunk - 1:chunk, :]
        qc = (q * jnp.exp(b_ch)).astype(BF16)
        kc = (k * jnp.exp(e_end - b_ch)).astype(BF16)
        dec = jnp.exp(e_end)
        heads = [slice(h * HEAD_DIM, (h + 1) * HEAD_DIM) for h in range(N_HEADS)]
        sts = [st_scr[h] for h in range(N_HEADS)]
        diag = [_dot_nt(ld[:, hs], rd[:, hs]) for hs in heads]
        off = []
        for hs in heads:
            acc = None
            for j in range(nblk - 1):
                term = _dot_nt(lj[j][:, hs], rj[j][:, hs])
                acc = term if acc is None else acc + term
            off.append(acc)
        inter = [_dot_nt(qc[:, hs], st.astype(BF16)) for hs, st in zip(heads, sts)]
        upd = [_dot_tn(v16[:, hs], kc[:, hs]) for hs in heads]
        sc16 = [(jnp.where(same_blk_causal, d, 0.0) + o_).astype(BF16) for d, o_ in zip(diag, off)]
        intra = [_dot(s_, v16[:, hs]) for s_, hs in zip(sc16, heads)]
        o_ref[sl, :] = jnp.concatenate([a + b for a, b in zip(intra, inter)], axis=1)
        for h, hs in enumerate(heads):
            st_scr[h] = sts[h] * dec[:, hs] + upd[h]
        return carry

    lax.fori_loop(0, nchunks, one_chunk, 0)
    sout_ref[0] = st_scr[...]


def _hgrn(qb, kb, vb, lf, s0t, row0, nseq, seq, chunk, nchunks):
    tt = chunk * nchunks
    steps = seq // tt
    blk0 = row0 // tt
    t = jnp.arange(chunk)
    lower = t[None, :] <= t[:, None]
    tfull = lower.astype(BF16)
    tblk = jnp.logical_and(lower, (t[None, :] // HGRN_BLOCK) == (t[:, None] // HGRN_BLOCK)).astype(BF16)
    tok = pl.BlockSpec((tt, D_HALF), lambda b, j: (blk0 + b * steps + j, 0))
    state = pl.BlockSpec((1, N_HEADS, HEAD_DIM, HEAD_DIM), lambda b, j: (b, 0, 0, 0))
    return pl.pallas_call(
        functools.partial(_hgrn_body, chunk=chunk, nchunks=nchunks),
        out_shape=[jax.ShapeDtypeStruct((nseq * seq, D_HALF), F32),
                   jax.ShapeDtypeStruct(s0t.shape, F32)],
        grid=(nseq, steps),
        in_specs=[tok, tok, tok, tok, state, _const_spec((chunk, chunk)), _const_spec((chunk, chunk))],
        out_specs=[pl.BlockSpec((tt, D_HALF), lambda b, j: (b * steps + j, 0)), state],
        scratch_shapes=[pltpu.VMEM((N_HEADS, HEAD_DIM, HEAD_DIM), F32)],
        compiler_params=_cparams("parallel", "arbitrary"),
        name="hgrn",
    )(qb, kb, vb, lf, s0t, tblk, tfull)


def _route(xn, wr_hi, wr_lo, br, tstrict, cnt_scr, route_o, cnt_o):
    rows = xn.shape[0]
    x_hi, x_lo = _split2(xn)
    logits = _dot(x_hi, wr_hi) + _dot(x_lo, wr_hi) + _dot(x_hi, wr_lo) + br
    lane = lax.broadcasted_iota(I32, (rows, LANES), 1)
    lane_f = lane.astype(F32)

    def first_argmax(vals):
        m = jnp.max(vals, axis=-1, keepdims=True)
        idx = jnp.min(jnp.where(vals == m, lane_f, float(LANES)), axis=-1, keepdims=True)
        return m, idx.astype(I32)

    is_grp = jnp.logical_and(lane >= N_EXPERTS, lane < N_EXPERTS + N_GROUPS)
    gl = jnp.where(is_grp, logits, -jnp.inf)
    gmax, gidx = first_argmax(gl)
    p_grp = 1.0 / jnp.sum(jnp.exp(gl - gmax), axis=-1, keepdims=True)
    grp = gidx - N_EXPERTS
    in_grp = jnp.logical_and(lane < N_EXPERTS, lane // EXPERTS_PER_GROUP == grp)
    el = jnp.where(in_grp, logits, -jnp.inf)
    v1, e1 = first_argmax(el)
    v2, e2 = first_argmax(jnp.where(lane == e1, -jnp.inf, el))
    t2 = jnp.exp(v2 - v1)
    den = 1.0 + t2
    g1 = (1.0 / den) * p_grp
    g2 = (t2 / den) * p_grp

    oh1 = lane == e1
    oh2 = lane == e2
    oh = jnp.where(jnp.logical_or(oh1, oh2), 1.0, 0.0)
    before = _dot(tstrict, oh.astype(BF16)) + cnt_scr[...]
    rank1 = jnp.sum(jnp.where(oh1, before, 0.0), axis=-1, keepdims=True)
    rank2 = jnp.sum(jnp.where(oh2, before, 0.0), axis=-1, keepdims=True)
    cnt_scr[...] = cnt_scr[...] + jnp.sum(oh, axis=0, keepdims=True)
    cnt_o[...] = cnt_scr[...]

    slab = jnp.zeros((rows, LANES), F32)
    for idx, val in enumerate((e1.astype(F32), e2.astype(F32), g1, g2, rank1, rank2)):
        slab = jnp.where(lane == idx, val, slab)
    route_o[...] = slab


def _out_ab_body(hp_ref, hs_ref, oap_ref, oas_ref, obp_ref, obs_ref, gb_ref, og_ref, bd_ref, w_ref,
                 gf_ref, wrh_ref, wrl_ref, br_ref, ts_ref, h_o, xn_o, route_o, cnt_o, cnt_scr,
                 *, tiles_p):
    @pl.when(pl.program_id(0) == 0)
    def _():
        cnt_scr[...] = jnp.zeros_like(cnt_scr)

    ob = _group_pick(obp_ref, obs_ref, tiles_p)
    obn = ob * lax.rsqrt(_head_mean_sq(ob, bd_ref[...]) + RMS_EPS) * og_ref[...]
    obg = (obn * jax.nn.silu(gb_ref[...])).astype(BF16)
    oa = _group_pick(oap_ref, oas_ref, tiles_p)
    mix = _dot(oa, w_ref[0:D_HALF, :]) + _dot(obg, w_ref[D_HALF:, :])
    h1 = _group_pick(hp_ref, hs_ref, tiles_p) + mix
    h_o[...] = h1
    xn = _rms(h1, gf_ref[...])
    _store_row_tiles(xn_o, xn)
    _route(xn, wrh_ref[...], wrl_ref[...], br_ref[...], ts_ref[...], cnt_scr, route_o, cnt_o)


def _out_ab(h_p, h_s, oa_p, oa_s, ob_p, ob_s, gb, og, bd, w16, gf, wrh, wrl, br, tstrict):
    tm = TOKEN_TILE
    n = h_p.shape[0] + h_s.shape[0]
    tiles_p = h_p.shape[0] // tm
    row = lambda w: pl.BlockSpec((tm, w), lambda i: (i, 0))
    return pl.pallas_call(
        functools.partial(_out_ab_body, tiles_p=tiles_p),
        out_shape=[jax.ShapeDtypeStruct((n, D_MODEL), F32), jax.ShapeDtypeStruct((n * ROW_TILE, LANES), F32),
                   jax.ShapeDtypeStruct((n, LANES), F32), jax.ShapeDtypeStruct((1, LANES), F32)],
        grid=(n // tm,),
        in_specs=_group_specs(D_MODEL, tiles_p) + _group_specs(D_HALF, tiles_p)
        + _group_specs(D_HALF, tiles_p) + [
            row(D_HALF), _const_spec((1, D_HALF)),
            _const_spec((D_HALF, D_HALF)), _const_spec((D_MODEL, D_MODEL)),
            _const_spec((1, D_MODEL)), _const_spec((D_MODEL, LANES)),
            _const_spec((D_MODEL, LANES)), _const_spec((1, LANES)), _const_spec((tm, tm))],
        out_specs=[row(D_MODEL), pl.BlockSpec((tm * ROW_TILE, LANES), lambda i: (i, 0)), row(LANES),
                   _const_spec((1, LANES))],
        scratch_shapes=[pltpu.VMEM((1, LANES), F32)],
        compiler_params=_cparams("arbitrary"),
        name="out_ab",
    )(h_p, h_s, oa_p, oa_s, ob_p, ob_s, gb, og, bd, w16, gf, wrh, wrl, br, tstrict)


def _row_copy(src, src_row, dst, dst_row, sem):
    s0 = pl.multiple_of(src_row * ROW_TILE, ROW_TILE)
    d0 = pl.multiple_of(dst_row * ROW_TILE, ROW_TILE)
    return pltpu.make_async_copy(src.at[pl.ds(s0, ROW_TILE)], dst.at[pl.ds(d0, ROW_TILE)], sem)


def _gather_rows_body(idx_ref, nt_ref, src_hbm, o_ref, buf, sems, *, tile):
    i = pl.program_id(0)
    nt = nt_ref[0]

    def issue(t):
        slot = t % 2
        base = t * tile

        def one(r, carry):
            _row_copy(src_hbm, idx_ref[base + r], buf.at[slot], r, sems.at[slot]).start()
            return carry

        lax.fori_loop(0, tile, one, 0, unroll=GATHER_UNROLL)

    @pl.when(jnp.logical_and(i == 0, nt > 0))
    def _():
        issue(0)

    @pl.when(i + 1 < nt)
    def _():
        issue(i + 1)

    @pl.when(i < nt)
    def _():
        slot = i % 2
        for _ in range(tile):
            _row_copy(src_hbm, 0, buf.at[slot], 0, sems.at[slot]).wait()
        o_ref[...] = buf[slot]

    @pl.when(i >= nt)
    def _():
        o_ref[...] = jnp.zeros_like(o_ref)


def _gather_rows(idx, n_tiles, src, n_out):
    tm = MOE_TILE
    return pl.pallas_call(
        functools.partial(_gather_rows_body, tile=tm),
        out_shape=jax.ShapeDtypeStruct((n_out * ROW_TILE, LANES), F32),
        grid_spec=pltpu.PrefetchScalarGridSpec(
            num_scalar_prefetch=2,
            grid=(n_out // tm,),
            in_specs=[pl.BlockSpec(memory_space=pl.ANY)],
            out_specs=pl.BlockSpec((tm * ROW_TILE, LANES), lambda i, idx, nt: (i, 0)),
            scratch_shapes=[pltpu.VMEM((2, tm * ROW_TILE, LANES), F32), pltpu.SemaphoreType.DMA((2,))],
        ),
        compiler_params=pltpu.CompilerParams(dimension_semantics=("arbitrary",)),
        name="moe_gather",
    )(idx, n_tiles, src)


def _scatter_rows_body(pos_ref, pad0_ref, padn_ref, nt_ref, x_ref, xs_hbm, zero_scr, sem, pad_sem,
                       *, tile, n_tok, out_tiles):
    i = pl.program_id(0)
    base = i * tile

    def issue(r, carry):
        tok = base + r
        _row_copy(x_ref, r, xs_hbm, pos_ref[tok], sem).start()
        _row_copy(x_ref, r, xs_hbm, pos_ref[n_tok + tok], sem).start()
        return carry

    lax.fori_loop(0, tile, issue, 0, unroll=GATHER_UNROLL)

    @pl.when(i == pl.num_programs(0) - 1)
    def _():
        zero_scr[...] = jnp.zeros_like(zero_scr)

        def fill(lo, count):
            def one(r, carry):
                _row_copy(zero_scr, 0, xs_hbm, lo + r, pad_sem).start()
                return carry

            def done(r, carry):
                _row_copy(zero_scr, 0, xs_hbm, 0, pad_sem).wait()
                return carry

            lax.fori_loop(0, count, one, 0)
            lax.fori_loop(0, count, done, 0)

        for e in range(N_EXPERTS):
            fill(pad0_ref[e], padn_ref[e])

        def tail_copy(t):
            rows = MOE_TILE * ROW_TILE
            return pltpu.make_async_copy(
                zero_scr, xs_hbm.at[pl.ds(pl.multiple_of(t * rows, rows), rows)], pad_sem)

        def tail_start(t, carry):
            tail_copy(t).start()
            return carry

        def tail_done(t, carry):
            tail_copy(t).wait()
            return carry

        lax.fori_loop(nt_ref[0], out_tiles, tail_start, 0)
        lax.fori_loop(nt_ref[0], out_tiles, tail_done, 0)

    for _ in range(2 * tile):
        _row_copy(x_ref, 0, xs_hbm, 0, sem).wait()


def _scatter_rows(pos, pad0, padn, n_tiles, xn, n_rows):
    tm = TOKEN_TILE
    n = xn.shape[0] // ROW_TILE
    return pl.pallas_call(
        functools.partial(_scatter_rows_body, tile=tm, n_tok=n, out_tiles=n_rows // MOE_TILE),
        out_shape=jax.ShapeDtypeStruct((n_rows * ROW_TILE, LANES), F32),
        grid_spec=pltpu.PrefetchScalarGridSpec(
            num_scalar_prefetch=4,
            grid=(n // tm,),
            in_specs=[pl.BlockSpec((tm * ROW_TILE, LANES), lambda i, p, a, b, t: (i, 0))],
            out_specs=pl.BlockSpec(memory_space=pl.ANY),
            scratch_shapes=[pltpu.VMEM((MOE_TILE * ROW_TILE, LANES), F32),
                            pltpu.SemaphoreType.DMA(()), pltpu.SemaphoreType.DMA(())],
        ),
        compiler_params=pltpu.CompilerParams(dimension_semantics=("arbitrary",)),
        name="moe_scatter",
    )(pos, pad0, padn, n_tiles, xn)


def _experts_body(te_ref, nt_ref, x_ref, w1_ref, w3_ref, w2_ref, y_ref, w1_s, w3_s, w2_s):
    i = pl.program_id(0)
    prev = te_ref[jnp.maximum(i - 1, 0)]
    fresh = jnp.logical_or(i == 0, te_ref[i] != prev)

    @pl.when(fresh)
    def _():
        w1_s[...] = w1_ref[0].astype(BF16)
        w3_s[...] = w3_ref[0].astype(BF16)
        w2_s[...] = w2_ref[0].astype(BF16)

    @pl.when(i < nt_ref[0])
    def _():
        x = _load_row_tiles(x_ref, MOE_TILE).astype(BF16)
        a = _dot(x, w1_s[...])
        b = _dot(x, w3_s[...])
        _store_row_tiles(y_ref, _dot((jax.nn.silu(a) * b).astype(BF16), w2_s[...]))

    @pl.when(i >= nt_ref[0])
    def _():
        y_ref[...] = jnp.zeros_like(y_ref)


def _experts(tile_expert, n_tiles, xs, w1, w3, w2):
    rows = xs.shape[0] // ROW_TILE
    tm = MOE_TILE
    return pl.pallas_call(
        _experts_body,
        out_shape=jax.ShapeDtypeStruct((rows * ROW_TILE, LANES), F32),
        grid_spec=pltpu.PrefetchScalarGridSpec(
            num_scalar_prefetch=2,
            grid=(rows // tm,),
            in_specs=[pl.BlockSpec((tm * ROW_TILE, LANES),
                                   lambda i, te, nt: (jnp.minimum(i, nt[0] - 1), 0)),
                      pl.BlockSpec((1, D_MODEL, D_EXPERT), lambda i, te, nt: (te[i], 0, 0)),
                      pl.BlockSpec((1, D_MODEL, D_EXPERT), lambda i, te, nt: (te[i], 0, 0)),
                      pl.BlockSpec((1, D_EXPERT, D_MODEL), lambda i, te, nt: (te[i], 0, 0))],
            out_specs=pl.BlockSpec((tm * ROW_TILE, LANES), lambda i, te, nt: (i, 0)),
            scratch_shapes=[pltpu.VMEM((D_MODEL, D_EXPERT), BF16), pltpu.VMEM((D_MODEL, D_EXPERT), BF16),
                            pltpu.VMEM((D_EXPERT, D_MODEL), BF16)],
        ),
        compiler_params=_cparams("arbitrary"),
        name="moe_experts",
    )(tile_expert, n_tiles, xs, w1, w3, w2)


def _moe_plan(route, counts):
    e = route[:, 0:2].astype(I32)
    rank = route[:, 4:6].astype(I32)
    counts = counts[0, :N_EXPERTS].astype(I32)
    padded = ((counts + MOE_TILE - 1) // MOE_TILE) * MOE_TILE
    ends = jnp.cumsum(padded)
    offs = ends - padded
    pos = jnp.take(offs, e) + rank
    pos_flat = jnp.concatenate([pos[:, 0], pos[:, 1]])
    n = route.shape[0]
    max_tiles = (2 * n + N_EXPERTS * (MOE_TILE - 1) + MOE_TILE - 1) // MOE_TILE
    tile_start = jnp.arange(max_tiles, dtype=I32) * MOE_TILE
    tile_expert = jnp.minimum(jnp.sum((tile_start[:, None] >= ends[None, :]).astype(I32), axis=1),
                              N_EXPERTS - 1)
    n_tiles = (ends[-1] // MOE_TILE).reshape(1)
    return pos_flat, offs + counts, padded - counts, tile_expert, n_tiles, max_tiles * MOE_TILE


def _moe(xn, route, counts, w1, w3, w2):
    n = xn.shape[0] // ROW_TILE
    pos, pad0, padn, tile_expert, n_tiles, n_rows = _moe_plan(route, counts)
    xs = _scatter_rows(pos, pad0, padn, n_tiles, xn, n_rows)
    ys = _experts(tile_expert, n_tiles, xs, w1, w3, w2)
    all_tiles = jnp.full((1,), 2 * n // MOE_TILE, I32)
    return _gather_rows(pos, all_tiles, ys, 2 * n).reshape(2, n * ROW_TILE, LANES)


def _add_moe(h, ya_ref, yb_ref, route):
    rows = h.shape[0]
    ya = _load_row_tiles(ya_ref.at[0], rows)
    yb = _load_row_tiles(yb_ref.at[0], rows)
    return h + (route[:, 2:3] * ya + route[:, 3:4] * yb)


def _moe_out_spec(k, tm):
    return pl.BlockSpec((1, tm * ROW_TILE, LANES), lambda i: (k, i, 0))


def _proj_c_body(h_ref, ya_ref, yb_ref, r_ref, g_ref, w_ref, h_o, u_o, bg_o):
    h2 = _add_moe(h_ref[...], ya_ref, yb_ref, r_ref[...])
    h_o[...] = h2
    xb = _rms(h2, g_ref[...]).astype(BF16)
    bg_o[...] = _dot(xb, w_ref[:, 0:D_MODEL])
    u_o[...] = _dot(xb, w_ref[:, D_MODEL:2 * D_MODEL]) * _dot(xb, w_ref[:, 2 * D_MODEL:])


def _proj_c(h, y2, route, gain, w16):
    n = h.shape[0]
    tm = TOKEN_TILE
    row = lambda w: pl.BlockSpec((tm, w), lambda i: (i, 0))
    ysp = lambda k: _moe_out_spec(k, tm)
    return pl.pallas_call(
        _proj_c_body,
        out_shape=[jax.ShapeDtypeStruct((n, D_MODEL), F32)] * 3,
        grid=(n // tm,),
        in_specs=[row(D_MODEL), ysp(0), ysp(1), row(LANES), _const_spec((1, D_MODEL)),
                  _const_spec(w16.shape)],
        out_specs=[row(D_MODEL)] * 3,
        compiler_params=_cparams("parallel"),
        name="proj_c",
    )(h, y2, y2, route, gain, w16)


def _conv_out_body(u_ref, up_ref, stp_ref, sts_ref, bg_ref, h_ref, cw_ref, w_ref, gf_ref,
                   wrh_ref, wrl_ref, br_ref, ts_ref, h_o, xn_o, route_o, cnt_o, cnt_scr,
                   *, tile, tiles_p, seq_p, seq_s):
    i = pl.program_id(0)

    @pl.when(i == 0)
    def _():
        cnt_scr[...] = jnp.zeros_like(cnt_scr)

    u = u_ref[...]
    rowi = lax.broadcasted_iota(I32, (tile, D_MODEL), 0)
    is_p = i < tiles_p
    at_start = (i * tile) % seq_p == 0
    stp = stp_ref[0]
    prev = up_ref[...]
    m2_p = jnp.where(at_start, stp[0:1, :], prev[6:7, :])
    m1_p = jnp.where(at_start, stp[1:2, :], prev[7:8, :])
    per = tile // seq_s
    sts = sts_ref[...]
    m2_s = jnp.broadcast_to(sts[:, 0:1, :], (per, seq_s, D_MODEL)).reshape(tile, D_MODEL)
    m1_s = jnp.broadcast_to(sts[:, 1:2, :], (per, seq_s, D_MODEL)).reshape(tile, D_MODEL)
    m2 = jnp.where(is_p, m2_p, m2_s)
    m1 = jnp.where(is_p, m1_p, m1_s)
    pos = jnp.where(is_p, rowi, rowi % seq_s)
    u1 = jnp.where(pos == 0, m1, pltpu.roll(u, 1, axis=0))
    u2 = jnp.where(pos == 0, m2, jnp.where(pos == 1, m1, pltpu.roll(u, 2, axis=0)))
    cw = cw_ref[...]
    conv = u2 * cw[0:1, :] + u1 * cw[1:2, :] + u * cw[2:3, :]
    mix = _dot((bg_ref[...] * conv).astype(BF16), w_ref[...])
    h3 = h_ref[...] + mix
    h_o[...] = h3
    xn = _rms(h3, gf_ref[...])
    _store_row_tiles(xn_o, xn)
    _route(xn, wrh_ref[...], wrl_ref[...], br_ref[...], ts_ref[...], cnt_scr, route_o, cnt_o)


def _conv_out(u, bg, h, state_p, state_s, cw, w16, gf, wrh, wrl, br, tstrict, seq_p, seq_s):
    tm = TOKEN_TILE
    n = u.shape[0]
    tiles_p = state_p.shape[0] * seq_p // tm
    per = tm // seq_s
    row = lambda w: pl.BlockSpec((tm, w), lambda i: (i, 0))
    prev = pl.BlockSpec((8, D_MODEL), lambda i: (jnp.maximum(i * (tm // 8) - 1, 0), 0))
    stp_spec = pl.BlockSpec((1, 2, D_MODEL),
                            lambda i: (jnp.minimum(i, tiles_p - 1) * tm // seq_p, 0, 0))
    sts_spec = pl.BlockSpec((per, 2, D_MODEL), lambda i: (jnp.maximum(i - tiles_p, 0), 0, 0))
    return pl.pallas_call(
        functools.partial(_conv_out_body, tile=tm, tiles_p=tiles_p, seq_p=seq_p, seq_s=seq_s),
        out_shape=[jax.ShapeDtypeStruct((n, D_MODEL), F32),
                   jax.ShapeDtypeStruct((n * ROW_TILE, LANES), F32),
                   jax.ShapeDtypeStruct((n, LANES), F32), jax.ShapeDtypeStruct((1, LANES), F32)],
        grid=(n // tm,),
        in_specs=[row(D_MODEL), prev, stp_spec, sts_spec, row(D_MODEL), row(D_MODEL),
                  _const_spec((3, D_MODEL)),
                  _const_spec((D_MODEL, D_MODEL)), _const_spec((1, D_MODEL)),
                  _const_spec((D_MODEL, LANES)), _const_spec((D_MODEL, LANES)),
                  _const_spec((1, LANES)), _const_spec((tm, tm))],
        out_specs=[row(D_MODEL), pl.BlockSpec((tm * ROW_TILE, LANES), lambda i: (i, 0)), row(LANES),
                   _const_spec((1, LANES))],
        scratch_shapes=[pltpu.VMEM((1, LANES), F32)],
        compiler_params=_cparams("arbitrary"),
        name="conv_out",
    )(u, u, state_p, state_s, bg, h, cw, w16, gf, wrh, wrl, br, tstrict)


def _combine_body(h_ref, ya_ref, yb_ref, r_ref, op_ref, os_ref, *, tiles_p):
    out = _add_moe(h_ref[...], ya_ref, yb_ref, r_ref[...])

    @pl.when(pl.program_id(0) < tiles_p)
    def _():
        op_ref[...] = out

    @pl.when(pl.program_id(0) >= tiles_p)
    def _():
        os_ref[...] = out


def _combine(h, y2, route, n_p):
    n = h.shape[0]
    tm = TOKEN_TILE
    tiles_p = n_p // tm
    row = lambda w: pl.BlockSpec((tm, w), lambda i: (i, 0))
    ysp = lambda k: _moe_out_spec(k, tm)
    return pl.pallas_call(
        functools.partial(_combine_body, tiles_p=tiles_p),
        out_shape=[jax.ShapeDtypeStruct((n_p, D_MODEL), F32), jax.ShapeDtypeStruct((n - n_p, D_MODEL), F32)],
        grid=(n // tm,),
        in_specs=[row(D_MODEL), ysp(0), ysp(1), row(LANES)],
        out_specs=_group_specs(D_MODEL, tiles_p),
        compiler_params=_cparams("arbitrary"),
        name="moe_combine",
    )(h, y2, y2, route)


def _rel_bias_toeplitz(rel_bias, rows, cols, lead):
    period = rows + cols
    k = jnp.arange(period)
    d = jnp.where(k < cols, k, k - period)
    idx = jnp.clip(lead - d, -REL_CLIP, REL_CLIP) + REL_CLIP
    v = jnp.take(rel_bias.astype(F32), idx, axis=1)
    heads = v.shape[0]
    skew = jnp.tile(v, (1, rows))[:, :rows * (period - 1)].reshape(heads, rows, period - 1)
    return skew[:, :, :cols]


def _bias_prompt(rel_bias):
    r = jnp.arange(ATTN_TILE)[:, None]
    c = jnp.arange(ATTN_WINDOW)[None, :]
    j = c // CHUNK - r // CHUNK
    band = jnp.logical_and(j >= 0, j <= BAND_CHUNKS)
    bias = _rel_bias_toeplitz(rel_bias, ATTN_TILE, ATTN_WINDOW, BAND_CHUNKS * CHUNK)
    return jnp.where(band[None], bias, NEG_INF)


def _bias_sample(rel_bias, la, tn):
    bias = _rel_bias_toeplitz(rel_bias, tn, la + tn, la)
    return bias[:, :, :la], bias[:, :, la:]


def _router_weights(w_group, b_group, w_expert, b_expert):
    pad = LANES - N_EXPERTS - N_GROUPS
    w = jnp.concatenate([w_expert, w_group, jnp.zeros((D_MODEL, pad), F32)], axis=1)
    b = jnp.concatenate([b_expert, b_group, jnp.zeros((pad,), F32)])[None, :].astype(F32)
    hi, lo = _split2(w.astype(F32))
    return hi, lo, b


def kernel(x_prompt, x_sample, cache_a_k, cache_a_v, state_hgrn, state_conv, norm_mix, norm_ffn,
           w_in_ab, w_out_ab, q_norm, k_norm, rel_bias, hgrn_lb_logits, hgrn_out_norm, w_in_c,
           conv_w, w_out_c, w_group, b_group, w_expert, b_expert, w1, w3, w2):
    batch, seq, d = x_prompt.shape
    nseq_s, tn, _ = x_sample.shape
    la = cache_a_k.shape[2]
    n_p = batch * seq
    n_s = nseq_s * tn
    n = n_p + n_s
    keep = min(BAND_CHUNKS * CHUNK, seq)

    x_p = x_prompt.reshape(n_p, d)
    x_s = x_sample.reshape(n_s, d)
    lb_all = jnp.cumsum(jax.nn.softmax(hgrn_lb_logits.astype(F32), axis=0), axis=0)
    head_avg = jnp.kron(jnp.eye(N_HEADS, dtype=F32),
                        jnp.full((HEAD_DIM, HEAD_DIM), 1.0 / HEAD_DIM, F32)).astype(BF16)
    t = jnp.arange(TOKEN_TILE)
    tstrict = (t[None, :] < t[:, None]).astype(BF16)
    row1 = lambda v: v.astype(F32).reshape(1, -1)
    tile8 = lambda v: jnp.tile(v.astype(F32), N_HEADS).reshape(1, -1)

    l = 0
    (q16, kf, vf, k16, v16, lf, kb, vb, qb, gb) = _proj_ab(
        x_p, x_s, row1(norm_mix[0]), w_in_ab[l].astype(BF16), tile8(q_norm[l]), tile8(k_norm[l]),
        row1(lb_all[l]), head_avg, seq, keep)

    oa_p = _attn_prompt(q16, k16, v16, _bias_prompt(rel_bias[l]), batch, seq)
    bias_c, bias_n = _bias_sample(rel_bias[l], la, tn)
    oa_s = _attn_sample(q16, k16, v16, cache_a_k[l].reshape(nseq_s, la, D_HALF),
                        cache_a_v[l].reshape(nseq_s, la, D_HALF), bias_c, bias_n, n_p, nseq_s, tn)

    zeros_state = jnp.zeros((batch, N_HEADS, HEAD_DIM, HEAD_DIM), F32)
    ob_p, st_p = _hgrn(qb, kb, vb, lf, zeros_state, 0, batch, seq, 64, 4)
    ob_s, st_s = _hgrn(qb, kb, vb, lf, jnp.swapaxes(state_hgrn[l].astype(F32), -1, -2),
                       n_p, nseq_s, tn, tn, 1)

    wrh, wrl, br = _router_weights(w_group[0], b_group[0], w_expert[0], b_expert[0])
    h1, xn1, route1, cnt1 = _out_ab(x_p, x_s, oa_p, oa_s, ob_p, ob_s, gb, row1(hgrn_out_norm[l]),
                                    head_avg, w_out_ab[l].astype(BF16), row1(norm_ffn[0]),
                                    wrh, wrl, br, tstrict)
    y1 = _moe(xn1, route1, cnt1, w1[0], w3[0], w2[0])

    h2, u, bg = _proj_c(h1, y1, route1, row1(norm_mix[1]), w_in_c[0].astype(BF16))
    wrh, wrl, br = _router_weights(w_group[1], b_group[1], w_expert[1], b_expert[1])
    h3, xn2, route2, cnt2 = _conv_out(u, bg, h2, jnp.zeros((batch, 2, d), F32),
                                      state_conv[0].astype(F32), conv_w[0].astype(F32),
                                      w_out_c[0].astype(BF16), row1(norm_ffn[1]), wrh, wrl, br,
                                      tstrict, seq, tn)
    y2 = _moe(xn2, route2, cnt2, w1[1], w3[1], w2[1])
    out_p, out_s = _combine(h3, y2, route2, n_p)

    y_prompt = out_p.reshape(batch, seq, d)
    y_sample = out_s.reshape(nseq_s, tn, d)
    n_kp = batch * keep
    heads = lambda a, b_, t_: a.reshape(b_, t_, N_HEADS, HEAD_DIM)
    nk_p = heads(kf[:n_kp], batch, keep)[None]
    nv_p = heads(vf[:n_kp], batch, keep)[None]
    kf_s = heads(kf[n_kp:n_kp + n_s], nseq_s, tn)
    vf_s = heads(vf[n_kp:n_kp + n_s], nseq_s, tn)
    nk_s = jnp.concatenate([cache_a_k[l].astype(F32), kf_s], axis=1)[:, -la:][None]
    nv_s = jnp.concatenate([cache_a_v[l].astype(F32), vf_s], axis=1)[:, -la:][None]
    nh_p = jnp.swapaxes(st_p, -1, -2)[None]
    nh_s = jnp.swapaxes(st_s, -1, -2)[None]
    nc_p = jnp.stack([u[(b + 1) * seq - 2:(b + 1) * seq] for b in range(batch)])[None]
    u_s = u[n_p:].reshape(nseq_s, tn, d)
    nc_s = jnp.concatenate([state_conv[0].astype(F32), u_s], axis=1)[:, -2:][None]
    return (y_prompt, y_sample, nk_p, nv_p, nk_s, nv_s, nh_p, nh_s, nc_p, nc_s)
```

```python
import functools

import jax
import jax.numpy as jnp
from jax import lax
from jax.experimental import pallas as pl
from jax.experimental.pallas import tpu as pltpu

F32 = jnp.float32
BF16 = jnp.bfloat16
I32 = jnp.int32

D_MODEL = 1024
CHUNK = 64
BAND_CHUNKS = 8
HEAD_DIM = 64
N_HEADS = 8
D_HALF = N_HEADS * HEAD_DIM
REL_CLIP = 128
HGRN_BLOCK = 16
N_GROUPS = 4
EXPERTS_PER_GROUP = 8
N_EXPERTS = N_GROUPS * EXPERTS_PER_GROUP
D_EXPERT = 256
RMS_EPS = 1e-6
NEG_INF = -1e30

LANES = 128
ROW_TILE = D_MODEL // LANES
TOKEN_TILE = 256
ATTN_TILE = 256
ATTN_WINDOW = ATTN_TILE + BAND_CHUNKS * CHUNK
ATTN_SPAN = -(-(BAND_CHUNKS + 1) * CHUNK // LANES) * LANES
ATTN_HEAD_GROUP = 4
MOE_TILE = 256
RANK_BITS = 16
RANK_SPAN = 1 << RANK_BITS
GATHER_UNROLL = 8
VMEM_LIMIT = 48 * 1024 * 1024


def _cparams(*sem):
    return pltpu.CompilerParams(dimension_semantics=sem, vmem_limit_bytes=VMEM_LIMIT)


def _const_spec(shape):
    nd = len(shape)
    return pl.BlockSpec(shape, lambda *_: (0,) * nd)


def _store_row_tiles(ref, val):
    rows = val.shape[0]
    for c in range(ROW_TILE):
        ref[pl.ds(c, rows, stride=ROW_TILE), :] = val[:, c * LANES:(c + 1) * LANES]


def _load_row_tiles(ref, rows):
    return jnp.concatenate([ref[pl.ds(c, rows, stride=ROW_TILE), :] for c in range(ROW_TILE)], axis=1)


def _rms(x, gain):
    ms = jnp.mean(x * x, axis=-1, keepdims=True)
    return (x * lax.rsqrt(ms + RMS_EPS)) * gain


def _split2(x):
    hi = x.astype(BF16)
    lo = (x - hi.astype(F32)).astype(BF16)
    return hi, lo


def _split3(x):
    p0 = x.astype(BF16)
    r = x - p0.astype(F32)
    p1 = r.astype(BF16)
    p2 = (r - p1.astype(F32)).astype(BF16)
    return p0, p1, p2


def _dot(a, b):
    return jnp.dot(a, b, preferred_element_type=F32)


def _dot_nt(a, b):
    return lax.dot_general(a, b, (((1,), (1,)), ((), ())), preferred_element_type=F32)


def _dot_tn(a, b):
    return lax.dot_general(a, b, (((0,), (0,)), ((), ())), preferred_element_type=F32)


def _head_mean_sq(v, bd):
    hi, lo = _split2(v * v)
    return _dot(hi, bd) + _dot(lo, bd)


def _group_specs(width, tiles_p, tm=TOKEN_TILE):
    return [pl.BlockSpec((tm, width), lambda i: (jnp.minimum(i, tiles_p - 1), 0)),
            pl.BlockSpec((tm, width), lambda i: (jnp.maximum(i - tiles_p, 0), 0))]


def _group_pick(p_ref, s_ref, tiles_p):
    return jnp.where(pl.program_id(0) < tiles_p, p_ref[...], s_ref[...])


def _proj_ab_body(xp_ref, xs_ref, g_ref, w_ref, qg_ref, kg_ref, lb_ref, bd_ref,
                  q_o, kf_o, vf_o, k16_o, v16_o, lf_o, kb_o, vb_o, qb_o, gb_o, *, tiles_p):
    xb = _rms(_group_pick(xp_ref, xs_ref, tiles_p), g_ref[...]).astype(BF16)
    bd = bd_ref[...]

    def seg(j):
        return _dot(xb, w_ref[:, j * D_HALF:(j + 1) * D_HALF])

    qa = seg(0)
    qn = qa * lax.rsqrt(_head_mean_sq(qa, bd) + RMS_EPS) * qg_ref[...]
    q_o[...] = (qn * (HEAD_DIM ** -0.5)).astype(BF16)
    ka = seg(1)
    kn = ka * lax.rsqrt(_head_mean_sq(ka, bd) + RMS_EPS) * kg_ref[...]
    kf_o[...] = kn
    k16_o[...] = kn.astype(BF16)
    va = seg(2)
    vf_o[...] = va
    v16_o[...] = va.astype(BF16)
    lb = lb_ref[...]
    f = lb + (1.0 - lb) * jax.nn.sigmoid(seg(3))
    lf_o[...] = jnp.log(f)
    kb_o[...] = 1.0 - f
    vb_o[...] = seg(4)
    qb_o[...] = jax.nn.silu(seg(5))
    gb_o[...] = seg(6)


def _proj_ab(x_p, x_s, gain, w16, qg, kg, lb, bd, seq, keep):
    tm = TOKEN_TILE
    n_p, n_s = x_p.shape[0], x_s.shape[0]
    n = n_p + n_s
    tiles_p, tiles_seq, tiles_keep = n_p // tm, seq // tm, keep // tm
    kept_tiles = (n_p // seq) * tiles_keep + n_s // tm

    def keep_map(i):
        b, j = i // tiles_seq, i % tiles_seq
        prompt_slot = b * tiles_keep + jnp.maximum(j - (tiles_seq - tiles_keep), 0)
        return (jnp.where(i >= tiles_p, kept_tiles - n_s // tm + (i - tiles_p), prompt_slot), 0)

    row = pl.BlockSpec((tm, D_HALF), lambda i: (i, 0))
    kept = pl.BlockSpec((tm, D_HALF), keep_map)
    full = lambda dt: jax.ShapeDtypeStruct((n, D_HALF), dt)
    kept_shape = jax.ShapeDtypeStruct((kept_tiles * tm, D_HALF), F32)
    return pl.pallas_call(
        functools.partial(_proj_ab_body, tiles_p=tiles_p),
        out_shape=[full(BF16), kept_shape, kept_shape, full(BF16), full(BF16),
                   full(F32), full(F32), full(F32), full(F32), full(F32)],
        grid=(n // tm,),
        in_specs=_group_specs(D_MODEL, tiles_p) + [
            _const_spec((1, D_MODEL)), _const_spec(w16.shape),
            _const_spec((1, D_HALF)), _const_spec((1, D_HALF)), _const_spec((1, D_HALF)),
            _const_spec((D_HALF, D_HALF))],
        out_specs=[row, kept, kept] + [row] * 7,
        compiler_params=_cparams("arbitrary"),
        name="proj_ab",
    )(x_p, x_s, gain, w16, qg, kg, lb, bd)


def _attn_prompt_body(q_ref, k0, k1, k2, v0, v1, v2, bias_ref, o_ref):
    i = pl.program_id(1)
    lim = jnp.maximum(2 - i, 0) * ATTN_TILE
    col = lax.broadcasted_iota(I32, (CHUNK, ATTN_SPAN), 1)
    cut = ATTN_SPAN - 2 * ATTN_TILE

    def span(parts, start):
        if start == 0:
            return (parts[0], parts[1], parts[2][:cut])
        return (parts[0][start:], parts[1], parts[2])

    for h0 in range(0, N_HEADS, ATTN_HEAD_GROUP):
        units = []
        for h in range(h0, h0 + ATTN_HEAD_GROUP):
            hs = slice(h * HEAD_DIM, (h + 1) * HEAD_DIM)
            kh = (k0[:, hs], k1[:, hs], k2[:, hs])
            vh = (v0[:, hs], v1[:, hs], v2[:, hs])
            for r in range(ATTN_TILE // CHUNK):
                start = (r * CHUNK // LANES) * LANES
                units.append((h, hs, slice(r * CHUNK, (r + 1) * CHUNK), start,
                              span(kh, start), span(vh, start)))
        s = [jnp.concatenate([_dot_nt(q_ref[rows, hs], kp) for kp in ks], axis=1)
             for (_, hs, rows, _, ks, _) in units]
        s = [jnp.where(col + start < lim, NEG_INF,
                       s_u + bias_ref[h, rows, start:start + ATTN_SPAN])
             for s_u, (h, _, rows, start, _, _) in zip(s, units)]
        m = [jnp.max(s_u, axis=-1, keepdims=True) for s_u in s]
        p = [jnp.exp(s_u - m_u) for s_u, m_u in zip(s, m)]
        l = [jnp.sum(p_u, axis=-1, keepdims=True) for p_u in p]
        outs = []
        for p_u, (_, _, _, _, _, vs) in zip(p, units):
            p16 = p_u.astype(BF16)
            o, at = 0.0, 0
            for vp in vs:
                o = o + _dot(p16[:, at:at + vp.shape[0]], vp)
                at += vp.shape[0]
            outs.append(o)
        for o_u, l_u, (_, hs, rows, _, _, _) in zip(outs, l, units):
            o_ref[rows, hs] = (o_u / l_u).astype(BF16)


def _attn_prompt(q16, k16, v16, bias_full, batch, seq):
    tiles = seq // ATTN_TILE
    qspec = pl.BlockSpec((ATTN_TILE, D_HALF), lambda b, i: (b * tiles + i, 0))

    def kv(back):
        return pl.BlockSpec((ATTN_TILE, D_HALF),
                            lambda b, i: (b * tiles + jnp.maximum(i - back, 0), 0))

    return pl.pallas_call(
        _attn_prompt_body,
        out_shape=jax.ShapeDtypeStruct((batch * seq, D_HALF), BF16),
        grid=(batch, tiles),
        in_specs=[qspec, kv(2), kv(1), kv(0), kv(2), kv(1), kv(0),
                  _const_spec(bias_full.shape)],
        out_specs=qspec,
        compiler_params=_cparams("parallel", "parallel"),
        name="attn_prompt",
    )(q16, k16, k16, k16, v16, v16, v16, bias_full)


def _attn_sample_body(q_ref, kc_ref, vc_ref, kn_ref, vn_ref, bc_ref, bn_ref, o_ref):
    kc = kc_ref[0].astype(BF16)
    vc = vc_ref[0].astype(BF16)
    for h in range(N_HEADS):
        hs = slice(h * HEAD_DIM, (h + 1) * HEAD_DIM)
        qh = q_ref[:, hs]
        sc = _dot_nt(qh, kc[:, hs]) + bc_ref[h]
        sn = _dot_nt(qh, kn_ref[:, hs]) + bn_ref[h]
        m = jnp.maximum(jnp.max(sc, axis=-1, keepdims=True), jnp.max(sn, axis=-1, keepdims=True))
        pc = jnp.exp(sc - m)
        pn = jnp.exp(sn - m)
        l = jnp.sum(pc, axis=-1, keepdims=True) + jnp.sum(pn, axis=-1, keepdims=True)
        o = _dot(pc.astype(BF16), vc[:, hs]) + _dot(pn.astype(BF16), vn_ref[:, hs])
        o_ref[:, hs] = (o / l).astype(BF16)


def _attn_sample(q16, k16, v16, cache_k, cache_v, bias_c, bias_n, row0, nseq, tn):
    blk0 = row0 // tn
    la = cache_k.shape[1]
    new = pl.BlockSpec((tn, D_HALF), lambda b: (blk0 + b, 0))
    cache = pl.BlockSpec((1, la, D_HALF), lambda b: (b, 0, 0))
    return pl.pallas_call(
        _attn_sample_body,
        out_shape=jax.ShapeDtypeStruct((nseq * tn, D_HALF), BF16),
        grid=(nseq,),
        in_specs=[new, cache, cache, new, new, _const_spec(bias_c.shape), _const_spec(bias_n.shape)],
        out_specs=pl.BlockSpec((tn, D_HALF), lambda b: (b, 0)),
        compiler_params=_cparams("parallel"),
        name="attn_sample",
    )(q16, cache_k, cache_v, k16, v16, bias_c, bias_n)


def _hgrn_body(q_ref, k_ref, v_ref, lf_ref, s0_ref, tblk_ref, tfull_ref, o_ref, sout_ref, st_scr,
               *, chunk, nchunks):
    nblk = chunk // HGRN_BLOCK

    @pl.when(pl.program_id(1) == 0)
    def _():
        st_scr[...] = s0_ref[0]

    row = lax.broadcasted_iota(I32, (chunk, D_HALF), 0)
    r2 = lax.broadcasted_iota(I32, (chunk, chunk), 0)
    c2 = lax.broadcasted_iota(I32, (chunk, chunk), 1)
    same_blk_causal = jnp.logical_and(r2 // HGRN_BLOCK == c2 // HGRN_BLOCK, c2 <= r2)

    def one_chunk(c, carry):
        sl = pl.ds(pl.multiple_of(c * chunk, chunk), chunk)
        q = q_ref[sl, :]
        k = k_ref[sl, :]
        v16 = v_ref[sl, :].astype(BF16)
        l0, l1, l2 = _split3(lf_ref[sl, :])
        tb = tblk_ref[...]
        tf = tfull_ref[...]
        b_in = _dot(tb, l0) + _dot(tb, l1) + _dot(tb, l2)
        b_ch = _dot(tf, l0) + _dot(tf, l1) + _dot(tf, l2)
        ld = (q * jnp.exp(b_in)).astype(BF16)
        rd = (k * jnp.exp(-b_in)).astype(BF16)
        lj, rj = [], []
        for j in range(nblk - 1):
            e_j = b_ch[(j + 1) * HGRN_BLOCK - 1:(j + 1) * HGRN_BLOCK, :]
            later = row >= (j + 1) * HGRN_BLOCK
            inside = jnp.logical_and(row >= j * HGRN_BLOCK, row < (j + 1) * HGRN_BLOCK)
            lj.append(jnp.where(later, q * jnp.exp(jnp.minimum(b_ch - e_j, 0.0)), 0.0).astype(BF16))
            rj.append(jnp.where(inside, k * jnp.exp(jnp.minimum(e_j - b_ch, 0.0)), 0.0).astype(BF16))
        e_end = b_ch[chunk - 1:chunk, :]
        qc = (q * jnp.exp(b_ch)).astype(BF16)
        kc = (k * jnp.exp(e_end - b_ch)).astype(BF16)
        dec = jnp.exp(e_end)
        heads = [slice(h * HEAD_DIM, (h + 1) * HEAD_DIM) for h in range(N_HEADS)]
        sts = [st_scr[h] for h in range(N_HEADS)]
        diag = [_dot_nt(ld[:, hs], rd[:, hs]) for hs in heads]
        off = []
        for hs in heads:
            acc = None
            for j in range(nblk - 1):
                term = _dot_nt(lj[j][:, hs], rj[j][:, hs])
                acc = term if acc is None else acc + term
            off.append(acc)
        inter = [_dot_nt(qc[:, hs], st.astype(BF16)) for hs, st in zip(heads, sts)]
        upd = [_dot_tn(v16[:, hs], kc[:, hs]) for hs in heads]
        sc16 = [(jnp.where(same_blk_causal, d, 0.0) + o_).astype(BF16) for d, o_ in zip(diag, off)]
        intra = [_dot(s_, v16[:, hs]) for s_, hs in zip(sc16, heads)]
        o_ref[sl, :] = jnp.concatenate([a + b for a, b in zip(intra, inter)], axis=1)
        for h, hs in enumerate(heads):
            st_scr[h] = sts[h] * dec[:, hs] + upd[h]
        return carry

    lax.fori_loop(0, nchunks, one_chunk, 0)
    sout_ref[0] = st_scr[...]


def _hgrn(qb, kb, vb, lf, s0t, row0, nseq, seq, chunk, nchunks):
    tt = chunk * nchunks
    steps = seq // tt
    blk0 = row0 // tt
    t = jnp.arange(chunk)
    lower = t[None, :] <= t[:, None]
    tfull = lower.astype(BF16)
    tblk = jnp.logical_and(lower, (t[None, :] // HGRN_BLOCK) == (t[:, None] // HGRN_BLOCK)).astype(BF16)
    tok = pl.BlockSpec((tt, D_HALF), lambda b, j: (blk0 + b * steps + j, 0))
    state = pl.BlockSpec((1, N_HEADS, HEAD_DIM, HEAD_DIM), lambda b, j: (b, 0, 0, 0))
    return pl.pallas_call(
        functools.partial(_hgrn_body, chunk=chunk, nchunks=nchunks),
        out_shape=[jax.ShapeDtypeStruct((nseq * seq, D_HALF), F32),
                   jax.ShapeDtypeStruct(s0t.shape, F32)],
        grid=(nseq, steps),
        in_specs=[tok, tok, tok, tok, state, _const_spec((chunk, chunk)), _const_spec((chunk, chunk))],
        out_specs=[pl.BlockSpec((tt, D_HALF), lambda b, j: (b * steps + j, 0)), state],
        scratch_shapes=[pltpu.VMEM((N_HEADS, HEAD_DIM, HEAD_DIM), F32)],
        compiler_params=_cparams("parallel", "arbitrary"),
        name="hgrn",
    )(qb, kb, vb, lf, s0t, tblk, tfull)


def _route(xn, wr_hi, wr_lo, br, tstrict, cnt_scr, route_o, cnt_o, code_o):
    rows = xn.shape[0]
    x_hi, x_lo = _split2(xn)
    logits = _dot(x_hi, wr_hi) + _dot(x_lo, wr_hi) + _dot(x_hi, wr_lo) + br
    lane = lax.broadcasted_iota(I32, (rows, LANES), 1)
    lane_f = lane.astype(F32)

    def first_argmax(vals):
        m = jnp.max(vals, axis=-1, keepdims=True)
        idx = jnp.min(jnp.where(vals == m, lane_f, float(LANES)), axis=-1, keepdims=True)
        return m, idx.astype(I32)

    is_grp = jnp.logical_and(lane >= N_EXPERTS, lane < N_EXPERTS + N_GROUPS)
    gl = jnp.where(is_grp, logits, -jnp.inf)
    gmax, gidx = first_argmax(gl)
    p_grp = 1.0 / jnp.sum(jnp.exp(gl - gmax), axis=-1, keepdims=True)
    grp = gidx - N_EXPERTS
    in_grp = jnp.logical_and(lane < N_EXPERTS, lane // EXPERTS_PER_GROUP == grp)
    el = jnp.where(in_grp, logits, -jnp.inf)
    v1, e1 = first_argmax(el)
    v2, e2 = first_argmax(jnp.where(lane == e1, -jnp.inf, el))
    t2 = jnp.exp(v2 - v1)
    den = 1.0 + t2
    g1 = (1.0 / den) * p_grp
    g2 = (t2 / den) * p_grp

    oh1 = lane == e1
    oh2 = lane == e2
    oh = jnp.where(jnp.logical_or(oh1, oh2), 1.0, 0.0)
    before = _dot(tstrict, oh.astype(BF16)) + cnt_scr[...]
    rank1 = jnp.sum(jnp.where(oh1, before, 0.0), axis=-1, keepdims=True)
    rank2 = jnp.sum(jnp.where(oh2, before, 0.0), axis=-1, keepdims=True)
    cnt_scr[...] = cnt_scr[...] + jnp.sum(oh, axis=0, keepdims=True)
    cnt_o[...] = cnt_scr[...]

    e1f, e2f = e1.astype(F32), e2.astype(F32)
    fields = (e1f, e2f, g1, g2, rank1, rank2, e1f * RANK_SPAN + rank1, e2f * RANK_SPAN + rank2)
    slab = jnp.zeros((rows, LANES), F32)
    for idx, val in enumerate(fields):
        slab = jnp.where(lane == idx, val, slab)
    route_o[...] = slab
    code_o[...] = slab.T[6:8, :]


def _out_ab_body(hp_ref, hs_ref, oap_ref, oas_ref, obp_ref, obs_ref, gb_ref, og_ref, bd_ref, w_ref,
                 gf_ref, wrh_ref, wrl_ref, br_ref, ts_ref, h_o, xn_o, route_o, cnt_o, code_o, cnt_scr,
                 *, tiles_p):
    @pl.when(pl.program_id(0) == 0)
    def _():
        cnt_scr[...] = jnp.zeros_like(cnt_scr)

    ob = _group_pick(obp_ref, obs_ref, tiles_p)
    obn = ob * lax.rsqrt(_head_mean_sq(ob, bd_ref[...]) + RMS_EPS) * og_ref[...]
    obg = (obn * jax.nn.silu(gb_ref[...])).astype(BF16)
    oa = _group_pick(oap_ref, oas_ref, tiles_p)
    mix = _dot(oa, w_ref[0:D_HALF, :]) + _dot(obg, w_ref[D_HALF:, :])
    h1 = _group_pick(hp_ref, hs_ref, tiles_p) + mix
    h_o[...] = h1
    xn = _rms(h1, gf_ref[...])
    _store_row_tiles(xn_o, xn)
    _route(xn, wrh_ref[...], wrl_ref[...], br_ref[...], ts_ref[...], cnt_scr, route_o, cnt_o, code_o)


def _out_ab(h_p, h_s, oa_p, oa_s, ob_p, ob_s, gb, og, bd, w16, gf, wrh, wrl, br, tstrict):
    tm = TOKEN_TILE
    n = h_p.shape[0] + h_s.shape[0]
    tiles_p = h_p.shape[0] // tm
    row = lambda w: pl.BlockSpec((tm, w), lambda i: (i, 0))
    return pl.pallas_call(
        functools.partial(_out_ab_body, tiles_p=tiles_p),
        out_shape=[jax.ShapeDtypeStruct((n, D_MODEL), F32), jax.ShapeDtypeStruct((n * ROW_TILE, LANES), F32),
                   jax.ShapeDtypeStruct((n, LANES), F32), jax.ShapeDtypeStruct((1, LANES), F32),
                   jax.ShapeDtypeStruct((2, n), F32)],
        grid=(n // tm,),
        in_specs=_group_specs(D_MODEL, tiles_p) + _group_specs(D_HALF, tiles_p)
        + _group_specs(D_HALF, tiles_p) + [
            row(D_HALF), _const_spec((1, D_HALF)),
            _const_spec((D_HALF, D_HALF)), _const_spec((D_MODEL, D_MODEL)),
            _const_spec((1, D_MODEL)), _const_spec((D_MODEL, LANES)),
            _const_spec((D_MODEL, LANES)), _const_spec((1, LANES)), _const_spec((tm, tm))],
        out_specs=[row(D_MODEL), pl.BlockSpec((tm * ROW_TILE, LANES), lambda i: (i, 0)), row(LANES),
                   _const_spec((1, LANES)), pl.BlockSpec((2, tm), lambda i: (0, i))],
        scratch_shapes=[pltpu.VMEM((1, LANES), F32)],
        compiler_params=_cparams("arbitrary"),
        name="out_ab",
    )(h_p, h_s, oa_p, oa_s, ob_p, ob_s, gb, og, bd, w16, gf, wrh, wrl, br, tstrict)


def _row_copy(src, src_row, dst, dst_row, sem):
    s0 = pl.multiple_of(src_row * ROW_TILE, ROW_TILE)
    d0 = pl.multiple_of(dst_row * ROW_TILE, ROW_TILE)
    return pltpu.make_async_copy(src.at[pl.ds(s0, ROW_TILE)], dst.at[pl.ds(d0, ROW_TILE)], sem)


def _sorted_row(code_ref, offs_ref, a):
    code = code_ref[a]
    return offs_ref[code >> RANK_BITS] + (code & (RANK_SPAN - 1))


def _gather_rows_body(code_ref, offs_ref, src_hbm, o_ref, buf, sems, *, tile):
    i = pl.program_id(0)
    nt = pl.num_programs(0)

    def issue(t):
        slot = t % 2
        base = t * tile

        def one(r, carry):
            row = _sorted_row(code_ref, offs_ref, base + r)
            _row_copy(src_hbm, row, buf.at[slot], r, sems.at[slot]).start()
            return carry

        lax.fori_loop(0, tile, one, 0, unroll=GATHER_UNROLL)

    @pl.when(i == 0)
    def _():
        issue(0)

    @pl.when(i + 1 < nt)
    def _():
        issue(i + 1)

    slot = i % 2
    for _ in range(tile):
        _row_copy(src_hbm, 0, buf.at[slot], 0, sems.at[slot]).wait()
    o_ref[...] = buf[slot]


def _gather_rows(code, offs, src):
    tm = MOE_TILE
    n_out = code.shape[0]
    return pl.pallas_call(
        functools.partial(_gather_rows_body, tile=tm),
        out_shape=jax.ShapeDtypeStruct((n_out * ROW_TILE, LANES), F32),
        grid_spec=pltpu.PrefetchScalarGridSpec(
            num_scalar_prefetch=2,
            grid=(n_out // tm,),
            in_specs=[pl.BlockSpec(memory_space=pl.ANY)],
            out_specs=pl.BlockSpec((tm * ROW_TILE, LANES), lambda i, code, offs: (i, 0)),
            scratch_shapes=[pltpu.VMEM((2, tm * ROW_TILE, LANES), F32), pltpu.SemaphoreType.DMA((2,))],
        ),
        compiler_params=pltpu.CompilerParams(dimension_semantics=("arbitrary",)),
        name="moe_gather",
    )(code, offs, src)


def _scatter_rows_body(code_ref, offs_ref, pad0_ref, padn_ref, nt_ref, x_ref, xs_hbm,
                       stage, zero_scr, sems, pad_sem, *, tile, n_tok, out_tiles):
    i = pl.program_id(0)
    last = pl.num_programs(0) - 1
    base = i * tile
    slot = i % 2

    def drain(s):
        for _ in range(2 * tile):
            _row_copy(stage.at[s], 0, xs_hbm, 0, sems.at[s]).wait()

    @pl.when(i >= 2)
    def _():
        drain(slot)

    stage[slot] = x_ref[...]

    def issue(r, carry):
        tok = base + r
        src = stage.at[slot]
        _row_copy(src, r, xs_hbm, _sorted_row(code_ref, offs_ref, tok), sems.at[slot]).start()
        _row_copy(src, r, xs_hbm, _sorted_row(code_ref, offs_ref, n_tok + tok),
                  sems.at[slot]).start()
        return carry

    lax.fori_loop(0, tile, issue, 0, unroll=GATHER_UNROLL)

    @pl.when(i == last)
    def _():
        zero_scr[...] = jnp.zeros_like(zero_scr)

        def fill(lo, count):
            def one(r, carry):
                _row_copy(zero_scr, 0, xs_hbm, lo + r, pad_sem).start()
                return carry

            def done(r, carry):
                _row_copy(zero_scr, 0, xs_hbm, 0, pad_sem).wait()
                return carry

            lax.fori_loop(0, count, one, 0)
            lax.fori_loop(0, count, done, 0)

        for e in range(N_EXPERTS):
            fill(pad0_ref[e], padn_ref[e])

        def tail_copy(t):
            rows = MOE_TILE * ROW_TILE
            return pltpu.make_async_copy(
                zero_scr, xs_hbm.at[pl.ds(pl.multiple_of(t * rows, rows), rows)], pad_sem)

        def tail_start(t, carry):
            tail_copy(t).start()
            return carry

        def tail_done(t, carry):
            tail_copy(t).wait()
            return carry

        lax.fori_loop(nt_ref[0], out_tiles, tail_start, 0)
        lax.fori_loop(nt_ref[0], out_tiles, tail_done, 0)
        drain(slot)

    @pl.when(jnp.logical_and(i == last, i >= 1))
    def _():
        drain(1 - slot)


def _scatter_rows(code, offs, pad0, padn, n_tiles, xn, n_rows):
    tm = TOKEN_TILE
    n = xn.shape[0] // ROW_TILE
    return pl.pallas_call(
        functools.partial(_scatter_rows_body, tile=tm, n_tok=n, out_tiles=n_rows // MOE_TILE),
        out_shape=jax.ShapeDtypeStruct((n_rows * ROW_TILE, LANES), F32),
        grid_spec=pltpu.PrefetchScalarGridSpec(
            num_scalar_prefetch=5,
            grid=(n // tm,),
            in_specs=[pl.BlockSpec((tm * ROW_TILE, LANES), lambda i, *_: (i, 0))],
            out_specs=pl.BlockSpec(memory_space=pl.ANY),
            scratch_shapes=[pltpu.VMEM((2, tm * ROW_TILE, LANES), F32),
                            pltpu.VMEM((MOE_TILE * ROW_TILE, LANES), F32),
                            pltpu.SemaphoreType.DMA((2,)), pltpu.SemaphoreType.DMA(())],
        ),
        compiler_params=pltpu.CompilerParams(dimension_semantics=("arbitrary",)),
        name="moe_scatter",
    )(code, offs, pad0, padn, n_tiles, xn)


def _experts_body(te_ref, nt_ref, x_ref, w1_ref, w3_ref, w2_ref, y_ref, w1_s, w3_s, w2_s):
    i = pl.program_id(0)
    prev = te_ref[jnp.maximum(i - 1, 0)]
    fresh = jnp.logical_or(i == 0, te_ref[i] != prev)

    @pl.when(fresh)
    def _():
        w1_s[...] = w1_ref[0].astype(BF16)
        w3_s[...] = w3_ref[0].astype(BF16)
        w2_s[...] = w2_ref[0].astype(BF16)

    @pl.when(i < nt_ref[0])
    def _():
        x = _load_row_tiles(x_ref, MOE_TILE).astype(BF16)
        a = _dot(x, w1_s[...])
        b = _dot(x, w3_s[...])
        _store_row_tiles(y_ref, _dot((jax.nn.silu(a) * b).astype(BF16), w2_s[...]))

    @pl.when(i >= nt_ref[0])
    def _():
        y_ref[...] = jnp.zeros_like(y_ref)


def _experts(tile_expert, n_tiles, xs, w1, w3, w2):
    rows = xs.shape[0] // ROW_TILE
    tm = MOE_TILE
    return pl.pallas_call(
        _experts_body,
        out_shape=jax.ShapeDtypeStruct((rows * ROW_TILE, LANES), F32),
        grid_spec=pltpu.PrefetchScalarGridSpec(
            num_scalar_prefetch=2,
            grid=(rows // tm,),
            in_specs=[pl.BlockSpec((tm * ROW_TILE, LANES),
                                   lambda i, te, nt: (jnp.minimum(i, nt[0] - 1), 0)),
                      pl.BlockSpec((1, D_MODEL, D_EXPERT), lambda i, te, nt: (te[i], 0, 0)),
                      pl.BlockSpec((1, D_MODEL, D_EXPERT), lambda i, te, nt: (te[i], 0, 0)),
                      pl.BlockSpec((1, D_EXPERT, D_MODEL), lambda i, te, nt: (te[i], 0, 0))],
            out_specs=pl.BlockSpec((tm * ROW_TILE, LANES), lambda i, te, nt: (i, 0)),
            scratch_shapes=[pltpu.VMEM((D_MODEL, D_EXPERT), BF16), pltpu.VMEM((D_MODEL, D_EXPERT), BF16),
                            pltpu.VMEM((D_EXPERT, D_MODEL), BF16)],
        ),
        compiler_params=_cparams("arbitrary"),
        name="moe_experts",
    )(tile_expert, n_tiles, xs, w1, w3, w2)


def _moe_plan(counts, n):
    counts = counts[0, :N_EXPERTS].astype(I32)
    padded = ((counts + MOE_TILE - 1) // MOE_TILE) * MOE_TILE
    ends = jnp.cumsum(padded)
    offs = ends - padded
    max_tiles = (2 * n + N_EXPERTS * (MOE_TILE - 1) + MOE_TILE - 1) // MOE_TILE
    tile_start = jnp.arange(max_tiles, dtype=I32) * MOE_TILE
    tile_expert = jnp.minimum(jnp.sum((tile_start[:, None] >= ends[None, :]).astype(I32), axis=1),
                              N_EXPERTS - 1)
    n_tiles = (ends[-1] // MOE_TILE).reshape(1)
    return offs, offs + counts, padded - counts, tile_expert, n_tiles, max_tiles * MOE_TILE


def _moe(xn, code, counts, w1, w3, w2):
    n = xn.shape[0] // ROW_TILE
    code = code.astype(I32).reshape(2 * n)
    offs, pad0, padn, tile_expert, n_tiles, n_rows = _moe_plan(counts, n)
    xs = _scatter_rows(code, offs, pad0, padn, n_tiles, xn, n_rows)
    ys = _experts(tile_expert, n_tiles, xs, w1, w3, w2)
    return _gather_rows(code, offs, ys).reshape(2, n * ROW_TILE, LANES)


def _add_moe(h, ya_ref, yb_ref, route):
    rows = h.shape[0]
    ya = _load_row_tiles(ya_ref.at[0], rows)
    yb = _load_row_tiles(yb_ref.at[0], rows)
    return h + (route[:, 2:3] * ya + route[:, 3:4] * yb)


def _moe_out_spec(k, tm):
    return pl.BlockSpec((1, tm * ROW_TILE, LANES), lambda i: (k, i, 0))


def _proj_c_body(h_ref, ya_ref, yb_ref, r_ref, g_ref, w_ref, h_o, u_o, bg_o):
    h2 = _add_moe(h_ref[...], ya_ref, yb_ref, r_ref[...])
    h_o[...] = h2
    xb = _rms(h2, g_ref[...]).astype(BF16)
    bg_o[...] = _dot(xb, w_ref[:, 0:D_MODEL])
    u_o[...] = _dot(xb, w_ref[:, D_MODEL:2 * D_MODEL]) * _dot(xb, w_ref[:, 2 * D_MODEL:])


def _proj_c(h, y2, route, gain, w16):
    n = h.shape[0]
    tm = TOKEN_TILE
    row = lambda w: pl.BlockSpec((tm, w), lambda i: (i, 0))
    ysp = lambda k: _moe_out_spec(k, tm)
    return pl.pallas_call(
        _proj_c_body,
        out_shape=[jax.ShapeDtypeStruct((n, D_MODEL), F32)] * 3,
        grid=(n // tm,),
        in_specs=[row(D_MODEL), ysp(0), ysp(1), row(LANES), _const_spec((1, D_MODEL)),
                  _const_spec(w16.shape)],
        out_specs=[row(D_MODEL)] * 3,
        compiler_params=_cparams("parallel"),
        name="proj_c",
    )(h, y2, y2, route, gain, w16)


def _conv_out_body(u_ref, up_ref, stp_ref, sts_ref, bg_ref, h_ref, cw_ref, w_ref, gf_ref,
                   wrh_ref, wrl_ref, br_ref, ts_ref, h_o, xn_o, route_o, cnt_o, code_o, cnt_scr,
                   *, tile, tiles_p, seq_p, seq_s):
    i = pl.program_id(0)

    @pl.when(i == 0)
    def _():
        cnt_scr[...] = jnp.zeros_like(cnt_scr)

    u = u_ref[...]
    rowi = lax.broadcasted_iota(I32, (tile, D_MODEL), 0)
    is_p = i < tiles_p
    at_start = (i * tile) % seq_p == 0
    stp = stp_ref[0]
    prev = up_ref[...]
    m2_p = jnp.where(at_start, stp[0:1, :], prev[6:7, :])
    m1_p = jnp.where(at_start, stp[1:2, :], prev[7:8, :])
    per = tile // seq_s
    sts = sts_ref[...]
    m2_s = jnp.broadcast_to(sts[:, 0:1, :], (per, seq_s, D_MODEL)).reshape(tile, D_MODEL)
    m1_s = jnp.broadcast_to(sts[:, 1:2, :], (per, seq_s, D_MODEL)).reshape(tile, D_MODEL)
    m2 = jnp.where(is_p, m2_p, m2_s)
    m1 = jnp.where(is_p, m1_p, m1_s)
    pos = jnp.where(is_p, rowi, rowi % seq_s)
    u1 = jnp.where(pos == 0, m1, pltpu.roll(u, 1, axis=0))
    u2 = jnp.where(pos == 0, m2, jnp.where(pos == 1, m1, pltpu.roll(u, 2, axis=0)))
    cw = cw_ref[...]
    conv = u2 * cw[0:1, :] + u1 * cw[1:2, :] + u * cw[2:3, :]
    mix = _dot((bg_ref[...] * conv).astype(BF16), w_ref[...])
    h3 = h_ref[...] + mix
    h_o[...] = h3
    xn = _rms(h3, gf_ref[...])
    _store_row_tiles(xn_o, xn)
    _route(xn, wrh_ref[...], wrl_ref[...], br_ref[...], ts_ref[...], cnt_scr, route_o, cnt_o, code_o)


def _conv_out(u, bg, h, state_p, state_s, cw, w16, gf, wrh, wrl, br, tstrict, seq_p, seq_s):
    tm = TOKEN_TILE
    n = u.shape[0]
    tiles_p = state_p.shape[0] * seq_p // tm
    per = tm // seq_s
    row = lambda w: pl.BlockSpec((tm, w), lambda i: (i, 0))
    prev = pl.BlockSpec((8, D_MODEL), lambda i: (jnp.maximum(i * (tm // 8) - 1, 0), 0))
    stp_spec = pl.BlockSpec((1, 2, D_MODEL),
                            lambda i: (jnp.minimum(i, tiles_p - 1) * tm // seq_p, 0, 0))
    sts_spec = pl.BlockSpec((per, 2, D_MODEL), lambda i: (jnp.maximum(i - tiles_p, 0), 0, 0))
    return pl.pallas_call(
        functools.partial(_conv_out_body, tile=tm, tiles_p=tiles_p, seq_p=seq_p, seq_s=seq_s),
        out_shape=[jax.ShapeDtypeStruct((n, D_MODEL), F32),
                   jax.ShapeDtypeStruct((n * ROW_TILE, LANES), F32),
                   jax.ShapeDtypeStruct((n, LANES), F32), jax.ShapeDtypeStruct((1, LANES), F32),
                   jax.ShapeDtypeStruct((2, n), F32)],
        grid=(n // tm,),
        in_specs=[row(D_MODEL), prev, stp_spec, sts_spec, row(D_MODEL), row(D_MODEL),
                  _const_spec((3, D_MODEL)),
                  _const_spec((D_MODEL, D_MODEL)), _const_spec((1, D_MODEL)),
                  _const_spec((D_MODEL, LANES)), _const_spec((D_MODEL, LANES)),
                  _const_spec((1, LANES)), _const_spec((tm, tm))],
        out_specs=[row(D_MODEL), pl.BlockSpec((tm * ROW_TILE, LANES), lambda i: (i, 0)), row(LANES),
                   _const_spec((1, LANES)), pl.BlockSpec((2, tm), lambda i: (0, i))],
        scratch_shapes=[pltpu.VMEM((1, LANES), F32)],
        compiler_params=_cparams("arbitrary"),
        name="conv_out",
    )(u, u, state_p, state_s, bg, h, cw, w16, gf, wrh, wrl, br, tstrict)


def _combine_body(h_ref, ya_ref, yb_ref, r_ref, op_ref, os_ref, *, tiles_p):
    out = _add_moe(h_ref[...], ya_ref, yb_ref, r_ref[...])

    @pl.when(pl.program_id(0) < tiles_p)
    def _():
        op_ref[...] = out

    @pl.when(pl.program_id(0) >= tiles_p)
    def _():
        os_ref[...] = out


def _combine(h, y2, route, n_p):
    n = h.shape[0]
    tm = TOKEN_TILE
    tiles_p = n_p // tm
    row = lambda w: pl.BlockSpec((tm, w), lambda i: (i, 0))
    ysp = lambda k: _moe_out_spec(k, tm)
    return pl.pallas_call(
        functools.partial(_combine_body, tiles_p=tiles_p),
        out_shape=[jax.ShapeDtypeStruct((n_p, D_MODEL), F32), jax.ShapeDtypeStruct((n - n_p, D_MODEL), F32)],
        grid=(n // tm,),
        in_specs=[row(D_MODEL), ysp(0), ysp(1), row(LANES)],
        out_specs=_group_specs(D_MODEL, tiles_p),
        compiler_params=_cparams("arbitrary"),
        name="moe_combine",
    )(h, y2, y2, route)


def _rel_bias_toeplitz(rel_bias, rows, cols, lead):
    period = rows + cols
    k = jnp.arange(period)
    d = jnp.where(k < cols, k, k - period)
    idx = jnp.clip(lead - d, -REL_CLIP, REL_CLIP) + REL_CLIP
    v = jnp.take(rel_bias.astype(F32), idx, axis=1)
    heads = v.shape[0]
    skew = jnp.tile(v, (1, rows))[:, :rows * (period - 1)].reshape(heads, rows, period - 1)
    return skew[:, :, :cols]


def _bias_prompt(rel_bias):
    r = jnp.arange(ATTN_TILE)[:, None]
    c = jnp.arange(ATTN_WINDOW)[None, :]
    j = c // CHUNK - r // CHUNK
    band = jnp.logical_and(j >= 0, j <= BAND_CHUNKS)
    bias = _rel_bias_toeplitz(rel_bias, ATTN_TILE, ATTN_WINDOW, BAND_CHUNKS * CHUNK)
    return jnp.where(band[None], bias, NEG_INF)


def _bias_sample(rel_bias, la, tn):
    bias = _rel_bias_toeplitz(rel_bias, tn, la + tn, la)
    return bias[:, :, :la], bias[:, :, la:]


def _router_weights(w_group, b_group, w_expert, b_expert):
    pad = LANES - N_EXPERTS - N_GROUPS
    w = jnp.concatenate([w_expert, w_group, jnp.zeros((D_MODEL, pad), F32)], axis=1)
    b = jnp.concatenate([b_expert, b_group, jnp.zeros((pad,), F32)])[None, :].astype(F32)
    hi, lo = _split2(w.astype(F32))
    return hi, lo, b


def kernel(x_prompt, x_sample, cache_a_k, cache_a_v, state_hgrn, state_conv, norm_mix, norm_ffn,
           w_in_ab, w_out_ab, q_norm, k_norm, rel_bias, hgrn_lb_logits, hgrn_out_norm, w_in_c,
           conv_w, w_out_c, w_group, b_group, w_expert, b_expert, w1, w3, w2):
    batch, seq, d = x_prompt.shape
    nseq_s, tn, _ = x_sample.shape
    la = cache_a_k.shape[2]
    n_p = batch * seq
    n_s = nseq_s * tn
    n = n_p + n_s
    keep = min(BAND_CHUNKS * CHUNK, seq)

    x_p = x_prompt.reshape(n_p, d)
    x_s = x_sample.reshape(n_s, d)
    lb_all = jnp.cumsum(jax.nn.softmax(hgrn_lb_logits.astype(F32), axis=0), axis=0)
    head_avg = jnp.kron(jnp.eye(N_HEADS, dtype=F32),
                        jnp.full((HEAD_DIM, HEAD_DIM), 1.0 / HEAD_DIM, F32)).astype(BF16)
    t = jnp.arange(TOKEN_TILE)
    tstrict = (t[None, :] < t[:, None]).astype(BF16)
    row1 = lambda v: v.astype(F32).reshape(1, -1)
    tile8 = lambda v: jnp.tile(v.astype(F32), N_HEADS).reshape(1, -1)

    l = 0
    (q16, kf, vf, k16, v16, lf, kb, vb, qb, gb) = _proj_ab(
        x_p, x_s, row1(norm_mix[0]), w_in_ab[l].astype(BF16), tile8(q_norm[l]), tile8(k_norm[l]),
        row1(lb_all[l]), head_avg, seq, keep)

    oa_p = _attn_prompt(q16, k16, v16, _bias_prompt(rel_bias[l]), batch, seq)
    bias_c, bias_n = _bias_sample(rel_bias[l], la, tn)
    oa_s = _attn_sample(q16, k16, v16, cache_a_k[l].reshape(nseq_s, la, D_HALF),
                        cache_a_v[l].reshape(nseq_s, la, D_HALF), bias_c, bias_n, n_p, nseq_s, tn)

    zeros_state = jnp.zeros((batch, N_HEADS, HEAD_DIM, HEAD_DIM), F32)
    ob_p, st_p = _hgrn(qb, kb, vb, lf, zeros_state, 0, batch, seq, 64, 4)
    ob_s, st_s = _hgrn(qb, kb, vb, lf, jnp.swapaxes(state_hgrn[l].astype(F32), -1, -2),
                       n_p, nseq_s, tn, tn, 1)

    wrh, wrl, br = _router_weights(w_group[0], b_group[0], w_expert[0], b_expert[0])
    h1, xn1, route1, cnt1, code1 = _out_ab(x_p, x_s, oa_p, oa_s, ob_p, ob_s, gb, row1(hgrn_out_norm[l]),
                                    head_avg, w_out_ab[l].astype(BF16), row1(norm_ffn[0]),
                                    wrh, wrl, br, tstrict)
    y1 = _moe(xn1, code1, cnt1, w1[0], w3[0], w2[0])

    h2, u, bg = _proj_c(h1, y1, route1, row1(norm_mix[1]), w_in_c[0].astype(BF16))
    wrh, wrl, br = _router_weights(w_group[1], b_group[1], w_expert[1], b_expert[1])
    h3, xn2, route2, cnt2, code2 = _conv_out(u, bg, h2, jnp.zeros((batch, 2, d), F32),
                                      state_conv[0].astype(F32), conv_w[0].astype(F32),
                                      w_out_c[0].astype(BF16), row1(norm_ffn[1]), wrh, wrl, br,
                                      tstrict, seq, tn)
    y2 = _moe(xn2, code2, cnt2, w1[1], w3[1], w2[1])
    out_p, out_s = _combine(h3, y2, route2, n_p)

    y_prompt = out_p.reshape(batch, seq, d)
    y_sample = out_s.reshape(nseq_s, tn, d)
    n_kp = batch * keep
    heads = lambda a, b_, t_: a.reshape(b_, t_, N_HEADS, HEAD_DIM)
    nk_p = heads(kf[:n_kp], batch, keep)[None]
    nv_p = heads(vf[:n_kp], batch, keep)[None]
    kf_s = heads(kf[n_kp:n_kp + n_s], nseq_s, tn)
    vf_s = heads(vf[n_kp:n_kp + n_s], nseq_s, tn)
    nk_s = jnp.concatenate([cache_a_k[l].astype(F32), kf_s], axis=1)[:, -la:][None]
    nv_s = jnp.concatenate([cache_a_v[l].astype(F32), vf_s], axis=1)[:, -la:][None]
    nh_p = jnp.swapaxes(st_p, -1, -2)[None]
    nh_s = jnp.swapaxes(st_s, -1, -2)[None]
    nc_p = jnp.stack([u[(b + 1) * seq - 2:(b + 1) * seq] for b in range(batch)])[None]
    u_s = u[n_p:].reshape(nseq_s, tn, d)
    nc_s = jnp.concatenate([state_conv[0].astype(F32), u_s], axis=1)[:, -2:][None]
    return (y_prompt, y_sample, nk_p, nv_p, nk_s, nv_s, nh_p, nh_s, nc_p, nc_s)
```

```python
import functools

import jax
import jax.numpy as jnp
from jax import lax
from jax.experimental import pallas as pl
from jax.experimental.pallas import tpu as pltpu

F32 = jnp.float32
BF16 = jnp.bfloat16
I32 = jnp.int32

D_MODEL = 1024
CHUNK = 64
BAND_CHUNKS = 8
HEAD_DIM = 64
N_HEADS = 8
D_HALF = N_HEADS * HEAD_DIM
REL_CLIP = 128
HGRN_BLOCK = 16
N_GROUPS = 4
EXPERTS_PER_GROUP = 8
N_EXPERTS = N_GROUPS * EXPERTS_PER_GROUP
D_EXPERT = 256
RMS_EPS = 1e-6
NEG_INF = -1e30

LANES = 128
ROW_TILE = D_MODEL // LANES
TOKEN_TILE = 256
ATTN_TILE = 256
ATTN_WINDOW = ATTN_TILE + BAND_CHUNKS * CHUNK
ATTN_SPAN = -(-(BAND_CHUNKS + 1) * CHUNK // LANES) * LANES
ATTN_HEAD_GROUP = 4
MOE_TILE = 256
RANK_BITS = 16
RANK_SPAN = 1 << RANK_BITS
GATHER_UNROLL = 8
VMEM_LIMIT = 48 * 1024 * 1024


def _cparams(*sem):
    return pltpu.CompilerParams(dimension_semantics=sem, vmem_limit_bytes=VMEM_LIMIT)


def _const_spec(shape):
    nd = len(shape)
    return pl.BlockSpec(shape, lambda *_: (0,) * nd)


def _store_row_tiles(ref, val):
    rows = val.shape[0]
    for c in range(ROW_TILE):
        ref[pl.ds(c, rows, stride=ROW_TILE), :] = val[:, c * LANES:(c + 1) * LANES]


def _load_row_tiles(ref, rows):
    return jnp.concatenate([ref[pl.ds(c, rows, stride=ROW_TILE), :] for c in range(ROW_TILE)], axis=1)


def _rms(x, gain):
    ms = jnp.mean(x * x, axis=-1, keepdims=True)
    return (x * lax.rsqrt(ms + RMS_EPS)) * gain


def _split2(x):
    hi = x.astype(BF16)
    lo = (x - hi.astype(F32)).astype(BF16)
    return hi, lo


def _split3(x):
    p0 = x.astype(BF16)
    r = x - p0.astype(F32)
    p1 = r.astype(BF16)
    p2 = (r - p1.astype(F32)).astype(BF16)
    return p0, p1, p2


def _dot(a, b):
    return jnp.dot(a, b, preferred_element_type=F32)


def _dot_nt(a, b):
    return lax.dot_general(a, b, (((1,), (1,)), ((), ())), preferred_element_type=F32)


def _dot_tn(a, b):
    return lax.dot_general(a, b, (((0,), (0,)), ((), ())), preferred_element_type=F32)


def _head_mean_sq(v, bd):
    hi, lo = _split2(v * v)
    return _dot(hi, bd) + _dot(lo, bd)


def _group_specs(width, tiles_p, tm=TOKEN_TILE):
    return [pl.BlockSpec((tm, width), lambda i: (jnp.minimum(i, tiles_p - 1), 0)),
            pl.BlockSpec((tm, width), lambda i: (jnp.maximum(i - tiles_p, 0), 0))]


def _group_pick(p_ref, s_ref, tiles_p):
    return jnp.where(pl.program_id(0) < tiles_p, p_ref[...], s_ref[...])


def _proj_ab_body(xp_ref, xs_ref, g_ref, w_ref, qg_ref, kg_ref, lb_ref, bd_ref,
                  q_o, kf_o, vf_o, k16_o, v16_o, lf_o, kb_o, vb_o, qb_o, gb_o, *, tiles_p):
    xb = _rms(_group_pick(xp_ref, xs_ref, tiles_p), g_ref[...]).astype(BF16)
    bd = bd_ref[...]

    def seg(j):
        return _dot(xb, w_ref[:, j * D_HALF:(j + 1) * D_HALF])

    qa = seg(0)
    qn = qa * lax.rsqrt(_head_mean_sq(qa, bd) + RMS_EPS) * qg_ref[...]
    q_o[...] = (qn * (HEAD_DIM ** -0.5)).astype(BF16)
    ka = seg(1)
    kn = ka * lax.rsqrt(_head_mean_sq(ka, bd) + RMS_EPS) * kg_ref[...]
    kf_o[...] = kn
    k16_o[...] = kn.astype(BF16)
    va = seg(2)
    vf_o[...] = va
    v16_o[...] = va.astype(BF16)
    lb = lb_ref[...]
    f = lb + (1.0 - lb) * jax.nn.sigmoid(seg(3))
    lf_o[...] = jnp.log(f)
    kb_o[...] = 1.0 - f
    vb_o[...] = seg(4)
    qb_o[...] = jax.nn.silu(seg(5))
    gb_o[...] = seg(6)


def _proj_ab(x_p, x_s, gain, w16, qg, kg, lb, bd, seq, keep):
    tm = TOKEN_TILE
    n_p, n_s = x_p.shape[0], x_s.shape[0]
    n = n_p + n_s
    tiles_p, tiles_seq, tiles_keep = n_p // tm, seq // tm, keep // tm
    kept_tiles = (n_p // seq) * tiles_keep + n_s // tm

    def keep_map(i):
        b, j = i // tiles_seq, i % tiles_seq
        prompt_slot = b * tiles_keep + jnp.maximum(j - (tiles_seq - tiles_keep), 0)
        return (jnp.where(i >= tiles_p, kept_tiles - n_s // tm + (i - tiles_p), prompt_slot), 0)

    row = pl.BlockSpec((tm, D_HALF), lambda i: (i, 0))
    kept = pl.BlockSpec((tm, D_HALF), keep_map)
    full = lambda dt: jax.ShapeDtypeStruct((n, D_HALF), dt)
    kept_shape = jax.ShapeDtypeStruct((kept_tiles * tm, D_HALF), F32)
    return pl.pallas_call(
        functools.partial(_proj_ab_body, tiles_p=tiles_p),
        out_shape=[full(BF16), kept_shape, kept_shape, full(BF16), full(BF16),
                   full(F32), full(F32), full(F32), full(F32), full(F32)],
        grid=(n // tm,),
        in_specs=_group_specs(D_MODEL, tiles_p) + [
            _const_spec((1, D_MODEL)), _const_spec(w16.shape),
            _const_spec((1, D_HALF)), _const_spec((1, D_HALF)), _const_spec((1, D_HALF)),
            _const_spec((D_HALF, D_HALF))],
        out_specs=[row, kept, kept] + [row] * 7,
        compiler_params=_cparams("arbitrary"),
        name="proj_ab",
    )(x_p, x_s, gain, w16, qg, kg, lb, bd)


def _attn_prompt_body(q_ref, k0, k1, k2, v0, v1, v2, bias_ref, o_ref):
    i = pl.program_id(1)
    lim = jnp.maximum(2 - i, 0) * ATTN_TILE
    col = lax.broadcasted_iota(I32, (CHUNK, ATTN_SPAN), 1)
    cut = ATTN_SPAN - 2 * ATTN_TILE

    def span(parts, start):
        if start == 0:
            return (parts[0], parts[1], parts[2][:cut])
        return (parts[0][start:], parts[1], parts[2])

    for h0 in range(0, N_HEADS, ATTN_HEAD_GROUP):
        units = []
        for h in range(h0, h0 + ATTN_HEAD_GROUP):
            hs = slice(h * HEAD_DIM, (h + 1) * HEAD_DIM)
            kh = (k0[:, hs], k1[:, hs], k2[:, hs])
            vh = (v0[:, hs], v1[:, hs], v2[:, hs])
            for r in range(ATTN_TILE // CHUNK):
                start = (r * CHUNK // LANES) * LANES
                units.append((h, hs, slice(r * CHUNK, (r + 1) * CHUNK), start,
                              span(kh, start), span(vh, start)))
        s = [jnp.concatenate([_dot_nt(q_ref[rows, hs], kp) for kp in ks], axis=1)
             for (_, hs, rows, _, ks, _) in units]
        s = [jnp.where(col + start < lim, NEG_INF,
                       s_u + bias_ref[h, rows, start:start + ATTN_SPAN])
             for s_u, (h, _, rows, start, _, _) in zip(s, units)]
        m = [jnp.max(s_u, axis=-1, keepdims=True) for s_u in s]
        p = [jnp.exp(s_u - m_u) for s_u, m_u in zip(s, m)]
        l = [jnp.sum(p_u, axis=-1, keepdims=True) for p_u in p]
        outs = []
        for p_u, (_, _, _, _, _, vs) in zip(p, units):
            p16 = p_u.astype(BF16)
            o, at = 0.0, 0
            for vp in vs:
                o = o + _dot(p16[:, at:at + vp.shape[0]], vp)
                at += vp.shape[0]
            outs.append(o)
        for o_u, l_u, (_, hs, rows, _, _, _) in zip(outs, l, units):
            o_ref[rows, hs] = (o_u / l_u).astype(BF16)


def _attn_prompt(q16, k16, v16, bias_full, batch, seq):
    tiles = seq // ATTN_TILE
    qspec = pl.BlockSpec((ATTN_TILE, D_HALF), lambda b, i: (b * tiles + i, 0))

    def kv(back):
        return pl.BlockSpec((ATTN_TILE, D_HALF),
                            lambda b, i: (b * tiles + jnp.maximum(i - back, 0), 0))

    return pl.pallas_call(
        _attn_prompt_body,
        out_shape=jax.ShapeDtypeStruct((batch * seq, D_HALF), BF16),
        grid=(batch, tiles),
        in_specs=[qspec, kv(2), kv(1), kv(0), kv(2), kv(1), kv(0),
                  _const_spec(bias_full.shape)],
        out_specs=qspec,
        compiler_params=_cparams("parallel", "parallel"),
        name="attn_prompt",
    )(q16, k16, k16, k16, v16, v16, v16, bias_full)


def _attn_sample_body(q_ref, kc_ref, vc_ref, kn_ref, vn_ref, bc_ref, bn_ref, o_ref):
    la = kc_ref.shape[1] // N_HEADS
    for h in range(N_HEADS):
        hs = slice(h * HEAD_DIM, (h + 1) * HEAD_DIM)
        kc = kc_ref[0, pl.ds(h, la, stride=N_HEADS), :].astype(BF16)
        vc = vc_ref[0, pl.ds(h, la, stride=N_HEADS), :].astype(BF16)
        qh = q_ref[:, hs]
        sc = _dot_nt(qh, kc) + bc_ref[h]
        sn = _dot_nt(qh, kn_ref[:, hs]) + bn_ref[h]
        m = jnp.maximum(jnp.max(sc, axis=-1, keepdims=True), jnp.max(sn, axis=-1, keepdims=True))
        pc = jnp.exp(sc - m)
        pn = jnp.exp(sn - m)
        l = jnp.sum(pc, axis=-1, keepdims=True) + jnp.sum(pn, axis=-1, keepdims=True)
        o = _dot(pc.astype(BF16), vc) + _dot(pn.astype(BF16), vn_ref[:, hs])
        o_ref[:, hs] = (o / l).astype(BF16)


def _attn_sample(q16, k16, v16, cache_k, cache_v, bias_c, bias_n, row0, nseq, tn):
    blk0 = row0 // tn
    la8 = cache_k.shape[1]
    new = pl.BlockSpec((tn, D_HALF), lambda b: (blk0 + b, 0))
    cache = pl.BlockSpec((1, la8, HEAD_DIM), lambda b: (b, 0, 0))
    return pl.pallas_call(
        _attn_sample_body,
        out_shape=jax.ShapeDtypeStruct((nseq * tn, D_HALF), BF16),
        grid=(nseq,),
        in_specs=[new, cache, cache, new, new, _const_spec(bias_c.shape), _const_spec(bias_n.shape)],
        out_specs=pl.BlockSpec((tn, D_HALF), lambda b: (b, 0)),
        compiler_params=_cparams("parallel"),
        name="attn_sample",
    )(q16, cache_k, cache_v, k16, v16, bias_c, bias_n)


def _hgrn_body(q_ref, k_ref, v_ref, lf_ref, s0_ref, tblk_ref, tfull_ref, o_ref, sout_ref, st_scr,
               *, chunk, nchunks):
    nblk = chunk // HGRN_BLOCK

    @pl.when(pl.program_id(1) == 0)
    def _():
        st_scr[...] = s0_ref[0]

    row = lax.broadcasted_iota(I32, (chunk, D_HALF), 0)
    r2 = lax.broadcasted_iota(I32, (chunk, chunk), 0)
    c2 = lax.broadcasted_iota(I32, (chunk, chunk), 1)
    same_blk_causal = jnp.logical_and(r2 // HGRN_BLOCK == c2 // HGRN_BLOCK, c2 <= r2)

    def one_chunk(c, carry):
        sl = pl.ds(pl.multiple_of(c * chunk, chunk), chunk)
        q = q_ref[sl, :]
        k = k_ref[sl, :]
        v16 = v_ref[sl, :].astype(BF16)
        l0, l1, l2 = _split3(lf_ref[sl, :])
        tb = tblk_ref[...]
        tf = tfull_ref[...]
        b_in = _dot(tb, l0) + _dot(tb, l1) + _dot(tb, l2)
        b_ch = _dot(tf, l0) + _dot(tf, l1) + _dot(tf, l2)
        ld = (q * jnp.exp(b_in)).astype(BF16)
        rd = (k * jnp.exp(-b_in)).astype(BF16)
        lj, rj = [], []
        for j in range(nblk - 1):
            e_j = b_ch[(j + 1) * HGRN_BLOCK - 1:(j + 1) * HGRN_BLOCK, :]
            later = row >= (j + 1) * HGRN_BLOCK
            inside = jnp.logical_and(row >= j * HGRN_BLOCK, row < (j + 1) * HGRN_BLOCK)
            lj.append(jnp.where(later, q * jnp.exp(jnp.minimum(b_ch - e_j, 0.0)), 0.0).astype(BF16))
            rj.append(jnp.where(inside, k * jnp.exp(jnp.minimum(e_j - b_ch, 0.0)), 0.0).astype(BF16))
        e_end = b_ch[chunk - 1:chunk, :]
        qc = (q * jnp.exp(b_ch)).astype(BF16)
        kc = (k * jnp.exp(e_end - b_ch)).astype(BF16)
        dec = jnp.exp(e_end)
        heads = [slice(h * HEAD_DIM, (h + 1) * HEAD_DIM) for h in range(N_HEADS)]
        sts = [st_scr[h] for h in range(N_HEADS)]
        diag = [_dot_nt(ld[:, hs], rd[:, hs]) for hs in heads]
        off = []
        for hs in heads:
            acc = None
            for j in range(nblk - 1):
                term = _dot_nt(lj[j][:, hs], rj[j][:, hs])
                acc = term if acc is None else acc + term
            off.append(acc)
        inter = [_dot_nt(qc[:, hs], st.astype(BF16)) for hs, st in zip(heads, sts)]
        upd = [_dot_tn(v16[:, hs], kc[:, hs]) for hs in heads]
        sc16 = [(jnp.where(same_blk_causal, d, 0.0) + o_).astype(BF16) for d, o_ in zip(diag, off)]
        intra = [_dot(s_, v16[:, hs]) for s_, hs in zip(sc16, heads)]
        o_ref[sl, :] = jnp.concatenate([a + b for a, b in zip(intra, inter)], axis=1)
        for h, hs in enumerate(heads):
            st_scr[h] = sts[h] * dec[:, hs] + upd[h]
        return carry

    lax.fori_loop(0, nchunks, one_chunk, 0)
    sout_ref[0] = st_scr[...]


def _hgrn(qb, kb, vb, lf, s0t, row0, nseq, seq, chunk, nchunks):
    tt = chunk * nchunks
    steps = seq // tt
    blk0 = row0 // tt
    t = jnp.arange(chunk)
    lower = t[None, :] <= t[:, None]
    tfull = lower.astype(BF16)
    tblk = jnp.logical_and(lower, (t[None, :] // HGRN_BLOCK) == (t[:, None] // HGRN_BLOCK)).astype(BF16)
    tok = pl.BlockSpec((tt, D_HALF), lambda b, j: (blk0 + b * steps + j, 0))
    state = pl.BlockSpec((1, N_HEADS, HEAD_DIM, HEAD_DIM), lambda b, j: (b, 0, 0, 0))
    return pl.pallas_call(
        functools.partial(_hgrn_body, chunk=chunk, nchunks=nchunks),
        out_shape=[jax.ShapeDtypeStruct((nseq * seq, D_HALF), F32),
                   jax.ShapeDtypeStruct(s0t.shape, F32)],
        grid=(nseq, steps),
        in_specs=[tok, tok, tok, tok, state, _const_spec((chunk, chunk)), _const_spec((chunk, chunk))],
        out_specs=[pl.BlockSpec((tt, D_HALF), lambda b, j: (b * steps + j, 0)), state],
        scratch_shapes=[pltpu.VMEM((N_HEADS, HEAD_DIM, HEAD_DIM), F32)],
        compiler_params=_cparams("parallel", "arbitrary"),
        name="hgrn",
    )(qb, kb, vb, lf, s0t, tblk, tfull)


def _route(xn, wr_hi, wr_lo, br, tstrict, cnt_scr, route_o, cnt_o, code_o):
    rows = xn.shape[0]
    x_hi, x_lo = _split2(xn)
    logits = _dot(x_hi, wr_hi) + _dot(x_lo, wr_hi) + _dot(x_hi, wr_lo) + br
    lane = lax.broadcasted_iota(I32, (rows, LANES), 1)
    lane_f = lane.astype(F32)

    def first_argmax(vals):
        m = jnp.max(vals, axis=-1, keepdims=True)
        idx = jnp.min(jnp.where(vals == m, lane_f, float(LANES)), axis=-1, keepdims=True)
        return m, idx.astype(I32)

    is_grp = jnp.logical_and(lane >= N_EXPERTS, lane < N_EXPERTS + N_GROUPS)
    gl = jnp.where(is_grp, logits, -jnp.inf)
    gmax, gidx = first_argmax(gl)
    p_grp = 1.0 / jnp.sum(jnp.exp(gl - gmax), axis=-1, keepdims=True)
    grp = gidx - N_EXPERTS
    in_grp = jnp.logical_and(lane < N_EXPERTS, lane // EXPERTS_PER_GROUP == grp)
    el = jnp.where(in_grp, logits, -jnp.inf)
    v1, e1 = first_argmax(el)
    v2, e2 = first_argmax(jnp.where(lane == e1, -jnp.inf, el))
    t2 = jnp.exp(v2 - v1)
    den = 1.0 + t2
    g1 = (1.0 / den) * p_grp
    g2 = (t2 / den) * p_grp

    oh1 = lane == e1
    oh2 = lane == e2
    oh = jnp.where(jnp.logical_or(oh1, oh2), 1.0, 0.0)
    before = _dot(tstrict, oh.astype(BF16)) + cnt_scr[...]
    rank1 = jnp.sum(jnp.where(oh1, before, 0.0), axis=-1, keepdims=True)
    rank2 = jnp.sum(jnp.where(oh2, before, 0.0), axis=-1, keepdims=True)
    cnt_scr[...] = cnt_scr[...] + jnp.sum(oh, axis=0, keepdims=True)
    cnt_o[...] = cnt_scr[...]

    e1f, e2f = e1.astype(F32), e2.astype(F32)
    fields = (e1f, e2f, g1, g2, rank1, rank2, e1f * RANK_SPAN + rank1, e2f * RANK_SPAN + rank2)
    slab = jnp.zeros((rows, LANES), F32)
    for idx, val in enumerate(fields):
        slab = jnp.where(lane == idx, val, slab)
    route_o[...] = slab
    code_o[...] = slab.T[6:8, :]


def _out_ab_body(hp_ref, hs_ref, oap_ref, oas_ref, obp_ref, obs_ref, gb_ref, og_ref, bd_ref, w_ref,
                 gf_ref, wrh_ref, wrl_ref, br_ref, ts_ref, h_o, xn_o, route_o, cnt_o, code_o, cnt_scr,
                 *, tiles_p):
    @pl.when(pl.program_id(0) == 0)
    def _():
        cnt_scr[...] = jnp.zeros_like(cnt_scr)

    ob = _group_pick(obp_ref, obs_ref, tiles_p)
    obn = ob * lax.rsqrt(_head_mean_sq(ob, bd_ref[...]) + RMS_EPS) * og_ref[...]
    obg = (obn * jax.nn.silu(gb_ref[...])).astype(BF16)
    oa = _group_pick(oap_ref, oas_ref, tiles_p)
    mix = _dot(oa, w_ref[0:D_HALF, :]) + _dot(obg, w_ref[D_HALF:, :])
    h1 = _group_pick(hp_ref, hs_ref, tiles_p) + mix
    h_o[...] = h1
    xn = _rms(h1, gf_ref[...])
    _store_row_tiles(xn_o, xn)
    _route(xn, wrh_ref[...], wrl_ref[...], br_ref[...], ts_ref[...], cnt_scr, route_o, cnt_o, code_o)


def _out_ab(h_p, h_s, oa_p, oa_s, ob_p, ob_s, gb, og, bd, w16, gf, wrh, wrl, br, tstrict):
    tm = TOKEN_TILE
    n = h_p.shape[0] + h_s.shape[0]
    tiles_p = h_p.shape[0] // tm
    row = lambda w: pl.BlockSpec((tm, w), lambda i: (i, 0))
    return pl.pallas_call(
        functools.partial(_out_ab_body, tiles_p=tiles_p),
        out_shape=[jax.ShapeDtypeStruct((n, D_MODEL), F32), jax.ShapeDtypeStruct((n * ROW_TILE, LANES), F32),
                   jax.ShapeDtypeStruct((n, LANES), F32), jax.ShapeDtypeStruct((1, LANES), F32),
                   jax.ShapeDtypeStruct((2, n), F32)],
        grid=(n // tm,),
        in_specs=_group_specs(D_MODEL, tiles_p) + _group_specs(D_HALF, tiles_p)
        + _group_specs(D_HALF, tiles_p) + [
            row(D_HALF), _const_spec((1, D_HALF)),
            _const_spec((D_HALF, D_HALF)), _const_spec((D_MODEL, D_MODEL)),
            _const_spec((1, D_MODEL)), _const_spec((D_MODEL, LANES)),
            _const_spec((D_MODEL, LANES)), _const_spec((1, LANES)), _const_spec((tm, tm))],
        out_specs=[row(D_MODEL), pl.BlockSpec((tm * ROW_TILE, LANES), lambda i: (i, 0)), row(LANES),
                   _const_spec((1, LANES)), pl.BlockSpec((2, tm), lambda i: (0, i))],
        scratch_shapes=[pltpu.VMEM((1, LANES), F32)],
        compiler_params=_cparams("arbitrary"),
        name="out_ab",
    )(h_p, h_s, oa_p, oa_s, ob_p, ob_s, gb, og, bd, w16, gf, wrh, wrl, br, tstrict)


def _row_copy(src, src_row, dst, dst_row, sem):
    s0 = pl.multiple_of(src_row * ROW_TILE, ROW_TILE)
    d0 = pl.multiple_of(dst_row * ROW_TILE, ROW_TILE)
    return pltpu.make_async_copy(src.at[pl.ds(s0, ROW_TILE)], dst.at[pl.ds(d0, ROW_TILE)], sem)


def _sorted_row(code_ref, offs_ref, a):
    code = code_ref[a]
    return offs_ref[code >> RANK_BITS] + (code & (RANK_SPAN - 1))


def _collect_body(code_ref, offs_ref, ys_hbm, h_ref, r_ref, *rest, tile, n_tok, tiles_p):
    outs, (buf, sems) = rest[:-2], rest[-2:]
    i = pl.program_id(0)
    nt = pl.num_programs(0)

    def issue(t):
        slot = t % 2
        base = t * tile

        def one(r, carry):
            tok = base + r
            for k in range(2):
                row = _sorted_row(code_ref, offs_ref, k * n_tok + tok)
                _row_copy(ys_hbm, row, buf.at[slot].at[k], r, sems.at[slot]).start()
            return carry

        lax.fori_loop(0, tile, one, 0, unroll=GATHER_UNROLL)

    @pl.when(i == 0)
    def _():
        issue(0)

    @pl.when(i + 1 < nt)
    def _():
        issue(i + 1)

    slot = i % 2
    for _ in range(2 * tile):
        _row_copy(ys_hbm, 0, buf.at[slot].at[0], 0, sems.at[slot]).wait()
    route = r_ref[...]
    ya = _load_row_tiles(buf.at[slot].at[0], tile)
    yb = _load_row_tiles(buf.at[slot].at[1], tile)
    out = h_ref[...] + (route[:, 2:3] * ya + route[:, 3:4] * yb)
    if tiles_p is None:
        outs[0][...] = out
    else:
        @pl.when(i < tiles_p)
        def _():
            outs[0][...] = out

        @pl.when(i >= tiles_p)
        def _():
            outs[1][...] = out


def _collect(code, offs, ys, h, route, n_p=None):
    tm = TOKEN_TILE
    n = h.shape[0]
    row = lambda w: pl.BlockSpec((tm, w), lambda i, code, offs: (i, 0))
    if n_p is None:
        tiles_p = None
        out_shape = [jax.ShapeDtypeStruct((n, D_MODEL), F32)]
        out_specs = [row(D_MODEL)]
    else:
        tiles_p = n_p // tm
        out_shape = [jax.ShapeDtypeStruct((n_p, D_MODEL), F32),
                     jax.ShapeDtypeStruct((n - n_p, D_MODEL), F32)]
        out_specs = [pl.BlockSpec((tm, D_MODEL), lambda i, code, offs: (jnp.minimum(i, tiles_p - 1), 0)),
                     pl.BlockSpec((tm, D_MODEL), lambda i, code, offs: (jnp.maximum(i - tiles_p, 0), 0))]
    return pl.pallas_call(
        functools.partial(_collect_body, tile=tm, n_tok=n, tiles_p=tiles_p),
        out_shape=out_shape,
        grid_spec=pltpu.PrefetchScalarGridSpec(
            num_scalar_prefetch=2,
            grid=(n // tm,),
            in_specs=[pl.BlockSpec(memory_space=pl.ANY), row(D_MODEL), row(LANES)],
            out_specs=out_specs,
            scratch_shapes=[pltpu.VMEM((2, 2, tm * ROW_TILE, LANES), F32),
                            pltpu.SemaphoreType.DMA((2,))],
        ),
        compiler_params=pltpu.CompilerParams(dimension_semantics=("arbitrary",),
                                             vmem_limit_bytes=VMEM_LIMIT),
        name="moe_collect",
    )(code, offs, ys, h, route)


def _scatter_rows_body(code_ref, offs_ref, pad0_ref, padn_ref, nt_ref, x_ref, xs_hbm,
                       stage, zero_scr, sems, pad_sem, *, tile, n_tok, out_tiles):
    i = pl.program_id(0)
    last = pl.num_programs(0) - 1
    base = i * tile
    slot = i % 2

    def drain(s):
        for _ in range(2 * tile):
            _row_copy(stage.at[s], 0, xs_hbm, 0, sems.at[s]).wait()

    @pl.when(i >= 2)
    def _():
        drain(slot)

    stage[slot] = x_ref[...]

    def issue(r, carry):
        tok = base + r
        src = stage.at[slot]
        _row_copy(src, r, xs_hbm, _sorted_row(code_ref, offs_ref, tok), sems.at[slot]).start()
        _row_copy(src, r, xs_hbm, _sorted_row(code_ref, offs_ref, n_tok + tok),
                  sems.at[slot]).start()
        return carry

    lax.fori_loop(0, tile, issue, 0, unroll=GATHER_UNROLL)

    @pl.when(i == last)
    def _():
        zero_scr[...] = jnp.zeros_like(zero_scr)

        def fill(lo, count):
            def one(r, carry):
                _row_copy(zero_scr, 0, xs_hbm, lo + r, pad_sem).start()
                return carry

            def done(r, carry):
                _row_copy(zero_scr, 0, xs_hbm, 0, pad_sem).wait()
                return carry

            lax.fori_loop(0, count, one, 0)
            lax.fori_loop(0, count, done, 0)

        for e in range(N_EXPERTS):
            fill(pad0_ref[e], padn_ref[e])

        def tail_copy(t):
            rows = MOE_TILE * ROW_TILE
            return pltpu.make_async_copy(
                zero_scr, xs_hbm.at[pl.ds(pl.multiple_of(t * rows, rows), rows)], pad_sem)

        def tail_start(t, carry):
            tail_copy(t).start()
            return carry

        def tail_done(t, carry):
            tail_copy(t).wait()
            return carry

        lax.fori_loop(nt_ref[0], out_tiles, tail_start, 0)
        lax.fori_loop(nt_ref[0], out_tiles, tail_done, 0)
        drain(slot)

    @pl.when(jnp.logical_and(i == last, i >= 1))
    def _():
        drain(1 - slot)


def _scatter_rows(code, offs, pad0, padn, n_tiles, xn, n_rows):
    tm = TOKEN_TILE
    n = xn.shape[0] // ROW_TILE
    return pl.pallas_call(
        functools.partial(_scatter_rows_body, tile=tm, n_tok=n, out_tiles=n_rows // MOE_TILE),
        out_shape=jax.ShapeDtypeStruct((n_rows * ROW_TILE, LANES), F32),
        grid_spec=pltpu.PrefetchScalarGridSpec(
            num_scalar_prefetch=5,
            grid=(n // tm,),
            in_specs=[pl.BlockSpec((tm * ROW_TILE, LANES), lambda i, *_: (i, 0))],
            out_specs=pl.BlockSpec(memory_space=pl.ANY),
            scratch_shapes=[pltpu.VMEM((2, tm * ROW_TILE, LANES), F32),
                            pltpu.VMEM((MOE_TILE * ROW_TILE, LANES), F32),
                            pltpu.SemaphoreType.DMA((2,)), pltpu.SemaphoreType.DMA(())],
        ),
        compiler_params=pltpu.CompilerParams(dimension_semantics=("arbitrary",)),
        name="moe_scatter",
    )(code, offs, pad0, padn, n_tiles, xn)


def _experts_body(te_ref, nt_ref, x_ref, w1_ref, w3_ref, w2_ref, y_ref, w1_s, w3_s, w2_s):
    i = pl.program_id(0)
    prev = te_ref[jnp.maximum(i - 1, 0)]
    fresh = jnp.logical_or(i == 0, te_ref[i] != prev)

    @pl.when(fresh)
    def _():
        w1_s[...] = w1_ref[0].astype(BF16)
        w3_s[...] = w3_ref[0].astype(BF16)
        w2_s[...] = w2_ref[0].astype(BF16)

    @pl.when(i < nt_ref[0])
    def _():
        x = _load_row_tiles(x_ref, MOE_TILE).astype(BF16)
        a = _dot(x, w1_s[...])
        b = _dot(x, w3_s[...])
        _store_row_tiles(y_ref, _dot((jax.nn.silu(a) * b).astype(BF16), w2_s[...]))

    @pl.when(i >= nt_ref[0])
    def _():
        y_ref[...] = jnp.zeros_like(y_ref)


def _experts(tile_expert, n_tiles, xs, w1, w3, w2):
    rows = xs.shape[0] // ROW_TILE
    tm = MOE_TILE
    return pl.pallas_call(
        _experts_body,
        out_shape=jax.ShapeDtypeStruct((rows * ROW_TILE, LANES), F32),
        grid_spec=pltpu.PrefetchScalarGridSpec(
            num_scalar_prefetch=2,
            grid=(rows // tm,),
            in_specs=[pl.BlockSpec((tm * ROW_TILE, LANES),
                                   lambda i, te, nt: (jnp.minimum(i, nt[0] - 1), 0)),
                      pl.BlockSpec((1, D_MODEL, D_EXPERT), lambda i, te, nt: (te[i], 0, 0)),
                      pl.BlockSpec((1, D_MODEL, D_EXPERT), lambda i, te, nt: (te[i], 0, 0)),
                      pl.BlockSpec((1, D_EXPERT, D_MODEL), lambda i, te, nt: (te[i], 0, 0))],
            out_specs=pl.BlockSpec((tm * ROW_TILE, LANES), lambda i, te, nt: (i, 0)),
            scratch_shapes=[pltpu.VMEM((D_MODEL, D_EXPERT), BF16), pltpu.VMEM((D_MODEL, D_EXPERT), BF16),
                            pltpu.VMEM((D_EXPERT, D_MODEL), BF16)],
        ),
        compiler_params=_cparams("arbitrary"),
        name="moe_experts",
    )(tile_expert, n_tiles, xs, w1, w3, w2)


def _moe_plan(counts, n):
    counts = counts[0, :N_EXPERTS].astype(I32)
    padded = ((counts + MOE_TILE - 1) // MOE_TILE) * MOE_TILE
    ends = jnp.cumsum(padded)
    offs = ends - padded
    max_tiles = (2 * n + N_EXPERTS * (MOE_TILE - 1) + MOE_TILE - 1) // MOE_TILE
    tile_start = jnp.arange(max_tiles, dtype=I32) * MOE_TILE
    tile_expert = jnp.minimum(jnp.sum((tile_start[:, None] >= ends[None, :]).astype(I32), axis=1),
                              N_EXPERTS - 1)
    n_tiles = (ends[-1] // MOE_TILE).reshape(1)
    return offs, offs + counts, padded - counts, tile_expert, n_tiles, max_tiles * MOE_TILE


def _moe(xn, h, route, code, counts, w1, w3, w2, n_p=None):
    n = h.shape[0]
    code = code.astype(I32).reshape(2 * n)
    offs, pad0, padn, tile_expert, n_tiles, n_rows = _moe_plan(counts, n)
    xs = _scatter_rows(code, offs, pad0, padn, n_tiles, xn, n_rows)
    ys = _experts(tile_expert, n_tiles, xs, w1, w3, w2)
    return _collect(code, offs, ys, h, route, n_p)


def _proj_c_body(h_ref, g_ref, w_ref, u_o, bg_o):
    xb = _rms(h_ref[...], g_ref[...]).astype(BF16)
    bg_o[...] = _dot(xb, w_ref[:, 0:D_MODEL])
    u_o[...] = _dot(xb, w_ref[:, D_MODEL:2 * D_MODEL]) * _dot(xb, w_ref[:, 2 * D_MODEL:])


def _proj_c(h, gain, w16):
    n = h.shape[0]
    tm = TOKEN_TILE
    row = lambda w: pl.BlockSpec((tm, w), lambda i: (i, 0))
    return pl.pallas_call(
        _proj_c_body,
        out_shape=[jax.ShapeDtypeStruct((n, D_MODEL), F32)] * 2,
        grid=(n // tm,),
        in_specs=[row(D_MODEL), _const_spec((1, D_MODEL)), _const_spec(w16.shape)],
        out_specs=[row(D_MODEL)] * 2,
        compiler_params=_cparams("parallel"),
        name="proj_c",
    )(h, gain, w16)


def _conv_out_body(u_ref, up_ref, stp_ref, sts_ref, bg_ref, h_ref, cw_ref, w_ref, gf_ref,
                   wrh_ref, wrl_ref, br_ref, ts_ref, h_o, xn_o, route_o, cnt_o, code_o, cnt_scr,
                   *, tile, tiles_p, seq_p, seq_s):
    i = pl.program_id(0)

    @pl.when(i == 0)
    def _():
        cnt_scr[...] = jnp.zeros_like(cnt_scr)

    u = u_ref[...]
    rowi = lax.broadcasted_iota(I32, (tile, D_MODEL), 0)
    is_p = i < tiles_p
    at_start = (i * tile) % seq_p == 0
    stp = stp_ref[0]
    prev = up_ref[...]
    m2_p = jnp.where(at_start, stp[0:1, :], prev[6:7, :])
    m1_p = jnp.where(at_start, stp[1:2, :], prev[7:8, :])
    per = tile // seq_s
    sts = sts_ref[...]
    m2_s = jnp.broadcast_to(sts[:, 0:1, :], (per, seq_s, D_MODEL)).reshape(tile, D_MODEL)
    m1_s = jnp.broadcast_to(sts[:, 1:2, :], (per, seq_s, D_MODEL)).reshape(tile, D_MODEL)
    m2 = jnp.where(is_p, m2_p, m2_s)
    m1 = jnp.where(is_p, m1_p, m1_s)
    pos = jnp.where(is_p, rowi, rowi % seq_s)
    u1 = jnp.where(pos == 0, m1, pltpu.roll(u, 1, axis=0))
    u2 = jnp.where(pos == 0, m2, jnp.where(pos == 1, m1, pltpu.roll(u, 2, axis=0)))
    cw = cw_ref[...]
    conv = u2 * cw[0:1, :] + u1 * cw[1:2, :] + u * cw[2:3, :]
    mix = _dot((bg_ref[...] * conv).astype(BF16), w_ref[...])
    h3 = h_ref[...] + mix
    h_o[...] = h3
    xn = _rms(h3, gf_ref[...])
    _store_row_tiles(xn_o, xn)
    _route(xn, wrh_ref[...], wrl_ref[...], br_ref[...], ts_ref[...], cnt_scr, route_o, cnt_o, code_o)


def _conv_out(u, bg, h, state_p, state_s, cw, w16, gf, wrh, wrl, br, tstrict, seq_p, seq_s):
    tm = TOKEN_TILE
    n = u.shape[0]
    tiles_p = state_p.shape[0] * seq_p // tm
    per = tm // seq_s
    row = lambda w: pl.BlockSpec((tm, w), lambda i: (i, 0))
    prev = pl.BlockSpec((8, D_MODEL), lambda i: (jnp.maximum(i * (tm // 8) - 1, 0), 0))
    stp_spec = pl.BlockSpec((1, 2, D_MODEL),
                            lambda i: (jnp.minimum(i, tiles_p - 1) * tm // seq_p, 0, 0))
    sts_spec = pl.BlockSpec((per, 2, D_MODEL), lambda i: (jnp.maximum(i - tiles_p, 0), 0, 0))
    return pl.pallas_call(
        functools.partial(_conv_out_body, tile=tm, tiles_p=tiles_p, seq_p=seq_p, seq_s=seq_s),
        out_shape=[jax.ShapeDtypeStruct((n, D_MODEL), F32),
                   jax.ShapeDtypeStruct((n * ROW_TILE, LANES), F32),
                   jax.ShapeDtypeStruct((n, LANES), F32), jax.ShapeDtypeStruct((1, LANES), F32),
                   jax.ShapeDtypeStruct((2, n), F32)],
        grid=(n // tm,),
        in_specs=[row(D_MODEL), prev, stp_spec, sts_spec, row(D_MODEL), row(D_MODEL),
                  _const_spec((3, D_MODEL)),
                  _const_spec((D_MODEL, D_MODEL)), _const_spec((1, D_MODEL)),
                  _const_spec((D_MODEL, LANES)), _const_spec((D_MODEL, LANES)),
                  _const_spec((1, LANES)), _const_spec((tm, tm))],
        out_specs=[row(D_MODEL), pl.BlockSpec((tm * ROW_TILE, LANES), lambda i: (i, 0)), row(LANES),
                   _const_spec((1, LANES)), pl.BlockSpec((2, tm), lambda i: (0, i))],
        scratch_shapes=[pltpu.VMEM((1, LANES), F32)],
        compiler_params=_cparams("arbitrary"),
        name="conv_out",
    )(u, u, state_p, state_s, bg, h, cw, w16, gf, wrh, wrl, br, tstrict)


def _rel_bias_toeplitz(rel_bias, rows, cols, lead):
    period = rows + cols
    k = jnp.arange(period)
    d = jnp.where(k < cols, k, k - period)
    idx = jnp.clip(lead - d, -REL_CLIP, REL_CLIP) + REL_CLIP
    v = jnp.take(rel_bias.astype(F32), idx, axis=1)
    heads = v.shape[0]
    skew = jnp.tile(v, (1, rows))[:, :rows * (period - 1)].reshape(heads, rows, period - 1)
    return skew[:, :, :cols]


def _bias_prompt(rel_bias):
    r = jnp.arange(ATTN_TILE)[:, None]
    c = jnp.arange(ATTN_WINDOW)[None, :]
    j = c // CHUNK - r // CHUNK
    band = jnp.logical_and(j >= 0, j <= BAND_CHUNKS)
    bias = _rel_bias_toeplitz(rel_bias, ATTN_TILE, ATTN_WINDOW, BAND_CHUNKS * CHUNK)
    return jnp.where(band[None], bias, NEG_INF)


def _bias_sample(rel_bias, la, tn):
    bias = _rel_bias_toeplitz(rel_bias, tn, la + tn, la)
    return bias[:, :, :la], bias[:, :, la:]


def _router_weights(w_group, b_group, w_expert, b_expert):
    pad = LANES - N_EXPERTS - N_GROUPS
    w = jnp.concatenate([w_expert, w_group, jnp.zeros((D_MODEL, pad), F32)], axis=1)
    b = jnp.concatenate([b_expert, b_group, jnp.zeros((pad,), F32)])[None, :].astype(F32)
    hi, lo = _split2(w.astype(F32))
    return hi, lo, b


def kernel(x_prompt, x_sample, cache_a_k, cache_a_v, state_hgrn, state_conv, norm_mix, norm_ffn,
           w_in_ab, w_out_ab, q_norm, k_norm, rel_bias, hgrn_lb_logits, hgrn_out_norm, w_in_c,
           conv_w, w_out_c, w_group, b_group, w_expert, b_expert, w1, w3, w2):
    batch, seq, d = x_prompt.shape
    nseq_s, tn, _ = x_sample.shape
    la = cache_a_k.shape[2]
    n_p = batch * seq
    n_s = nseq_s * tn
    n = n_p + n_s
    keep = min(BAND_CHUNKS * CHUNK, seq)

    x_p = x_prompt.reshape(n_p, d)
    x_s = x_sample.reshape(n_s, d)
    lb_all = jnp.cumsum(jax.nn.softmax(hgrn_lb_logits.astype(F32), axis=0), axis=0)
    head_avg = jnp.kron(jnp.eye(N_HEADS, dtype=F32),
                        jnp.full((HEAD_DIM, HEAD_DIM), 1.0 / HEAD_DIM, F32)).astype(BF16)
    t = jnp.arange(TOKEN_TILE)
    tstrict = (t[None, :] < t[:, None]).astype(BF16)
    row1 = lambda v: v.astype(F32).reshape(1, -1)
    tile8 = lambda v: jnp.tile(v.astype(F32), N_HEADS).reshape(1, -1)

    l = 0
    (q16, kf, vf, k16, v16, lf, kb, vb, qb, gb) = _proj_ab(
        x_p, x_s, row1(norm_mix[0]), w_in_ab[l].astype(BF16), tile8(q_norm[l]), tile8(k_norm[l]),
        row1(lb_all[l]), head_avg, seq, keep)

    oa_p = _attn_prompt(q16, k16, v16, _bias_prompt(rel_bias[l]), batch, seq)
    bias_c, bias_n = _bias_sample(rel_bias[l], la, tn)
    oa_s = _attn_sample(q16, k16, v16, cache_a_k[l].reshape(nseq_s, la * N_HEADS, HEAD_DIM),
                        cache_a_v[l].reshape(nseq_s, la * N_HEADS, HEAD_DIM), bias_c, bias_n,
                        n_p, nseq_s, tn)

    zeros_state = jnp.zeros((batch, N_HEADS, HEAD_DIM, HEAD_DIM), F32)
    ob_p, st_p = _hgrn(qb, kb, vb, lf, zeros_state, 0, batch, seq, 64, 4)
    ob_s, st_s = _hgrn(qb, kb, vb, lf, jnp.swapaxes(state_hgrn[l].astype(F32), -1, -2),
                       n_p, nseq_s, tn, tn, 1)

    wrh, wrl, br = _router_weights(w_group[0], b_group[0], w_expert[0], b_expert[0])
    h1, xn1, route1, cnt1, code1 = _out_ab(x_p, x_s, oa_p, oa_s, ob_p, ob_s, gb, row1(hgrn_out_norm[l]),
                                    head_avg, w_out_ab[l].astype(BF16), row1(norm_ffn[0]),
                                    wrh, wrl, br, tstrict)
    (h2,) = _moe(xn1, h1, route1, code1, cnt1, w1[0], w3[0], w2[0])

    u, bg = _proj_c(h2, row1(norm_mix[1]), w_in_c[0].astype(BF16))
    wrh, wrl, br = _router_weights(w_group[1], b_group[1], w_expert[1], b_expert[1])
    h3, xn2, route2, cnt2, code2 = _conv_out(u, bg, h2, jnp.zeros((batch, 2, d), F32),
                                      state_conv[0].astype(F32), conv_w[0].astype(F32),
                                      w_out_c[0].astype(BF16), row1(norm_ffn[1]), wrh, wrl, br,
                                      tstrict, seq, tn)
    out_p, out_s = _moe(xn2, h3, route2, code2, cnt2, w1[1], w3[1], w2[1], n_p)

    y_prompt = out_p.reshape(batch, seq, d)
    y_sample = out_s.reshape(nseq_s, tn, d)
    n_kp = batch * keep
    heads = lambda a, b_, t_: a.reshape(b_, t_, N_HEADS, HEAD_DIM)
    nk_p = heads(kf[:n_kp], batch, keep)[None]
    nv_p = heads(vf[:n_kp], batch, keep)[None]
    kf_s = heads(kf[n_kp:n_kp + n_s], nseq_s, tn)
    vf_s = heads(vf[n_kp:n_kp + n_s], nseq_s, tn)
    nk_s = jnp.concatenate([cache_a_k[l].astype(F32), kf_s], axis=1)[:, -la:][None]
    nv_s = jnp.concatenate([cache_a_v[l].astype(F32), vf_s], axis=1)[:, -la:][None]
    nh_p = jnp.swapaxes(st_p, -1, -2)[None]
    nh_s = jnp.swapaxes(st_s, -1, -2)[None]
    nc_p = jnp.stack([u[(b + 1) * seq - 2:(b + 1) * seq] for b in range(batch)])[None]
    u_s = u[n_p:].reshape(nseq_s, tn, d)
    nc_s = jnp.concatenate([state_conv[0].astype(F32), u_s], axis=1)[:, -2:][None]
    return (y_prompt, y_sample, nk_p, nv_p, nk_s, nv_s, nh_p, nh_s, nc_p, nc_s)
```

```python
import functools

import jax
import jax.numpy as jnp
from jax import lax
from jax.experimental import pallas as pl
from jax.experimental.pallas import tpu as pltpu

F32 = jnp.float32
BF16 = jnp.bfloat16
I32 = jnp.int32

D_MODEL = 1024
CHUNK = 64
BAND_CHUNKS = 8
HEAD_DIM = 64
N_HEADS = 8
D_HALF = N_HEADS * HEAD_DIM
REL_CLIP = 128
HGRN_BLOCK = 16
N_GROUPS = 4
EXPERTS_PER_GROUP = 8
N_EXPERTS = N_GROUPS * EXPERTS_PER_GROUP
D_EXPERT = 256
RMS_EPS = 1e-6
NEG_INF = -1e30

LANES = 128
ROW_TILE = D_MODEL // LANES
TOKEN_TILE = 256
ATTN_TILE = 256
ATTN_WINDOW = ATTN_TILE + BAND_CHUNKS * CHUNK
ATTN_SPAN = -(-(BAND_CHUNKS + 1) * CHUNK // LANES) * LANES
ATTN_HEAD_GROUP = 4
MOE_TILE = 256
RANK_BITS = 16
RANK_SPAN = 1 << RANK_BITS
GATHER_UNROLL = 8
VMEM_LIMIT = 48 * 1024 * 1024


def _cparams(*sem):
    return pltpu.CompilerParams(dimension_semantics=sem, vmem_limit_bytes=VMEM_LIMIT)


def _const_spec(shape):
    nd = len(shape)
    return pl.BlockSpec(shape, lambda *_: (0,) * nd)


def _store_row_tiles(ref, val):
    rows = val.shape[0]
    for c in range(ROW_TILE):
        ref[pl.ds(c, rows, stride=ROW_TILE), :] = val[:, c * LANES:(c + 1) * LANES]


def _load_row_tiles(ref, rows):
    return jnp.concatenate([ref[pl.ds(c, rows, stride=ROW_TILE), :] for c in range(ROW_TILE)], axis=1)


def _rms(x, gain):
    ms = jnp.mean(x * x, axis=-1, keepdims=True)
    return (x * lax.rsqrt(ms + RMS_EPS)) * gain


def _split2(x):
    hi = x.astype(BF16)
    lo = (x - hi.astype(F32)).astype(BF16)
    return hi, lo


def _split3(x):
    p0 = x.astype(BF16)
    r = x - p0.astype(F32)
    p1 = r.astype(BF16)
    p2 = (r - p1.astype(F32)).astype(BF16)
    return p0, p1, p2


def _dot(a, b):
    return jnp.dot(a, b, preferred_element_type=F32)


def _dot_nt(a, b):
    return lax.dot_general(a, b, (((1,), (1,)), ((), ())), preferred_element_type=F32)


def _dot_tn(a, b):
    return lax.dot_general(a, b, (((0,), (0,)), ((), ())), preferred_element_type=F32)


def _head_mean_sq(v, bd):
    hi, lo = _split2(v * v)
    return _dot(hi, bd) + _dot(lo, bd)


def _group_specs(width, tiles_p, tm=TOKEN_TILE):
    return [pl.BlockSpec((tm, width), lambda i: (jnp.minimum(i, tiles_p - 1), 0)),
            pl.BlockSpec((tm, width), lambda i: (jnp.maximum(i - tiles_p, 0), 0))]


def _group_pick(p_ref, s_ref, tiles_p):
    return jnp.where(pl.program_id(0) < tiles_p, p_ref[...], s_ref[...])


def _proj_ab_body(xp_ref, xs_ref, g_ref, w_ref, qg_ref, kg_ref, lb_ref, bd_ref,
                  q_o, kf_o, vf_o, k16_o, v16_o, lf_o, kb_o, vb_o, qb_o, gb_o, *, tiles_p):
    xb = _rms(_group_pick(xp_ref, xs_ref, tiles_p), g_ref[...]).astype(BF16)
    bd = bd_ref[...]

    def seg(j):
        return _dot(xb, w_ref[:, j * D_HALF:(j + 1) * D_HALF])

    qa = seg(0)
    qn = qa * lax.rsqrt(_head_mean_sq(qa, bd) + RMS_EPS) * qg_ref[...]
    q_o[...] = (qn * (HEAD_DIM ** -0.5)).astype(BF16)
    ka = seg(1)
    kn = ka * lax.rsqrt(_head_mean_sq(ka, bd) + RMS_EPS) * kg_ref[...]
    kf_o[...] = kn
    k16_o[...] = kn.astype(BF16)
    va = seg(2)
    vf_o[...] = va
    v16_o[...] = va.astype(BF16)
    lb = lb_ref[...]
    f = lb + (1.0 - lb) * jax.nn.sigmoid(seg(3))
    lf_o[...] = jnp.log(f)
    kb_o[...] = 1.0 - f
    vb_o[...] = seg(4)
    qb_o[...] = jax.nn.silu(seg(5))
    gb_o[...] = seg(6)


def _proj_ab(x_p, x_s, gain, w16, qg, kg, lb, bd, seq, keep):
    tm = TOKEN_TILE
    n_p, n_s = x_p.shape[0], x_s.shape[0]
    n = n_p + n_s
    tiles_p, tiles_seq, tiles_keep = n_p // tm, seq // tm, keep // tm
    kept_tiles = (n_p // seq) * tiles_keep + n_s // tm

    def keep_map(i):
        b, j = i // tiles_seq, i % tiles_seq
        prompt_slot = b * tiles_keep + jnp.maximum(j - (tiles_seq - tiles_keep), 0)
        return (jnp.where(i >= tiles_p, kept_tiles - n_s // tm + (i - tiles_p), prompt_slot), 0)

    row = pl.BlockSpec((tm, D_HALF), lambda i: (i, 0))
    kept = pl.BlockSpec((tm, D_HALF), keep_map)
    full = lambda dt: jax.ShapeDtypeStruct((n, D_HALF), dt)
    kept_shape = jax.ShapeDtypeStruct((kept_tiles * tm, D_HALF), F32)
    return pl.pallas_call(
        functools.partial(_proj_ab_body, tiles_p=tiles_p),
        out_shape=[full(BF16), kept_shape, kept_shape, full(BF16), full(BF16),
                   full(F32), full(F32), full(F32), full(F32), full(F32)],
        grid=(n // tm,),
        in_specs=_group_specs(D_MODEL, tiles_p) + [
            _const_spec((1, D_MODEL)), _const_spec(w16.shape),
            _const_spec((1, D_HALF)), _const_spec((1, D_HALF)), _const_spec((1, D_HALF)),
            _const_spec((D_HALF, D_HALF))],
        out_specs=[row, kept, kept] + [row] * 7,
        compiler_params=_cparams("arbitrary"),
        name="proj_ab",
    )(x_p, x_s, gain, w16, qg, kg, lb, bd)


def _attn_prompt_body(q_ref, k0, k1, k2, v0, v1, v2, bias_ref, o_ref):
    i = pl.program_id(1)
    lim = jnp.maximum(2 - i, 0) * ATTN_TILE
    col = lax.broadcasted_iota(I32, (CHUNK, ATTN_SPAN), 1)
    cut = ATTN_SPAN - 2 * ATTN_TILE

    def span(parts, start):
        if start == 0:
            return (parts[0], parts[1], parts[2][:cut])
        return (parts[0][start:], parts[1], parts[2])

    for h0 in range(0, N_HEADS, ATTN_HEAD_GROUP):
        units = []
        for h in range(h0, h0 + ATTN_HEAD_GROUP):
            hs = slice(h * HEAD_DIM, (h + 1) * HEAD_DIM)
            kh = (k0[:, hs], k1[:, hs], k2[:, hs])
            vh = (v0[:, hs], v1[:, hs], v2[:, hs])
            for r in range(ATTN_TILE // CHUNK):
                start = (r * CHUNK // LANES) * LANES
                units.append((h, hs, slice(r * CHUNK, (r + 1) * CHUNK), start,
                              span(kh, start), span(vh, start)))
        s = [jnp.concatenate([_dot_nt(q_ref[rows, hs], kp) for kp in ks], axis=1)
             for (_, hs, rows, _, ks, _) in units]
        s = [jnp.where(col + start < lim, NEG_INF,
                       s_u + bias_ref[h, rows, start:start + ATTN_SPAN])
             for s_u, (h, _, rows, start, _, _) in zip(s, units)]
        m = [jnp.max(s_u, axis=-1, keepdims=True) for s_u in s]
        p = [jnp.exp(s_u - m_u) for s_u, m_u in zip(s, m)]
        l = [jnp.sum(p_u, axis=-1, keepdims=True) for p_u in p]
        outs = []
        for p_u, (_, _, _, _, _, vs) in zip(p, units):
            p16 = p_u.astype(BF16)
            o, at = 0.0, 0
            for vp in vs:
                o = o + _dot(p16[:, at:at + vp.shape[0]], vp)
                at += vp.shape[0]
            outs.append(o)
        for o_u, l_u, (_, hs, rows, _, _, _) in zip(outs, l, units):
            o_ref[rows, hs] = (o_u / l_u).astype(BF16)


def _attn_prompt(q16, k16, v16, bias_full, batch, seq):
    tiles = seq // ATTN_TILE
    qspec = pl.BlockSpec((ATTN_TILE, D_HALF), lambda b, i: (b * tiles + i, 0))

    def kv(back):
        return pl.BlockSpec((ATTN_TILE, D_HALF),
                            lambda b, i: (b * tiles + jnp.maximum(i - back, 0), 0))

    return pl.pallas_call(
        _attn_prompt_body,
        out_shape=jax.ShapeDtypeStruct((batch * seq, D_HALF), BF16),
        grid=(batch, tiles),
        in_specs=[qspec, kv(2), kv(1), kv(0), kv(2), kv(1), kv(0),
                  _const_spec(bias_full.shape)],
        out_specs=qspec,
        compiler_params=_cparams("parallel", "parallel"),
        name="attn_prompt",
    )(q16, k16, k16, k16, v16, v16, v16, bias_full)


def _attn_sample_body(q_ref, kc_ref, vc_ref, kn_ref, vn_ref, bc_ref, bn_ref, o_ref):
    la = kc_ref.shape[1] // N_HEADS
    for h in range(N_HEADS):
        hs = slice(h * HEAD_DIM, (h + 1) * HEAD_DIM)
        kc = kc_ref[0, pl.ds(h, la, stride=N_HEADS), :].astype(BF16)
        vc = vc_ref[0, pl.ds(h, la, stride=N_HEADS), :].astype(BF16)
        qh = q_ref[:, hs]
        sc = _dot_nt(qh, kc) + bc_ref[h]
        sn = _dot_nt(qh, kn_ref[:, hs]) + bn_ref[h]
        m = jnp.maximum(jnp.max(sc, axis=-1, keepdims=True), jnp.max(sn, axis=-1, keepdims=True))
        pc = jnp.exp(sc - m)
        pn = jnp.exp(sn - m)
        l = jnp.sum(pc, axis=-1, keepdims=True) + jnp.sum(pn, axis=-1, keepdims=True)
        o = _dot(pc.astype(BF16), vc) + _dot(pn.astype(BF16), vn_ref[:, hs])
        o_ref[:, hs] = (o / l).astype(BF16)


def _attn_sample(q16, k16, v16, cache_k, cache_v, bias_c, bias_n, row0, nseq, tn):
    blk0 = row0 // tn
    la8 = cache_k.shape[1]
    new = pl.BlockSpec((tn, D_HALF), lambda b: (blk0 + b, 0))
    cache = pl.BlockSpec((1, la8, HEAD_DIM), lambda b: (b, 0, 0))
    return pl.pallas_call(
        _attn_sample_body,
        out_shape=jax.ShapeDtypeStruct((nseq * tn, D_HALF), BF16),
        grid=(nseq,),
        in_specs=[new, cache, cache, new, new, _const_spec(bias_c.shape), _const_spec(bias_n.shape)],
        out_specs=pl.BlockSpec((tn, D_HALF), lambda b: (b, 0)),
        compiler_params=_cparams("parallel"),
        name="attn_sample",
    )(q16, cache_k, cache_v, k16, v16, bias_c, bias_n)


def _hgrn_body(q_ref, k_ref, v_ref, lf_ref, s0_ref, tblk_ref, tfull_ref, o_ref, sout_ref, st_scr,
               *, chunk, nchunks):
    nblk = chunk // HGRN_BLOCK

    @pl.when(pl.program_id(1) == 0)
    def _():
        st_scr[...] = s0_ref[0]

    row = lax.broadcasted_iota(I32, (chunk, D_HALF), 0)
    r2 = lax.broadcasted_iota(I32, (chunk, chunk), 0)
    c2 = lax.broadcasted_iota(I32, (chunk, chunk), 1)
    same_blk_causal = jnp.logical_and(r2 // HGRN_BLOCK == c2 // HGRN_BLOCK, c2 <= r2)

    def one_chunk(c, carry):
        sl = pl.ds(pl.multiple_of(c * chunk, chunk), chunk)
        q = q_ref[sl, :]
        k = k_ref[sl, :]
        v16 = v_ref[sl, :].astype(BF16)
        l0, l1, l2 = _split3(lf_ref[sl, :])
        tb = tblk_ref[...]
        tf = tfull_ref[...]
        b_in = _dot(tb, l0) + _dot(tb, l1) + _dot(tb, l2)
        b_ch = _dot(tf, l0) + _dot(tf, l1) + _dot(tf, l2)
        ld = (q * jnp.exp(b_in)).astype(BF16)
        rd = (k * jnp.exp(-b_in)).astype(BF16)
        lj, rj = [], []
        for j in range(nblk - 1):
            e_j = b_ch[(j + 1) * HGRN_BLOCK - 1:(j + 1) * HGRN_BLOCK, :]
            later = row >= (j + 1) * HGRN_BLOCK
            inside = jnp.logical_and(row >= j * HGRN_BLOCK, row < (j + 1) * HGRN_BLOCK)
            lj.append(jnp.where(later, q * jnp.exp(jnp.minimum(b_ch - e_j, 0.0)), 0.0).astype(BF16))
            rj.append(jnp.where(inside, k * jnp.exp(jnp.minimum(e_j - b_ch, 0.0)), 0.0).astype(BF16))
        e_end = b_ch[chunk - 1:chunk, :]
        qc = (q * jnp.exp(b_ch)).astype(BF16)
        kc = (k * jnp.exp(e_end - b_ch)).astype(BF16)
        dec = jnp.exp(e_end)
        heads = [slice(h * HEAD_DIM, (h + 1) * HEAD_DIM) for h in range(N_HEADS)]
        sts = [st_scr[h] for h in range(N_HEADS)]
        diag = [_dot_nt(ld[:, hs], rd[:, hs]) for hs in heads]
        off = []
        for hs in heads:
            acc = None
            for j in range(nblk - 1):
                term = _dot_nt(lj[j][:, hs], rj[j][:, hs])
                acc = term if acc is None else acc + term
            off.append(acc)
        inter = [_dot_nt(qc[:, hs], st.astype(BF16)) for hs, st in zip(heads, sts)]
        upd = [_dot_tn(v16[:, hs], kc[:, hs]) for hs in heads]
        sc16 = [(jnp.where(same_blk_causal, d, 0.0) + o_).astype(BF16) for d, o_ in zip(diag, off)]
        intra = [_dot(s_, v16[:, hs]) for s_, hs in zip(sc16, heads)]
        o_ref[sl, :] = jnp.concatenate([a + b for a, b in zip(intra, inter)], axis=1)
        for h, hs in enumerate(heads):
            st_scr[h] = sts[h] * dec[:, hs] + upd[h]
        return carry

    lax.fori_loop(0, nchunks, one_chunk, 0)
    sout_ref[0] = st_scr[...]


def _hgrn(qb, kb, vb, lf, s0t, row0, nseq, seq, chunk, nchunks):
    tt = chunk * nchunks
    steps = seq // tt
    blk0 = row0 // tt
    t = jnp.arange(chunk)
    lower = t[None, :] <= t[:, None]
    tfull = lower.astype(BF16)
    tblk = jnp.logical_and(lower, (t[None, :] // HGRN_BLOCK) == (t[:, None] // HGRN_BLOCK)).astype(BF16)
    tok = pl.BlockSpec((tt, D_HALF), lambda b, j: (blk0 + b * steps + j, 0))
    state = pl.BlockSpec((1, N_HEADS, HEAD_DIM, HEAD_DIM), lambda b, j: (b, 0, 0, 0))
    return pl.pallas_call(
        functools.partial(_hgrn_body, chunk=chunk, nchunks=nchunks),
        out_shape=[jax.ShapeDtypeStruct((nseq * seq, D_HALF), F32),
                   jax.ShapeDtypeStruct(s0t.shape, F32)],
        grid=(nseq, steps),
        in_specs=[tok, tok, tok, tok, state, _const_spec((chunk, chunk)), _const_spec((chunk, chunk))],
        out_specs=[pl.BlockSpec((tt, D_HALF), lambda b, j: (b * steps + j, 0)), state],
        scratch_shapes=[pltpu.VMEM((N_HEADS, HEAD_DIM, HEAD_DIM), F32)],
        compiler_params=_cparams("parallel", "arbitrary"),
        name="hgrn",
    )(qb, kb, vb, lf, s0t, tblk, tfull)


def _route(xn, wr_hi, wr_lo, br, tstrict, cnt_scr, route_o, cnt_o, code_o):
    rows = xn.shape[0]
    x_hi, x_lo = _split2(xn)
    logits = _dot(x_hi, wr_hi) + _dot(x_lo, wr_hi) + _dot(x_hi, wr_lo) + br
    lane = lax.broadcasted_iota(I32, (rows, LANES), 1)
    lane_f = lane.astype(F32)

    def first_argmax(vals):
        m = jnp.max(vals, axis=-1, keepdims=True)
        idx = jnp.min(jnp.where(vals == m, lane_f, float(LANES)), axis=-1, keepdims=True)
        return m, idx.astype(I32)

    is_grp = jnp.logical_and(lane >= N_EXPERTS, lane < N_EXPERTS + N_GROUPS)
    gl = jnp.where(is_grp, logits, -jnp.inf)
    gmax, gidx = first_argmax(gl)
    p_grp = 1.0 / jnp.sum(jnp.exp(gl - gmax), axis=-1, keepdims=True)
    grp = gidx - N_EXPERTS
    in_grp = jnp.logical_and(lane < N_EXPERTS, lane // EXPERTS_PER_GROUP == grp)
    el = jnp.where(in_grp, logits, -jnp.inf)
    v1, e1 = first_argmax(el)
    v2, e2 = first_argmax(jnp.where(lane == e1, -jnp.inf, el))
    t2 = jnp.exp(v2 - v1)
    den = 1.0 + t2
    g1 = (1.0 / den) * p_grp
    g2 = (t2 / den) * p_grp

    oh1 = lane == e1
    oh2 = lane == e2
    oh = jnp.where(jnp.logical_or(oh1, oh2), 1.0, 0.0)
    before = _dot(tstrict, oh.astype(BF16)) + cnt_scr[...]
    rank1 = jnp.sum(jnp.where(oh1, before, 0.0), axis=-1, keepdims=True)
    rank2 = jnp.sum(jnp.where(oh2, before, 0.0), axis=-1, keepdims=True)
    cnt_scr[...] = cnt_scr[...] + jnp.sum(oh, axis=0, keepdims=True)
    cnt_o[...] = cnt_scr[...]

    e1f, e2f = e1.astype(F32), e2.astype(F32)
    fields = (e1f, e2f, g1, g2, rank1, rank2, e1f * RANK_SPAN + rank1, e2f * RANK_SPAN + rank2)
    slab = jnp.zeros((rows, LANES), F32)
    for idx, val in enumerate(fields):
        slab = jnp.where(lane == idx, val, slab)
    route_o[...] = slab
    code_o[...] = slab.T[6:8, :]


def _out_ab_body(hp_ref, hs_ref, oap_ref, oas_ref, obp_ref, obs_ref, gb_ref, og_ref, bd_ref, w_ref,
                 gf_ref, wrh_ref, wrl_ref, br_ref, ts_ref, h_o, xn_o, route_o, cnt_o, code_o, cnt_scr,
                 *, tiles_p):
    @pl.when(pl.program_id(0) == 0)
    def _():
        cnt_scr[...] = jnp.zeros_like(cnt_scr)

    ob = _group_pick(obp_ref, obs_ref, tiles_p)
    obn = ob * lax.rsqrt(_head_mean_sq(ob, bd_ref[...]) + RMS_EPS) * og_ref[...]
    obg = (obn * jax.nn.silu(gb_ref[...])).astype(BF16)
    oa = _group_pick(oap_ref, oas_ref, tiles_p)
    mix = _dot(oa, w_ref[0:D_HALF, :]) + _dot(obg, w_ref[D_HALF:, :])
    h1 = _group_pick(hp_ref, hs_ref, tiles_p) + mix
    h_o[...] = h1
    xn = _rms(h1, gf_ref[...])
    _store_row_tiles(xn_o, xn)
    _route(xn, wrh_ref[...], wrl_ref[...], br_ref[...], ts_ref[...], cnt_scr, route_o, cnt_o, code_o)


def _out_ab(h_p, h_s, oa_p, oa_s, ob_p, ob_s, gb, og, bd, w16, gf, wrh, wrl, br, tstrict):
    tm = TOKEN_TILE
    n = h_p.shape[0] + h_s.shape[0]
    tiles_p = h_p.shape[0] // tm
    row = lambda w: pl.BlockSpec((tm, w), lambda i: (i, 0))
    return pl.pallas_call(
        functools.partial(_out_ab_body, tiles_p=tiles_p),
        out_shape=[jax.ShapeDtypeStruct((n, D_MODEL), F32), jax.ShapeDtypeStruct((n * ROW_TILE, LANES), F32),
                   jax.ShapeDtypeStruct((n, LANES), F32), jax.ShapeDtypeStruct((1, LANES), F32),
                   jax.ShapeDtypeStruct((2, n), F32)],
        grid=(n // tm,),
        in_specs=_group_specs(D_MODEL, tiles_p) + _group_specs(D_HALF, tiles_p)
        + _group_specs(D_HALF, tiles_p) + [
            row(D_HALF), _const_spec((1, D_HALF)),
            _const_spec((D_HALF, D_HALF)), _const_spec((D_MODEL, D_MODEL)),
            _const_spec((1, D_MODEL)), _const_spec((D_MODEL, LANES)),
            _const_spec((D_MODEL, LANES)), _const_spec((1, LANES)), _const_spec((tm, tm))],
        out_specs=[row(D_MODEL), pl.BlockSpec((tm * ROW_TILE, LANES), lambda i: (i, 0)), row(LANES),
                   _const_spec((1, LANES)), pl.BlockSpec((2, tm), lambda i: (0, i))],
        scratch_shapes=[pltpu.VMEM((1, LANES), F32)],
        compiler_params=_cparams("arbitrary"),
        name="out_ab",
    )(h_p, h_s, oa_p, oa_s, ob_p, ob_s, gb, og, bd, w16, gf, wrh, wrl, br, tstrict)


def _row_copy(src, src_row, dst, dst_row, sem):
    s0 = pl.multiple_of(src_row * ROW_TILE, ROW_TILE)
    d0 = pl.multiple_of(dst_row * ROW_TILE, ROW_TILE)
    return pltpu.make_async_copy(src.at[pl.ds(s0, ROW_TILE)], dst.at[pl.ds(d0, ROW_TILE)], sem)


def _sorted_row(code_ref, offs_ref, a):
    code = code_ref[a]
    return offs_ref[code >> RANK_BITS] + (code & (RANK_SPAN - 1))


def _collect_body(code_ref, offs_ref, ys_hbm, h_ref, r_ref, *rest, tile, n_tok, tiles_p, proj):
    if proj:
        g_ref, w_ref = rest[:2]
        rest = rest[2:]
    outs, (buf, sems) = rest[:-2], rest[-2:]
    i = pl.program_id(0)
    nt = pl.num_programs(0)

    def issue(t):
        slot = t % 2
        base = t * tile

        def one(r, carry):
            tok = base + r
            for k in range(2):
                row = _sorted_row(code_ref, offs_ref, k * n_tok + tok)
                _row_copy(ys_hbm, row, buf.at[slot].at[k], r, sems.at[slot]).start()
            return carry

        lax.fori_loop(0, tile, one, 0, unroll=GATHER_UNROLL)

    @pl.when(i == 0)
    def _():
        issue(0)

    @pl.when(i + 1 < nt)
    def _():
        issue(i + 1)

    slot = i % 2
    for _ in range(2 * tile):
        _row_copy(ys_hbm, 0, buf.at[slot].at[0], 0, sems.at[slot]).wait()
    route = r_ref[...]
    ya = _load_row_tiles(buf.at[slot].at[0], tile)
    yb = _load_row_tiles(buf.at[slot].at[1], tile)
    out = h_ref[...] + (route[:, 2:3] * ya + route[:, 3:4] * yb)
    if proj:
        h_o, u_o, bg_o = outs
        h_o[...] = out
        xb = _rms(out, g_ref[...]).astype(BF16)
        bg_o[...] = _dot(xb, w_ref[:, 0:D_MODEL])
        u_o[...] = _dot(xb, w_ref[:, D_MODEL:2 * D_MODEL]) * _dot(xb, w_ref[:, 2 * D_MODEL:])
    else:
        @pl.when(i < tiles_p)
        def _():
            outs[0][...] = out

        @pl.when(i >= tiles_p)
        def _():
            outs[1][...] = out


def _collect(code, offs, ys, h, route, *, proj=None, n_p=None):
    tm = TOKEN_TILE
    n = h.shape[0]
    row = lambda w: pl.BlockSpec((tm, w), lambda i, code, offs: (i, 0))
    const = lambda shape: pl.BlockSpec(shape, lambda i, code, offs: (0,) * len(shape))
    if proj is not None:
        tiles_p = None
        extra_in, extra_specs = list(proj), [const(proj[0].shape), const(proj[1].shape)]
        out_shape = [jax.ShapeDtypeStruct((n, D_MODEL), F32)] * 3
        out_specs = [row(D_MODEL)] * 3
    else:
        tiles_p = n_p // tm
        extra_in, extra_specs = [], []
        out_shape = [jax.ShapeDtypeStruct((n_p, D_MODEL), F32),
                     jax.ShapeDtypeStruct((n - n_p, D_MODEL), F32)]
        out_specs = [pl.BlockSpec((tm, D_MODEL), lambda i, code, offs: (jnp.minimum(i, tiles_p - 1), 0)),
                     pl.BlockSpec((tm, D_MODEL), lambda i, code, offs: (jnp.maximum(i - tiles_p, 0), 0))]
    return pl.pallas_call(
        functools.partial(_collect_body, tile=tm, n_tok=n, tiles_p=tiles_p, proj=proj is not None),
        out_shape=out_shape,
        grid_spec=pltpu.PrefetchScalarGridSpec(
            num_scalar_prefetch=2,
            grid=(n // tm,),
            in_specs=[pl.BlockSpec(memory_space=pl.ANY), row(D_MODEL), row(LANES)] + extra_specs,
            out_specs=out_specs,
            scratch_shapes=[pltpu.VMEM((2, 2, tm * ROW_TILE, LANES), F32),
                            pltpu.SemaphoreType.DMA((2,))],
        ),
        compiler_params=pltpu.CompilerParams(dimension_semantics=("arbitrary",),
                                             vmem_limit_bytes=VMEM_LIMIT),
        name="moe_collect",
    )(code, offs, ys, h, route, *extra_in)


def _scatter_rows_body(code_ref, offs_ref, pad0_ref, padn_ref, nt_ref, x_ref, xs_hbm,
                       stage, zero_scr, sems, pad_sem, *, tile, n_tok, out_tiles):
    i = pl.program_id(0)
    last = pl.num_programs(0) - 1
    base = i * tile
    slot = i % 2

    def drain(s):
        for _ in range(2 * tile):
            _row_copy(stage.at[s], 0, xs_hbm, 0, sems.at[s]).wait()

    @pl.when(i >= 2)
    def _():
        drain(slot)

    stage[slot] = x_ref[...]

    def issue(r, carry):
        tok = base + r
        src = stage.at[slot]
        _row_copy(src, r, xs_hbm, _sorted_row(code_ref, offs_ref, tok), sems.at[slot]).start()
        _row_copy(src, r, xs_hbm, _sorted_row(code_ref, offs_ref, n_tok + tok),
                  sems.at[slot]).start()
        return carry

    lax.fori_loop(0, tile, issue, 0, unroll=GATHER_UNROLL)

    @pl.when(i == last)
    def _():
        zero_scr[...] = jnp.zeros_like(zero_scr)

        def fill(lo, count):
            def one(r, carry):
                _row_copy(zero_scr, 0, xs_hbm, lo + r, pad_sem).start()
                return carry

            def done(r, carry):
                _row_copy(zero_scr, 0, xs_hbm, 0, pad_sem).wait()
                return carry

            lax.fori_loop(0, count, one, 0)
            lax.fori_loop(0, count, done, 0)

        for e in range(N_EXPERTS):
            fill(pad0_ref[e], padn_ref[e])

        def tail_copy(t):
            rows = MOE_TILE * ROW_TILE
            return pltpu.make_async_copy(
                zero_scr, xs_hbm.at[pl.ds(pl.multiple_of(t * rows, rows), rows)], pad_sem)

        def tail_start(t, carry):
            tail_copy(t).start()
            return carry

        def tail_done(t, carry):
            tail_copy(t).wait()
            return carry

        lax.fori_loop(nt_ref[0], out_tiles, tail_start, 0)
        lax.fori_loop(nt_ref[0], out_tiles, tail_done, 0)
        drain(slot)

    @pl.when(jnp.logical_and(i == last, i >= 1))
    def _():
        drain(1 - slot)


def _scatter_rows(code, offs, pad0, padn, n_tiles, xn, n_rows):
    tm = TOKEN_TILE
    n = xn.shape[0] // ROW_TILE
    return pl.pallas_call(
        functools.partial(_scatter_rows_body, tile=tm, n_tok=n, out_tiles=n_rows // MOE_TILE),
        out_shape=jax.ShapeDtypeStruct((n_rows * ROW_TILE, LANES), F32),
        grid_spec=pltpu.PrefetchScalarGridSpec(
            num_scalar_prefetch=5,
            grid=(n // tm,),
            in_specs=[pl.BlockSpec((tm * ROW_TILE, LANES), lambda i, *_: (i, 0))],
            out_specs=pl.BlockSpec(memory_space=pl.ANY),
            scratch_shapes=[pltpu.VMEM((2, tm * ROW_TILE, LANES), F32),
                            pltpu.VMEM((MOE_TILE * ROW_TILE, LANES), F32),
                            pltpu.SemaphoreType.DMA((2,)), pltpu.SemaphoreType.DMA(())],
        ),
        compiler_params=pltpu.CompilerParams(dimension_semantics=("arbitrary",)),
        name="moe_scatter",
    )(code, offs, pad0, padn, n_tiles, xn)


def _experts_body(te_ref, nt_ref, x_ref, w1_ref, w3_ref, w2_ref, y_ref, w1_s, w3_s, w2_s):
    i = pl.program_id(0)
    prev = te_ref[jnp.maximum(i - 1, 0)]
    fresh = jnp.logical_or(i == 0, te_ref[i] != prev)

    @pl.when(fresh)
    def _():
        w1_s[...] = w1_ref[0].astype(BF16)
        w3_s[...] = w3_ref[0].astype(BF16)
        w2_s[...] = w2_ref[0].astype(BF16)

    @pl.when(i < nt_ref[0])
    def _():
        x = _load_row_tiles(x_ref, MOE_TILE).astype(BF16)
        a = _dot(x, w1_s[...])
        b = _dot(x, w3_s[...])
        _store_row_tiles(y_ref, _dot((jax.nn.silu(a) * b).astype(BF16), w2_s[...]))

    @pl.when(i >= nt_ref[0])
    def _():
        y_ref[...] = jnp.zeros_like(y_ref)


def _experts(tile_expert, n_tiles, xs, w1, w3, w2):
    rows = xs.shape[0] // ROW_TILE
    tm = MOE_TILE
    return pl.pallas_call(
        _experts_body,
        out_shape=jax.ShapeDtypeStruct((rows * ROW_TILE, LANES), F32),
        grid_spec=pltpu.PrefetchScalarGridSpec(
            num_scalar_prefetch=2,
            grid=(rows // tm,),
            in_specs=[pl.BlockSpec((tm * ROW_TILE, LANES),
                                   lambda i, te, nt: (jnp.minimum(i, nt[0] - 1), 0)),
                      pl.BlockSpec((1, D_MODEL, D_EXPERT), lambda i, te, nt: (te[i], 0, 0)),
                      pl.BlockSpec((1, D_MODEL, D_EXPERT), lambda i, te, nt: (te[i], 0, 0)),
                      pl.BlockSpec((1, D_EXPERT, D_MODEL), lambda i, te, nt: (te[i], 0, 0))],
            out_specs=pl.BlockSpec((tm * ROW_TILE, LANES), lambda i, te, nt: (i, 0)),
            scratch_shapes=[pltpu.VMEM((D_MODEL, D_EXPERT), BF16), pltpu.VMEM((D_MODEL, D_EXPERT), BF16),
                            pltpu.VMEM((D_EXPERT, D_MODEL), BF16)],
        ),
        compiler_params=_cparams("arbitrary"),
        name="moe_experts",
    )(tile_expert, n_tiles, xs, w1, w3, w2)


def _moe_plan(counts, n):
    counts = counts[0, :N_EXPERTS].astype(I32)
    padded = ((counts + MOE_TILE - 1) // MOE_TILE) * MOE_TILE
    ends = jnp.cumsum(padded)
    offs = ends - padded
    max_tiles = (2 * n + N_EXPERTS * (MOE_TILE - 1) + MOE_TILE - 1) // MOE_TILE
    tile_start = jnp.arange(max_tiles, dtype=I32) * MOE_TILE
    tile_expert = jnp.minimum(jnp.sum((tile_start[:, None] >= ends[None, :]).astype(I32), axis=1),
                              N_EXPERTS - 1)
    n_tiles = (ends[-1] // MOE_TILE).reshape(1)
    return offs, offs + counts, padded - counts, tile_expert, n_tiles, max_tiles * MOE_TILE


def _moe(xn, h, route, code, counts, layer, w1, w3, w2, **collect_mode):
    n = h.shape[0]
    code = code.astype(I32).reshape(2 * n)
    offs, pad0, padn, tile_expert, n_tiles, n_rows = _moe_plan(counts, n)
    xs = _scatter_rows(code, offs, pad0, padn, n_tiles, xn, n_rows)
    flat = lambda w: w.reshape((-1,) + w.shape[2:])
    ys = _experts(tile_expert + layer * N_EXPERTS, n_tiles, xs, flat(w1), flat(w3), flat(w2))
    return _collect(code, offs, ys, h, route, **collect_mode)


def _conv_out_body(u_ref, up_ref, stp_ref, sts_ref, bg_ref, h_ref, cw_ref, w_ref, gf_ref,
                   wrh_ref, wrl_ref, br_ref, ts_ref, h_o, xn_o, route_o, cnt_o, code_o, cnt_scr,
                   *, tile, tiles_p, seq_p, seq_s):
    i = pl.program_id(0)

    @pl.when(i == 0)
    def _():
        cnt_scr[...] = jnp.zeros_like(cnt_scr)

    u = u_ref[...]
    rowi = lax.broadcasted_iota(I32, (tile, D_MODEL), 0)
    is_p = i < tiles_p
    at_start = (i * tile) % seq_p == 0
    stp = stp_ref[0]
    prev = up_ref[...]
    m2_p = jnp.where(at_start, stp[0:1, :], prev[6:7, :])
    m1_p = jnp.where(at_start, stp[1:2, :], prev[7:8, :])
    per = tile // seq_s
    sts = sts_ref[...]
    m2_s = jnp.broadcast_to(sts[:, 0:1, :], (per, seq_s, D_MODEL)).reshape(tile, D_MODEL)
    m1_s = jnp.broadcast_to(sts[:, 1:2, :], (per, seq_s, D_MODEL)).reshape(tile, D_MODEL)
    m2 = jnp.where(is_p, m2_p, m2_s)
    m1 = jnp.where(is_p, m1_p, m1_s)
    pos = jnp.where(is_p, rowi, rowi % seq_s)
    u1 = jnp.where(pos == 0, m1, pltpu.roll(u, 1, axis=0))
    u2 = jnp.where(pos == 0, m2, jnp.where(pos == 1, m1, pltpu.roll(u, 2, axis=0)))
    cw = cw_ref[...]
    conv = u2 * cw[0:1, :] + u1 * cw[1:2, :] + u * cw[2:3, :]
    mix = _dot((bg_ref[...] * conv).astype(BF16), w_ref[...])
    h3 = h_ref[...] + mix
    h_o[...] = h3
    xn = _rms(h3, gf_ref[...])
    _store_row_tiles(xn_o, xn)
    _route(xn, wrh_ref[...], wrl_ref[...], br_ref[...], ts_ref[...], cnt_scr, route_o, cnt_o, code_o)


def _conv_out(u, bg, h, state_p, state_s, cw, w16, gf, wrh, wrl, br, tstrict, seq_p, seq_s):
    tm = TOKEN_TILE
    n = u.shape[0]
    tiles_p = state_p.shape[0] * seq_p // tm
    per = tm // seq_s
    row = lambda w: pl.BlockSpec((tm, w), lambda i: (i, 0))
    prev = pl.BlockSpec((8, D_MODEL), lambda i: (jnp.maximum(i * (tm // 8) - 1, 0), 0))
    stp_spec = pl.BlockSpec((1, 2, D_MODEL),
                            lambda i: (jnp.minimum(i, tiles_p - 1) * tm // seq_p, 0, 0))
    sts_spec = pl.BlockSpec((per, 2, D_MODEL), lambda i: (jnp.maximum(i - tiles_p, 0), 0, 0))
    return pl.pallas_call(
        functools.partial(_conv_out_body, tile=tm, tiles_p=tiles_p, seq_p=seq_p, seq_s=seq_s),
        out_shape=[jax.ShapeDtypeStruct((n, D_MODEL), F32),
                   jax.ShapeDtypeStruct((n * ROW_TILE, LANES), F32),
                   jax.ShapeDtypeStruct((n, LANES), F32), jax.ShapeDtypeStruct((1, LANES), F32),
                   jax.ShapeDtypeStruct((2, n), F32)],
        grid=(n // tm,),
        in_specs=[row(D_MODEL), prev, stp_spec, sts_spec, row(D_MODEL), row(D_MODEL),
                  _const_spec((3, D_MODEL)),
                  _const_spec((D_MODEL, D_MODEL)), _const_spec((1, D_MODEL)),
                  _const_spec((D_MODEL, LANES)), _const_spec((D_MODEL, LANES)),
                  _const_spec((1, LANES)), _const_spec((tm, tm))],
        out_specs=[row(D_MODEL), pl.BlockSpec((tm * ROW_TILE, LANES), lambda i: (i, 0)), row(LANES),
                   _const_spec((1, LANES)), pl.BlockSpec((2, tm), lambda i: (0, i))],
        scratch_shapes=[pltpu.VMEM((1, LANES), F32)],
        compiler_params=_cparams("arbitrary"),
        name="conv_out",
    )(u, u, state_p, state_s, bg, h, cw, w16, gf, wrh, wrl, br, tstrict)


def _rel_bias_toeplitz(rel_bias, rows, cols, lead):
    period = rows + cols
    k = jnp.arange(period)
    d = jnp.where(k < cols, k, k - period)
    idx = jnp.clip(lead - d, -REL_CLIP, REL_CLIP) + REL_CLIP
    v = jnp.take(rel_bias.astype(F32), idx, axis=1)
    heads = v.shape[0]
    skew = jnp.tile(v, (1, rows))[:, :rows * (period - 1)].reshape(heads, rows, period - 1)
    return skew[:, :, :cols]


def _bias_prompt(rel_bias):
    r = jnp.arange(ATTN_TILE)[:, None]
    c = jnp.arange(ATTN_WINDOW)[None, :]
    j = c // CHUNK - r // CHUNK
    band = jnp.logical_and(j >= 0, j <= BAND_CHUNKS)
    bias = _rel_bias_toeplitz(rel_bias, ATTN_TILE, ATTN_WINDOW, BAND_CHUNKS * CHUNK)
    return jnp.where(band[None], bias, NEG_INF)


def _bias_sample(rel_bias, la, tn):
    bias = _rel_bias_toeplitz(rel_bias, tn, la + tn, la)
    return bias[:, :, :la], bias[:, :, la:]


def _router_weights(w_group, b_group, w_expert, b_expert):
    pad = LANES - N_EXPERTS - N_GROUPS
    w = jnp.concatenate([w_expert, w_group, jnp.zeros((D_MODEL, pad), F32)], axis=1)
    b = jnp.concatenate([b_expert, b_group, jnp.zeros((pad,), F32)])[None, :].astype(F32)
    hi, lo = _split2(w.astype(F32))
    return hi, lo, b


def kernel(x_prompt, x_sample, cache_a_k, cache_a_v, state_hgrn, state_conv, norm_mix, norm_ffn,
           w_in_ab, w_out_ab, q_norm, k_norm, rel_bias, hgrn_lb_logits, hgrn_out_norm, w_in_c,
           conv_w, w_out_c, w_group, b_group, w_expert, b_expert, w1, w3, w2):
    batch, seq, d = x_prompt.shape
    nseq_s, tn, _ = x_sample.shape
    la = cache_a_k.shape[2]
    n_p = batch * seq
    n_s = nseq_s * tn
    n = n_p + n_s
    keep = min(BAND_CHUNKS * CHUNK, seq)

    x_p = x_prompt.reshape(n_p, d)
    x_s = x_sample.reshape(n_s, d)
    lb_all = jnp.cumsum(jax.nn.softmax(hgrn_lb_logits.astype(F32), axis=0), axis=0)
    head_avg = jnp.kron(jnp.eye(N_HEADS, dtype=F32),
                        jnp.full((HEAD_DIM, HEAD_DIM), 1.0 / HEAD_DIM, F32)).astype(BF16)
    t = jnp.arange(TOKEN_TILE)
    tstrict = (t[None, :] < t[:, None]).astype(BF16)
    row1 = lambda v: v.astype(F32).reshape(1, -1)
    tile8 = lambda v: jnp.tile(v.astype(F32), N_HEADS).reshape(1, -1)

    l = 0
    (q16, kf, vf, k16, v16, lf, kb, vb, qb, gb) = _proj_ab(
        x_p, x_s, row1(norm_mix[0]), w_in_ab[l].astype(BF16), tile8(q_norm[l]), tile8(k_norm[l]),
        row1(lb_all[l]), head_avg, seq, keep)

    oa_p = _attn_prompt(q16, k16, v16, _bias_prompt(rel_bias[l]), batch, seq)
    bias_c, bias_n = _bias_sample(rel_bias[l], la, tn)
    oa_s = _attn_sample(q16, k16, v16, cache_a_k[l].reshape(nseq_s, la * N_HEADS, HEAD_DIM),
                        cache_a_v[l].reshape(nseq_s, la * N_HEADS, HEAD_DIM), bias_c, bias_n,
                        n_p, nseq_s, tn)

    zeros_state = jnp.zeros((batch, N_HEADS, HEAD_DIM, HEAD_DIM), F32)
    ob_p, st_p = _hgrn(qb, kb, vb, lf, zeros_state, 0, batch, seq, 64, 4)
    ob_s, st_s = _hgrn(qb, kb, vb, lf, jnp.swapaxes(state_hgrn[l].astype(F32), -1, -2),
                       n_p, nseq_s, tn, tn, 1)

    wrh, wrl, br = _router_weights(w_group[0], b_group[0], w_expert[0], b_expert[0])
    h1, xn1, route1, cnt1, code1 = _out_ab(x_p, x_s, oa_p, oa_s, ob_p, ob_s, gb, row1(hgrn_out_norm[l]),
                                    head_avg, w_out_ab[l].astype(BF16), row1(norm_ffn[0]),
                                    wrh, wrl, br, tstrict)
    h2, u, bg = _moe(xn1, h1, route1, code1, cnt1, 0, w1, w3, w2,
                     proj=(row1(norm_mix[1]), w_in_c[0].astype(BF16)))

    wrh, wrl, br = _router_weights(w_group[1], b_group[1], w_expert[1], b_expert[1])
    h3, xn2, route2, cnt2, code2 = _conv_out(u, bg, h2, jnp.zeros((batch, 2, d), F32),
                                      state_conv[0].astype(F32), conv_w[0].astype(F32),
                                      w_out_c[0].astype(BF16), row1(norm_ffn[1]), wrh, wrl, br,
                                      tstrict, seq, tn)
    out_p, out_s = _moe(xn2, h3, route2, code2, cnt2, 1, w1, w3, w2, n_p=n_p)

    y_prompt = out_p.reshape(batch, seq, d)
    y_sample = out_s.reshape(nseq_s, tn, d)
    n_kp = batch * keep
    heads = lambda a, b_, t_: a.reshape(b_, t_, N_HEADS, HEAD_DIM)
    nk_p = heads(kf[:n_kp], batch, keep)[None]
    nv_p = heads(vf[:n_kp], batch, keep)[None]
    kf_s = heads(kf[n_kp:n_kp + n_s], nseq_s, tn)
    vf_s = heads(vf[n_kp:n_kp + n_s], nseq_s, tn)
    nk_s = jnp.concatenate([cache_a_k[l].astype(F32), kf_s], axis=1)[:, -la:][None]
    nv_s = jnp.concatenate([cache_a_v[l].astype(F32), vf_s], axis=1)[:, -la:][None]
    nh_p = jnp.swapaxes(st_p, -1, -2)[None]
    nh_s = jnp.swapaxes(st_s, -1, -2)[None]
    nc_p = jnp.stack([u[(b + 1) * seq - 2:(b + 1) * seq] for b in range(batch)])[None]
    u_s = u[n_p:].reshape(nseq_s, tn, d)
    nc_s = jnp.concatenate([state_conv[0].astype(F32), u_s], axis=1)[:, -2:][None]
    return (y_prompt, y_sample, nk_p, nv_p, nk_s, nv_s, nh_p, nh_s, nc_p, nc_s)
```

```python
import functools

import jax
import jax.numpy as jnp
from jax import lax
from jax.experimental import pallas as pl
from jax.experimental.pallas import tpu as pltpu

F32 = jnp.float32
BF16 = jnp.bfloat16
I32 = jnp.int32

D_MODEL = 1024
CHUNK = 64
BAND_CHUNKS = 8
HEAD_DIM = 64
N_HEADS = 8
D_HALF = N_HEADS * HEAD_DIM
REL_CLIP = 128
HGRN_BLOCK = 16
N_GROUPS = 4
EXPERTS_PER_GROUP = 8
N_EXPERTS = N_GROUPS * EXPERTS_PER_GROUP
D_EXPERT = 256
RMS_EPS = 1e-6
NEG_INF = -1e30

LANES = 128
ROW_TILE = D_MODEL // LANES
TOKEN_TILE = 256
ATTN_TILE = 256
ATTN_WINDOW = ATTN_TILE + BAND_CHUNKS * CHUNK
ATTN_SPAN = -(-(BAND_CHUNKS + 1) * CHUNK // LANES) * LANES
ATTN_HEAD_GROUP = 4
MOE_TILE = 256
RANK_BITS = 16
RANK_SPAN = 1 << RANK_BITS
GATHER_UNROLL = 8
VMEM_LIMIT = 48 * 1024 * 1024


def _cparams(*sem):
    return pltpu.CompilerParams(dimension_semantics=sem, vmem_limit_bytes=VMEM_LIMIT)


def _const_spec(shape):
    nd = len(shape)
    return pl.BlockSpec(shape, lambda *_: (0,) * nd)


def _store_row_tiles(ref, val):
    rows = val.shape[0]
    for c in range(ROW_TILE):
        ref[pl.ds(c, rows, stride=ROW_TILE), :] = val[:, c * LANES:(c + 1) * LANES]


def _load_row_tiles(ref, rows):
    return jnp.concatenate([ref[pl.ds(c, rows, stride=ROW_TILE), :] for c in range(ROW_TILE)], axis=1)


def _rms(x, gain):
    ms = jnp.mean(x * x, axis=-1, keepdims=True)
    return (x * lax.rsqrt(ms + RMS_EPS)) * gain


def _split2(x):
    hi = x.astype(BF16)
    lo = (x - hi.astype(F32)).astype(BF16)
    return hi, lo


def _split3(x):
    p0 = x.astype(BF16)
    r = x - p0.astype(F32)
    p1 = r.astype(BF16)
    p2 = (r - p1.astype(F32)).astype(BF16)
    return p0, p1, p2


def _dot(a, b):
    return jnp.dot(a, b, preferred_element_type=F32)


def _dot_nt(a, b):
    return lax.dot_general(a, b, (((1,), (1,)), ((), ())), preferred_element_type=F32)


def _dot_tn(a, b):
    return lax.dot_general(a, b, (((0,), (0,)), ((), ())), preferred_element_type=F32)


def _head_mean_sq(v, bd):
    hi, lo = _split2(v * v)
    return _dot(hi, bd) + _dot(lo, bd)


def _group_specs(width, tiles_p, tm=TOKEN_TILE):
    return [pl.BlockSpec((tm, width), lambda i: (jnp.minimum(i, tiles_p - 1), 0)),
            pl.BlockSpec((tm, width), lambda i: (jnp.maximum(i - tiles_p, 0), 0))]


def _group_pick(p_ref, s_ref, tiles_p):
    return jnp.where(pl.program_id(0) < tiles_p, p_ref[...], s_ref[...])


def _proj_ab_body(xp_ref, xs_ref, g_ref, w_ref, qg_ref, kg_ref, lb_ref, bd_ref,
                  q_o, kf_o, vf_o, k16_o, v16_o, lf_o, kb_o, vb_o, qb_o, gb_o, *, tiles_p):
    xb = _rms(_group_pick(xp_ref, xs_ref, tiles_p), g_ref[...]).astype(BF16)
    bd = bd_ref[...]

    def seg(j):
        return _dot(xb, w_ref[:, j * D_HALF:(j + 1) * D_HALF])

    qa = seg(0)
    qn = qa * lax.rsqrt(_head_mean_sq(qa, bd) + RMS_EPS) * qg_ref[...]
    q_o[...] = (qn * (HEAD_DIM ** -0.5)).astype(BF16)
    ka = seg(1)
    kn = ka * lax.rsqrt(_head_mean_sq(ka, bd) + RMS_EPS) * kg_ref[...]
    kf_o[...] = kn
    k16_o[...] = kn.astype(BF16)
    va = seg(2)
    vf_o[...] = va
    v16_o[...] = va.astype(BF16)
    lb = lb_ref[...]
    f = lb + (1.0 - lb) * jax.nn.sigmoid(seg(3))
    lf_o[...] = jnp.log(f)
    kb_o[...] = 1.0 - f
    vb_o[...] = seg(4)
    qb_o[...] = jax.nn.silu(seg(5))
    gb_o[...] = seg(6)


def _proj_ab(x_p, x_s, gain, w16, qg, kg, lb, bd, seq, keep):
    tm = TOKEN_TILE
    n_p, n_s = x_p.shape[0], x_s.shape[0]
    n = n_p + n_s
    tiles_p, tiles_seq, tiles_keep = n_p // tm, seq // tm, keep // tm
    kept_tiles = (n_p // seq) * tiles_keep + n_s // tm

    def keep_map(i):
        b, j = i // tiles_seq, i % tiles_seq
        prompt_slot = b * tiles_keep + jnp.maximum(j - (tiles_seq - tiles_keep), 0)
        return (jnp.where(i >= tiles_p, kept_tiles - n_s // tm + (i - tiles_p), prompt_slot), 0)

    row = pl.BlockSpec((tm, D_HALF), lambda i: (i, 0))
    kept = pl.BlockSpec((tm, D_HALF), keep_map)
    full = lambda dt: jax.ShapeDtypeStruct((n, D_HALF), dt)
    kept_shape = jax.ShapeDtypeStruct((kept_tiles * tm, D_HALF), F32)
    return pl.pallas_call(
        functools.partial(_proj_ab_body, tiles_p=tiles_p),
        out_shape=[full(BF16), kept_shape, kept_shape, full(BF16), full(BF16),
                   full(F32), full(F32), full(F32), full(F32), full(F32)],
        grid=(n // tm,),
        in_specs=_group_specs(D_MODEL, tiles_p) + [
            _const_spec((1, D_MODEL)), _const_spec(w16.shape),
            _const_spec((1, D_HALF)), _const_spec((1, D_HALF)), _const_spec((1, D_HALF)),
            _const_spec((D_HALF, D_HALF))],
        out_specs=[row, kept, kept] + [row] * 7,
        compiler_params=_cparams("arbitrary"),
        name="proj_ab",
    )(x_p, x_s, gain, w16, qg, kg, lb, bd)


def _attn_prompt_body(q_ref, k0, k1, k2, v0, v1, v2, bias_ref, o_ref):
    i = pl.program_id(1)
    lim = jnp.maximum(2 - i, 0) * ATTN_TILE
    col = lax.broadcasted_iota(I32, (CHUNK, ATTN_SPAN), 1)
    cut = ATTN_SPAN - 2 * ATTN_TILE

    def span(parts, start):
        if start == 0:
            return (parts[0], parts[1], parts[2][:cut])
        return (parts[0][start:], parts[1], parts[2])

    for h0 in range(0, N_HEADS, ATTN_HEAD_GROUP):
        units = []
        for h in range(h0, h0 + ATTN_HEAD_GROUP):
            hs = slice(h * HEAD_DIM, (h + 1) * HEAD_DIM)
            kh = (k0[:, hs], k1[:, hs], k2[:, hs])
            vh = (v0[:, hs], v1[:, hs], v2[:, hs])
            for r in range(ATTN_TILE // CHUNK):
                start = (r * CHUNK // LANES) * LANES
                units.append((h, hs, slice(r * CHUNK, (r + 1) * CHUNK), start,
                              span(kh, start), span(vh, start)))
        s = [jnp.concatenate([_dot_nt(q_ref[rows, hs], kp) for kp in ks], axis=1)
             for (_, hs, rows, _, ks, _) in units]
        s = [jnp.where(col + start < lim, NEG_INF,
                       s_u + bias_ref[h, rows, start:start + ATTN_SPAN])
             for s_u, (h, _, rows, start, _, _) in zip(s, units)]
        m = [jnp.max(s_u, axis=-1, keepdims=True) for s_u in s]
        p = [jnp.exp(s_u - m_u) for s_u, m_u in zip(s, m)]
        l = [jnp.sum(p_u, axis=-1, keepdims=True) for p_u in p]
        outs = []
        for p_u, (_, _, _, _, _, vs) in zip(p, units):
            p16 = p_u.astype(BF16)
            o, at = 0.0, 0
            for vp in vs:
                o = o + _dot(p16[:, at:at + vp.shape[0]], vp)
                at += vp.shape[0]
            outs.append(o)
        for o_u, l_u, (_, hs, rows, _, _, _) in zip(outs, l, units):
            o_ref[rows, hs] = (o_u / l_u).astype(BF16)


def _attn_prompt(q16, k16, v16, bias_full, batch, seq):
    tiles = seq // ATTN_TILE
    qspec = pl.BlockSpec((ATTN_TILE, D_HALF), lambda b, i: (b * tiles + i, 0))

    def kv(back):
        return pl.BlockSpec((ATTN_TILE, D_HALF),
                            lambda b, i: (b * tiles + jnp.maximum(i - back, 0), 0))

    return pl.pallas_call(
        _attn_prompt_body,
        out_shape=jax.ShapeDtypeStruct((batch * seq, D_HALF), BF16),
        grid=(batch, tiles),
        in_specs=[qspec, kv(2), kv(1), kv(0), kv(2), kv(1), kv(0),
                  _const_spec(bias_full.shape)],
        out_specs=qspec,
        compiler_params=_cparams("parallel", "parallel"),
        name="attn_prompt",
    )(q16, k16, k16, k16, v16, v16, v16, bias_full)


def _attn_sample_body(q_ref, kc_ref, vc_ref, kn_ref, vn_ref, bc_ref, bn_ref, o_ref):
    la = kc_ref.shape[1] // N_HEADS
    for h in range(N_HEADS):
        hs = slice(h * HEAD_DIM, (h + 1) * HEAD_DIM)
        kc = kc_ref[0, pl.ds(h, la, stride=N_HEADS), :].astype(BF16)
        vc = vc_ref[0, pl.ds(h, la, stride=N_HEADS), :].astype(BF16)
        qh = q_ref[:, hs]
        sc = _dot_nt(qh, kc) + bc_ref[h]
        sn = _dot_nt(qh, kn_ref[:, hs]) + bn_ref[h]
        m = jnp.maximum(jnp.max(sc, axis=-1, keepdims=True), jnp.max(sn, axis=-1, keepdims=True))
        pc = jnp.exp(sc - m)
        pn = jnp.exp(sn - m)
        l = jnp.sum(pc, axis=-1, keepdims=True) + jnp.sum(pn, axis=-1, keepdims=True)
        o = _dot(pc.astype(BF16), vc) + _dot(pn.astype(BF16), vn_ref[:, hs])
        o_ref[:, hs] = (o / l).astype(BF16)


def _attn_sample(q16, k16, v16, cache_k, cache_v, bias_c, bias_n, row0, nseq, tn):
    blk0 = row0 // tn
    la8 = cache_k.shape[1]
    new = pl.BlockSpec((tn, D_HALF), lambda b: (blk0 + b, 0))
    cache = pl.BlockSpec((1, la8, HEAD_DIM), lambda b: (b, 0, 0))
    return pl.pallas_call(
        _attn_sample_body,
        out_shape=jax.ShapeDtypeStruct((nseq * tn, D_HALF), BF16),
        grid=(nseq,),
        in_specs=[new, cache, cache, new, new, _const_spec(bias_c.shape), _const_spec(bias_n.shape)],
        out_specs=pl.BlockSpec((tn, D_HALF), lambda b: (b, 0)),
        compiler_params=_cparams("parallel"),
        name="attn_sample",
    )(q16, cache_k, cache_v, k16, v16, bias_c, bias_n)


def _hgrn_body(q_ref, k_ref, v_ref, lf_ref, s0_ref, tcat_ref, o_ref, sout_ref, st_scr,
               *, chunk, nchunks):
    nblk = chunk // HGRN_BLOCK

    @pl.when(pl.program_id(1) == 0)
    def _():
        st_scr[...] = s0_ref[0]

    row = lax.broadcasted_iota(I32, (chunk, D_HALF), 0)
    r2 = lax.broadcasted_iota(I32, (chunk, chunk), 0)
    c2 = lax.broadcasted_iota(I32, (chunk, chunk), 1)
    same_blk_causal = jnp.logical_and(r2 // HGRN_BLOCK == c2 // HGRN_BLOCK, c2 <= r2)

    def one_chunk(c, carry):
        sl = pl.ds(pl.multiple_of(c * chunk, chunk), chunk)
        q = q_ref[sl, :]
        k = k_ref[sl, :]
        v16 = v_ref[sl, :].astype(BF16)
        l0, l1, l2 = _split3(lf_ref[sl, :])
        tcat = tcat_ref[...]
        both = _dot(tcat, l0) + _dot(tcat, l1) + _dot(tcat, l2)
        b_in = both[:chunk]
        b_ch = both[chunk:]
        ld = (q * jnp.exp(b_in)).astype(BF16)
        rd = (k * jnp.exp(-b_in)).astype(BF16)
        lj, rj = [], []
        for j in range(nblk - 1):
            e_j = b_ch[(j + 1) * HGRN_BLOCK - 1:(j + 1) * HGRN_BLOCK, :]
            later = row >= (j + 1) * HGRN_BLOCK
            inside = jnp.logical_and(row >= j * HGRN_BLOCK, row < (j + 1) * HGRN_BLOCK)
            lj.append(jnp.where(later, q * jnp.exp(jnp.minimum(b_ch - e_j, 0.0)), 0.0).astype(BF16))
            rj.append(jnp.where(inside, k * jnp.exp(jnp.minimum(e_j - b_ch, 0.0)), 0.0).astype(BF16))
        e_end = b_ch[chunk - 1:chunk, :]
        qc = (q * jnp.exp(b_ch)).astype(BF16)
        kc = (k * jnp.exp(e_end - b_ch)).astype(BF16)
        dec = jnp.exp(e_end)
        heads = [slice(h * HEAD_DIM, (h + 1) * HEAD_DIM) for h in range(N_HEADS)]
        sts = [st_scr[h] for h in range(N_HEADS)]
        diag = [_dot_nt(ld[:, hs], rd[:, hs]) for hs in heads]
        off = []
        for hs in heads:
            acc = None
            for j in range(nblk - 1):
                term = _dot_nt(lj[j][:, hs], rj[j][:, hs])
                acc = term if acc is None else acc + term
            off.append(acc)
        inter = [_dot_nt(qc[:, hs], st.astype(BF16)) for hs, st in zip(heads, sts)]
        upd = [_dot_tn(v16[:, hs], kc[:, hs]) for hs in heads]
        sc16 = [(jnp.where(same_blk_causal, d, 0.0) + o_).astype(BF16) for d, o_ in zip(diag, off)]
        intra = [_dot(s_, v16[:, hs]) for s_, hs in zip(sc16, heads)]
        o_ref[sl, :] = jnp.concatenate([a + b for a, b in zip(intra, inter)], axis=1)
        for h, hs in enumerate(heads):
            st_scr[h] = sts[h] * dec[:, hs] + upd[h]
        return carry

    lax.fori_loop(0, nchunks, one_chunk, 0)
    sout_ref[0] = st_scr[...]


def _hgrn(qb, kb, vb, lf, s0t, row0, nseq, seq, chunk, nchunks):
    tt = chunk * nchunks
    steps = seq // tt
    blk0 = row0 // tt
    t = jnp.arange(chunk)
    lower = t[None, :] <= t[:, None]
    same_blk = (t[None, :] // HGRN_BLOCK) == (t[:, None] // HGRN_BLOCK)
    tcat = jnp.concatenate([jnp.logical_and(lower, same_blk), lower], axis=0).astype(BF16)
    tok = pl.BlockSpec((tt, D_HALF), lambda b, j: (blk0 + b * steps + j, 0))
    state = pl.BlockSpec((1, N_HEADS, HEAD_DIM, HEAD_DIM), lambda b, j: (b, 0, 0, 0))
    return pl.pallas_call(
        functools.partial(_hgrn_body, chunk=chunk, nchunks=nchunks),
        out_shape=[jax.ShapeDtypeStruct((nseq * seq, D_HALF), F32),
                   jax.ShapeDtypeStruct(s0t.shape, F32)],
        grid=(nseq, steps),
        in_specs=[tok, tok, tok, tok, state, _const_spec((2 * chunk, chunk))],
        out_specs=[pl.BlockSpec((tt, D_HALF), lambda b, j: (b * steps + j, 0)), state],
        scratch_shapes=[pltpu.VMEM((N_HEADS, HEAD_DIM, HEAD_DIM), F32)],
        compiler_params=_cparams("parallel", "arbitrary"),
        name="hgrn",
    )(qb, kb, vb, lf, s0t, tcat)


def _route(xn, wr_hi, wr_lo, br, tstrict, cnt_scr, route_o, cnt_o, code_o):
    rows = xn.shape[0]
    x_hi, x_lo = _split2(xn)
    logits = _dot(x_hi, wr_hi) + _dot(x_lo, wr_hi) + _dot(x_hi, wr_lo) + br
    lane = lax.broadcasted_iota(I32, (rows, LANES), 1)
    lane_f = lane.astype(F32)

    def first_argmax(vals):
        m = jnp.max(vals, axis=-1, keepdims=True)
        idx = jnp.min(jnp.where(vals == m, lane_f, float(LANES)), axis=-1, keepdims=True)
        return m, idx.astype(I32)

    is_grp = jnp.logical_and(lane >= N_EXPERTS, lane < N_EXPERTS + N_GROUPS)
    gl = jnp.where(is_grp, logits, -jnp.inf)
    gmax, gidx = first_argmax(gl)
    p_grp = 1.0 / jnp.sum(jnp.exp(gl - gmax), axis=-1, keepdims=True)
    grp = gidx - N_EXPERTS
    in_grp = jnp.logical_and(lane < N_EXPERTS, lane // EXPERTS_PER_GROUP == grp)
    el = jnp.where(in_grp, logits, -jnp.inf)
    v1, e1 = first_argmax(el)
    v2, e2 = first_argmax(jnp.where(lane == e1, -jnp.inf, el))
    t2 = jnp.exp(v2 - v1)
    den = 1.0 + t2
    g1 = (1.0 / den) * p_grp
    g2 = (t2 / den) * p_grp

    oh1 = lane == e1
    oh2 = lane == e2
    oh = jnp.where(jnp.logical_or(oh1, oh2), 1.0, 0.0)
    before = _dot(tstrict, oh.astype(BF16)) + cnt_scr[...]
    rank1 = jnp.sum(jnp.where(oh1, before, 0.0), axis=-1, keepdims=True)
    rank2 = jnp.sum(jnp.where(oh2, before, 0.0), axis=-1, keepdims=True)
    cnt_scr[...] = cnt_scr[...] + jnp.sum(oh, axis=0, keepdims=True)
    cnt_o[...] = cnt_scr[...]

    e1f, e2f = e1.astype(F32), e2.astype(F32)
    fields = (e1f, e2f, g1, g2, rank1, rank2, e1f * RANK_SPAN + rank1, e2f * RANK_SPAN + rank2)
    slab = jnp.zeros((rows, LANES), F32)
    for idx, val in enumerate(fields):
        slab = jnp.where(lane == idx, val, slab)
    route_o[...] = slab
    code_o[...] = slab.T[6:8, :]


def _out_ab_body(hp_ref, hs_ref, oap_ref, oas_ref, obp_ref, obs_ref, gb_ref, og_ref, bd_ref, w_ref,
                 gf_ref, wrh_ref, wrl_ref, br_ref, ts_ref, h_o, xn_o, route_o, cnt_o, code_o, cnt_scr,
                 *, tiles_p):
    @pl.when(pl.program_id(0) == 0)
    def _():
        cnt_scr[...] = jnp.zeros_like(cnt_scr)

    ob = _group_pick(obp_ref, obs_ref, tiles_p)
    obn = ob * lax.rsqrt(_head_mean_sq(ob, bd_ref[...]) + RMS_EPS) * og_ref[...]
    obg = (obn * jax.nn.silu(gb_ref[...])).astype(BF16)
    oa = _group_pick(oap_ref, oas_ref, tiles_p)
    mix = _dot(oa, w_ref[0:D_HALF, :]) + _dot(obg, w_ref[D_HALF:, :])
    h1 = _group_pick(hp_ref, hs_ref, tiles_p) + mix
    h_o[...] = h1
    xn = _rms(h1, gf_ref[...])
    _store_row_tiles(xn_o, xn)
    _route(xn, wrh_ref[...], wrl_ref[...], br_ref[...], ts_ref[...], cnt_scr, route_o, cnt_o, code_o)


def _out_ab(h_p, h_s, oa_p, oa_s, ob_p, ob_s, gb, og, bd, w16, gf, wrh, wrl, br, tstrict):
    tm = TOKEN_TILE
    n = h_p.shape[0] + h_s.shape[0]
    tiles_p = h_p.shape[0] // tm
    row = lambda w: pl.BlockSpec((tm, w), lambda i: (i, 0))
    return pl.pallas_call(
        functools.partial(_out_ab_body, tiles_p=tiles_p),
        out_shape=[jax.ShapeDtypeStruct((n, D_MODEL), F32), jax.ShapeDtypeStruct((n * ROW_TILE, LANES), F32),
                   jax.ShapeDtypeStruct((n, LANES), F32), jax.ShapeDtypeStruct((1, LANES), F32),
                   jax.ShapeDtypeStruct((2, n), F32)],
        grid=(n // tm,),
        in_specs=_group_specs(D_MODEL, tiles_p) + _group_specs(D_HALF, tiles_p)
        + _group_specs(D_HALF, tiles_p) + [
            row(D_HALF), _const_spec((1, D_HALF)),
            _const_spec((D_HALF, D_HALF)), _const_spec((D_MODEL, D_MODEL)),
            _const_spec((1, D_MODEL)), _const_spec((D_MODEL, LANES)),
            _const_spec((D_MODEL, LANES)), _const_spec((1, LANES)), _const_spec((tm, tm))],
        out_specs=[row(D_MODEL), pl.BlockSpec((tm * ROW_TILE, LANES), lambda i: (i, 0)), row(LANES),
                   _const_spec((1, LANES)), pl.BlockSpec((2, tm), lambda i: (0, i))],
        scratch_shapes=[pltpu.VMEM((1, LANES), F32)],
        compiler_params=_cparams("arbitrary"),
        name="out_ab",
    )(h_p, h_s, oa_p, oa_s, ob_p, ob_s, gb, og, bd, w16, gf, wrh, wrl, br, tstrict)


def _row_copy(src, src_row, dst, dst_row, sem):
    s0 = pl.multiple_of(src_row * ROW_TILE, ROW_TILE)
    d0 = pl.multiple_of(dst_row * ROW_TILE, ROW_TILE)
    return pltpu.make_async_copy(src.at[pl.ds(s0, ROW_TILE)], dst.at[pl.ds(d0, ROW_TILE)], sem)


def _sorted_row(code_ref, offs_ref, a):
    code = code_ref[a]
    return offs_ref[code >> RANK_BITS] + (code & (RANK_SPAN - 1))


def _collect_body(code_ref, offs_ref, ys_hbm, h_ref, r_ref, *rest, tile, n_tok, tiles_p, proj):
    if proj:
        g_ref, w_ref = rest[:2]
        rest = rest[2:]
    outs, (buf, sems) = rest[:-2], rest[-2:]
    i = pl.program_id(0)
    nt = pl.num_programs(0)

    def issue(t):
        slot = t % 2
        base = t * tile

        def one(r, carry):
            tok = base + r
            for k in range(2):
                row = _sorted_row(code_ref, offs_ref, k * n_tok + tok)
                _row_copy(ys_hbm, row, buf.at[slot].at[k], r, sems.at[slot]).start(priority=k)
            return carry

        lax.fori_loop(0, tile, one, 0, unroll=GATHER_UNROLL)

    @pl.when(i == 0)
    def _():
        issue(0)

    @pl.when(i + 1 < nt)
    def _():
        issue(i + 1)

    slot = i % 2
    for _ in range(2 * tile):
        _row_copy(ys_hbm, 0, buf.at[slot].at[0], 0, sems.at[slot]).wait()
    route = r_ref[...]
    ya = _load_row_tiles(buf.at[slot].at[0], tile)
    yb = _load_row_tiles(buf.at[slot].at[1], tile)
    out = h_ref[...] + (route[:, 2:3] * ya + route[:, 3:4] * yb)
    if proj:
        h_o, u_o, bg_o = outs
        h_o[...] = out
        xb = _rms(out, g_ref[...]).astype(BF16)
        bg_o[...] = _dot(xb, w_ref[:, 0:D_MODEL])
        u_o[...] = _dot(xb, w_ref[:, D_MODEL:2 * D_MODEL]) * _dot(xb, w_ref[:, 2 * D_MODEL:])
    else:
        @pl.when(i < tiles_p)
        def _():
            outs[0][...] = out

        @pl.when(i >= tiles_p)
        def _():
            outs[1][...] = out


def _collect(code, offs, ys, h, route, *, proj=None, n_p=None):
    tm = TOKEN_TILE
    n = h.shape[0]
    row = lambda w: pl.BlockSpec((tm, w), lambda i, code, offs: (i, 0))
    const = lambda shape: pl.BlockSpec(shape, lambda i, code, offs: (0,) * len(shape))
    if proj is not None:
        tiles_p = None
        extra_in, extra_specs = list(proj), [const(proj[0].shape), const(proj[1].shape)]
        out_shape = [jax.ShapeDtypeStruct((n, D_MODEL), F32)] * 3
        out_specs = [row(D_MODEL)] * 3
    else:
        tiles_p = n_p // tm
        extra_in, extra_specs = [], []
        out_shape = [jax.ShapeDtypeStruct((n_p, D_MODEL), F32),
                     jax.ShapeDtypeStruct((n - n_p, D_MODEL), F32)]
        out_specs = [pl.BlockSpec((tm, D_MODEL), lambda i, code, offs: (jnp.minimum(i, tiles_p - 1), 0)),
                     pl.BlockSpec((tm, D_MODEL), lambda i, code, offs: (jnp.maximum(i - tiles_p, 0), 0))]
    return pl.pallas_call(
        functools.partial(_collect_body, tile=tm, n_tok=n, tiles_p=tiles_p, proj=proj is not None),
        out_shape=out_shape,
        grid_spec=pltpu.PrefetchScalarGridSpec(
            num_scalar_prefetch=2,
            grid=(n // tm,),
            in_specs=[pl.BlockSpec(memory_space=pl.ANY), row(D_MODEL), row(LANES)] + extra_specs,
            out_specs=out_specs,
            scratch_shapes=[pltpu.VMEM((2, 2, tm * ROW_TILE, LANES), F32),
                            pltpu.SemaphoreType.DMA((2,))],
        ),
        compiler_params=pltpu.CompilerParams(dimension_semantics=("arbitrary",),
                                             vmem_limit_bytes=VMEM_LIMIT),
        name="moe_collect",
    )(code, offs, ys, h, route, *extra_in)


def _scatter_rows_body(code_ref, offs_ref, pad0_ref, padn_ref, nt_ref, x_ref, xs_hbm,
                       stage, zero_scr, sems, pad_sem, *, tile, n_tok, out_tiles):
    i = pl.program_id(0)
    last = pl.num_programs(0) - 1
    base = i * tile
    slot = i % 2

    def drain(s):
        for _ in range(2 * tile):
            _row_copy(stage.at[s], 0, xs_hbm, 0, sems.at[s]).wait()

    @pl.when(i >= 2)
    def _():
        drain(slot)

    stage[slot] = x_ref[...]

    def issue(r, carry):
        tok = base + r
        src = stage.at[slot]
        for k in range(2):
            row = _sorted_row(code_ref, offs_ref, k * n_tok + tok)
            _row_copy(src, r, xs_hbm, row, sems.at[slot]).start(priority=k)
        return carry

    lax.fori_loop(0, tile, issue, 0, unroll=GATHER_UNROLL)

    @pl.when(i == last)
    def _():
        zero_scr[...] = jnp.zeros_like(zero_scr)

        def fill(lo, count):
            def one(r, carry):
                _row_copy(zero_scr, 0, xs_hbm, lo + r, pad_sem).start()
                return carry

            def done(r, carry):
                _row_copy(zero_scr, 0, xs_hbm, 0, pad_sem).wait()
                return carry

            lax.fori_loop(0, count, one, 0)
            lax.fori_loop(0, count, done, 0)

        for e in range(N_EXPERTS):
            fill(pad0_ref[e], padn_ref[e])

        def tail_copy(t):
            rows = MOE_TILE * ROW_TILE
            return pltpu.make_async_copy(
                zero_scr, xs_hbm.at[pl.ds(pl.multiple_of(t * rows, rows), rows)], pad_sem)

        def tail_start(t, carry):
            tail_copy(t).start()
            return carry

        def tail_done(t, carry):
            tail_copy(t).wait()
            return carry

        lax.fori_loop(nt_ref[0], out_tiles, tail_start, 0)
        lax.fori_loop(nt_ref[0], out_tiles, tail_done, 0)
        drain(slot)

    @pl.when(jnp.logical_and(i == last, i >= 1))
    def _():
        drain(1 - slot)


def _scatter_rows(code, offs, pad0, padn, n_tiles, xn, n_rows):
    tm = TOKEN_TILE
    n = xn.shape[0] // ROW_TILE
    return pl.pallas_call(
        functools.partial(_scatter_rows_body, tile=tm, n_tok=n, out_tiles=n_rows // MOE_TILE),
        out_shape=jax.ShapeDtypeStruct((n_rows * ROW_TILE, LANES), F32),
        grid_spec=pltpu.PrefetchScalarGridSpec(
            num_scalar_prefetch=5,
            grid=(n // tm,),
            in_specs=[pl.BlockSpec((tm * ROW_TILE, LANES), lambda i, *_: (i, 0))],
            out_specs=pl.BlockSpec(memory_space=pl.ANY),
            scratch_shapes=[pltpu.VMEM((2, tm * ROW_TILE, LANES), F32),
                            pltpu.VMEM((MOE_TILE * ROW_TILE, LANES), F32),
                            pltpu.SemaphoreType.DMA((2,)), pltpu.SemaphoreType.DMA(())],
        ),
        compiler_params=pltpu.CompilerParams(dimension_semantics=("arbitrary",)),
        name="moe_scatter",
    )(code, offs, pad0, padn, n_tiles, xn)


def _experts_body(te_ref, nt_ref, x_ref, w1_ref, w3_ref, w2_ref, y_ref, w1_s, w3_s, w2_s):
    i = pl.program_id(0)
    prev = te_ref[jnp.maximum(i - 1, 0)]
    fresh = jnp.logical_or(i == 0, te_ref[i] != prev)

    @pl.when(fresh)
    def _():
        w1_s[...] = w1_ref[0].astype(BF16)
        w3_s[...] = w3_ref[0].astype(BF16)
        w2_s[...] = w2_ref[0].astype(BF16)

    @pl.when(i < nt_ref[0])
    def _():
        x = _load_row_tiles(x_ref, MOE_TILE).astype(BF16)
        a = _dot(x, w1_s[...])
        b = _dot(x, w3_s[...])
        _store_row_tiles(y_ref, _dot((jax.nn.silu(a) * b).astype(BF16), w2_s[...]))

    @pl.when(i >= nt_ref[0])
    def _():
        y_ref[...] = jnp.zeros_like(y_ref)


def _experts(tile_expert, n_tiles, xs, w1, w3, w2):
    rows = xs.shape[0] // ROW_TILE
    tm = MOE_TILE
    return pl.pallas_call(
        _experts_body,
        out_shape=jax.ShapeDtypeStruct((rows * ROW_TILE, LANES), F32),
        grid_spec=pltpu.PrefetchScalarGridSpec(
            num_scalar_prefetch=2,
            grid=(rows // tm,),
            in_specs=[pl.BlockSpec((tm * ROW_TILE, LANES),
                                   lambda i, te, nt: (jnp.minimum(i, nt[0] - 1), 0)),
                      pl.BlockSpec((1, D_MODEL, D_EXPERT), lambda i, te, nt: (te[i], 0, 0)),
                      pl.BlockSpec((1, D_MODEL, D_EXPERT), lambda i, te, nt: (te[i], 0, 0)),
                      pl.BlockSpec((1, D_EXPERT, D_MODEL), lambda i, te, nt: (te[i], 0, 0))],
            out_specs=pl.BlockSpec((tm * ROW_TILE, LANES), lambda i, te, nt: (i, 0)),
            scratch_shapes=[pltpu.VMEM((D_MODEL, D_EXPERT), BF16), pltpu.VMEM((D_MODEL, D_EXPERT), BF16),
                            pltpu.VMEM((D_EXPERT, D_MODEL), BF16)],
        ),
        compiler_params=_cparams("arbitrary"),
        name="moe_experts",
    )(tile_expert, n_tiles, xs, w1, w3, w2)


def _moe_plan(counts, n):
    counts = counts[0, :N_EXPERTS].astype(I32)
    padded = ((counts + MOE_TILE - 1) // MOE_TILE) * MOE_TILE
    ends = jnp.cumsum(padded)
    offs = ends - padded
    max_tiles = (2 * n + N_EXPERTS * (MOE_TILE - 1) + MOE_TILE - 1) // MOE_TILE
    tile_start = jnp.arange(max_tiles, dtype=I32) * MOE_TILE
    tile_expert = jnp.minimum(jnp.sum((tile_start[:, None] >= ends[None, :]).astype(I32), axis=1),
                              N_EXPERTS - 1)
    n_tiles = (ends[-1] // MOE_TILE).reshape(1)
    return offs, offs + counts, padded - counts, tile_expert, n_tiles, max_tiles * MOE_TILE


def _moe(xn, h, route, code, counts, layer, w1, w3, w2, **collect_mode):
    n = h.shape[0]
    code = code.astype(I32).reshape(2 * n)
    offs, pad0, padn, tile_expert, n_tiles, n_rows = _moe_plan(counts, n)
    xs = _scatter_rows(code, offs, pad0, padn, n_tiles, xn, n_rows)
    flat = lambda w: w.reshape((-1,) + w.shape[2:])
    ys = _experts(tile_expert + layer * N_EXPERTS, n_tiles, xs, flat(w1), flat(w3), flat(w2))
    return _collect(code, offs, ys, h, route, **collect_mode)


def _conv_out_body(u_ref, up_ref, stp_ref, sts_ref, bg_ref, h_ref, cw_ref, w_ref, gf_ref,
                   wrh_ref, wrl_ref, br_ref, ts_ref, h_o, xn_o, route_o, cnt_o, code_o, cnt_scr,
                   *, tile, tiles_p, seq_p, seq_s):
    i = pl.program_id(0)

    @pl.when(i == 0)
    def _():
        cnt_scr[...] = jnp.zeros_like(cnt_scr)

    u = u_ref[...]
    rowi = lax.broadcasted_iota(I32, (tile, D_MODEL), 0)
    is_p = i < tiles_p
    at_start = (i * tile) % seq_p == 0
    stp = stp_ref[0]
    prev = up_ref[...]
    m2_p = jnp.where(at_start, stp[0:1, :], prev[6:7, :])
    m1_p = jnp.where(at_start, stp[1:2, :], prev[7:8, :])
    per = tile // seq_s
    sts = sts_ref[...]
    m2_s = jnp.broadcast_to(sts[:, 0:1, :], (per, seq_s, D_MODEL)).reshape(tile, D_MODEL)
    m1_s = jnp.broadcast_to(sts[:, 1:2, :], (per, seq_s, D_MODEL)).reshape(tile, D_MODEL)
    m2 = jnp.where(is_p, m2_p, m2_s)
    m1 = jnp.where(is_p, m1_p, m1_s)
    pos = jnp.where(is_p, rowi, rowi % seq_s)
    u1 = jnp.where(pos == 0, m1, pltpu.roll(u, 1, axis=0))
    u2 = jnp.where(pos == 0, m2, jnp.where(pos == 1, m1, pltpu.roll(u, 2, axis=0)))
    cw = cw_ref[...]
    conv = u2 * cw[0:1, :] + u1 * cw[1:2, :] + u * cw[2:3, :]
    mix = _dot((bg_ref[...] * conv).astype(BF16), w_ref[...])
    h3 = h_ref[...] + mix
    h_o[...] = h3
    xn = _rms(h3, gf_ref[...])
    _store_row_tiles(xn_o, xn)
    _route(xn, wrh_ref[...], wrl_ref[...], br_ref[...], ts_ref[...], cnt_scr, route_o, cnt_o, code_o)


def _conv_out(u, bg, h, state_p, state_s, cw, w16, gf, wrh, wrl, br, tstrict, seq_p, seq_s):
    tm = TOKEN_TILE
    n = u.shape[0]
    tiles_p = state_p.shape[0] * seq_p // tm
    per = tm // seq_s
    row = lambda w: pl.BlockSpec((tm, w), lambda i: (i, 0))
    prev = pl.BlockSpec((8, D_MODEL), lambda i: (jnp.maximum(i * (tm // 8) - 1, 0), 0))
    stp_spec = pl.BlockSpec((1, 2, D_MODEL),
                            lambda i: (jnp.minimum(i, tiles_p - 1) * tm // seq_p, 0, 0))
    sts_spec = pl.BlockSpec((per, 2, D_MODEL), lambda i: (jnp.maximum(i - tiles_p, 0), 0, 0))
    return pl.pallas_call(
        functools.partial(_conv_out_body, tile=tm, tiles_p=tiles_p, seq_p=seq_p, seq_s=seq_s),
        out_shape=[jax.ShapeDtypeStruct((n, D_MODEL), F32),
                   jax.ShapeDtypeStruct((n * ROW_TILE, LANES), F32),
                   jax.ShapeDtypeStruct((n, LANES), F32), jax.ShapeDtypeStruct((1, LANES), F32),
                   jax.ShapeDtypeStruct((2, n), F32)],
        grid=(n // tm,),
        in_specs=[row(D_MODEL), prev, stp_spec, sts_spec, row(D_MODEL), row(D_MODEL),
                  _const_spec((3, D_MODEL)),
                  _const_spec((D_MODEL, D_MODEL)), _const_spec((1, D_MODEL)),
                  _const_spec((D_MODEL, LANES)), _const_spec((D_MODEL, LANES)),
                  _const_spec((1, LANES)), _const_spec((tm, tm))],
        out_specs=[row(D_MODEL), pl.BlockSpec((tm * ROW_TILE, LANES), lambda i: (i, 0)), row(LANES),
                   _const_spec((1, LANES)), pl.BlockSpec((2, tm), lambda i: (0, i))],
        scratch_shapes=[pltpu.VMEM((1, LANES), F32)],
        compiler_params=_cparams("arbitrary"),
        name="conv_out",
    )(u, u, state_p, state_s, bg, h, cw, w16, gf, wrh, wrl, br, tstrict)


def _rel_bias_toeplitz(rel_bias, rows, cols, lead):
    period = rows + cols
    k = jnp.arange(period)
    d = jnp.where(k < cols, k, k - period)
    idx = jnp.clip(lead - d, -REL_CLIP, REL_CLIP) + REL_CLIP
    v = jnp.take(rel_bias.astype(F32), idx, axis=1)
    heads = v.shape[0]
    skew = jnp.tile(v, (1, rows))[:, :rows * (period - 1)].reshape(heads, rows, period - 1)
    return skew[:, :, :cols]


def _bias_prompt(rel_bias):
    r = jnp.arange(ATTN_TILE)[:, None]
    c = jnp.arange(ATTN_WINDOW)[None, :]
    j = c // CHUNK - r // CHUNK
    band = jnp.logical_and(j >= 0, j <= BAND_CHUNKS)
    bias = _rel_bias_toeplitz(rel_bias, ATTN_TILE, ATTN_WINDOW, BAND_CHUNKS * CHUNK)
    return jnp.where(band[None], bias, NEG_INF)


def _bias_sample(rel_bias, la, tn):
    bias = _rel_bias_toeplitz(rel_bias, tn, la + tn, la)
    return bias[:, :, :la], bias[:, :, la:]


def _router_weights(w_group, b_group, w_expert, b_expert):
    pad = LANES - N_EXPERTS - N_GROUPS
    w = jnp.concatenate([w_expert, w_group, jnp.zeros((D_MODEL, pad), F32)], axis=1)
    b = jnp.concatenate([b_expert, b_group, jnp.zeros((pad,), F32)])[None, :].astype(F32)
    hi, lo = _split2(w.astype(F32))
    return hi, lo, b


def kernel(x_prompt, x_sample, cache_a_k, cache_a_v, state_hgrn, state_conv, norm_mix, norm_ffn,
           w_in_ab, w_out_ab, q_norm, k_norm, rel_bias, hgrn_lb_logits, hgrn_out_norm, w_in_c,
           conv_w, w_out_c, w_group, b_group, w_expert, b_expert, w1, w3, w2):
    batch, seq, d = x_prompt.shape
    nseq_s, tn, _ = x_sample.shape
    la = cache_a_k.shape[2]
    n_p = batch * seq
    n_s = nseq_s * tn
    n = n_p + n_s
    keep = min(BAND_CHUNKS * CHUNK, seq)

    x_p = x_prompt.reshape(n_p, d)
    x_s = x_sample.reshape(n_s, d)
    lb_all = jnp.cumsum(jax.nn.softmax(hgrn_lb_logits.astype(F32), axis=0), axis=0)
    head_avg = jnp.kron(jnp.eye(N_HEADS, dtype=F32),
                        jnp.full((HEAD_DIM, HEAD_DIM), 1.0 / HEAD_DIM, F32)).astype(BF16)
    t = jnp.arange(TOKEN_TILE)
    tstrict = (t[None, :] < t[:, None]).astype(BF16)
    row1 = lambda v: v.astype(F32).reshape(1, -1)
    tile8 = lambda v: jnp.tile(v.astype(F32), N_HEADS).reshape(1, -1)

    l = 0
    (q16, kf, vf, k16, v16, lf, kb, vb, qb, gb) = _proj_ab(
        x_p, x_s, row1(norm_mix[0]), w_in_ab[l].astype(BF16), tile8(q_norm[l]), tile8(k_norm[l]),
        row1(lb_all[l]), head_avg, seq, keep)

    oa_p = _attn_prompt(q16, k16, v16, _bias_prompt(rel_bias[l]), batch, seq)
    bias_c, bias_n = _bias_sample(rel_bias[l], la, tn)
    oa_s = _attn_sample(q16, k16, v16, cache_a_k[l].reshape(nseq_s, la * N_HEADS, HEAD_DIM),
                        cache_a_v[l].reshape(nseq_s, la * N_HEADS, HEAD_DIM), bias_c, bias_n,
                        n_p, nseq_s, tn)

    zeros_state = jnp.zeros((batch, N_HEADS, HEAD_DIM, HEAD_DIM), F32)
    ob_p, st_p = _hgrn(qb, kb, vb, lf, zeros_state, 0, batch, seq, 64, 4)
    ob_s, st_s = _hgrn(qb, kb, vb, lf, jnp.swapaxes(state_hgrn[l].astype(F32), -1, -2),
                       n_p, nseq_s, tn, tn, 1)

    wrh, wrl, br = _router_weights(w_group[0], b_group[0], w_expert[0], b_expert[0])
    h1, xn1, route1, cnt1, code1 = _out_ab(x_p, x_s, oa_p, oa_s, ob_p, ob_s, gb, row1(hgrn_out_norm[l]),
                                    head_avg, w_out_ab[l].astype(BF16), row1(norm_ffn[0]),
                                    wrh, wrl, br, tstrict)
    h2, u, bg = _moe(xn1, h1, route1, code1, cnt1, 0, w1, w3, w2,
                     proj=(row1(norm_mix[1]), w_in_c[0].astype(BF16)))

    wrh, wrl, br = _router_weights(w_group[1], b_group[1], w_expert[1], b_expert[1])
    h3, xn2, route2, cnt2, code2 = _conv_out(u, bg, h2, jnp.zeros((batch, 2, d), F32),
                                      state_conv[0].astype(F32), conv_w[0].astype(F32),
                                      w_out_c[0].astype(BF16), row1(norm_ffn[1]), wrh, wrl, br,
                                      tstrict, seq, tn)
    out_p, out_s = _moe(xn2, h3, route2, code2, cnt2, 1, w1, w3, w2, n_p=n_p)

    y_prompt = out_p.reshape(batch, seq, d)
    y_sample = out_s.reshape(nseq_s, tn, d)
    n_kp = batch * keep
    heads = lambda a, b_, t_: a.reshape(b_, t_, N_HEADS, HEAD_DIM)
    nk_p = heads(kf[:n_kp], batch, keep)[None]
    nv_p = heads(vf[:n_kp], batch, keep)[None]
    kf_s = heads(kf[n_kp:n_kp + n_s], nseq_s, tn)
    vf_s = heads(vf[n_kp:n_kp + n_s], nseq_s, tn)
    nk_s = jnp.concatenate([cache_a_k[l].astype(F32), kf_s], axis=1)[:, -la:][None]
    nv_s = jnp.concatenate([cache_a_v[l].astype(F32), vf_s], axis=1)[:, -la:][None]
    nh_p = jnp.swapaxes(st_p, -1, -2)[None]
    nh_s = jnp.swapaxes(st_s, -1, -2)[None]
    nc_p = jnp.stack([u[(b + 1) * seq - 2:(b + 1) * seq] for b in range(batch)])[None]
    u_s = u[n_p:].reshape(nseq_s, tn, d)
    nc_s = jnp.concatenate([state_conv[0].astype(F32), u_s], axis=1)[:, -2:][None]
    return (y_prompt, y_sample, nk_p, nv_p, nk_s, nv_s, nh_p, nh_s, nc_p, nc_s)
```

```python
import functools

import jax
import jax.numpy as jnp
from jax import lax
from jax.experimental import pallas as pl
from jax.experimental.pallas import tpu as pltpu

F32 = jnp.float32
BF16 = jnp.bfloat16
I32 = jnp.int32

D_MODEL = 1024
CHUNK = 64
BAND_CHUNKS = 8
HEAD_DIM = 64
N_HEADS = 8
D_HALF = N_HEADS * HEAD_DIM
REL_CLIP = 128
HGRN_BLOCK = 16
N_GROUPS = 4
EXPERTS_PER_GROUP = 8
N_EXPERTS = N_GROUPS * EXPERTS_PER_GROUP
D_EXPERT = 256
RMS_EPS = 1e-6
NEG_INF = -1e30

LANES = 128
ROW_TILE = D_MODEL // LANES
TOKEN_TILE = 256
PROJ_TILE = 512
ATTN_TILE = 256
ATTN_WINDOW = ATTN_TILE + BAND_CHUNKS * CHUNK
ATTN_UNIT = 2 * CHUNK
ATTN_SPAN = ATTN_UNIT + BAND_CHUNKS * CHUNK
HGRN_UNROLL = 4
ATTN_HEAD_GROUP = 2
MOE_TILE = 256
RANK_BITS = 16
RANK_SPAN = 1 << RANK_BITS
GATHER_UNROLL = 8
VMEM_LIMIT = 48 * 1024 * 1024


def _cparams(*sem):
    return pltpu.CompilerParams(dimension_semantics=sem, vmem_limit_bytes=VMEM_LIMIT)


def _const_spec(shape):
    nd = len(shape)
    return pl.BlockSpec(shape, lambda *_: (0,) * nd)


def _store_row_tiles(ref, val):
    rows = val.shape[0]
    for c in range(ROW_TILE):
        ref[pl.ds(c, rows, stride=ROW_TILE), :] = val[:, c * LANES:(c + 1) * LANES]


def _load_row_tiles(ref, rows):
    return jnp.concatenate([ref[pl.ds(c, rows, stride=ROW_TILE), :] for c in range(ROW_TILE)], axis=1)


def _rms(x, gain):
    ms = jnp.mean(x * x, axis=-1, keepdims=True)
    return (x * lax.rsqrt(ms + RMS_EPS)) * gain


def _split2(x):
    hi = x.astype(BF16)
    lo = (x - hi.astype(F32)).astype(BF16)
    return hi, lo


def _split3(x):
    p0 = x.astype(BF16)
    r = x - p0.astype(F32)
    p1 = r.astype(BF16)
    p2 = (r - p1.astype(F32)).astype(BF16)
    return p0, p1, p2


def _dot(a, b):
    return jnp.dot(a, b, preferred_element_type=F32)


def _dot_nt(a, b):
    return lax.dot_general(a, b, (((1,), (1,)), ((), ())), preferred_element_type=F32)


def _dot_tn(a, b):
    return lax.dot_general(a, b, (((0,), (0,)), ((), ())), preferred_element_type=F32)


def _head_mean_sq(v, bd):
    hi, lo = _split2(v * v)
    return _dot(hi, bd) + _dot(lo, bd)


def _group_specs(width, tiles_p, tm=TOKEN_TILE):
    return [pl.BlockSpec((tm, width), lambda i: (jnp.minimum(i, tiles_p - 1), 0)),
            pl.BlockSpec((tm, width), lambda i: (jnp.maximum(i - tiles_p, 0), 0))]


def _group_pick(p_ref, s_ref, tiles_p):
    return jnp.where(pl.program_id(0) < tiles_p, p_ref[...], s_ref[...])


def _proj_ab_body(xp_ref, xs_ref, g_ref, w_ref, qg_ref, kg_ref, lb_ref, bd_ref,
                  q_o, kf_o, vf_o, k16_o, v16_o, lf_o, kb_o, vb_o, qb_o, gb_o, *, tiles_p):
    xb = _rms(_group_pick(xp_ref, xs_ref, tiles_p), g_ref[...]).astype(BF16)
    bd = bd_ref[...]

    def seg(j):
        return _dot(xb, w_ref[:, j * D_HALF:(j + 1) * D_HALF])

    qa = seg(0)
    qn = qa * lax.rsqrt(_head_mean_sq(qa, bd) + RMS_EPS) * qg_ref[...]
    q_o[...] = (qn * (HEAD_DIM ** -0.5)).astype(BF16)
    ka = seg(1)
    kn = ka * lax.rsqrt(_head_mean_sq(ka, bd) + RMS_EPS) * kg_ref[...]
    kf_o[...] = kn
    k16_o[...] = kn.astype(BF16)
    va = seg(2)
    vf_o[...] = va
    v16_o[...] = va.astype(BF16)
    lb = lb_ref[...]
    f = lb + (1.0 - lb) * jax.nn.sigmoid(seg(3))
    lf_o[...] = jnp.log(f)
    kb_o[...] = 1.0 - f
    vb_o[...] = seg(4)
    qb_o[...] = jax.nn.silu(seg(5))
    gb_o[...] = seg(6)


def _proj_ab(x_p, x_s, gain, w16, qg, kg, lb, bd, seq, keep):
    tm = PROJ_TILE
    n_p, n_s = x_p.shape[0], x_s.shape[0]
    n = n_p + n_s
    tiles_p, tiles_seq, tiles_keep = n_p // tm, seq // tm, keep // tm
    kept_tiles = (n_p // seq) * tiles_keep + n_s // tm

    def keep_map(i):
        b, j = i // tiles_seq, i % tiles_seq
        prompt_slot = b * tiles_keep + jnp.maximum(j - (tiles_seq - tiles_keep), 0)
        return (jnp.where(i >= tiles_p, kept_tiles - n_s // tm + (i - tiles_p), prompt_slot), 0)

    row = pl.BlockSpec((tm, D_HALF), lambda i: (i, 0))
    kept = pl.BlockSpec((tm, D_HALF), keep_map)
    full = lambda dt: jax.ShapeDtypeStruct((n, D_HALF), dt)
    kept_shape = jax.ShapeDtypeStruct((kept_tiles * tm, D_HALF), F32)
    return pl.pallas_call(
        functools.partial(_proj_ab_body, tiles_p=tiles_p),
        out_shape=[full(BF16), kept_shape, kept_shape, full(BF16), full(BF16),
                   full(F32), full(F32), full(F32), full(F32), full(F32)],
        grid=(n // tm,),
        in_specs=_group_specs(D_MODEL, tiles_p, tm) + [
            _const_spec((1, D_MODEL)), _const_spec(w16.shape),
            _const_spec((1, D_HALF)), _const_spec((1, D_HALF)), _const_spec((1, D_HALF)),
            _const_spec((D_HALF, D_HALF))],
        out_specs=[row, kept, kept] + [row] * 7,
        compiler_params=_cparams("arbitrary"),
        name="proj_ab",
    )(x_p, x_s, gain, w16, qg, kg, lb, bd)


def _attn_prompt_body(q_ref, k0, k1, k2, v0, v1, v2, bias_ref, o_ref):
    i = pl.program_id(1)
    lim = jnp.maximum(2 - i, 0) * ATTN_TILE
    col = lax.broadcasted_iota(I32, (ATTN_UNIT, ATTN_SPAN), 1)
    cut = ATTN_SPAN - 2 * ATTN_TILE

    def span(parts, start):
        if start == 0:
            return (parts[0], parts[1], parts[2][:cut])
        return (parts[0][start:], parts[1], parts[2])

    def head_group(h0):
        units = []
        for h in range(h0, h0 + ATTN_HEAD_GROUP):
            hs = slice(h * HEAD_DIM, (h + 1) * HEAD_DIM)
            kh = (k0[:, hs], k1[:, hs], k2[:, hs])
            vh = (v0[:, hs], v1[:, hs], v2[:, hs])
            for r in range(ATTN_TILE // ATTN_UNIT):
                start = r * ATTN_UNIT
                units.append((h, hs, slice(r * ATTN_UNIT, (r + 1) * ATTN_UNIT), start,
                              span(kh, start), span(vh, start)))
        s = [jnp.concatenate([_dot_nt(q_ref[rows, hs], kp) for kp in ks], axis=1)
             for (_, hs, rows, _, ks, _) in units]
        s = [jnp.where(col + start < lim, NEG_INF,
                       s_u + bias_ref[h, rows, start:start + ATTN_SPAN])
             for s_u, (h, _, rows, start, _, _) in zip(s, units)]
        m = [jnp.max(s_u, axis=-1, keepdims=True) for s_u in s]
        p = [jnp.exp(s_u - m_u) for s_u, m_u in zip(s, m)]
        l = [jnp.sum(p_u, axis=-1, keepdims=True) for p_u in p]
        outs = []
        for p_u, (_, _, _, _, _, vs) in zip(p, units):
            p16 = p_u.astype(BF16)
            o, at = 0.0, 0
            for vp in vs:
                o = o + _dot(p16[:, at:at + vp.shape[0]], vp)
                at += vp.shape[0]
            outs.append(o)
        for o_u, l_u, (_, hs, rows, _, _, _) in zip(outs, l, units):
            o_ref[rows, hs] = (o_u / l_u).astype(BF16)

    for h0 in range(0, N_HEADS, ATTN_HEAD_GROUP):
        head_group(h0)


def _attn_prompt(q16, k16, v16, bias_full, batch, seq):
    tiles = seq // ATTN_TILE
    qspec = pl.BlockSpec((ATTN_TILE, D_HALF), lambda b, i: (b * tiles + i, 0))

    def kv(back):
        return pl.BlockSpec((ATTN_TILE, D_HALF),
                            lambda b, i: (b * tiles + jnp.maximum(i - back, 0), 0))

    return pl.pallas_call(
        _attn_prompt_body,
        out_shape=jax.ShapeDtypeStruct((batch * seq, D_HALF), BF16),
        grid=(batch, tiles),
        in_specs=[qspec, kv(2), kv(1), kv(0), kv(2), kv(1), kv(0),
                  _const_spec(bias_full.shape)],
        out_specs=qspec,
        compiler_params=_cparams("parallel", "parallel"),
        name="attn_prompt",
    )(q16, k16, k16, k16, v16, v16, v16, bias_full)


def _attn_sample_body(q_ref, kc_ref, vc_ref, kn_ref, vn_ref, bc_ref, bn_ref, o_ref):
    la = kc_ref.shape[1] // N_HEADS
    for h in range(N_HEADS):
        hs = slice(h * HEAD_DIM, (h + 1) * HEAD_DIM)
        kc = kc_ref[0, pl.ds(h, la, stride=N_HEADS), :].astype(BF16)
        vc = vc_ref[0, pl.ds(h, la, stride=N_HEADS), :].astype(BF16)
        qh = q_ref[:, hs]
        sc = _dot_nt(qh, kc) + bc_ref[h]
        sn = _dot_nt(qh, kn_ref[:, hs]) + bn_ref[h]
        m = jnp.maximum(jnp.max(sc, axis=-1, keepdims=True), jnp.max(sn, axis=-1, keepdims=True))
        pc = jnp.exp(sc - m)
        pn = jnp.exp(sn - m)
        l = jnp.sum(pc, axis=-1, keepdims=True) + jnp.sum(pn, axis=-1, keepdims=True)
        o = _dot(pc.astype(BF16), vc) + _dot(pn.astype(BF16), vn_ref[:, hs])
        o_ref[:, hs] = (o / l).astype(BF16)


def _attn_sample(q16, k16, v16, cache_k, cache_v, bias_c, bias_n, row0, nseq, tn):
    blk0 = row0 // tn
    la8 = cache_k.shape[1]
    new = pl.BlockSpec((tn, D_HALF), lambda b: (blk0 + b, 0))
    cache = pl.BlockSpec((1, la8, HEAD_DIM), lambda b: (b, 0, 0))
    return pl.pallas_call(
        _attn_sample_body,
        out_shape=jax.ShapeDtypeStruct((nseq * tn, D_HALF), BF16),
        grid=(nseq,),
        in_specs=[new, cache, cache, new, new, _const_spec(bias_c.shape), _const_spec(bias_n.shape)],
        out_specs=pl.BlockSpec((tn, D_HALF), lambda b: (b, 0)),
        compiler_params=_cparams("parallel"),
        name="attn_sample",
    )(q16, cache_k, cache_v, k16, v16, bias_c, bias_n)


def _hgrn_body(q_ref, k_ref, v_ref, lf_ref, s0_ref, tcat_ref, o_ref, sout_ref, st_scr,
               *, chunk, nchunks):
    nblk = chunk // HGRN_BLOCK

    @pl.when(pl.program_id(1) == 0)
    def _():
        st_scr[...] = s0_ref[0]

    row = lax.broadcasted_iota(I32, (chunk, D_HALF), 0)
    r2 = lax.broadcasted_iota(I32, (chunk, chunk), 0)
    c2 = lax.broadcasted_iota(I32, (chunk, chunk), 1)
    same_blk_causal = jnp.logical_and(r2 // HGRN_BLOCK == c2 // HGRN_BLOCK, c2 <= r2)

    def one_chunk(c, carry):
        sl = pl.ds(pl.multiple_of(c * chunk, chunk), chunk)
        q = q_ref[sl, :]
        k = k_ref[sl, :]
        v16 = v_ref[sl, :].astype(BF16)
        l0, l1, l2 = _split3(lf_ref[sl, :])
        tcat = tcat_ref[...]
        both = _dot(tcat, l0) + _dot(tcat, l1) + _dot(tcat, l2)
        b_in = both[:chunk]
        b_ch = both[chunk:]
        ld = (q * jnp.exp(b_in)).astype(BF16)
        rd = (k * jnp.exp(-b_in)).astype(BF16)
        lj, rj = [], []
        for j in range(nblk - 1):
            e_j = b_ch[(j + 1) * HGRN_BLOCK - 1:(j + 1) * HGRN_BLOCK, :]
            later = row >= (j + 1) * HGRN_BLOCK
            inside = jnp.logical_and(row >= j * HGRN_BLOCK, row < (j + 1) * HGRN_BLOCK)
            lj.append(jnp.where(later, q * jnp.exp(jnp.minimum(b_ch - e_j, 0.0)), 0.0).astype(BF16))
            rj.append(jnp.where(inside, k * jnp.exp(jnp.minimum(e_j - b_ch, 0.0)), 0.0).astype(BF16))
        e_end = b_ch[chunk - 1:chunk, :]
        qc = (q * jnp.exp(b_ch)).astype(BF16)
        kc = (k * jnp.exp(e_end - b_ch)).astype(BF16)
        dec = jnp.exp(e_end)
        heads = [slice(h * HEAD_DIM, (h + 1) * HEAD_DIM) for h in range(N_HEADS)]
        sts = [st_scr[h] for h in range(N_HEADS)]
        diag = [_dot_nt(ld[:, hs], rd[:, hs]) for hs in heads]
        off = []
        for hs in heads:
            acc = None
            for j in range(nblk - 1):
                term = _dot_nt(lj[j][:, hs], rj[j][:, hs])
                acc = term if acc is None else acc + term
            off.append(acc)
        inter = [_dot_nt(qc[:, hs], st.astype(BF16)) for hs, st in zip(heads, sts)]
        upd = [_dot_tn(v16[:, hs], kc[:, hs]) for hs in heads]
        sc16 = [(jnp.where(same_blk_causal, d, 0.0) + o_).astype(BF16) for d, o_ in zip(diag, off)]
        intra = [_dot(s_, v16[:, hs]) for s_, hs in zip(sc16, heads)]
        o_ref[sl, :] = jnp.concatenate([a + b for a, b in zip(intra, inter)], axis=1)
        for h, hs in enumerate(heads):
            st_scr[h] = sts[h] * dec[:, hs] + upd[h]
        return carry

    lax.fori_loop(0, nchunks, one_chunk, 0, unroll=min(nchunks, HGRN_UNROLL))
    sout_ref[0] = st_scr[...]


def _hgrn(qb, kb, vb, lf, s0t, row0, nseq, seq, chunk, nchunks):
    tt = chunk * nchunks
    steps = seq // tt
    blk0 = row0 // tt
    t = jnp.arange(chunk)
    lower = t[None, :] <= t[:, None]
    same_blk = (t[None, :] // HGRN_BLOCK) == (t[:, None] // HGRN_BLOCK)
    tcat = jnp.concatenate([jnp.logical_and(lower, same_blk), lower], axis=0).astype(BF16)
    tok = pl.BlockSpec((tt, D_HALF), lambda b, j: (blk0 + b * steps + j, 0))
    state = pl.BlockSpec((1, N_HEADS, HEAD_DIM, HEAD_DIM), lambda b, j: (b, 0, 0, 0))
    return pl.pallas_call(
        functools.partial(_hgrn_body, chunk=chunk, nchunks=nchunks),
        out_shape=[jax.ShapeDtypeStruct((nseq * seq, D_HALF), F32),
                   jax.ShapeDtypeStruct(s0t.shape, F32)],
        grid=(nseq, steps),
        in_specs=[tok, tok, tok, tok, state, _const_spec((2 * chunk, chunk))],
        out_specs=[pl.BlockSpec((tt, D_HALF), lambda b, j: (b * steps + j, 0)), state],
        scratch_shapes=[pltpu.VMEM((N_HEADS, HEAD_DIM, HEAD_DIM), F32)],
        compiler_params=_cparams("parallel", "arbitrary"),
        name="hgrn",
    )(qb, kb, vb, lf, s0t, tcat)


def _route(xn, wr_hi, wr_lo, br, tstrict, cnt_scr, route_o, cnt_o, code_o):
    rows = xn.shape[0]
    x_hi, x_lo = _split2(xn)
    logits = _dot(x_hi, wr_hi) + _dot(x_lo, wr_hi) + _dot(x_hi, wr_lo) + br
    lane = lax.broadcasted_iota(I32, (rows, LANES), 1)
    lane_f = lane.astype(F32)

    def first_argmax(vals):
        m = jnp.max(vals, axis=-1, keepdims=True)
        idx = jnp.min(jnp.where(vals == m, lane_f, float(LANES)), axis=-1, keepdims=True)
        return m, idx.astype(I32)

    is_grp = jnp.logical_and(lane >= N_EXPERTS, lane < N_EXPERTS + N_GROUPS)
    gl = jnp.where(is_grp, logits, -jnp.inf)
    gmax, gidx = first_argmax(gl)
    p_grp = 1.0 / jnp.sum(jnp.exp(gl - gmax), axis=-1, keepdims=True)
    grp = gidx - N_EXPERTS
    in_grp = jnp.logical_and(lane < N_EXPERTS, lane // EXPERTS_PER_GROUP == grp)
    el = jnp.where(in_grp, logits, -jnp.inf)
    v1, e1 = first_argmax(el)
    v2, e2 = first_argmax(jnp.where(lane == e1, -jnp.inf, el))
    t2 = jnp.exp(v2 - v1)
    den = 1.0 + t2
    g1 = (1.0 / den) * p_grp
    g2 = (t2 / den) * p_grp

    oh1 = lane == e1
    oh2 = lane == e2
    oh = jnp.where(jnp.logical_or(oh1, oh2), 1.0, 0.0)
    before = _dot(tstrict, oh.astype(BF16)) + cnt_scr[...]
    rank1 = jnp.sum(jnp.where(oh1, before, 0.0), axis=-1, keepdims=True)
    rank2 = jnp.sum(jnp.where(oh2, before, 0.0), axis=-1, keepdims=True)
    cnt_scr[...] = cnt_scr[...] + jnp.sum(oh, axis=0, keepdims=True)
    cnt_o[...] = cnt_scr[...]

    e1f, e2f = e1.astype(F32), e2.astype(F32)
    fields = (e1f, e2f, g1, g2, rank1, rank2, e1f * RANK_SPAN + rank1, e2f * RANK_SPAN + rank2)
    slab = jnp.zeros((rows, LANES), F32)
    for idx, val in enumerate(fields):
        slab = jnp.where(lane == idx, val, slab)
    route_o[...] = slab
    code_o[...] = slab.T[6:8, :]


def _out_ab_body(hp_ref, hs_ref, oap_ref, oas_ref, obp_ref, obs_ref, gb_ref, og_ref, bd_ref, w_ref,
                 gf_ref, wrh_ref, wrl_ref, br_ref, ts_ref, h_o, xn_o, route_o, cnt_o, code_o, cnt_scr,
                 *, tiles_p):
    @pl.when(pl.program_id(0) == 0)
    def _():
        cnt_scr[...] = jnp.zeros_like(cnt_scr)

    ob = _group_pick(obp_ref, obs_ref, tiles_p)
    obn = ob * lax.rsqrt(_head_mean_sq(ob, bd_ref[...]) + RMS_EPS) * og_ref[...]
    obg = (obn * jax.nn.silu(gb_ref[...])).astype(BF16)
    oa = _group_pick(oap_ref, oas_ref, tiles_p)
    mix = _dot(oa, w_ref[0:D_HALF, :]) + _dot(obg, w_ref[D_HALF:, :])
    h1 = _group_pick(hp_ref, hs_ref, tiles_p) + mix
    h_o[...] = h1
    xn = _rms(h1, gf_ref[...])
    _store_row_tiles(xn_o, xn)
    _route(xn, wrh_ref[...], wrl_ref[...], br_ref[...], ts_ref[...], cnt_scr, route_o, cnt_o, code_o)


def _out_ab(h_p, h_s, oa_p, oa_s, ob_p, ob_s, gb, og, bd, w16, gf, wrh, wrl, br, tstrict):
    tm = TOKEN_TILE
    n = h_p.shape[0] + h_s.shape[0]
    tiles_p = h_p.shape[0] // tm
    row = lambda w: pl.BlockSpec((tm, w), lambda i: (i, 0))
    return pl.pallas_call(
        functools.partial(_out_ab_body, tiles_p=tiles_p),
        out_shape=[jax.ShapeDtypeStruct((n, D_MODEL), F32), jax.ShapeDtypeStruct((n * ROW_TILE, LANES), F32),
                   jax.ShapeDtypeStruct((n, LANES), F32), jax.ShapeDtypeStruct((1, LANES), F32),
                   jax.ShapeDtypeStruct((2, n), F32)],
        grid=(n // tm,),
        in_specs=_group_specs(D_MODEL, tiles_p) + _group_specs(D_HALF, tiles_p)
        + _group_specs(D_HALF, tiles_p) + [
            row(D_HALF), _const_spec((1, D_HALF)),
            _const_spec((D_HALF, D_HALF)), _const_spec((D_MODEL, D_MODEL)),
            _const_spec((1, D_MODEL)), _const_spec((D_MODEL, LANES)),
            _const_spec((D_MODEL, LANES)), _const_spec((1, LANES)), _const_spec((tm, tm))],
        out_specs=[row(D_MODEL), pl.BlockSpec((tm * ROW_TILE, LANES), lambda i: (i, 0)), row(LANES),
                   _const_spec((1, LANES)), pl.BlockSpec((2, tm), lambda i: (0, i))],
        scratch_shapes=[pltpu.VMEM((1, LANES), F32)],
        compiler_params=_cparams("arbitrary"),
        name="out_ab",
    )(h_p, h_s, oa_p, oa_s, ob_p, ob_s, gb, og, bd, w16, gf, wrh, wrl, br, tstrict)


def _row_copy(src, src_row, dst, dst_row, sem):
    s0 = pl.multiple_of(src_row * ROW_TILE, ROW_TILE)
    d0 = pl.multiple_of(dst_row * ROW_TILE, ROW_TILE)
    return pltpu.make_async_copy(src.at[pl.ds(s0, ROW_TILE)], dst.at[pl.ds(d0, ROW_TILE)], sem)


def _sorted_row(code_ref, offs_ref, a):
    code = code_ref[a]
    return offs_ref[code >> RANK_BITS] + (code & (RANK_SPAN - 1))


def _collect_body(code_ref, offs_ref, ys_hbm, h_ref, r_ref, *rest, tile, n_tok, tiles_p, proj):
    if proj:
        g_ref, w_ref = rest[:2]
        rest = rest[2:]
    outs, (buf, sems) = rest[:-2], rest[-2:]
    i = pl.program_id(0)
    nt = pl.num_programs(0)

    def issue(t):
        slot = t % 2
        base = t * tile

        def one(r, carry):
            tok = base + r
            for k in range(2):
                row = _sorted_row(code_ref, offs_ref, k * n_tok + tok)
                _row_copy(ys_hbm, row, buf.at[slot].at[k], r, sems.at[slot]).start(priority=k)
            return carry

        lax.fori_loop(0, tile, one, 0, unroll=GATHER_UNROLL)

    @pl.when(i == 0)
    def _():
        issue(0)

    @pl.when(i + 1 < nt)
    def _():
        issue(i + 1)

    slot = i % 2
    for _ in range(2 * tile):
        _row_copy(ys_hbm, 0, buf.at[slot].at[0], 0, sems.at[slot]).wait()
    route = r_ref[...]
    ya = _load_row_tiles(buf.at[slot].at[0], tile)
    yb = _load_row_tiles(buf.at[slot].at[1], tile)
    out = h_ref[...] + (route[:, 2:3] * ya + route[:, 3:4] * yb)
    if proj:
        h_o, u_o, bg_o = outs
        h_o[...] = out
        xb = _rms(out, g_ref[...]).astype(BF16)
        bg_o[...] = _dot(xb, w_ref[:, 0:D_MODEL])
        u_o[...] = _dot(xb, w_ref[:, D_MODEL:2 * D_MODEL]) * _dot(xb, w_ref[:, 2 * D_MODEL:])
    else:
        @pl.when(i < tiles_p)
        def _():
            outs[0][...] = out

        @pl.when(i >= tiles_p)
        def _():
            outs[1][...] = out


def _collect(code, offs, ys, h, route, *, proj=None, n_p=None):
    tm = TOKEN_TILE
    n = h.shape[0]
    row = lambda w: pl.BlockSpec((tm, w), lambda i, code, offs: (i, 0))
    const = lambda shape: pl.BlockSpec(shape, lambda i, code, offs: (0,) * len(shape))
    if proj is not None:
        tiles_p = None
        extra_in, extra_specs = list(proj), [const(proj[0].shape), const(proj[1].shape)]
        out_shape = [jax.ShapeDtypeStruct((n, D_MODEL), F32)] * 3
        out_specs = [row(D_MODEL)] * 3
    else:
        tiles_p = n_p // tm
        extra_in, extra_specs = [], []
        out_shape = [jax.ShapeDtypeStruct((n_p, D_MODEL), F32),
                     jax.ShapeDtypeStruct((n - n_p, D_MODEL), F32)]
        out_specs = [pl.BlockSpec((tm, D_MODEL), lambda i, code, offs: (jnp.minimum(i, tiles_p - 1), 0)),
                     pl.BlockSpec((tm, D_MODEL), lambda i, code, offs: (jnp.maximum(i - tiles_p, 0), 0))]
    return pl.pallas_call(
        functools.partial(_collect_body, tile=tm, n_tok=n, tiles_p=tiles_p, proj=proj is not None),
        out_shape=out_shape,
        grid_spec=pltpu.PrefetchScalarGridSpec(
            num_scalar_prefetch=2,
            grid=(n // tm,),
            in_specs=[pl.BlockSpec(memory_space=pl.ANY), row(D_MODEL), row(LANES)] + extra_specs,
            out_specs=out_specs,
            scratch_shapes=[pltpu.VMEM((2, 2, tm * ROW_TILE, LANES), F32),
                            pltpu.SemaphoreType.DMA((2,))],
        ),
        compiler_params=pltpu.CompilerParams(dimension_semantics=("arbitrary",),
                                             vmem_limit_bytes=VMEM_LIMIT),
        name="moe_collect",
    )(code, offs, ys, h, route, *extra_in)


def _scatter_rows_body(code_ref, offs_ref, pad0_ref, padn_ref, nt_ref, x_ref, xs_hbm,
                       stage, zero_scr, sems, pad_sem, *, tile, n_tok, out_tiles):
    i = pl.program_id(0)
    last = pl.num_programs(0) - 1
    base = i * tile
    slot = i % 2

    def drain(s):
        for _ in range(2 * tile):
            _row_copy(stage.at[s], 0, xs_hbm, 0, sems.at[s]).wait()

    @pl.when(i >= 2)
    def _():
        drain(slot)

    stage[slot] = x_ref[...]

    def issue(r, carry):
        tok = base + r
        src = stage.at[slot]
        for k in range(2):
            row = _sorted_row(code_ref, offs_ref, k * n_tok + tok)
            _row_copy(src, r, xs_hbm, row, sems.at[slot]).start(priority=k)
        return carry

    lax.fori_loop(0, tile, issue, 0, unroll=GATHER_UNROLL)

    @pl.when(i == last)
    def _():
        zero_scr[...] = jnp.zeros_like(zero_scr)

        def fill(lo, count):
            def one(r, carry):
                _row_copy(zero_scr, 0, xs_hbm, lo + r, pad_sem).start()
                return carry

            def done(r, carry):
                _row_copy(zero_scr, 0, xs_hbm, 0, pad_sem).wait()
                return carry

            lax.fori_loop(0, count, one, 0)
            lax.fori_loop(0, count, done, 0)

        for e in range(N_EXPERTS):
            fill(pad0_ref[e], padn_ref[e])

        def tail_copy(t):
            rows = MOE_TILE * ROW_TILE
            return pltpu.make_async_copy(
                zero_scr, xs_hbm.at[pl.ds(pl.multiple_of(t * rows, rows), rows)], pad_sem)

        def tail_start(t, carry):
            tail_copy(t).start()
            return carry

        def tail_done(t, carry):
            tail_copy(t).wait()
            return carry

        lax.fori_loop(nt_ref[0], out_tiles, tail_start, 0)
        lax.fori_loop(nt_ref[0], out_tiles, tail_done, 0)
        drain(slot)

    @pl.when(jnp.logical_and(i == last, i >= 1))
    def _():
        drain(1 - slot)


def _scatter_rows(code, offs, pad0, padn, n_tiles, xn, n_rows):
    tm = TOKEN_TILE
    n = xn.shape[0] // ROW_TILE
    return pl.pallas_call(
        functools.partial(_scatter_rows_body, tile=tm, n_tok=n, out_tiles=n_rows // MOE_TILE),
        out_shape=jax.ShapeDtypeStruct((n_rows * ROW_TILE, LANES), F32),
        grid_spec=pltpu.PrefetchScalarGridSpec(
            num_scalar_prefetch=5,
            grid=(n // tm,),
            in_specs=[pl.BlockSpec((tm * ROW_TILE, LANES), lambda i, *_: (i, 0))],
            out_specs=pl.BlockSpec(memory_space=pl.ANY),
            scratch_shapes=[pltpu.VMEM((2, tm * ROW_TILE, LANES), F32),
                            pltpu.VMEM((MOE_TILE * ROW_TILE, LANES), F32),
                            pltpu.SemaphoreType.DMA((2,)), pltpu.SemaphoreType.DMA(())],
        ),
        compiler_params=pltpu.CompilerParams(dimension_semantics=("arbitrary",)),
        name="moe_scatter",
    )(code, offs, pad0, padn, n_tiles, xn)


def _experts_body(te_ref, nt_ref, x_ref, w1_ref, w3_ref, w2_ref, y_ref, w1_s, w3_s, w2_s):
    i = pl.program_id(0)
    prev = te_ref[jnp.maximum(i - 1, 0)]
    fresh = jnp.logical_or(i == 0, te_ref[i] != prev)

    @pl.when(fresh)
    def _():
        w1_s[...] = w1_ref[0].astype(BF16)
        w3_s[...] = w3_ref[0].astype(BF16)
        w2_s[...] = w2_ref[0].astype(BF16)

    @pl.when(i < nt_ref[0])
    def _():
        x = _load_row_tiles(x_ref, MOE_TILE).astype(BF16)
        a = _dot(x, w1_s[...])
        b = _dot(x, w3_s[...])
        _store_row_tiles(y_ref, _dot((jax.nn.silu(a) * b).astype(BF16), w2_s[...]))

    @pl.when(i >= nt_ref[0])
    def _():
        y_ref[...] = jnp.zeros_like(y_ref)


def _experts(tile_expert, n_tiles, xs, w1, w3, w2):
    rows = xs.shape[0] // ROW_TILE
    tm = MOE_TILE
    return pl.pallas_call(
        _experts_body,
        out_shape=jax.ShapeDtypeStruct((rows * ROW_TILE, LANES), F32),
        grid_spec=pltpu.PrefetchScalarGridSpec(
            num_scalar_prefetch=2,
            grid=(rows // tm,),
            in_specs=[pl.BlockSpec((tm * ROW_TILE, LANES),
                                   lambda i, te, nt: (jnp.minimum(i, nt[0] - 1), 0)),
                      pl.BlockSpec((1, D_MODEL, D_EXPERT), lambda i, te, nt: (te[i], 0, 0)),
                      pl.BlockSpec((1, D_MODEL, D_EXPERT), lambda i, te, nt: (te[i], 0, 0)),
                      pl.BlockSpec((1, D_EXPERT, D_MODEL), lambda i, te, nt: (te[i], 0, 0))],
            out_specs=pl.BlockSpec((tm * ROW_TILE, LANES), lambda i, te, nt: (i, 0)),
            scratch_shapes=[pltpu.VMEM((D_MODEL, D_EXPERT), BF16), pltpu.VMEM((D_MODEL, D_EXPERT), BF16),
                            pltpu.VMEM((D_EXPERT, D_MODEL), BF16)],
        ),
        compiler_params=_cparams("arbitrary"),
        name="moe_experts",
    )(tile_expert, n_tiles, xs, w1, w3, w2)


def _moe_plan(counts, n):
    counts = counts[0, :N_EXPERTS].astype(I32)
    padded = ((counts + MOE_TILE - 1) // MOE_TILE) * MOE_TILE
    ends = jnp.cumsum(padded)
    offs = ends - padded
    max_tiles = (2 * n + N_EXPERTS * (MOE_TILE - 1) + MOE_TILE - 1) // MOE_TILE
    tile_start = jnp.arange(max_tiles, dtype=I32) * MOE_TILE
    tile_expert = jnp.minimum(jnp.sum((tile_start[:, None] >= ends[None, :]).astype(I32), axis=1),
                              N_EXPERTS - 1)
    n_tiles = (ends[-1] // MOE_TILE).reshape(1)
    return offs, offs + counts, padded - counts, tile_expert, n_tiles, max_tiles * MOE_TILE


def _moe(xn, h, route, code, counts, layer, w1, w3, w2, **collect_mode):
    n = h.shape[0]
    code = code.astype(I32).reshape(2 * n)
    offs, pad0, padn, tile_expert, n_tiles, n_rows = _moe_plan(counts, n)
    xs = _scatter_rows(code, offs, pad0, padn, n_tiles, xn, n_rows)
    flat = lambda w: w.reshape((-1,) + w.shape[2:])
    ys = _experts(tile_expert + layer * N_EXPERTS, n_tiles, xs, flat(w1), flat(w3), flat(w2))
    return _collect(code, offs, ys, h, route, **collect_mode)


def _conv_out_body(u_ref, up_ref, stp_ref, sts_ref, bg_ref, h_ref, cw_ref, w_ref, gf_ref,
                   wrh_ref, wrl_ref, br_ref, ts_ref, h_o, xn_o, route_o, cnt_o, code_o, cnt_scr,
                   *, tile, tiles_p, seq_p, seq_s):
    i = pl.program_id(0)

    @pl.when(i == 0)
    def _():
        cnt_scr[...] = jnp.zeros_like(cnt_scr)

    u = u_ref[...]
    rowi = lax.broadcasted_iota(I32, (tile, D_MODEL), 0)
    is_p = i < tiles_p
    at_start = (i * tile) % seq_p == 0
    stp = stp_ref[0]
    prev = up_ref[...]
    m2_p = jnp.where(at_start, stp[0:1, :], prev[6:7, :])
    m1_p = jnp.where(at_start, stp[1:2, :], prev[7:8, :])
    per = tile // seq_s
    sts = sts_ref[...]
    m2_s = jnp.broadcast_to(sts[:, 0:1, :], (per, seq_s, D_MODEL)).reshape(tile, D_MODEL)
    m1_s = jnp.broadcast_to(sts[:, 1:2, :], (per, seq_s, D_MODEL)).reshape(tile, D_MODEL)
    m2 = jnp.where(is_p, m2_p, m2_s)
    m1 = jnp.where(is_p, m1_p, m1_s)
    pos = jnp.where(is_p, rowi, rowi % seq_s)
    u1 = jnp.where(pos == 0, m1, pltpu.roll(u, 1, axis=0))
    u2 = jnp.where(pos == 0, m2, jnp.where(pos == 1, m1, pltpu.roll(u, 2, axis=0)))
    cw = cw_ref[...]
    conv = u2 * cw[0:1, :] + u1 * cw[1:2, :] + u * cw[2:3, :]
    mix = _dot((bg_ref[...] * conv).astype(BF16), w_ref[...])
    h3 = h_ref[...] + mix
    h_o[...] = h3
    xn = _rms(h3, gf_ref[...])
    _store_row_tiles(xn_o, xn)
    _route(xn, wrh_ref[...], wrl_ref[...], br_ref[...], ts_ref[...], cnt_scr, route_o, cnt_o, code_o)


def _conv_out(u, bg, h, state_p, state_s, cw, w16, gf, wrh, wrl, br, tstrict, seq_p, seq_s):
    tm = TOKEN_TILE
    n = u.shape[0]
    tiles_p = state_p.shape[0] * seq_p // tm
    per = tm // seq_s
    row = lambda w: pl.BlockSpec((tm, w), lambda i: (i, 0))
    prev = pl.BlockSpec((8, D_MODEL), lambda i: (jnp.maximum(i * (tm // 8) - 1, 0), 0))
    stp_spec = pl.BlockSpec((1, 2, D_MODEL),
                            lambda i: (jnp.minimum(i, tiles_p - 1) * tm // seq_p, 0, 0))
    sts_spec = pl.BlockSpec((per, 2, D_MODEL), lambda i: (jnp.maximum(i - tiles_p, 0), 0, 0))
    return pl.pallas_call(
        functools.partial(_conv_out_body, tile=tm, tiles_p=tiles_p, seq_p=seq_p, seq_s=seq_s),
        out_shape=[jax.ShapeDtypeStruct((n, D_MODEL), F32),
                   jax.ShapeDtypeStruct((n * ROW_TILE, LANES), F32),
                   jax.ShapeDtypeStruct((n, LANES), F32), jax.ShapeDtypeStruct((1, LANES), F32),
                   jax.ShapeDtypeStruct((2, n), F32)],
        grid=(n // tm,),
        in_specs=[row(D_MODEL), prev, stp_spec, sts_spec, row(D_MODEL), row(D_MODEL),
                  _const_spec((3, D_MODEL)),
                  _const_spec((D_MODEL, D_MODEL)), _const_spec((1, D_MODEL)),
                  _const_spec((D_MODEL, LANES)), _const_spec((D_MODEL, LANES)),
                  _const_spec((1, LANES)), _const_spec((tm, tm))],
        out_specs=[row(D_MODEL), pl.BlockSpec((tm * ROW_TILE, LANES), lambda i: (i, 0)), row(LANES),
                   _const_spec((1, LANES)), pl.BlockSpec((2, tm), lambda i: (0, i))],
        scratch_shapes=[pltpu.VMEM((1, LANES), F32)],
        compiler_params=_cparams("arbitrary"),
        name="conv_out",
    )(u, u, state_p, state_s, bg, h, cw, w16, gf, wrh, wrl, br, tstrict)


def _rel_bias_toeplitz(rel_bias, rows, cols, lead):
    period = rows + cols
    k = jnp.arange(period)
    d = jnp.where(k < cols, k, k - period)
    idx = jnp.clip(lead - d, -REL_CLIP, REL_CLIP) + REL_CLIP
    v = jnp.take(rel_bias.astype(F32), idx, axis=1)
    heads = v.shape[0]
    skew = jnp.tile(v, (1, rows))[:, :rows * (period - 1)].reshape(heads, rows, period - 1)
    return skew[:, :, :cols]


def _bias_prompt(rel_bias):
    r = jnp.arange(ATTN_TILE)[:, None]
    c = jnp.arange(ATTN_WINDOW)[None, :]
    j = c // CHUNK - r // CHUNK
    band = jnp.logical_and(j >= 0, j <= BAND_CHUNKS)
    bias = _rel_bias_toeplitz(rel_bias, ATTN_TILE, ATTN_WINDOW, BAND_CHUNKS * CHUNK)
    return jnp.where(band[None], bias, NEG_INF)


def _bias_sample(rel_bias, la, tn):
    bias = _rel_bias_toeplitz(rel_bias, tn, la + tn, la)
    return bias[:, :, :la], bias[:, :, la:]


def _router_weights(w_group, b_group, w_expert, b_expert):
    pad = LANES - N_EXPERTS - N_GROUPS
    w = jnp.concatenate([w_expert, w_group, jnp.zeros((D_MODEL, pad), F32)], axis=1)
    b = jnp.concatenate([b_expert, b_group, jnp.zeros((pad,), F32)])[None, :].astype(F32)
    hi, lo = _split2(w.astype(F32))
    return hi, lo, b


def kernel(x_prompt, x_sample, cache_a_k, cache_a_v, state_hgrn, state_conv, norm_mix, norm_ffn,
           w_in_ab, w_out_ab, q_norm, k_norm, rel_bias, hgrn_lb_logits, hgrn_out_norm, w_in_c,
           conv_w, w_out_c, w_group, b_group, w_expert, b_expert, w1, w3, w2):
    batch, seq, d = x_prompt.shape
    nseq_s, tn, _ = x_sample.shape
    la = cache_a_k.shape[2]
    n_p = batch * seq
    n_s = nseq_s * tn
    n = n_p + n_s
    keep = min(BAND_CHUNKS * CHUNK, seq)

    x_p = x_prompt.reshape(n_p, d)
    x_s = x_sample.reshape(n_s, d)
    lb_all = jnp.cumsum(jax.nn.softmax(hgrn_lb_logits.astype(F32), axis=0), axis=0)
    head_avg = jnp.kron(jnp.eye(N_HEADS, dtype=F32),
                        jnp.full((HEAD_DIM, HEAD_DIM), 1.0 / HEAD_DIM, F32)).astype(BF16)
    t = jnp.arange(TOKEN_TILE)
    tstrict = (t[None, :] < t[:, None]).astype(BF16)
    row1 = lambda v: v.astype(F32).reshape(1, -1)
    tile8 = lambda v: jnp.tile(v.astype(F32), N_HEADS).reshape(1, -1)

    l = 0
    (q16, kf, vf, k16, v16, lf, kb, vb, qb, gb) = _proj_ab(
        x_p, x_s, row1(norm_mix[0]), w_in_ab[l].astype(BF16), tile8(q_norm[l]), tile8(k_norm[l]),
        row1(lb_all[l]), head_avg, seq, keep)

    oa_p = _attn_prompt(q16, k16, v16, _bias_prompt(rel_bias[l]), batch, seq)
    bias_c, bias_n = _bias_sample(rel_bias[l], la, tn)
    oa_s = _attn_sample(q16, k16, v16, cache_a_k[l].reshape(nseq_s, la * N_HEADS, HEAD_DIM),
                        cache_a_v[l].reshape(nseq_s, la * N_HEADS, HEAD_DIM), bias_c, bias_n,
                        n_p, nseq_s, tn)

    zeros_state = jnp.zeros((batch, N_HEADS, HEAD_DIM, HEAD_DIM), F32)
    ob_p, st_p = _hgrn(qb, kb, vb, lf, zeros_state, 0, batch, seq, 64, 4)
    ob_s, st_s = _hgrn(qb, kb, vb, lf, jnp.swapaxes(state_hgrn[l].astype(F32), -1, -2),
                       n_p, nseq_s, tn, tn, 1)

    wrh, wrl, br = _router_weights(w_group[0], b_group[0], w_expert[0], b_expert[0])
    h1, xn1, route1, cnt1, code1 = _out_ab(x_p, x_s, oa_p, oa_s, ob_p, ob_s, gb, row1(hgrn_out_norm[l]),
                                    head_avg, w_out_ab[l].astype(BF16), row1(norm_ffn[0]),
                                    wrh, wrl, br, tstrict)
    h2, u, bg = _moe(xn1, h1, route1, code1, cnt1, 0, w1, w3, w2,
                     proj=(row1(norm_mix[1]), w_in_c[0].astype(BF16)))

    wrh, wrl, br = _router_weights(w_group[1], b_group[1], w_expert[1], b_expert[1])
    h3, xn2, route2, cnt2, code2 = _conv_out(u, bg, h2, jnp.zeros((batch, 2, d), F32),
                                      state_conv[0].astype(F32), conv_w[0].astype(F32),
                                      w_out_c[0].astype(BF16), row1(norm_ffn[1]), wrh, wrl, br,
                                      tstrict, seq, tn)
    out_p, out_s = _moe(xn2, h3, route2, code2, cnt2, 1, w1, w3, w2, n_p=n_p)

    y_prompt = out_p.reshape(batch, seq, d)
    y_sample = out_s.reshape(nseq_s, tn, d)
    n_kp = batch * keep
    heads = lambda a, b_, t_: a.reshape(b_, t_, N_HEADS, HEAD_DIM)
    nk_p = heads(kf[:n_kp], batch, keep)[None]
    nv_p = heads(vf[:n_kp], batch, keep)[None]
    kf_s = heads(kf[n_kp:n_kp + n_s], nseq_s, tn)
    vf_s = heads(vf[n_kp:n_kp + n_s], nseq_s, tn)
    nk_s = jnp.concatenate([cache_a_k[l].astype(F32), kf_s], axis=1)[:, -la:][None]
    nv_s = jnp.concatenate([cache_a_v[l].astype(F32), vf_s], axis=1)[:, -la:][None]
    nh_p = jnp.swapaxes(st_p, -1, -2)[None]
    nh_s = jnp.swapaxes(st_s, -1, -2)[None]
    nc_p = jnp.stack([u[(b + 1) * seq - 2:(b + 1) * seq] for b in range(batch)])[None]
    u_s = u[n_p:].reshape(nseq_s, tn, d)
    nc_s = jnp.concatenate([state_conv[0].astype(F32), u_s], axis=1)[:, -2:][None]
    return (y_prompt, y_sample, nk_p, nv_p, nk_s, nv_s, nh_p, nh_s, nc_p, nc_s)
```

```python
import functools

import jax
import jax.numpy as jnp
from jax import lax
from jax.experimental import pallas as pl
from jax.experimental.pallas import tpu as pltpu

F32 = jnp.float32
BF16 = jnp.bfloat16
I32 = jnp.int32

D_MODEL = 1024
CHUNK = 64
BAND_CHUNKS = 8
HEAD_DIM = 64
N_HEADS = 8
D_HALF = N_HEADS * HEAD_DIM
REL_CLIP = 128
HGRN_BLOCK = 16
N_GROUPS = 4
EXPERTS_PER_GROUP = 8
N_EXPERTS = N_GROUPS * EXPERTS_PER_GROUP
D_EXPERT = 256
RMS_EPS = 1e-6
NEG_INF = -1e30

LANES = 128
ROW_TILE = D_MODEL // LANES
TOKEN_TILE = 256
PROJ_TILE = 512
ROUTE_TILE = 512
ATTN_TILE = 256
ATTN_WINDOW = ATTN_TILE + BAND_CHUNKS * CHUNK
ATTN_UNIT = 2 * CHUNK
ATTN_SPAN = ATTN_UNIT + BAND_CHUNKS * CHUNK
HGRN_UNROLL = 4
ATTN_HEAD_GROUP = 2
MOE_TILE = 256
RANK_BITS = 16
RANK_SPAN = 1 << RANK_BITS
GATHER_UNROLL = 8
VMEM_LIMIT = 48 * 1024 * 1024


def _cparams(*sem):
    return pltpu.CompilerParams(dimension_semantics=sem, vmem_limit_bytes=VMEM_LIMIT)


def _const_spec(shape):
    nd = len(shape)
    return pl.BlockSpec(shape, lambda *_: (0,) * nd)


def _store_row_tiles(ref, val):
    rows = val.shape[0]
    for c in range(ROW_TILE):
        ref[pl.ds(c, rows, stride=ROW_TILE), :] = val[:, c * LANES:(c + 1) * LANES]


def _load_row_tiles(ref, rows):
    return jnp.concatenate([ref[pl.ds(c, rows, stride=ROW_TILE), :] for c in range(ROW_TILE)], axis=1)


def _rms(x, gain):
    ms = jnp.mean(x * x, axis=-1, keepdims=True)
    return (x * lax.rsqrt(ms + RMS_EPS)) * gain


def _split2(x):
    hi = x.astype(BF16)
    lo = (x - hi.astype(F32)).astype(BF16)
    return hi, lo


def _split3(x):
    p0 = x.astype(BF16)
    r = x - p0.astype(F32)
    p1 = r.astype(BF16)
    p2 = (r - p1.astype(F32)).astype(BF16)
    return p0, p1, p2


def _dot(a, b):
    return jnp.dot(a, b, preferred_element_type=F32)


def _dot_nt(a, b):
    return lax.dot_general(a, b, (((1,), (1,)), ((), ())), preferred_element_type=F32)


def _dot_tn(a, b):
    return lax.dot_general(a, b, (((0,), (0,)), ((), ())), preferred_element_type=F32)


def _head_mean_sq(v, bd):
    hi, lo = _split2(v * v)
    return _dot(hi, bd) + _dot(lo, bd)


def _group_specs(width, tiles_p, tm=TOKEN_TILE):
    return [pl.BlockSpec((tm, width), lambda i: (jnp.minimum(i, tiles_p - 1), 0)),
            pl.BlockSpec((tm, width), lambda i: (jnp.maximum(i - tiles_p, 0), 0))]


def _group_pick(p_ref, s_ref, tiles_p):
    return jnp.where(pl.program_id(0) < tiles_p, p_ref[...], s_ref[...])


def _proj_ab_body(xp_ref, xs_ref, g_ref, w_ref, qg_ref, kg_ref, lb_ref, bd_ref,
                  q_o, kf_o, vf_o, k16_o, v16_o, lf_o, kb_o, vb_o, qb_o, gb_o, *, tiles_p):
    xb = _rms(_group_pick(xp_ref, xs_ref, tiles_p), g_ref[...]).astype(BF16)
    bd = bd_ref[...]

    def seg(j):
        return _dot(xb, w_ref[:, j * D_HALF:(j + 1) * D_HALF])

    qa = seg(0)
    qn = qa * lax.rsqrt(_head_mean_sq(qa, bd) + RMS_EPS) * qg_ref[...]
    q_o[...] = (qn * (HEAD_DIM ** -0.5)).astype(BF16)
    ka = seg(1)
    kn = ka * lax.rsqrt(_head_mean_sq(ka, bd) + RMS_EPS) * kg_ref[...]
    kf_o[...] = kn
    k16_o[...] = kn.astype(BF16)
    va = seg(2)
    vf_o[...] = va
    v16_o[...] = va.astype(BF16)
    lb = lb_ref[...]
    f = lb + (1.0 - lb) * jax.nn.sigmoid(seg(3))
    lf_o[...] = jnp.log(f)
    kb_o[...] = 1.0 - f
    vb_o[...] = seg(4)
    qb_o[...] = jax.nn.silu(seg(5))
    gb_o[...] = seg(6)


def _proj_ab(x_p, x_s, gain, w16, qg, kg, lb, bd, seq, keep):
    tm = PROJ_TILE
    n_p, n_s = x_p.shape[0], x_s.shape[0]
    n = n_p + n_s
    tiles_p, tiles_seq, tiles_keep = n_p // tm, seq // tm, keep // tm
    kept_tiles = (n_p // seq) * tiles_keep + n_s // tm

    def keep_map(i):
        b, j = i // tiles_seq, i % tiles_seq
        prompt_slot = b * tiles_keep + jnp.maximum(j - (tiles_seq - tiles_keep), 0)
        return (jnp.where(i >= tiles_p, kept_tiles - n_s // tm + (i - tiles_p), prompt_slot), 0)

    row = pl.BlockSpec((tm, D_HALF), lambda i: (i, 0))
    kept = pl.BlockSpec((tm, D_HALF), keep_map)
    full = lambda dt: jax.ShapeDtypeStruct((n, D_HALF), dt)
    kept_shape = jax.ShapeDtypeStruct((kept_tiles * tm, D_HALF), F32)
    return pl.pallas_call(
        functools.partial(_proj_ab_body, tiles_p=tiles_p),
        out_shape=[full(BF16), kept_shape, kept_shape, full(BF16), full(BF16),
                   full(F32), full(F32), full(F32), full(F32), full(F32)],
        grid=(n // tm,),
        in_specs=_group_specs(D_MODEL, tiles_p, tm) + [
            _const_spec((1, D_MODEL)), _const_spec(w16.shape),
            _const_spec((1, D_HALF)), _const_spec((1, D_HALF)), _const_spec((1, D_HALF)),
            _const_spec((D_HALF, D_HALF))],
        out_specs=[row, kept, kept] + [row] * 7,
        compiler_params=_cparams("arbitrary"),
        name="proj_ab",
    )(x_p, x_s, gain, w16, qg, kg, lb, bd)


def _attn_prompt_body(q_ref, k0, k1, k2, v0, v1, v2, bias_ref, o_ref):
    i = pl.program_id(1)
    lim = jnp.maximum(2 - i, 0) * ATTN_TILE
    col = lax.broadcasted_iota(I32, (ATTN_UNIT, ATTN_SPAN), 1)
    cut = ATTN_SPAN - 2 * ATTN_TILE

    def span(parts, start):
        if start == 0:
            return (parts[0], parts[1], parts[2][:cut])
        return (parts[0][start:], parts[1], parts[2])

    def head_group(h0):
        units = []
        for h in range(h0, h0 + ATTN_HEAD_GROUP):
            hs = slice(h * HEAD_DIM, (h + 1) * HEAD_DIM)
            kh = (k0[:, hs], k1[:, hs], k2[:, hs])
            vh = (v0[:, hs], v1[:, hs], v2[:, hs])
            for r in range(ATTN_TILE // ATTN_UNIT):
                start = r * ATTN_UNIT
                units.append((h, hs, slice(r * ATTN_UNIT, (r + 1) * ATTN_UNIT), start,
                              span(kh, start), span(vh, start)))
        s = [jnp.concatenate([_dot_nt(q_ref[rows, hs], kp) for kp in ks], axis=1)
             for (_, hs, rows, _, ks, _) in units]
        s = [jnp.where(col + start < lim, NEG_INF,
                       s_u + bias_ref[h, rows, start:start + ATTN_SPAN])
             for s_u, (h, _, rows, start, _, _) in zip(s, units)]
        m = [jnp.max(s_u, axis=-1, keepdims=True) for s_u in s]
        p = [jnp.exp(s_u - m_u) for s_u, m_u in zip(s, m)]
        l = [jnp.sum(p_u, axis=-1, keepdims=True) for p_u in p]
        outs = []
        for p_u, (_, _, _, _, _, vs) in zip(p, units):
            p16 = p_u.astype(BF16)
            o, at = 0.0, 0
            for vp in vs:
                o = o + _dot(p16[:, at:at + vp.shape[0]], vp)
                at += vp.shape[0]
            outs.append(o)
        for o_u, l_u, (_, hs, rows, _, _, _) in zip(outs, l, units):
            o_ref[rows, hs] = (o_u / l_u).astype(BF16)

    for h0 in range(0, N_HEADS, ATTN_HEAD_GROUP):
        head_group(h0)


def _attn_prompt(q16, k16, v16, bias_full, batch, seq):
    tiles = seq // ATTN_TILE
    qspec = pl.BlockSpec((ATTN_TILE, D_HALF), lambda b, i: (b * tiles + i, 0))

    def kv(back):
        return pl.BlockSpec((ATTN_TILE, D_HALF),
                            lambda b, i: (b * tiles + jnp.maximum(i - back, 0), 0))

    return pl.pallas_call(
        _attn_prompt_body,
        out_shape=jax.ShapeDtypeStruct((batch * seq, D_HALF), BF16),
        grid=(batch, tiles),
        in_specs=[qspec, kv(2), kv(1), kv(0), kv(2), kv(1), kv(0),
                  _const_spec(bias_full.shape)],
        out_specs=qspec,
        compiler_params=_cparams("parallel", "parallel"),
        name="attn_prompt",
    )(q16, k16, k16, k16, v16, v16, v16, bias_full)


def _attn_sample_body(q_ref, kc_ref, vc_ref, kn_ref, vn_ref, bc_ref, bn_ref, o_ref):
    la = kc_ref.shape[1] // N_HEADS
    for h in range(N_HEADS):
        hs = slice(h * HEAD_DIM, (h + 1) * HEAD_DIM)
        kc = kc_ref[0, pl.ds(h, la, stride=N_HEADS), :].astype(BF16)
        vc = vc_ref[0, pl.ds(h, la, stride=N_HEADS), :].astype(BF16)
        qh = q_ref[:, hs]
        sc = _dot_nt(qh, kc) + bc_ref[h]
        sn = _dot_nt(qh, kn_ref[:, hs]) + bn_ref[h]
        m = jnp.maximum(jnp.max(sc, axis=-1, keepdims=True), jnp.max(sn, axis=-1, keepdims=True))
        pc = jnp.exp(sc - m)
        pn = jnp.exp(sn - m)
        l = jnp.sum(pc, axis=-1, keepdims=True) + jnp.sum(pn, axis=-1, keepdims=True)
        o = _dot(pc.astype(BF16), vc) + _dot(pn.astype(BF16), vn_ref[:, hs])
        o_ref[:, hs] = (o / l).astype(BF16)


def _attn_sample(q16, k16, v16, cache_k, cache_v, bias_c, bias_n, row0, nseq, tn):
    blk0 = row0 // tn
    la8 = cache_k.shape[1]
    new = pl.BlockSpec((tn, D_HALF), lambda b: (blk0 + b, 0))
    cache = pl.BlockSpec((1, la8, HEAD_DIM), lambda b: (b, 0, 0))
    return pl.pallas_call(
        _attn_sample_body,
        out_shape=jax.ShapeDtypeStruct((nseq * tn, D_HALF), BF16),
        grid=(nseq,),
        in_specs=[new, cache, cache, new, new, _const_spec(bias_c.shape), _const_spec(bias_n.shape)],
        out_specs=pl.BlockSpec((tn, D_HALF), lambda b: (b, 0)),
        compiler_params=_cparams("parallel"),
        name="attn_sample",
    )(q16, cache_k, cache_v, k16, v16, bias_c, bias_n)


def _hgrn_body(q_ref, k_ref, v_ref, lf_ref, s0_ref, tcat_ref, o_ref, sout_ref, st_scr,
               *, chunk, nchunks):
    nblk = chunk // HGRN_BLOCK

    @pl.when(pl.program_id(1) == 0)
    def _():
        st_scr[...] = s0_ref[0]

    row = lax.broadcasted_iota(I32, (chunk, D_HALF), 0)
    r2 = lax.broadcasted_iota(I32, (chunk, chunk), 0)
    c2 = lax.broadcasted_iota(I32, (chunk, chunk), 1)
    same_blk_causal = jnp.logical_and(r2 // HGRN_BLOCK == c2 // HGRN_BLOCK, c2 <= r2)

    def one_chunk(c, carry):
        sl = pl.ds(pl.multiple_of(c * chunk, chunk), chunk)
        q = q_ref[sl, :]
        k = k_ref[sl, :]
        v16 = v_ref[sl, :].astype(BF16)
        l0, l1, l2 = _split3(lf_ref[sl, :])
        tcat = tcat_ref[...]
        both = _dot(tcat, l0) + _dot(tcat, l1) + _dot(tcat, l2)
        b_in = both[:chunk]
        b_ch = both[chunk:]
        ld = (q * jnp.exp(b_in)).astype(BF16)
        rd = (k * jnp.exp(-b_in)).astype(BF16)
        lj, rj = [], []
        for j in range(nblk - 1):
            e_j = b_ch[(j + 1) * HGRN_BLOCK - 1:(j + 1) * HGRN_BLOCK, :]
            later = row >= (j + 1) * HGRN_BLOCK
            inside = jnp.logical_and(row >= j * HGRN_BLOCK, row < (j + 1) * HGRN_BLOCK)
            lj.append(jnp.where(later, q * jnp.exp(jnp.minimum(b_ch - e_j, 0.0)), 0.0).astype(BF16))
            rj.append(jnp.where(inside, k * jnp.exp(jnp.minimum(e_j - b_ch, 0.0)), 0.0).astype(BF16))
        e_end = b_ch[chunk - 1:chunk, :]
        qc = (q * jnp.exp(b_ch)).astype(BF16)
        kc = (k * jnp.exp(e_end - b_ch)).astype(BF16)
        dec = jnp.exp(e_end)
        heads = [slice(h * HEAD_DIM, (h + 1) * HEAD_DIM) for h in range(N_HEADS)]
        sts = [st_scr[h] for h in range(N_HEADS)]
        diag = [_dot_nt(ld[:, hs], rd[:, hs]) for hs in heads]
        off = []
        for hs in heads:
            acc = None
            for j in range(nblk - 1):
                term = _dot_nt(lj[j][:, hs], rj[j][:, hs])
                acc = term if acc is None else acc + term
            off.append(acc)
        inter = [_dot_nt(qc[:, hs], st.astype(BF16)) for hs, st in zip(heads, sts)]
        upd = [_dot_tn(v16[:, hs], kc[:, hs]) for hs in heads]
        sc16 = [(jnp.where(same_blk_causal, d, 0.0) + o_).astype(BF16) for d, o_ in zip(diag, off)]
        intra = [_dot(s_, v16[:, hs]) for s_, hs in zip(sc16, heads)]
        o_ref[sl, :] = jnp.concatenate([a + b for a, b in zip(intra, inter)], axis=1)
        for h, hs in enumerate(heads):
            st_scr[h] = sts[h] * dec[:, hs] + upd[h]
        return carry

    lax.fori_loop(0, nchunks, one_chunk, 0, unroll=min(nchunks, HGRN_UNROLL))
    sout_ref[0] = st_scr[...]


def _hgrn(qb, kb, vb, lf, s0t, row0, nseq, seq, chunk, nchunks):
    tt = chunk * nchunks
    steps = seq // tt
    blk0 = row0 // tt
    t = jnp.arange(chunk)
    lower = t[None, :] <= t[:, None]
    same_blk = (t[None, :] // HGRN_BLOCK) == (t[:, None] // HGRN_BLOCK)
    tcat = jnp.concatenate([jnp.logical_and(lower, same_blk), lower], axis=0).astype(BF16)
    tok = pl.BlockSpec((tt, D_HALF), lambda b, j: (blk0 + b * steps + j, 0))
    state = pl.BlockSpec((1, N_HEADS, HEAD_DIM, HEAD_DIM), lambda b, j: (b, 0, 0, 0))
    return pl.pallas_call(
        functools.partial(_hgrn_body, chunk=chunk, nchunks=nchunks),
        out_shape=[jax.ShapeDtypeStruct((nseq * seq, D_HALF), F32),
                   jax.ShapeDtypeStruct(s0t.shape, F32)],
        grid=(nseq, steps),
        in_specs=[tok, tok, tok, tok, state, _const_spec((2 * chunk, chunk))],
        out_specs=[pl.BlockSpec((tt, D_HALF), lambda b, j: (b * steps + j, 0)), state],
        scratch_shapes=[pltpu.VMEM((N_HEADS, HEAD_DIM, HEAD_DIM), F32)],
        compiler_params=_cparams("parallel", "arbitrary"),
        name="hgrn",
    )(qb, kb, vb, lf, s0t, tcat)


def _route(xn, wr_hi, wr_lo, br, tstrict, cnt_scr, route_o, cnt_o, code_o):
    rows = xn.shape[0]
    x_hi, x_lo = _split2(xn)
    logits = _dot(x_hi, wr_hi) + _dot(x_lo, wr_hi) + _dot(x_hi, wr_lo) + br
    lane = lax.broadcasted_iota(I32, (rows, LANES), 1)
    lane_f = lane.astype(F32)

    def first_argmax(vals):
        m = jnp.max(vals, axis=-1, keepdims=True)
        idx = jnp.min(jnp.where(vals == m, lane_f, float(LANES)), axis=-1, keepdims=True)
        return m, idx.astype(I32)

    is_grp = jnp.logical_and(lane >= N_EXPERTS, lane < N_EXPERTS + N_GROUPS)
    gl = jnp.where(is_grp, logits, -jnp.inf)
    gmax, gidx = first_argmax(gl)
    p_grp = 1.0 / jnp.sum(jnp.exp(gl - gmax), axis=-1, keepdims=True)
    grp = gidx - N_EXPERTS
    in_grp = jnp.logical_and(lane < N_EXPERTS, lane // EXPERTS_PER_GROUP == grp)
    el = jnp.where(in_grp, logits, -jnp.inf)
    v1, e1 = first_argmax(el)
    v2, e2 = first_argmax(jnp.where(lane == e1, -jnp.inf, el))
    t2 = jnp.exp(v2 - v1)
    den = 1.0 + t2
    g1 = (1.0 / den) * p_grp
    g2 = (t2 / den) * p_grp

    oh1 = lane == e1
    oh2 = lane == e2
    oh = jnp.where(jnp.logical_or(oh1, oh2), 1.0, 0.0)
    before = _dot(tstrict, oh.astype(BF16)) + cnt_scr[...]
    rank1 = jnp.sum(jnp.where(oh1, before, 0.0), axis=-1, keepdims=True)
    rank2 = jnp.sum(jnp.where(oh2, before, 0.0), axis=-1, keepdims=True)
    cnt_scr[...] = cnt_scr[...] + jnp.sum(oh, axis=0, keepdims=True)
    cnt_o[...] = cnt_scr[...]

    e1f, e2f = e1.astype(F32), e2.astype(F32)
    fields = (e1f, e2f, g1, g2, rank1, rank2, e1f * RANK_SPAN + rank1, e2f * RANK_SPAN + rank2)
    slab = jnp.zeros((rows, LANES), F32)
    for idx, val in enumerate(fields):
        slab = jnp.where(lane == idx, val, slab)
    route_o[...] = slab
    code_o[...] = slab.T[6:8, :]


def _out_ab_body(hp_ref, hs_ref, oap_ref, oas_ref, obp_ref, obs_ref, gb_ref, og_ref, bd_ref, w_ref,
                 gf_ref, wrh_ref, wrl_ref, br_ref, ts_ref, h_o, xn_o, route_o, cnt_o, code_o, cnt_scr,
                 *, tiles_p):
    @pl.when(pl.program_id(0) == 0)
    def _():
        cnt_scr[...] = jnp.zeros_like(cnt_scr)

    ob = _group_pick(obp_ref, obs_ref, tiles_p)
    obn = ob * lax.rsqrt(_head_mean_sq(ob, bd_ref[...]) + RMS_EPS) * og_ref[...]
    obg = (obn * jax.nn.silu(gb_ref[...])).astype(BF16)
    oa = _group_pick(oap_ref, oas_ref, tiles_p)
    mix = _dot(oa, w_ref[0:D_HALF, :]) + _dot(obg, w_ref[D_HALF:, :])
    h1 = _group_pick(hp_ref, hs_ref, tiles_p) + mix
    h_o[...] = h1
    xn = _rms(h1, gf_ref[...])
    _store_row_tiles(xn_o, xn)
    _route(xn, wrh_ref[...], wrl_ref[...], br_ref[...], ts_ref[...], cnt_scr, route_o, cnt_o, code_o)


def _out_ab(h_p, h_s, oa_p, oa_s, ob_p, ob_s, gb, og, bd, w16, gf, wrh, wrl, br, tstrict):
    tm = ROUTE_TILE
    n = h_p.shape[0] + h_s.shape[0]
    tiles_p = h_p.shape[0] // tm
    row = lambda w: pl.BlockSpec((tm, w), lambda i: (i, 0))
    return pl.pallas_call(
        functools.partial(_out_ab_body, tiles_p=tiles_p),
        out_shape=[jax.ShapeDtypeStruct((n, D_MODEL), F32), jax.ShapeDtypeStruct((n * ROW_TILE, LANES), F32),
                   jax.ShapeDtypeStruct((n, LANES), F32), jax.ShapeDtypeStruct((1, LANES), F32),
                   jax.ShapeDtypeStruct((2, n), F32)],
        grid=(n // tm,),
        in_specs=_group_specs(D_MODEL, tiles_p, tm) + _group_specs(D_HALF, tiles_p, tm)
        + _group_specs(D_HALF, tiles_p, tm) + [
            row(D_HALF), _const_spec((1, D_HALF)),
            _const_spec((D_HALF, D_HALF)), _const_spec((D_MODEL, D_MODEL)),
            _const_spec((1, D_MODEL)), _const_spec((D_MODEL, LANES)),
            _const_spec((D_MODEL, LANES)), _const_spec((1, LANES)), _const_spec((tm, tm))],
        out_specs=[row(D_MODEL), pl.BlockSpec((tm * ROW_TILE, LANES), lambda i: (i, 0)), row(LANES),
                   _const_spec((1, LANES)), pl.BlockSpec((2, tm), lambda i: (0, i))],
        scratch_shapes=[pltpu.VMEM((1, LANES), F32)],
        compiler_params=_cparams("arbitrary"),
        name="out_ab",
    )(h_p, h_s, oa_p, oa_s, ob_p, ob_s, gb, og, bd, w16, gf, wrh, wrl, br, tstrict)


def _row_copy(src, src_row, dst, dst_row, sem):
    s0 = pl.multiple_of(src_row * ROW_TILE, ROW_TILE)
    d0 = pl.multiple_of(dst_row * ROW_TILE, ROW_TILE)
    return pltpu.make_async_copy(src.at[pl.ds(s0, ROW_TILE)], dst.at[pl.ds(d0, ROW_TILE)], sem)


def _sorted_row(code_ref, offs_ref, a):
    code = code_ref[a]
    return offs_ref[code >> RANK_BITS] + (code & (RANK_SPAN - 1))


def _collect_body(code_ref, offs_ref, ys_hbm, h_ref, r_ref, *rest, tile, n_tok, tiles_p, proj):
    if proj:
        g_ref, w_ref = rest[:2]
        rest = rest[2:]
    outs, (buf, sems) = rest[:-2], rest[-2:]
    i = pl.program_id(0)
    nt = pl.num_programs(0)

    def issue(t):
        slot = t % 2
        base = t * tile

        def one(r, carry):
            tok = base + r
            for k in range(2):
                row = _sorted_row(code_ref, offs_ref, k * n_tok + tok)
                _row_copy(ys_hbm, row, buf.at[slot].at[k], r, sems.at[slot]).start(priority=k)
            return carry

        lax.fori_loop(0, tile, one, 0, unroll=GATHER_UNROLL)

    @pl.when(i == 0)
    def _():
        issue(0)

    @pl.when(i + 1 < nt)
    def _():
        issue(i + 1)

    slot = i % 2
    for _ in range(2 * tile):
        _row_copy(ys_hbm, 0, buf.at[slot].at[0], 0, sems.at[slot]).wait()
    route = r_ref[...]
    ya = _load_row_tiles(buf.at[slot].at[0], tile)
    yb = _load_row_tiles(buf.at[slot].at[1], tile)
    out = h_ref[...] + (route[:, 2:3] * ya + route[:, 3:4] * yb)
    if proj:
        h_o, u_o, bg_o = outs
        h_o[...] = out
        xb = _rms(out, g_ref[...]).astype(BF16)
        bg_o[...] = _dot(xb, w_ref[:, 0:D_MODEL])
        u_o[...] = _dot(xb, w_ref[:, D_MODEL:2 * D_MODEL]) * _dot(xb, w_ref[:, 2 * D_MODEL:])
    else:
        @pl.when(i < tiles_p)
        def _():
            outs[0][...] = out

        @pl.when(i >= tiles_p)
        def _():
            outs[1][...] = out


def _collect(code, offs, ys, h, route, *, proj=None, n_p=None):
    tm = TOKEN_TILE
    n = h.shape[0]
    row = lambda w: pl.BlockSpec((tm, w), lambda i, code, offs: (i, 0))
    const = lambda shape: pl.BlockSpec(shape, lambda i, code, offs: (0,) * len(shape))
    if proj is not None:
        tiles_p = None
        extra_in, extra_specs = list(proj), [const(proj[0].shape), const(proj[1].shape)]
        out_shape = [jax.ShapeDtypeStruct((n, D_MODEL), F32)] * 3
        out_specs = [row(D_MODEL)] * 3
    else:
        tiles_p = n_p // tm
        extra_in, extra_specs = [], []
        out_shape = [jax.ShapeDtypeStruct((n_p, D_MODEL), F32),
                     jax.ShapeDtypeStruct((n - n_p, D_MODEL), F32)]
        out_specs = [pl.BlockSpec((tm, D_MODEL), lambda i, code, offs: (jnp.minimum(i, tiles_p - 1), 0)),
                     pl.BlockSpec((tm, D_MODEL), lambda i, code, offs: (jnp.maximum(i - tiles_p, 0), 0))]
    return pl.pallas_call(
        functools.partial(_collect_body, tile=tm, n_tok=n, tiles_p=tiles_p, proj=proj is not None),
        out_shape=out_shape,
        grid_spec=pltpu.PrefetchScalarGridSpec(
            num_scalar_prefetch=2,
            grid=(n // tm,),
            in_specs=[pl.BlockSpec(memory_space=pl.ANY), row(D_MODEL), row(LANES)] + extra_specs,
            out_specs=out_specs,
            scratch_shapes=[pltpu.VMEM((2, 2, tm * ROW_TILE, LANES), F32),
                            pltpu.SemaphoreType.DMA((2,))],
        ),
        compiler_params=pltpu.CompilerParams(dimension_semantics=("arbitrary",),
                                             vmem_limit_bytes=VMEM_LIMIT),
        name="moe_collect",
    )(code, offs, ys, h, route, *extra_in)


def _scatter_rows_body(code_ref, offs_ref, pad0_ref, padn_ref, nt_ref, x_ref, xs_hbm,
                       stage, zero_scr, sems, pad_sem, *, tile, n_tok, out_tiles):
    i = pl.program_id(0)
    last = pl.num_programs(0) - 1
    base = i * tile
    slot = i % 2

    def drain(s):
        for _ in range(2 * tile):
            _row_copy(stage.at[s], 0, xs_hbm, 0, sems.at[s]).wait()

    @pl.when(i >= 2)
    def _():
        drain(slot)

    stage[slot] = x_ref[...]

    def issue(r, carry):
        tok = base + r
        src = stage.at[slot]
        for k in range(2):
            row = _sorted_row(code_ref, offs_ref, k * n_tok + tok)
            _row_copy(src, r, xs_hbm, row, sems.at[slot]).start(priority=k)
        return carry

    lax.fori_loop(0, tile, issue, 0, unroll=GATHER_UNROLL)

    @pl.when(i == last)
    def _():
        zero_scr[...] = jnp.zeros_like(zero_scr)

        def fill(lo, count):
            def one(r, carry):
                _row_copy(zero_scr, 0, xs_hbm, lo + r, pad_sem).start()
                return carry

            def done(r, carry):
                _row_copy(zero_scr, 0, xs_hbm, 0, pad_sem).wait()
                return carry

            lax.fori_loop(0, count, one, 0)
            lax.fori_loop(0, count, done, 0)

        for e in range(N_EXPERTS):
            fill(pad0_ref[e], padn_ref[e])

        def tail_copy(t):
            rows = MOE_TILE * ROW_TILE
            return pltpu.make_async_copy(
                zero_scr, xs_hbm.at[pl.ds(pl.multiple_of(t * rows, rows), rows)], pad_sem)

        def tail_start(t, carry):
            tail_copy(t).start()
            return carry

        def tail_done(t, carry):
            tail_copy(t).wait()
            return carry

        lax.fori_loop(nt_ref[0], out_tiles, tail_start, 0)
        lax.fori_loop(nt_ref[0], out_tiles, tail_done, 0)
        drain(slot)

    @pl.when(jnp.logical_and(i == last, i >= 1))
    def _():
        drain(1 - slot)


def _scatter_rows(code, offs, pad0, padn, n_tiles, xn, n_rows):
    tm = TOKEN_TILE
    n = xn.shape[0] // ROW_TILE
    return pl.pallas_call(
        functools.partial(_scatter_rows_body, tile=tm, n_tok=n, out_tiles=n_rows // MOE_TILE),
        out_shape=jax.ShapeDtypeStruct((n_rows * ROW_TILE, LANES), F32),
        grid_spec=pltpu.PrefetchScalarGridSpec(
            num_scalar_prefetch=5,
            grid=(n // tm,),
            in_specs=[pl.BlockSpec((tm * ROW_TILE, LANES), lambda i, *_: (i, 0))],
            out_specs=pl.BlockSpec(memory_space=pl.ANY),
            scratch_shapes=[pltpu.VMEM((2, tm * ROW_TILE, LANES), F32),
                            pltpu.VMEM((MOE_TILE * ROW_TILE, LANES), F32),
                            pltpu.SemaphoreType.DMA((2,)), pltpu.SemaphoreType.DMA(())],
        ),
        compiler_params=pltpu.CompilerParams(dimension_semantics=("arbitrary",)),
        name="moe_scatter",
    )(code, offs, pad0, padn, n_tiles, xn)


def _experts_body(te_ref, nt_ref, x_ref, w1_ref, w3_ref, w2_ref, y_ref, w1_s, w3_s, w2_s):
    i = pl.program_id(0)
    prev = te_ref[jnp.maximum(i - 1, 0)]
    fresh = jnp.logical_or(i == 0, te_ref[i] != prev)

    @pl.when(fresh)
    def _():
        w1_s[...] = w1_ref[0].astype(BF16)
        w3_s[...] = w3_ref[0].astype(BF16)
        w2_s[...] = w2_ref[0].astype(BF16)

    @pl.when(i < nt_ref[0])
    def _():
        x = _load_row_tiles(x_ref, MOE_TILE).astype(BF16)
        a = _dot(x, w1_s[...])
        b = _dot(x, w3_s[...])
        _store_row_tiles(y_ref, _dot((jax.nn.silu(a) * b).astype(BF16), w2_s[...]))

    @pl.when(i >= nt_ref[0])
    def _():
        y_ref[...] = jnp.zeros_like(y_ref)


def _experts(tile_expert, n_tiles, xs, w1, w3, w2):
    rows = xs.shape[0] // ROW_TILE
    tm = MOE_TILE
    return pl.pallas_call(
        _experts_body,
        out_shape=jax.ShapeDtypeStruct((rows * ROW_TILE, LANES), F32),
        grid_spec=pltpu.PrefetchScalarGridSpec(
            num_scalar_prefetch=2,
            grid=(rows // tm,),
            in_specs=[pl.BlockSpec((tm * ROW_TILE, LANES),
                                   lambda i, te, nt: (jnp.minimum(i, nt[0] - 1), 0)),
                      pl.BlockSpec((1, D_MODEL, D_EXPERT), lambda i, te, nt: (te[i], 0, 0)),
                      pl.BlockSpec((1, D_MODEL, D_EXPERT), lambda i, te, nt: (te[i], 0, 0)),
                      pl.BlockSpec((1, D_EXPERT, D_MODEL), lambda i, te, nt: (te[i], 0, 0))],
            out_specs=pl.BlockSpec((tm * ROW_TILE, LANES), lambda i, te, nt: (i, 0)),
            scratch_shapes=[pltpu.VMEM((D_MODEL, D_EXPERT), BF16), pltpu.VMEM((D_MODEL, D_EXPERT), BF16),
                            pltpu.VMEM((D_EXPERT, D_MODEL), BF16)],
        ),
        compiler_params=_cparams("arbitrary"),
        name="moe_experts",
    )(tile_expert, n_tiles, xs, w1, w3, w2)


def _moe_plan(counts, n):
    counts = counts[0, :N_EXPERTS].astype(I32)
    padded = ((counts + MOE_TILE - 1) // MOE_TILE) * MOE_TILE
    ends = jnp.cumsum(padded)
    offs = ends - padded
    max_tiles = (2 * n + N_EXPERTS * (MOE_TILE - 1) + MOE_TILE - 1) // MOE_TILE
    tile_start = jnp.arange(max_tiles, dtype=I32) * MOE_TILE
    tile_expert = jnp.minimum(jnp.sum((tile_start[:, None] >= ends[None, :]).astype(I32), axis=1),
                              N_EXPERTS - 1)
    n_tiles = (ends[-1] // MOE_TILE).reshape(1)
    return offs, offs + counts, padded - counts, tile_expert, n_tiles, max_tiles * MOE_TILE


def _moe(xn, h, route, code, counts, layer, w1, w3, w2, **collect_mode):
    n = h.shape[0]
    code = code.astype(I32).reshape(2 * n)
    offs, pad0, padn, tile_expert, n_tiles, n_rows = _moe_plan(counts, n)
    xs = _scatter_rows(code, offs, pad0, padn, n_tiles, xn, n_rows)
    flat = lambda w: w.reshape((-1,) + w.shape[2:])
    ys = _experts(tile_expert + layer * N_EXPERTS, n_tiles, xs, flat(w1), flat(w3), flat(w2))
    return _collect(code, offs, ys, h, route, **collect_mode)


def _conv_out_body(u_ref, up_ref, stp_ref, sts_ref, bg_ref, h_ref, cw_ref, w_ref, gf_ref,
                   wrh_ref, wrl_ref, br_ref, ts_ref, h_o, xn_o, route_o, cnt_o, code_o, cnt_scr,
                   *, tile, tiles_p, seq_p, seq_s):
    i = pl.program_id(0)

    @pl.when(i == 0)
    def _():
        cnt_scr[...] = jnp.zeros_like(cnt_scr)

    u = u_ref[...]
    rowi = lax.broadcasted_iota(I32, (tile, D_MODEL), 0)
    is_p = i < tiles_p
    at_start = (i * tile) % seq_p == 0
    stp = stp_ref[0]
    prev = up_ref[...]
    m2_p = jnp.where(at_start, stp[0:1, :], prev[6:7, :])
    m1_p = jnp.where(at_start, stp[1:2, :], prev[7:8, :])
    per = tile // seq_s
    sts = sts_ref[...]
    m2_s = jnp.broadcast_to(sts[:, 0:1, :], (per, seq_s, D_MODEL)).reshape(tile, D_MODEL)
    m1_s = jnp.broadcast_to(sts[:, 1:2, :], (per, seq_s, D_MODEL)).reshape(tile, D_MODEL)
    m2 = jnp.where(is_p, m2_p, m2_s)
    m1 = jnp.where(is_p, m1_p, m1_s)
    pos = jnp.where(is_p, rowi, rowi % seq_s)
    u1 = jnp.where(pos == 0, m1, pltpu.roll(u, 1, axis=0))
    u2 = jnp.where(pos == 0, m2, jnp.where(pos == 1, m1, pltpu.roll(u, 2, axis=0)))
    cw = cw_ref[...]
    conv = u2 * cw[0:1, :] + u1 * cw[1:2, :] + u * cw[2:3, :]
    mix = _dot((bg_ref[...] * conv).astype(BF16), w_ref[...])
    h3 = h_ref[...] + mix
    h_o[...] = h3
    xn = _rms(h3, gf_ref[...])
    _store_row_tiles(xn_o, xn)
    _route(xn, wrh_ref[...], wrl_ref[...], br_ref[...], ts_ref[...], cnt_scr, route_o, cnt_o, code_o)


def _conv_out(u, bg, h, state_p, state_s, cw, w16, gf, wrh, wrl, br, tstrict, seq_p, seq_s):
    tm = ROUTE_TILE
    n = u.shape[0]
    tiles_p = state_p.shape[0] * seq_p // tm
    per = tm // seq_s
    row = lambda w: pl.BlockSpec((tm, w), lambda i: (i, 0))
    prev = pl.BlockSpec((8, D_MODEL), lambda i: (jnp.maximum(i * (tm // 8) - 1, 0), 0))
    stp_spec = pl.BlockSpec((1, 2, D_MODEL),
                            lambda i: (jnp.minimum(i, tiles_p - 1) * tm // seq_p, 0, 0))
    sts_spec = pl.BlockSpec((per, 2, D_MODEL), lambda i: (jnp.maximum(i - tiles_p, 0), 0, 0))
    return pl.pallas_call(
        functools.partial(_conv_out_body, tile=tm, tiles_p=tiles_p, seq_p=seq_p, seq_s=seq_s),
        out_shape=[jax.ShapeDtypeStruct((n, D_MODEL), F32),
                   jax.ShapeDtypeStruct((n * ROW_TILE, LANES), F32),
                   jax.ShapeDtypeStruct((n, LANES), F32), jax.ShapeDtypeStruct((1, LANES), F32),
                   jax.ShapeDtypeStruct((2, n), F32)],
        grid=(n // tm,),
        in_specs=[row(D_MODEL), prev, stp_spec, sts_spec, row(D_MODEL), row(D_MODEL),
                  _const_spec((3, D_MODEL)),
                  _const_spec((D_MODEL, D_MODEL)), _const_spec((1, D_MODEL)),
                  _const_spec((D_MODEL, LANES)), _const_spec((D_MODEL, LANES)),
                  _const_spec((1, LANES)), _const_spec((tm, tm))],
        out_specs=[row(D_MODEL), pl.BlockSpec((tm * ROW_TILE, LANES), lambda i: (i, 0)), row(LANES),
                   _const_spec((1, LANES)), pl.BlockSpec((2, tm), lambda i: (0, i))],
        scratch_shapes=[pltpu.VMEM((1, LANES), F32)],
        compiler_params=_cparams("arbitrary"),
        name="conv_out",
    )(u, u, state_p, state_s, bg, h, cw, w16, gf, wrh, wrl, br, tstrict)


def _rel_bias_toeplitz(rel_bias, rows, cols, lead):
    period = rows + cols
    k = jnp.arange(period)
    d = jnp.where(k < cols, k, k - period)
    idx = jnp.clip(lead - d, -REL_CLIP, REL_CLIP) + REL_CLIP
    v = jnp.take(rel_bias.astype(F32), idx, axis=1)
    heads = v.shape[0]
    skew = jnp.tile(v, (1, rows))[:, :rows * (period - 1)].reshape(heads, rows, period - 1)
    return skew[:, :, :cols]


def _bias_prompt(rel_bias):
    r = jnp.arange(ATTN_TILE)[:, None]
    c = jnp.arange(ATTN_WINDOW)[None, :]
    j = c // CHUNK - r // CHUNK
    band = jnp.logical_and(j >= 0, j <= BAND_CHUNKS)
    bias = _rel_bias_toeplitz(rel_bias, ATTN_TILE, ATTN_WINDOW, BAND_CHUNKS * CHUNK)
    return jnp.where(band[None], bias, NEG_INF)


def _bias_sample(rel_bias, la, tn):
    bias = _rel_bias_toeplitz(rel_bias, tn, la + tn, la)
    return bias[:, :, :la], bias[:, :, la:]


def _router_weights(w_group, b_group, w_expert, b_expert):
    pad = LANES - N_EXPERTS - N_GROUPS
    w = jnp.concatenate([w_expert, w_group, jnp.zeros((D_MODEL, pad), F32)], axis=1)
    b = jnp.concatenate([b_expert, b_group, jnp.zeros((pad,), F32)])[None, :].astype(F32)
    hi, lo = _split2(w.astype(F32))
    return hi, lo, b


def kernel(x_prompt, x_sample, cache_a_k, cache_a_v, state_hgrn, state_conv, norm_mix, norm_ffn,
           w_in_ab, w_out_ab, q_norm, k_norm, rel_bias, hgrn_lb_logits, hgrn_out_norm, w_in_c,
           conv_w, w_out_c, w_group, b_group, w_expert, b_expert, w1, w3, w2):
    batch, seq, d = x_prompt.shape
    nseq_s, tn, _ = x_sample.shape
    la = cache_a_k.shape[2]
    n_p = batch * seq
    n_s = nseq_s * tn
    n = n_p + n_s
    keep = min(BAND_CHUNKS * CHUNK, seq)

    x_p = x_prompt.reshape(n_p, d)
    x_s = x_sample.reshape(n_s, d)
    lb_all = jnp.cumsum(jax.nn.softmax(hgrn_lb_logits.astype(F32), axis=0), axis=0)
    head_avg = jnp.kron(jnp.eye(N_HEADS, dtype=F32),
                        jnp.full((HEAD_DIM, HEAD_DIM), 1.0 / HEAD_DIM, F32)).astype(BF16)
    t = jnp.arange(ROUTE_TILE)
    tstrict = (t[None, :] < t[:, None]).astype(BF16)
    row1 = lambda v: v.astype(F32).reshape(1, -1)
    tile8 = lambda v: jnp.tile(v.astype(F32), N_HEADS).reshape(1, -1)

    l = 0
    (q16, kf, vf, k16, v16, lf, kb, vb, qb, gb) = _proj_ab(
        x_p, x_s, row1(norm_mix[0]), w_in_ab[l].astype(BF16), tile8(q_norm[l]), tile8(k_norm[l]),
        row1(lb_all[l]), head_avg, seq, keep)

    oa_p = _attn_prompt(q16, k16, v16, _bias_prompt(rel_bias[l]), batch, seq)
    bias_c, bias_n = _bias_sample(rel_bias[l], la, tn)
    oa_s = _attn_sample(q16, k16, v16, cache_a_k[l].reshape(nseq_s, la * N_HEADS, HEAD_DIM),
                        cache_a_v[l].reshape(nseq_s, la * N_HEADS, HEAD_DIM), bias_c, bias_n,
                        n_p, nseq_s, tn)

    zeros_state = jnp.zeros((batch, N_HEADS, HEAD_DIM, HEAD_DIM), F32)
    ob_p, st_p = _hgrn(qb, kb, vb, lf, zeros_state, 0, batch, seq, 64, 4)
    ob_s, st_s = _hgrn(qb, kb, vb, lf, jnp.swapaxes(state_hgrn[l].astype(F32), -1, -2),
                       n_p, nseq_s, tn, tn, 1)

    wrh, wrl, br = _router_weights(w_group[0], b_group[0], w_expert[0], b_expert[0])
    h1, xn1, route1, cnt1, code1 = _out_ab(x_p, x_s, oa_p, oa_s, ob_p, ob_s, gb, row1(hgrn_out_norm[l]),
                                    head_avg, w_out_ab[l].astype(BF16), row1(norm_ffn[0]),
                                    wrh, wrl, br, tstrict)
    h2, u, bg = _moe(xn1, h1, route1, code1, cnt1, 0, w1, w3, w2,
                     proj=(row1(norm_mix[1]), w_in_c[0].astype(BF16)))

    wrh, wrl, br = _router_weights(w_group[1], b_group[1], w_expert[1], b_expert[1])
    h3, xn2, route2, cnt2, code2 = _conv_out(u, bg, h2, jnp.zeros((batch, 2, d), F32),
                                      state_conv[0].astype(F32), conv_w[0].astype(F32),
                                      w_out_c[0].astype(BF16), row1(norm_ffn[1]), wrh, wrl, br,
                                      tstrict, seq, tn)
    out_p, out_s = _moe(xn2, h3, route2, code2, cnt2, 1, w1, w3, w2, n_p=n_p)

    y_prompt = out_p.reshape(batch, seq, d)
    y_sample = out_s.reshape(nseq_s, tn, d)
    n_kp = batch * keep
    heads = lambda a, b_, t_: a.reshape(b_, t_, N_HEADS, HEAD_DIM)
    nk_p = heads(kf[:n_kp], batch, keep)[None]
    nv_p = heads(vf[:n_kp], batch, keep)[None]
    kf_s = heads(kf[n_kp:n_kp + n_s], nseq_s, tn)
    vf_s = heads(vf[n_kp:n_kp + n_s], nseq_s, tn)
    nk_s = jnp.concatenate([cache_a_k[l].astype(F32), kf_s], axis=1)[:, -la:][None]
    nv_s = jnp.concatenate([cache_a_v[l].astype(F32), vf_s], axis=1)[:, -la:][None]
    nh_p = jnp.swapaxes(st_p, -1, -2)[None]
    nh_s = jnp.swapaxes(st_s, -1, -2)[None]
    nc_p = jnp.stack([u[(b + 1) * seq - 2:(b + 1) * seq] for b in range(batch)])[None]
    u_s = u[n_p:].reshape(nseq_s, tn, d)
    nc_s = jnp.concatenate([state_conv[0].astype(F32), u_s], axis=1)[:, -2:][None]
    return (y_prompt, y_sample, nk_p, nv_p, nk_s, nv_s, nh_p, nh_s, nc_p, nc_s)
```

```python
import functools

import jax
import jax.numpy as jnp
from jax import lax
from jax.experimental import pallas as pl
from jax.experimental.pallas import tpu as pltpu

F32 = jnp.float32
BF16 = jnp.bfloat16
I32 = jnp.int32

D_MODEL = 1024
CHUNK = 64
BAND_CHUNKS = 8
HEAD_DIM = 64
N_HEADS = 8
D_HALF = N_HEADS * HEAD_DIM
REL_CLIP = 128
HGRN_BLOCK = 16
N_GROUPS = 4
EXPERTS_PER_GROUP = 8
N_EXPERTS = N_GROUPS * EXPERTS_PER_GROUP
D_EXPERT = 256
RMS_EPS = 1e-6
NEG_INF = -1e30

LANES = 128
ROW_TILE = D_MODEL // LANES
TOKEN_TILE = 256
PROJ_TILE = 512
ROUTE_TILE = 512
ATTN_TILE = 256
ATTN_WINDOW = ATTN_TILE + BAND_CHUNKS * CHUNK
ATTN_UNIT = 2 * CHUNK
ATTN_SPAN = ATTN_UNIT + BAND_CHUNKS * CHUNK
HGRN_UNROLL = 4
ATTN_HEAD_GROUP = 2
MOE_TILE = 256
RANK_BITS = 16
RANK_SPAN = 1 << RANK_BITS
GATHER_UNROLL = 8
VMEM_LIMIT = 48 * 1024 * 1024


def _cparams(*sem):
    return pltpu.CompilerParams(dimension_semantics=sem, vmem_limit_bytes=VMEM_LIMIT)


def _const_spec(shape):
    nd = len(shape)
    return pl.BlockSpec(shape, lambda *_: (0,) * nd)


def _store_row_tiles(ref, val):
    rows = val.shape[0]
    for c in range(ROW_TILE):
        ref[pl.ds(c, rows, stride=ROW_TILE), :] = val[:, c * LANES:(c + 1) * LANES]


def _load_row_tiles(ref, rows):
    return jnp.concatenate([ref[pl.ds(c, rows, stride=ROW_TILE), :] for c in range(ROW_TILE)], axis=1)


def _rms(x, gain):
    ms = jnp.mean(x * x, axis=-1, keepdims=True)
    return (x * lax.rsqrt(ms + RMS_EPS)) * gain


def _split2(x):
    hi = x.astype(BF16)
    lo = (x - hi.astype(F32)).astype(BF16)
    return hi, lo


def _split3(x):
    p0 = x.astype(BF16)
    r = x - p0.astype(F32)
    p1 = r.astype(BF16)
    p2 = (r - p1.astype(F32)).astype(BF16)
    return p0, p1, p2


def _dot(a, b):
    return jnp.dot(a, b, preferred_element_type=F32)


def _dot_nt(a, b):
    return lax.dot_general(a, b, (((1,), (1,)), ((), ())), preferred_element_type=F32)


def _dot_tn(a, b):
    return lax.dot_general(a, b, (((0,), (0,)), ((), ())), preferred_element_type=F32)


def _head_mean_sq(v, bd):
    hi, lo = _split2(v * v)
    return _dot(hi, bd) + _dot(lo, bd)


def _group_specs(width, tiles_p, tm=TOKEN_TILE):
    return [pl.BlockSpec((tm, width), lambda i: (jnp.minimum(i, tiles_p - 1), 0)),
            pl.BlockSpec((tm, width), lambda i: (jnp.maximum(i - tiles_p, 0), 0))]


def _group_pick(p_ref, s_ref, tiles_p):
    return jnp.where(pl.program_id(0) < tiles_p, p_ref[...], s_ref[...])


def _proj_ab_body(xp_ref, xs_ref, g_ref, w_ref, qg_ref, kg_ref, lb_ref, bd_ref,
                  q_o, kf_o, vf_o, k16_o, v16_o, lf_o, kb_o, vb_o, qb_o, gb_o, *, tiles_p):
    xb = _rms(_group_pick(xp_ref, xs_ref, tiles_p), g_ref[...]).astype(BF16)
    bd = bd_ref[...]

    def seg(j):
        return _dot(xb, w_ref[:, j * D_HALF:(j + 1) * D_HALF])

    qa = seg(0)
    qn = qa * lax.rsqrt(_head_mean_sq(qa, bd) + RMS_EPS) * qg_ref[...]
    q_o[...] = (qn * (HEAD_DIM ** -0.5)).astype(BF16)
    ka = seg(1)
    kn = ka * lax.rsqrt(_head_mean_sq(ka, bd) + RMS_EPS) * kg_ref[...]
    kf_o[...] = kn
    k16_o[...] = kn.astype(BF16)
    va = seg(2)
    vf_o[...] = va
    v16_o[...] = va.astype(BF16)
    lb = lb_ref[...]
    f = lb + (1.0 - lb) * jax.nn.sigmoid(seg(3))
    lf_o[...] = jnp.log(f)
    kb_o[...] = 1.0 - f
    vb_o[...] = seg(4)
    qb_o[...] = jax.nn.silu(seg(5))
    gb_o[...] = seg(6)


def _proj_ab(x_p, x_s, gain, w16, qg, kg, lb, bd, seq, keep):
    tm = PROJ_TILE
    n_p, n_s = x_p.shape[0], x_s.shape[0]
    n = n_p + n_s
    tiles_p, tiles_seq, tiles_keep = n_p // tm, seq // tm, keep // tm
    kept_tiles = (n_p // seq) * tiles_keep + n_s // tm

    def keep_map(i):
        b, j = i // tiles_seq, i % tiles_seq
        prompt_slot = b * tiles_keep + jnp.maximum(j - (tiles_seq - tiles_keep), 0)
        return (jnp.where(i >= tiles_p, kept_tiles - n_s // tm + (i - tiles_p), prompt_slot), 0)

    row = pl.BlockSpec((tm, D_HALF), lambda i: (i, 0))
    kept = pl.BlockSpec((tm, D_HALF), keep_map)
    full = lambda dt: jax.ShapeDtypeStruct((n, D_HALF), dt)
    kept_shape = jax.ShapeDtypeStruct((kept_tiles * tm, D_HALF), F32)
    return pl.pallas_call(
        functools.partial(_proj_ab_body, tiles_p=tiles_p),
        out_shape=[full(BF16), kept_shape, kept_shape, full(BF16), full(BF16),
                   full(F32), full(F32), full(F32), full(F32), full(F32)],
        grid=(n // tm,),
        in_specs=_group_specs(D_MODEL, tiles_p, tm) + [
            _const_spec((1, D_MODEL)), _const_spec(w16.shape),
            _const_spec((1, D_HALF)), _const_spec((1, D_HALF)), _const_spec((1, D_HALF)),
            _const_spec((D_HALF, D_HALF))],
        out_specs=[row, kept, kept] + [row] * 7,
        compiler_params=_cparams("arbitrary"),
        name="proj_ab",
    )(x_p, x_s, gain, w16, qg, kg, lb, bd)


def _attn_prompt_body(q_ref, k0, k1, k2, v0, v1, v2, bias_ref, o_ref):
    i = pl.program_id(1)
    lim = jnp.maximum(2 - i, 0) * ATTN_TILE
    col = lax.broadcasted_iota(I32, (ATTN_UNIT, ATTN_SPAN), 1)
    cut = ATTN_SPAN - 2 * ATTN_TILE

    def span(parts, start):
        if start == 0:
            return (parts[0], parts[1], parts[2][:cut])
        return (parts[0][start:], parts[1], parts[2])

    def head_group(h0):
        units = []
        for h in range(h0, h0 + ATTN_HEAD_GROUP):
            hs = slice(h * HEAD_DIM, (h + 1) * HEAD_DIM)
            kh = (k0[:, hs], k1[:, hs], k2[:, hs])
            vh = (v0[:, hs], v1[:, hs], v2[:, hs])
            for r in range(ATTN_TILE // ATTN_UNIT):
                start = r * ATTN_UNIT
                units.append((h, hs, slice(r * ATTN_UNIT, (r + 1) * ATTN_UNIT), start,
                              span(kh, start), span(vh, start)))
        s = [jnp.concatenate([_dot_nt(q_ref[rows, hs], kp) for kp in ks], axis=1)
             for (_, hs, rows, _, ks, _) in units]
        s = [jnp.where(col + start < lim, NEG_INF,
                       s_u + bias_ref[h, rows, start:start + ATTN_SPAN])
             for s_u, (h, _, rows, start, _, _) in zip(s, units)]
        m = [jnp.max(s_u, axis=-1, keepdims=True) for s_u in s]
        p = [jnp.exp(s_u - m_u) for s_u, m_u in zip(s, m)]
        l = [jnp.sum(p_u, axis=-1, keepdims=True) for p_u in p]
        outs = []
        for p_u, (_, _, _, _, _, vs) in zip(p, units):
            p16 = p_u.astype(BF16)
            o, at = 0.0, 0
            for vp in vs:
                o = o + _dot(p16[:, at:at + vp.shape[0]], vp)
                at += vp.shape[0]
            outs.append(o)
        for o_u, l_u, (_, hs, rows, _, _, _) in zip(outs, l, units):
            o_ref[rows, hs] = (o_u / l_u).astype(BF16)

    for h0 in range(0, N_HEADS, ATTN_HEAD_GROUP):
        head_group(h0)


def _attn_prompt(q16, k16, v16, bias_full, batch, seq):
    tiles = seq // ATTN_TILE
    qspec = pl.BlockSpec((ATTN_TILE, D_HALF), lambda b, i: (b * tiles + i, 0))

    def kv(back):
        return pl.BlockSpec((ATTN_TILE, D_HALF),
                            lambda b, i: (b * tiles + jnp.maximum(i - back, 0), 0))

    return pl.pallas_call(
        _attn_prompt_body,
        out_shape=jax.ShapeDtypeStruct((batch * seq, D_HALF), BF16),
        grid=(batch, tiles),
        in_specs=[qspec, kv(2), kv(1), kv(0), kv(2), kv(1), kv(0),
                  _const_spec(bias_full.shape)],
        out_specs=qspec,
        compiler_params=_cparams("parallel", "parallel"),
        name="attn_prompt",
    )(q16, k16, k16, k16, v16, v16, v16, bias_full)


def _attn_sample_body(q_ref, kc_ref, vc_ref, kn_ref, vn_ref, bc_ref, bn_ref, o_ref):
    la = kc_ref.shape[1] // N_HEADS
    for h in range(N_HEADS):
        hs = slice(h * HEAD_DIM, (h + 1) * HEAD_DIM)
        kc = kc_ref[0, pl.ds(h, la, stride=N_HEADS), :].astype(BF16)
        vc = vc_ref[0, pl.ds(h, la, stride=N_HEADS), :].astype(BF16)
        qh = q_ref[:, hs]
        sc = _dot_nt(qh, kc) + bc_ref[h]
        sn = _dot_nt(qh, kn_ref[:, hs]) + bn_ref[h]
        m = jnp.maximum(jnp.max(sc, axis=-1, keepdims=True), jnp.max(sn, axis=-1, keepdims=True))
        pc = jnp.exp(sc - m)
        pn = jnp.exp(sn - m)
        l = jnp.sum(pc, axis=-1, keepdims=True) + jnp.sum(pn, axis=-1, keepdims=True)
        o = _dot(pc.astype(BF16), vc) + _dot(pn.astype(BF16), vn_ref[:, hs])
        o_ref[:, hs] = (o / l).astype(BF16)


def _attn_sample(q16, k16, v16, cache_k, cache_v, bias_c, bias_n, row0, nseq, tn):
    blk0 = row0 // tn
    la8 = cache_k.shape[1]
    new = pl.BlockSpec((tn, D_HALF), lambda b: (blk0 + b, 0))
    cache = pl.BlockSpec((1, la8, HEAD_DIM), lambda b: (b, 0, 0))
    return pl.pallas_call(
        _attn_sample_body,
        out_shape=jax.ShapeDtypeStruct((nseq * tn, D_HALF), BF16),
        grid=(nseq,),
        in_specs=[new, cache, cache, new, new, _const_spec(bias_c.shape), _const_spec(bias_n.shape)],
        out_specs=pl.BlockSpec((tn, D_HALF), lambda b: (b, 0)),
        compiler_params=_cparams("parallel"),
        name="attn_sample",
    )(q16, cache_k, cache_v, k16, v16, bias_c, bias_n)


def _hgrn_body(q_ref, k_ref, v_ref, lf_ref, s0_ref, tcat_ref, o_ref, sout_ref, st_scr,
               *, chunk, nchunks):
    nblk = chunk // HGRN_BLOCK

    @pl.when(pl.program_id(1) == 0)
    def _():
        st_scr[...] = s0_ref[0]

    row = lax.broadcasted_iota(I32, (chunk, D_HALF), 0)
    r2 = lax.broadcasted_iota(I32, (chunk, chunk), 0)
    c2 = lax.broadcasted_iota(I32, (chunk, chunk), 1)
    same_blk_causal = jnp.logical_and(r2 // HGRN_BLOCK == c2 // HGRN_BLOCK, c2 <= r2)

    def one_chunk(c, carry):
        sl = pl.ds(pl.multiple_of(c * chunk, chunk), chunk)
        q = q_ref[sl, :]
        k = k_ref[sl, :]
        v16 = v_ref[sl, :].astype(BF16)
        l0, l1, l2 = _split3(lf_ref[sl, :])
        tcat = tcat_ref[...]
        both = _dot(tcat, l0) + _dot(tcat, l1) + _dot(tcat, l2)
        b_in = both[:chunk]
        b_ch = both[chunk:]
        ld = (q * jnp.exp(b_in)).astype(BF16)
        rd = (k * jnp.exp(-b_in)).astype(BF16)
        lj, rj = [], []
        for j in range(nblk - 1):
            e_j = b_ch[(j + 1) * HGRN_BLOCK - 1:(j + 1) * HGRN_BLOCK, :]
            later = row >= (j + 1) * HGRN_BLOCK
            inside = jnp.logical_and(row >= j * HGRN_BLOCK, row < (j + 1) * HGRN_BLOCK)
            lj.append(jnp.where(later, q * jnp.exp(jnp.minimum(b_ch - e_j, 0.0)), 0.0).astype(BF16))
            rj.append(jnp.where(inside, k * jnp.exp(jnp.minimum(e_j - b_ch, 0.0)), 0.0).astype(BF16))
        e_end = b_ch[chunk - 1:chunk, :]
        qc = (q * jnp.exp(b_ch)).astype(BF16)
        kc = (k * jnp.exp(e_end - b_ch)).astype(BF16)
        dec = jnp.exp(e_end)
        heads = [slice(h * HEAD_DIM, (h + 1) * HEAD_DIM) for h in range(N_HEADS)]
        sts = [st_scr[h] for h in range(N_HEADS)]
        diag = [_dot_nt(ld[:, hs], rd[:, hs]) for hs in heads]
        off = []
        for hs in heads:
            acc = None
            for j in range(nblk - 1):
                term = _dot_nt(lj[j][:, hs], rj[j][:, hs])
                acc = term if acc is None else acc + term
            off.append(acc)
        inter = [_dot_nt(qc[:, hs], st.astype(BF16)) for hs, st in zip(heads, sts)]
        upd = [_dot_tn(v16[:, hs], kc[:, hs]) for hs in heads]
        sc16 = [(jnp.where(same_blk_causal, d, 0.0) + o_).astype(BF16) for d, o_ in zip(diag, off)]
        intra = [_dot(s_, v16[:, hs]) for s_, hs in zip(sc16, heads)]
        o_ref[sl, :] = jnp.concatenate([a + b for a, b in zip(intra, inter)], axis=1)
        for h, hs in enumerate(heads):
            st_scr[h] = sts[h] * dec[:, hs] + upd[h]
        return carry

    lax.fori_loop(0, nchunks, one_chunk, 0, unroll=min(nchunks, HGRN_UNROLL))
    sout_ref[0] = st_scr[...]


def _hgrn(qb, kb, vb, lf, s0t, row0, nseq, seq, chunk, nchunks):
    tt = chunk * nchunks
    steps = seq // tt
    blk0 = row0 // tt
    t = jnp.arange(chunk)
    lower = t[None, :] <= t[:, None]
    same_blk = (t[None, :] // HGRN_BLOCK) == (t[:, None] // HGRN_BLOCK)
    tcat = jnp.concatenate([jnp.logical_and(lower, same_blk), lower], axis=0).astype(BF16)
    tok = pl.BlockSpec((tt, D_HALF), lambda b, j: (blk0 + b * steps + j, 0))
    state = pl.BlockSpec((1, N_HEADS, HEAD_DIM, HEAD_DIM), lambda b, j: (b, 0, 0, 0))
    return pl.pallas_call(
        functools.partial(_hgrn_body, chunk=chunk, nchunks=nchunks),
        out_shape=[jax.ShapeDtypeStruct((nseq * seq, D_HALF), F32),
                   jax.ShapeDtypeStruct(s0t.shape, F32)],
        grid=(nseq, steps),
        in_specs=[tok, tok, tok, tok, state, _const_spec((2 * chunk, chunk))],
        out_specs=[pl.BlockSpec((tt, D_HALF), lambda b, j: (b * steps + j, 0)), state],
        scratch_shapes=[pltpu.VMEM((N_HEADS, HEAD_DIM, HEAD_DIM), F32)],
        compiler_params=_cparams("parallel", "arbitrary"),
        name="hgrn",
    )(qb, kb, vb, lf, s0t, tcat)


def _route(xn, wr_hi, wr_lo, br, tstrict, cnt_scr, route_o, cnt_o, code_o):
    rows = xn.shape[0]
    x_hi, x_lo = _split2(xn)
    logits = _dot(x_hi, wr_hi) + _dot(x_lo, wr_hi) + _dot(x_hi, wr_lo) + br
    lane = lax.broadcasted_iota(I32, (rows, LANES), 1)
    lane_f = lane.astype(F32)

    def first_argmax(vals):
        m = jnp.max(vals, axis=-1, keepdims=True)
        idx = jnp.min(jnp.where(vals == m, lane_f, float(LANES)), axis=-1, keepdims=True)
        return m, idx.astype(I32)

    is_grp = jnp.logical_and(lane >= N_EXPERTS, lane < N_EXPERTS + N_GROUPS)
    gl = jnp.where(is_grp, logits, -jnp.inf)
    gmax, gidx = first_argmax(gl)
    p_grp = 1.0 / jnp.sum(jnp.exp(gl - gmax), axis=-1, keepdims=True)
    grp = gidx - N_EXPERTS
    in_grp = jnp.logical_and(lane < N_EXPERTS, lane // EXPERTS_PER_GROUP == grp)
    el = jnp.where(in_grp, logits, -jnp.inf)
    v1, e1 = first_argmax(el)
    v2, e2 = first_argmax(jnp.where(lane == e1, -jnp.inf, el))
    t2 = jnp.exp(v2 - v1)
    den = 1.0 + t2
    g1 = (1.0 / den) * p_grp
    g2 = (t2 / den) * p_grp

    oh1 = lane == e1
    oh2 = lane == e2
    oh = jnp.where(jnp.logical_or(oh1, oh2), 1.0, 0.0)
    before = _dot(tstrict, oh.astype(BF16)) + cnt_scr[...]
    rank1 = jnp.sum(jnp.where(oh1, before, 0.0), axis=-1, keepdims=True)
    rank2 = jnp.sum(jnp.where(oh2, before, 0.0), axis=-1, keepdims=True)
    cnt_scr[...] = cnt_scr[...] + jnp.sum(oh, axis=0, keepdims=True)
    cnt_o[...] = cnt_scr[...]

    e1f, e2f = e1.astype(F32), e2.astype(F32)
    fields = (e1f, e2f, g1, g2, rank1, rank2, e1f * RANK_SPAN + rank1, e2f * RANK_SPAN + rank2)
    slab = jnp.zeros((rows, LANES), F32)
    for idx, val in enumerate(fields):
        slab = jnp.where(lane == idx, val, slab)
    route_o[...] = slab
    code_o[...] = slab.T[6:8, :]


def _out_ab_body(hp_ref, hs_ref, oap_ref, oas_ref, obp_ref, obs_ref, gb_ref, og_ref, bd_ref, w_ref,
                 gf_ref, wrh_ref, wrl_ref, br_ref, ts_ref, h_o, xn_o, route_o, cnt_o, code_o, cnt_scr,
                 *, tiles_p):
    @pl.when(pl.program_id(0) == 0)
    def _():
        cnt_scr[...] = jnp.zeros_like(cnt_scr)

    ob = _group_pick(obp_ref, obs_ref, tiles_p)
    obn = ob * lax.rsqrt(_head_mean_sq(ob, bd_ref[...]) + RMS_EPS) * og_ref[...]
    obg = (obn * jax.nn.silu(gb_ref[...])).astype(BF16)
    oa = _group_pick(oap_ref, oas_ref, tiles_p)
    mix = _dot(oa, w_ref[0:D_HALF, :]) + _dot(obg, w_ref[D_HALF:, :])
    h1 = _group_pick(hp_ref, hs_ref, tiles_p) + mix
    h_o[...] = h1
    xn = _rms(h1, gf_ref[...])
    _store_row_tiles(xn_o, xn)
    _route(xn, wrh_ref[...], wrl_ref[...], br_ref[...], ts_ref[...], cnt_scr, route_o, cnt_o, code_o)


def _out_ab(h_p, h_s, oa_p, oa_s, ob_p, ob_s, gb, og, bd, w16, gf, wrh, wrl, br, tstrict):
    tm = ROUTE_TILE
    n = h_p.shape[0] + h_s.shape[0]
    tiles_p = h_p.shape[0] // tm
    row = lambda w: pl.BlockSpec((tm, w), lambda i: (i, 0))
    return pl.pallas_call(
        functools.partial(_out_ab_body, tiles_p=tiles_p),
        out_shape=[jax.ShapeDtypeStruct((n, D_MODEL), F32), jax.ShapeDtypeStruct((n * ROW_TILE, LANES), F32),
                   jax.ShapeDtypeStruct((n, LANES), F32), jax.ShapeDtypeStruct((1, LANES), F32),
                   jax.ShapeDtypeStruct((2, n), F32)],
        grid=(n // tm,),
        in_specs=_group_specs(D_MODEL, tiles_p, tm) + _group_specs(D_HALF, tiles_p, tm)
        + _group_specs(D_HALF, tiles_p, tm) + [
            row(D_HALF), _const_spec((1, D_HALF)),
            _const_spec((D_HALF, D_HALF)), _const_spec((D_MODEL, D_MODEL)),
            _const_spec((1, D_MODEL)), _const_spec((D_MODEL, LANES)),
            _const_spec((D_MODEL, LANES)), _const_spec((1, LANES)), _const_spec((tm, tm))],
        out_specs=[row(D_MODEL), pl.BlockSpec((tm * ROW_TILE, LANES), lambda i: (i, 0)), row(LANES),
                   _const_spec((1, LANES)), pl.BlockSpec((2, tm), lambda i: (0, i))],
        scratch_shapes=[pltpu.VMEM((1, LANES), F32)],
        compiler_params=_cparams("arbitrary"),
        name="out_ab",
    )(h_p, h_s, oa_p, oa_s, ob_p, ob_s, gb, og, bd, w16, gf, wrh, wrl, br, tstrict)


def _row_copy(src, src_row, dst, dst_row, sem):
    s0 = pl.multiple_of(src_row * ROW_TILE, ROW_TILE)
    d0 = pl.multiple_of(dst_row * ROW_TILE, ROW_TILE)
    return pltpu.make_async_copy(src.at[pl.ds(s0, ROW_TILE)], dst.at[pl.ds(d0, ROW_TILE)], sem)


def _sorted_row(code_ref, offs_ref, a):
    code = code_ref[a]
    return offs_ref[code >> RANK_BITS] + (code & (RANK_SPAN - 1))


def _collect_body(code_ref, offs_ref, ys_hbm, h_ref, r_ref, *rest, tile, n_tok, tiles_p, proj):
    if proj:
        g_ref, w_ref = rest[:2]
        rest = rest[2:]
    outs, (buf, sems) = rest[:-2], rest[-2:]
    i = pl.program_id(0)
    nt = pl.num_programs(0)

    def issue(t):
        slot = t % 2
        base = t * tile

        def one(r, carry):
            tok = base + r
            for k in range(2):
                row = _sorted_row(code_ref, offs_ref, k * n_tok + tok)
                _row_copy(ys_hbm, row, buf.at[slot].at[k], r, sems.at[slot]).start(priority=k)
            return carry

        lax.fori_loop(0, tile, one, 0, unroll=GATHER_UNROLL)

    @pl.when(i == 0)
    def _():
        issue(0)

    @pl.when(i + 1 < nt)
    def _():
        issue(i + 1)

    slot = i % 2
    for _ in range(2 * tile):
        _row_copy(ys_hbm, 0, buf.at[slot].at[0], 0, sems.at[slot]).wait()
    route = r_ref[...]
    ya = _load_row_tiles(buf.at[slot].at[0], tile)
    yb = _load_row_tiles(buf.at[slot].at[1], tile)
    out = h_ref[...] + (route[:, 2:3] * ya + route[:, 3:4] * yb)
    if proj:
        h_o, u_o, bg_o = outs
        h_o[...] = out
        xb = _rms(out, g_ref[...]).astype(BF16)
        bg_o[...] = _dot(xb, w_ref[:, 0:D_MODEL])
        u_o[...] = _dot(xb, w_ref[:, D_MODEL:2 * D_MODEL]) * _dot(xb, w_ref[:, 2 * D_MODEL:])
    else:
        @pl.when(i < tiles_p)
        def _():
            outs[0][...] = out

        @pl.when(i >= tiles_p)
        def _():
            outs[1][...] = out


def _collect(code, offs, ys, h, route, *, proj=None, n_p=None):
    tm = TOKEN_TILE
    n = h.shape[0]
    row = lambda w: pl.BlockSpec((tm, w), lambda i, code, offs: (i, 0))
    const = lambda shape: pl.BlockSpec(shape, lambda i, code, offs: (0,) * len(shape))
    if proj is not None:
        tiles_p = None
        extra_in, extra_specs = list(proj), [const(proj[0].shape), const(proj[1].shape)]
        out_shape = [jax.ShapeDtypeStruct((n, D_MODEL), F32)] * 3
        out_specs = [row(D_MODEL)] * 3
    else:
        tiles_p = n_p // tm
        extra_in, extra_specs = [], []
        out_shape = [jax.ShapeDtypeStruct((n_p, D_MODEL), F32),
                     jax.ShapeDtypeStruct((n - n_p, D_MODEL), F32)]
        out_specs = [pl.BlockSpec((tm, D_MODEL), lambda i, code, offs: (jnp.minimum(i, tiles_p - 1), 0)),
                     pl.BlockSpec((tm, D_MODEL), lambda i, code, offs: (jnp.maximum(i - tiles_p, 0), 0))]
    return pl.pallas_call(
        functools.partial(_collect_body, tile=tm, n_tok=n, tiles_p=tiles_p, proj=proj is not None),
        out_shape=out_shape,
        grid_spec=pltpu.PrefetchScalarGridSpec(
            num_scalar_prefetch=2,
            grid=(n // tm,),
            in_specs=[pl.BlockSpec(memory_space=pl.ANY), row(D_MODEL), row(LANES)] + extra_specs,
            out_specs=out_specs,
            scratch_shapes=[pltpu.VMEM((2, 2, tm * ROW_TILE, LANES), F32),
                            pltpu.SemaphoreType.DMA((2,))],
        ),
        compiler_params=pltpu.CompilerParams(dimension_semantics=("arbitrary",),
                                             vmem_limit_bytes=VMEM_LIMIT),
        name="moe_collect",
    )(code, offs, ys, h, route, *extra_in)


def _scatter_rows_body(code_ref, offs_ref, pad0_ref, padn_ref, nt_ref, x_ref, xs_hbm,
                       stage, zero_scr, sems, pad_sem, *, tile, n_tok, out_tiles):
    i = pl.program_id(0)
    last = pl.num_programs(0) - 1
    base = i * tile
    slot = i % 2

    def drain(s):
        for _ in range(2 * tile):
            _row_copy(stage.at[s], 0, xs_hbm, 0, sems.at[s]).wait()

    @pl.when(i >= 2)
    def _():
        drain(slot)

    stage[slot] = x_ref[...]

    def issue(r, carry):
        tok = base + r
        src = stage.at[slot]
        for k in range(2):
            row = _sorted_row(code_ref, offs_ref, k * n_tok + tok)
            _row_copy(src, r, xs_hbm, row, sems.at[slot]).start(priority=k)
        return carry

    lax.fori_loop(0, tile, issue, 0, unroll=GATHER_UNROLL)

    @pl.when(i == last)
    def _():
        zero_scr[...] = jnp.zeros_like(zero_scr)

        def fill(lo, count):
            def one(r, carry):
                _row_copy(zero_scr, 0, xs_hbm, lo + r, pad_sem).start()
                return carry

            def done(r, carry):
                _row_copy(zero_scr, 0, xs_hbm, 0, pad_sem).wait()
                return carry

            lax.fori_loop(0, count, one, 0)
            lax.fori_loop(0, count, done, 0)

        for e in range(N_EXPERTS):
            fill(pad0_ref[e], padn_ref[e])

        def tail_copy(t):
            rows = MOE_TILE * ROW_TILE
            return pltpu.make_async_copy(
                zero_scr, xs_hbm.at[pl.ds(pl.multiple_of(t * rows, rows), rows)], pad_sem)

        def tail_start(t, carry):
            tail_copy(t).start()
            return carry

        def tail_done(t, carry):
            tail_copy(t).wait()
            return carry

        lax.fori_loop(nt_ref[0], out_tiles, tail_start, 0)
        lax.fori_loop(nt_ref[0], out_tiles, tail_done, 0)
        drain(slot)

    @pl.when(jnp.logical_and(i == last, i >= 1))
    def _():
        drain(1 - slot)


def _scatter_rows(code, offs, pad0, padn, n_tiles, xn, n_rows):
    tm = TOKEN_TILE
    n = xn.shape[0] // ROW_TILE
    return pl.pallas_call(
        functools.partial(_scatter_rows_body, tile=tm, n_tok=n, out_tiles=n_rows // MOE_TILE),
        out_shape=jax.ShapeDtypeStruct((n_rows * ROW_TILE, LANES), F32),
        grid_spec=pltpu.PrefetchScalarGridSpec(
            num_scalar_prefetch=5,
            grid=(n // tm,),
            in_specs=[pl.BlockSpec((tm * ROW_TILE, LANES), lambda i, *_: (i, 0))],
            out_specs=pl.BlockSpec(memory_space=pl.ANY),
            scratch_shapes=[pltpu.VMEM((2, tm * ROW_TILE, LANES), F32),
                            pltpu.VMEM((MOE_TILE * ROW_TILE, LANES), F32),
                            pltpu.SemaphoreType.DMA((2,)), pltpu.SemaphoreType.DMA(())],
        ),
        compiler_params=pltpu.CompilerParams(dimension_semantics=("arbitrary",)),
        name="moe_scatter",
    )(code, offs, pad0, padn, n_tiles, xn)


def _experts_body(te_ref, nt_ref, x_ref, w1_ref, w3_ref, w2_ref, y_ref, w1_s, w3_s, w2_s):
    i = pl.program_id(0)
    prev = te_ref[jnp.maximum(i - 1, 0)]
    fresh = jnp.logical_or(i == 0, te_ref[i] != prev)

    @pl.when(fresh)
    def _():
        w1_s[...] = w1_ref[0].astype(BF16)
        w3_s[...] = w3_ref[0].astype(BF16)
        w2_s[...] = w2_ref[0].astype(BF16)

    @pl.when(i < nt_ref[0])
    def _():
        x = _load_row_tiles(x_ref, MOE_TILE).astype(BF16)
        a = _dot(x, w1_s[...])
        b = _dot(x, w3_s[...])
        _store_row_tiles(y_ref, _dot((jax.nn.silu(a) * b).astype(BF16), w2_s[...]))


def _experts(tile_expert, n_tiles, xs, w1, w3, w2):
    rows = xs.shape[0] // ROW_TILE
    tm = MOE_TILE
    tile = pl.BlockSpec((tm * ROW_TILE, LANES), lambda i, te, nt: (jnp.minimum(i, nt[0] - 1), 0))
    return pl.pallas_call(
        _experts_body,
        out_shape=jax.ShapeDtypeStruct((rows * ROW_TILE, LANES), F32),
        grid_spec=pltpu.PrefetchScalarGridSpec(
            num_scalar_prefetch=2,
            grid=(rows // tm,),
            in_specs=[tile,
                      pl.BlockSpec((1, D_MODEL, D_EXPERT), lambda i, te, nt: (te[i], 0, 0)),
                      pl.BlockSpec((1, D_MODEL, D_EXPERT), lambda i, te, nt: (te[i], 0, 0)),
                      pl.BlockSpec((1, D_EXPERT, D_MODEL), lambda i, te, nt: (te[i], 0, 0))],
            out_specs=tile,
            scratch_shapes=[pltpu.VMEM((D_MODEL, D_EXPERT), BF16), pltpu.VMEM((D_MODEL, D_EXPERT), BF16),
                            pltpu.VMEM((D_EXPERT, D_MODEL), BF16)],
        ),
        input_output_aliases={2: 0},
        compiler_params=_cparams("arbitrary"),
        name="moe_experts",
    )(tile_expert, n_tiles, xs, w1, w3, w2)


def _moe_plan(counts, n):
    counts = counts[0, :N_EXPERTS].astype(I32)
    padded = ((counts + MOE_TILE - 1) // MOE_TILE) * MOE_TILE
    ends = jnp.cumsum(padded)
    offs = ends - padded
    max_tiles = (2 * n + N_EXPERTS * (MOE_TILE - 1) + MOE_TILE - 1) // MOE_TILE
    tile_start = jnp.arange(max_tiles, dtype=I32) * MOE_TILE
    tile_expert = jnp.minimum(jnp.sum((tile_start[:, None] >= ends[None, :]).astype(I32), axis=1),
                              N_EXPERTS - 1)
    n_tiles = (ends[-1] // MOE_TILE).reshape(1)
    return offs, offs + counts, padded - counts, tile_expert, n_tiles, max_tiles * MOE_TILE


def _moe(xn, h, route, code, counts, layer, w1, w3, w2, **collect_mode):
    n = h.shape[0]
    code = code.astype(I32).reshape(2 * n)
    offs, pad0, padn, tile_expert, n_tiles, n_rows = _moe_plan(counts, n)
    xs = _scatter_rows(code, offs, pad0, padn, n_tiles, xn, n_rows)
    flat = lambda w: w.reshape((-1,) + w.shape[2:])
    ys = _experts(tile_expert + layer * N_EXPERTS, n_tiles, xs, flat(w1), flat(w3), flat(w2))
    return _collect(code, offs, ys, h, route, **collect_mode)


def _conv_out_body(u_ref, up_ref, stp_ref, sts_ref, bg_ref, h_ref, cw_ref, w_ref, gf_ref,
                   wrh_ref, wrl_ref, br_ref, ts_ref, h_o, xn_o, route_o, cnt_o, code_o, cnt_scr,
                   *, tile, tiles_p, seq_p, seq_s):
    i = pl.program_id(0)

    @pl.when(i == 0)
    def _():
        cnt_scr[...] = jnp.zeros_like(cnt_scr)

    u = u_ref[...]
    rowi = lax.broadcasted_iota(I32, (tile, D_MODEL), 0)
    is_p = i < tiles_p
    at_start = (i * tile) % seq_p == 0
    stp = stp_ref[0]
    prev = up_ref[...]
    m2_p = jnp.where(at_start, stp[0:1, :], prev[6:7, :])
    m1_p = jnp.where(at_start, stp[1:2, :], prev[7:8, :])
    per = tile // seq_s
    sts = sts_ref[...]
    m2_s = jnp.broadcast_to(sts[:, 0:1, :], (per, seq_s, D_MODEL)).reshape(tile, D_MODEL)
    m1_s = jnp.broadcast_to(sts[:, 1:2, :], (per, seq_s, D_MODEL)).reshape(tile, D_MODEL)
    m2 = jnp.where(is_p, m2_p, m2_s)
    m1 = jnp.where(is_p, m1_p, m1_s)
    pos = jnp.where(is_p, rowi, rowi % seq_s)
    u1 = jnp.where(pos == 0, m1, pltpu.roll(u, 1, axis=0))
    u2 = jnp.where(pos == 0, m2, jnp.where(pos == 1, m1, pltpu.roll(u, 2, axis=0)))
    cw = cw_ref[...]
    conv = u2 * cw[0:1, :] + u1 * cw[1:2, :] + u * cw[2:3, :]
    mix = _dot((bg_ref[...] * conv).astype(BF16), w_ref[...])
    h3 = h_ref[...] + mix
    h_o[...] = h3
    xn = _rms(h3, gf_ref[...])
    _store_row_tiles(xn_o, xn)
    _route(xn, wrh_ref[...], wrl_ref[...], br_ref[...], ts_ref[...], cnt_scr, route_o, cnt_o, code_o)


def _conv_out(u, bg, h, state_p, state_s, cw, w16, gf, wrh, wrl, br, tstrict, seq_p, seq_s):
    tm = ROUTE_TILE
    n = u.shape[0]
    tiles_p = state_p.shape[0] * seq_p // tm
    per = tm // seq_s
    row = lambda w: pl.BlockSpec((tm, w), lambda i: (i, 0))
    prev = pl.BlockSpec((8, D_MODEL), lambda i: (jnp.maximum(i * (tm // 8) - 1, 0), 0))
    stp_spec = pl.BlockSpec((1, 2, D_MODEL),
                            lambda i: (jnp.minimum(i, tiles_p - 1) * tm // seq_p, 0, 0))
    sts_spec = pl.BlockSpec((per, 2, D_MODEL), lambda i: (jnp.maximum(i - tiles_p, 0), 0, 0))
    return pl.pallas_call(
        functools.partial(_conv_out_body, tile=tm, tiles_p=tiles_p, seq_p=seq_p, seq_s=seq_s),
        out_shape=[jax.ShapeDtypeStruct((n, D_MODEL), F32),
                   jax.ShapeDtypeStruct((n * ROW_TILE, LANES), F32),
                   jax.ShapeDtypeStruct((n, LANES), F32), jax.ShapeDtypeStruct((1, LANES), F32),
                   jax.ShapeDtypeStruct((2, n), F32)],
        grid=(n // tm,),
        in_specs=[row(D_MODEL), prev, stp_spec, sts_spec, row(D_MODEL), row(D_MODEL),
                  _const_spec((3, D_MODEL)),
                  _const_spec((D_MODEL, D_MODEL)), _const_spec((1, D_MODEL)),
                  _const_spec((D_MODEL, LANES)), _const_spec((D_MODEL, LANES)),
                  _const_spec((1, LANES)), _const_spec((tm, tm))],
        out_specs=[row(D_MODEL), pl.BlockSpec((tm * ROW_TILE, LANES), lambda i: (i, 0)), row(LANES),
                   _const_spec((1, LANES)), pl.BlockSpec((2, tm), lambda i: (0, i))],
        scratch_shapes=[pltpu.VMEM((1, LANES), F32)],
        compiler_params=_cparams("arbitrary"),
        name="conv_out",
    )(u, u, state_p, state_s, bg, h, cw, w16, gf, wrh, wrl, br, tstrict)


def _rel_bias_toeplitz(rel_bias, rows, cols, lead):
    period = rows + cols
    k = jnp.arange(period)
    d = jnp.where(k < cols, k, k - period)
    idx = jnp.clip(lead - d, -REL_CLIP, REL_CLIP) + REL_CLIP
    v = jnp.take(rel_bias.astype(F32), idx, axis=1)
    heads = v.shape[0]
    skew = jnp.tile(v, (1, rows))[:, :rows * (period - 1)].reshape(heads, rows, period - 1)
    return skew[:, :, :cols]


def _bias_prompt(rel_bias):
    r = jnp.arange(ATTN_TILE)[:, None]
    c = jnp.arange(ATTN_WINDOW)[None, :]
    j = c // CHUNK - r // CHUNK
    band = jnp.logical_and(j >= 0, j <= BAND_CHUNKS)
    bias = _rel_bias_toeplitz(rel_bias, ATTN_TILE, ATTN_WINDOW, BAND_CHUNKS * CHUNK)
    return jnp.where(band[None], bias, NEG_INF)


def _bias_sample(rel_bias, la, tn):
    bias = _rel_bias_toeplitz(rel_bias, tn, la + tn, la)
    return bias[:, :, :la], bias[:, :, la:]


def _router_weights(w_group, b_group, w_expert, b_expert):
    pad = LANES - N_EXPERTS - N_GROUPS
    w = jnp.concatenate([w_expert, w_group, jnp.zeros((D_MODEL, pad), F32)], axis=1)
    b = jnp.concatenate([b_expert, b_group, jnp.zeros((pad,), F32)])[None, :].astype(F32)
    hi, lo = _split2(w.astype(F32))
    return hi, lo, b


def kernel(x_prompt, x_sample, cache_a_k, cache_a_v, state_hgrn, state_conv, norm_mix, norm_ffn,
           w_in_ab, w_out_ab, q_norm, k_norm, rel_bias, hgrn_lb_logits, hgrn_out_norm, w_in_c,
           conv_w, w_out_c, w_group, b_group, w_expert, b_expert, w1, w3, w2):
    batch, seq, d = x_prompt.shape
    nseq_s, tn, _ = x_sample.shape
    la = cache_a_k.shape[2]
    n_p = batch * seq
    n_s = nseq_s * tn
    n = n_p + n_s
    keep = min(BAND_CHUNKS * CHUNK, seq)

    x_p = x_prompt.reshape(n_p, d)
    x_s = x_sample.reshape(n_s, d)
    lb_all = jnp.cumsum(jax.nn.softmax(hgrn_lb_logits.astype(F32), axis=0), axis=0)
    head_avg = jnp.kron(jnp.eye(N_HEADS, dtype=F32),
                        jnp.full((HEAD_DIM, HEAD_DIM), 1.0 / HEAD_DIM, F32)).astype(BF16)
    t = jnp.arange(ROUTE_TILE)
    tstrict = (t[None, :] < t[:, None]).astype(BF16)
    row1 = lambda v: v.astype(F32).reshape(1, -1)
    tile8 = lambda v: jnp.tile(v.astype(F32), N_HEADS).reshape(1, -1)

    l = 0
    (q16, kf, vf, k16, v16, lf, kb, vb, qb, gb) = _proj_ab(
        x_p, x_s, row1(norm_mix[0]), w_in_ab[l].astype(BF16), tile8(q_norm[l]), tile8(k_norm[l]),
        row1(lb_all[l]), head_avg, seq, keep)

    oa_p = _attn_prompt(q16, k16, v16, _bias_prompt(rel_bias[l]), batch, seq)
    bias_c, bias_n = _bias_sample(rel_bias[l], la, tn)
    oa_s = _attn_sample(q16, k16, v16, cache_a_k[l].reshape(nseq_s, la * N_HEADS, HEAD_DIM),
                        cache_a_v[l].reshape(nseq_s, la * N_HEADS, HEAD_DIM), bias_c, bias_n,
                        n_p, nseq_s, tn)

    zeros_state = jnp.zeros((batch, N_HEADS, HEAD_DIM, HEAD_DIM), F32)
    ob_p, st_p = _hgrn(qb, kb, vb, lf, zeros_state, 0, batch, seq, 64, 4)
    ob_s, st_s = _hgrn(qb, kb, vb, lf, jnp.swapaxes(state_hgrn[l].astype(F32), -1, -2),
                       n_p, nseq_s, tn, tn, 1)

    wrh, wrl, br = _router_weights(w_group[0], b_group[0], w_expert[0], b_expert[0])
    h1, xn1, route1, cnt1, code1 = _out_ab(x_p, x_s, oa_p, oa_s, ob_p, ob_s, gb, row1(hgrn_out_norm[l]),
                                    head_avg, w_out_ab[l].astype(BF16), row1(norm_ffn[0]),
                                    wrh, wrl, br, tstrict)
    h2, u, bg = _moe(xn1, h1, route1, code1, cnt1, 0, w1, w3, w2,
                     proj=(row1(norm_mix[1]), w_in_c[0].astype(BF16)))

    wrh, wrl, br = _router_weights(w_group[1], b_group[1], w_expert[1], b_expert[1])
    h3, xn2, route2, cnt2, code2 = _conv_out(u, bg, h2, jnp.zeros((batch, 2, d), F32),
                                      state_conv[0].astype(F32), conv_w[0].astype(F32),
                                      w_out_c[0].astype(BF16), row1(norm_ffn[1]), wrh, wrl, br,
                                      tstrict, seq, tn)
    out_p, out_s = _moe(xn2, h3, route2, code2, cnt2, 1, w1, w3, w2, n_p=n_p)

    y_prompt = out_p.reshape(batch, seq, d)
    y_sample = out_s.reshape(nseq_s, tn, d)
    n_kp = batch * keep
    heads = lambda a, b_, t_: a.reshape(b_, t_, N_HEADS, HEAD_DIM)
    nk_p = heads(kf[:n_kp], batch, keep)[None]
    nv_p = heads(vf[:n_kp], batch, keep)[None]
    kf_s = heads(kf[n_kp:n_kp + n_s], nseq_s, tn)
    vf_s = heads(vf[n_kp:n_kp + n_s], nseq_s, tn)
    nk_s = jnp.concatenate([cache_a_k[l].astype(F32), kf_s], axis=1)[:, -la:][None]
    nv_s = jnp.concatenate([cache_a_v[l].astype(F32), vf_s], axis=1)[:, -la:][None]
    nh_p = jnp.swapaxes(st_p, -1, -2)[None]
    nh_s = jnp.swapaxes(st_s, -1, -2)[None]
    nc_p = jnp.stack([u[(b + 1) * seq - 2:(b + 1) * seq] for b in range(batch)])[None]
    u_s = u[n_p:].reshape(nseq_s, tn, d)
    nc_s = jnp.concatenate([state_conv[0].astype(F32), u_s], axis=1)[:, -2:][None]
    return (y_prompt, y_sample, nk_p, nv_p, nk_s, nv_s, nh_p, nh_s, nc_p, nc_s)
```

```python
import functools

import jax
import jax.numpy as jnp
from jax import lax
from jax.experimental import pallas as pl
from jax.experimental.pallas import tpu as pltpu

F32 = jnp.float32
BF16 = jnp.bfloat16
I32 = jnp.int32

D_MODEL = 1024
CHUNK = 64
BAND_CHUNKS = 8
HEAD_DIM = 64
N_HEADS = 8
D_HALF = N_HEADS * HEAD_DIM
REL_CLIP = 128
HGRN_BLOCK = 16
N_GROUPS = 4
EXPERTS_PER_GROUP = 8
N_EXPERTS = N_GROUPS * EXPERTS_PER_GROUP
D_EXPERT = 256
RMS_EPS = 1e-6
NEG_INF = -1e30

LANES = 128
ROW_TILE = D_MODEL // LANES
TOKEN_TILE = 256
PROJ_TILE = 512
ROUTE_TILE = 512
ATTN_TILE = 256
ATTN_WINDOW = ATTN_TILE + BAND_CHUNKS * CHUNK
ATTN_UNIT = 2 * CHUNK
ATTN_SPAN = ATTN_UNIT + BAND_CHUNKS * CHUNK
HGRN_UNROLL = 4
ATTN_HEAD_GROUP = 2
MOE_TILE = 256
RANK_BITS = 16
RANK_SPAN = 1 << RANK_BITS
GATHER_UNROLL = 8
VMEM_LIMIT = 48 * 1024 * 1024


def _cparams(*sem):
    return pltpu.CompilerParams(dimension_semantics=sem, vmem_limit_bytes=VMEM_LIMIT)


def _const_spec(shape):
    nd = len(shape)
    return pl.BlockSpec(shape, lambda *_: (0,) * nd)


def _store_row_tiles(ref, val):
    rows = val.shape[0]
    for c in range(ROW_TILE):
        ref[pl.ds(c, rows, stride=ROW_TILE), :] = val[:, c * LANES:(c + 1) * LANES]


def _load_row_tiles(ref, rows):
    return jnp.concatenate([ref[pl.ds(c, rows, stride=ROW_TILE), :] for c in range(ROW_TILE)], axis=1)


def _rms(x, gain):
    ms = jnp.mean(x * x, axis=-1, keepdims=True)
    return (x * lax.rsqrt(ms + RMS_EPS)) * gain


def _split2(x):
    hi = x.astype(BF16)
    lo = (x - hi.astype(F32)).astype(BF16)
    return hi, lo


def _split3(x):
    p0 = x.astype(BF16)
    r = x - p0.astype(F32)
    p1 = r.astype(BF16)
    p2 = (r - p1.astype(F32)).astype(BF16)
    return p0, p1, p2


def _dot(a, b):
    return jnp.dot(a, b, preferred_element_type=F32)


def _dot_nt(a, b):
    return lax.dot_general(a, b, (((1,), (1,)), ((), ())), preferred_element_type=F32)


def _dot_tn(a, b):
    return lax.dot_general(a, b, (((0,), (0,)), ((), ())), preferred_element_type=F32)


def _head_mean_sq(v, bd):
    hi, lo = _split2(v * v)
    return _dot(hi, bd) + _dot(lo, bd)


def _group_specs(width, tiles_p, tm=TOKEN_TILE):
    return [pl.BlockSpec((tm, width), lambda i: (jnp.minimum(i, tiles_p - 1), 0)),
            pl.BlockSpec((tm, width), lambda i: (jnp.maximum(i - tiles_p, 0), 0))]


def _group_pick(p_ref, s_ref, tiles_p):
    return jnp.where(pl.program_id(0) < tiles_p, p_ref[...], s_ref[...])


def _proj_ab_body(xp_ref, xs_ref, g_ref, w_ref, qg_ref, kg_ref, lb_ref, bd_ref,
                  q_o, kf_o, vf_o, k16_o, v16_o, lf_o, kb_o, vb_o, qb_o, gb_o, *, tiles_p):
    xb = _rms(_group_pick(xp_ref, xs_ref, tiles_p), g_ref[...]).astype(BF16)
    bd = bd_ref[...]

    def seg(j):
        return _dot(xb, w_ref[:, j * D_HALF:(j + 1) * D_HALF])

    qa = seg(0)
    qn = qa * lax.rsqrt(_head_mean_sq(qa, bd) + RMS_EPS) * qg_ref[...]
    q_o[...] = (qn * (HEAD_DIM ** -0.5)).astype(BF16)
    ka = seg(1)
    kn = ka * lax.rsqrt(_head_mean_sq(ka, bd) + RMS_EPS) * kg_ref[...]
    kf_o[...] = kn
    k16_o[...] = kn.astype(BF16)
    va = seg(2)
    vf_o[...] = va
    v16_o[...] = va.astype(BF16)
    lb = lb_ref[...]
    f = lb + (1.0 - lb) * jax.nn.sigmoid(seg(3))
    lf_o[...] = jnp.log(f)
    kb_o[...] = 1.0 - f
    vb_o[...] = seg(4)
    qb_o[...] = jax.nn.silu(seg(5))
    gb_o[...] = seg(6)


def _proj_ab(x_p, x_s, gain, w16, qg, kg, lb, bd, seq, keep):
    tm = PROJ_TILE
    n_p, n_s = x_p.shape[0], x_s.shape[0]
    n = n_p + n_s
    tiles_p, tiles_seq, tiles_keep = n_p // tm, seq // tm, keep // tm
    kept_tiles = (n_p // seq) * tiles_keep + n_s // tm

    def keep_map(i):
        b, j = i // tiles_seq, i % tiles_seq
        prompt_slot = b * tiles_keep + jnp.maximum(j - (tiles_seq - tiles_keep), 0)
        return (jnp.where(i >= tiles_p, kept_tiles - n_s // tm + (i - tiles_p), prompt_slot), 0)

    row = pl.BlockSpec((tm, D_HALF), lambda i: (i, 0))
    kept = pl.BlockSpec((tm, D_HALF), keep_map)
    full = lambda dt: jax.ShapeDtypeStruct((n, D_HALF), dt)
    kept_shape = jax.ShapeDtypeStruct((kept_tiles * tm, D_HALF), F32)
    return pl.pallas_call(
        functools.partial(_proj_ab_body, tiles_p=tiles_p),
        out_shape=[full(BF16), kept_shape, kept_shape, full(BF16), full(BF16),
                   full(F32), full(F32), full(F32), full(F32), full(F32)],
        grid=(n // tm,),
        in_specs=_group_specs(D_MODEL, tiles_p, tm) + [
            _const_spec((1, D_MODEL)), _const_spec(w16.shape),
            _const_spec((1, D_HALF)), _const_spec((1, D_HALF)), _const_spec((1, D_HALF)),
            _const_spec((D_HALF, D_HALF))],
        out_specs=[row, kept, kept] + [row] * 7,
        compiler_params=_cparams("arbitrary"),
        name="proj_ab",
    )(x_p, x_s, gain, w16, qg, kg, lb, bd)


def _attn_prompt_body(q_ref, k0, k1, k2, v0, v1, v2, bias_ref, o_ref):
    i = pl.program_id(1)
    lim = jnp.maximum(2 - i, 0) * ATTN_TILE
    col = lax.broadcasted_iota(I32, (ATTN_UNIT, ATTN_SPAN), 1)
    cut = ATTN_SPAN - 2 * ATTN_TILE

    def span(parts, start):
        if start == 0:
            return (parts[0], parts[1], parts[2][:cut])
        return (parts[0][start:], parts[1], parts[2])

    def head_group(h0):
        units = []
        for h in range(h0, h0 + ATTN_HEAD_GROUP):
            hs = slice(h * HEAD_DIM, (h + 1) * HEAD_DIM)
            kh = (k0[:, hs], k1[:, hs], k2[:, hs])
            vh = (v0[:, hs], v1[:, hs], v2[:, hs])
            for r in range(ATTN_TILE // ATTN_UNIT):
                start = r * ATTN_UNIT
                units.append((h, hs, slice(r * ATTN_UNIT, (r + 1) * ATTN_UNIT), start,
                              span(kh, start), span(vh, start)))
        s = [jnp.concatenate([_dot_nt(q_ref[rows, hs], kp) for kp in ks], axis=1)
             for (_, hs, rows, _, ks, _) in units]
        s = [jnp.where(col + start < lim, NEG_INF,
                       s_u + bias_ref[h, rows, start:start + ATTN_SPAN])
             for s_u, (h, _, rows, start, _, _) in zip(s, units)]
        m = [jnp.max(s_u, axis=-1, keepdims=True) for s_u in s]
        p = [jnp.exp(s_u - m_u) for s_u, m_u in zip(s, m)]
        l = [jnp.sum(p_u, axis=-1, keepdims=True) for p_u in p]
        outs = []
        for p_u, (_, _, _, _, _, vs) in zip(p, units):
            p16 = p_u.astype(BF16)
            o, at = 0.0, 0
            for vp in vs:
                o = o + _dot(p16[:, at:at + vp.shape[0]], vp)
                at += vp.shape[0]
            outs.append(o)
        for o_u, l_u, (_, hs, rows, _, _, _) in zip(outs, l, units):
            o_ref[rows, hs] = (o_u / l_u).astype(BF16)

    for h0 in range(0, N_HEADS, ATTN_HEAD_GROUP):
        head_group(h0)


def _attn_prompt(q16, k16, v16, bias_full, batch, seq):
    tiles = seq // ATTN_TILE
    qspec = pl.BlockSpec((ATTN_TILE, D_HALF), lambda b, i: (b * tiles + i, 0))

    def kv(back):
        return pl.BlockSpec((ATTN_TILE, D_HALF),
                            lambda b, i: (b * tiles + jnp.maximum(i - back, 0), 0))

    return pl.pallas_call(
        _attn_prompt_body,
        out_shape=jax.ShapeDtypeStruct((batch * seq, D_HALF), BF16),
        grid=(batch, tiles),
        in_specs=[qspec, kv(2), kv(1), kv(0), kv(2), kv(1), kv(0),
                  _const_spec(bias_full.shape)],
        out_specs=qspec,
        compiler_params=_cparams("parallel", "parallel"),
        name="attn_prompt",
    )(q16, k16, k16, k16, v16, v16, v16, bias_full)


def _attn_sample_body(q_ref, kc_ref, vc_ref, kn_ref, vn_ref, bc_ref, bn_ref, o_ref):
    la = kc_ref.shape[1] // N_HEADS
    for h in range(N_HEADS):
        hs = slice(h * HEAD_DIM, (h + 1) * HEAD_DIM)
        kc = kc_ref[0, pl.ds(h, la, stride=N_HEADS), :].astype(BF16)
        vc = vc_ref[0, pl.ds(h, la, stride=N_HEADS), :].astype(BF16)
        qh = q_ref[:, hs]
        sc = _dot_nt(qh, kc) + bc_ref[h]
        sn = _dot_nt(qh, kn_ref[:, hs]) + bn_ref[h]
        m = jnp.maximum(jnp.max(sc, axis=-1, keepdims=True), jnp.max(sn, axis=-1, keepdims=True))
        pc = jnp.exp(sc - m)
        pn = jnp.exp(sn - m)
        l = jnp.sum(pc, axis=-1, keepdims=True) + jnp.sum(pn, axis=-1, keepdims=True)
        o = _dot(pc.astype(BF16), vc) + _dot(pn.astype(BF16), vn_ref[:, hs])
        o_ref[:, hs] = (o / l).astype(BF16)


def _attn_sample(q16, k16, v16, cache_k, cache_v, bias_c, bias_n, row0, nseq, tn):
    blk0 = row0 // tn
    la8 = cache_k.shape[1]
    new = pl.BlockSpec((tn, D_HALF), lambda b: (blk0 + b, 0))
    cache = pl.BlockSpec((1, la8, HEAD_DIM), lambda b: (b, 0, 0))
    return pl.pallas_call(
        _attn_sample_body,
        out_shape=jax.ShapeDtypeStruct((nseq * tn, D_HALF), BF16),
        grid=(nseq,),
        in_specs=[new, cache, cache, new, new, _const_spec(bias_c.shape), _const_spec(bias_n.shape)],
        out_specs=pl.BlockSpec((tn, D_HALF), lambda b: (b, 0)),
        compiler_params=_cparams("parallel"),
        name="attn_sample",
    )(q16, cache_k, cache_v, k16, v16, bias_c, bias_n)


def _hgrn_body(q_ref, k_ref, v_ref, lf_ref, s0_ref, tcat_ref, o_ref, sout_ref, st_scr,
               *, chunk, nchunks):
    nblk = chunk // HGRN_BLOCK

    @pl.when(pl.program_id(1) == 0)
    def _():
        st_scr[...] = s0_ref[0]

    row = lax.broadcasted_iota(I32, (chunk, D_HALF), 0)
    r2 = lax.broadcasted_iota(I32, (chunk, chunk), 0)
    c2 = lax.broadcasted_iota(I32, (chunk, chunk), 1)
    same_blk_causal = jnp.logical_and(r2 // HGRN_BLOCK == c2 // HGRN_BLOCK, c2 <= r2)

    def one_chunk(c, carry):
        sl = pl.ds(pl.multiple_of(c * chunk, chunk), chunk)
        q = q_ref[sl, :]
        k = k_ref[sl, :]
        v16 = v_ref[sl, :].astype(BF16)
        l0, l1, l2 = _split3(lf_ref[sl, :])
        tcat = tcat_ref[...]
        both = _dot(tcat, l0) + _dot(tcat, l1) + _dot(tcat, l2)
        b_in = both[:chunk]
        b_ch = both[chunk:]
        ld = (q * jnp.exp(b_in)).astype(BF16)
        rd = (k * jnp.exp(-b_in)).astype(BF16)
        lj, rj = [], []
        for j in range(nblk - 1):
            e_j = b_ch[(j + 1) * HGRN_BLOCK - 1:(j + 1) * HGRN_BLOCK, :]
            later = row >= (j + 1) * HGRN_BLOCK
            inside = jnp.logical_and(row >= j * HGRN_BLOCK, row < (j + 1) * HGRN_BLOCK)
            lj.append(jnp.where(later, q * jnp.exp(jnp.minimum(b_ch - e_j, 0.0)), 0.0).astype(BF16))
            rj.append(jnp.where(inside, k * jnp.exp(jnp.minimum(e_j - b_ch, 0.0)), 0.0).astype(BF16))
        e_end = b_ch[chunk - 1:chunk, :]
        qc = (q * jnp.exp(b_ch)).astype(BF16)
        kc = (k * jnp.exp(e_end - b_ch)).astype(BF16)
        dec = jnp.exp(e_end)
        heads = [slice(h * HEAD_DIM, (h + 1) * HEAD_DIM) for h in range(N_HEADS)]
        sts = [st_scr[h] for h in range(N_HEADS)]
        diag = [_dot_nt(ld[:, hs], rd[:, hs]) for hs in heads]
        off = []
        for hs in heads:
            acc = None
            for j in range(nblk - 1):
                term = _dot_nt(lj[j][:, hs], rj[j][:, hs])
                acc = term if acc is None else acc + term
            off.append(acc)
        inter = [_dot_nt(qc[:, hs], st.astype(BF16)) for hs, st in zip(heads, sts)]
        upd = [_dot_tn(v16[:, hs], kc[:, hs]) for hs in heads]
        sc16 = [(jnp.where(same_blk_causal, d, 0.0) + o_).astype(BF16) for d, o_ in zip(diag, off)]
        intra = [_dot(s_, v16[:, hs]) for s_, hs in zip(sc16, heads)]
        o_ref[sl, :] = jnp.concatenate([a + b for a, b in zip(intra, inter)], axis=1)
        for h, hs in enumerate(heads):
            st_scr[h] = sts[h] * dec[:, hs] + upd[h]
        return carry

    lax.fori_loop(0, nchunks, one_chunk, 0, unroll=min(nchunks, HGRN_UNROLL))
    sout_ref[0] = st_scr[...]


def _hgrn(qb, kb, vb, lf, s0t, row0, nseq, seq, chunk, nchunks):
    tt = chunk * nchunks
    steps = seq // tt
    blk0 = row0 // tt
    t = jnp.arange(chunk)
    lower = t[None, :] <= t[:, None]
    same_blk = (t[None, :] // HGRN_BLOCK) == (t[:, None] // HGRN_BLOCK)
    tcat = jnp.concatenate([jnp.logical_and(lower, same_blk), lower], axis=0).astype(BF16)
    tok = pl.BlockSpec((tt, D_HALF), lambda b, j: (blk0 + b * steps + j, 0))
    state = pl.BlockSpec((1, N_HEADS, HEAD_DIM, HEAD_DIM), lambda b, j: (b, 0, 0, 0))
    return pl.pallas_call(
        functools.partial(_hgrn_body, chunk=chunk, nchunks=nchunks),
        out_shape=[jax.ShapeDtypeStruct((nseq * seq, D_HALF), F32),
                   jax.ShapeDtypeStruct(s0t.shape, F32)],
        grid=(nseq, steps),
        in_specs=[tok, tok, tok, tok, state, _const_spec((2 * chunk, chunk))],
        out_specs=[pl.BlockSpec((tt, D_HALF), lambda b, j: (b * steps + j, 0)), state],
        scratch_shapes=[pltpu.VMEM((N_HEADS, HEAD_DIM, HEAD_DIM), F32)],
        compiler_params=_cparams("parallel", "arbitrary"),
        name="hgrn",
    )(qb, kb, vb, lf, s0t, tcat)


def _route(xn, wr_hi, wr_lo, br, tstrict, cnt_scr, route_o, cnt_o, code_o):
    rows = xn.shape[0]
    x_hi, x_lo = _split2(xn)
    logits = _dot(x_hi, wr_hi) + _dot(x_lo, wr_hi) + _dot(x_hi, wr_lo) + br
    lane = lax.broadcasted_iota(I32, (rows, LANES), 1)
    lane_f = lane.astype(F32)

    def first_argmax(vals):
        m = jnp.max(vals, axis=-1, keepdims=True)
        idx = jnp.min(jnp.where(vals == m, lane_f, float(LANES)), axis=-1, keepdims=True)
        return m, idx.astype(I32)

    is_grp = jnp.logical_and(lane >= N_EXPERTS, lane < N_EXPERTS + N_GROUPS)
    gl = jnp.where(is_grp, logits, -jnp.inf)
    gmax, gidx = first_argmax(gl)
    p_grp = 1.0 / jnp.sum(jnp.exp(gl - gmax), axis=-1, keepdims=True)
    grp = gidx - N_EXPERTS
    in_grp = jnp.logical_and(lane < N_EXPERTS, lane // EXPERTS_PER_GROUP == grp)
    el = jnp.where(in_grp, logits, -jnp.inf)
    v1, e1 = first_argmax(el)
    v2, e2 = first_argmax(jnp.where(lane == e1, -jnp.inf, el))
    t2 = jnp.exp(v2 - v1)
    den = 1.0 + t2
    g1 = (1.0 / den) * p_grp
    g2 = (t2 / den) * p_grp

    oh1 = lane == e1
    oh2 = lane == e2
    oh = jnp.where(jnp.logical_or(oh1, oh2), 1.0, 0.0)
    before = _dot(tstrict, oh.astype(BF16)) + cnt_scr[...]
    rank1 = jnp.sum(jnp.where(oh1, before, 0.0), axis=-1, keepdims=True)
    rank2 = jnp.sum(jnp.where(oh2, before, 0.0), axis=-1, keepdims=True)
    cnt_scr[...] = cnt_scr[...] + jnp.sum(oh, axis=0, keepdims=True)
    cnt_o[...] = cnt_scr[...]

    e1f, e2f = e1.astype(F32), e2.astype(F32)
    fields = (e1f, e2f, g1, g2, rank1, rank2, e1f * RANK_SPAN + rank1, e2f * RANK_SPAN + rank2)
    slab = jnp.zeros((rows, LANES), F32)
    for idx, val in enumerate(fields):
        slab = jnp.where(lane == idx, val, slab)
    route_o[...] = slab
    code_o[...] = slab.T[6:8, :]


def _out_ab_body(hp_ref, hs_ref, oap_ref, oas_ref, obp_ref, obs_ref, gb_ref, og_ref, bd_ref, w_ref,
                 gf_ref, wrh_ref, wrl_ref, br_ref, ts_ref, h_o, xn_o, route_o, cnt_o, code_o, cnt_scr,
                 *, tiles_p):
    @pl.when(pl.program_id(0) == 0)
    def _():
        cnt_scr[...] = jnp.zeros_like(cnt_scr)

    ob = _group_pick(obp_ref, obs_ref, tiles_p)
    obn = ob * lax.rsqrt(_head_mean_sq(ob, bd_ref[...]) + RMS_EPS) * og_ref[...]
    obg = (obn * jax.nn.silu(gb_ref[...])).astype(BF16)
    oa = _group_pick(oap_ref, oas_ref, tiles_p)
    mix = _dot(oa, w_ref[0:D_HALF, :]) + _dot(obg, w_ref[D_HALF:, :])
    h1 = _group_pick(hp_ref, hs_ref, tiles_p) + mix
    h_o[...] = h1
    xn = _rms(h1, gf_ref[...])
    _store_row_tiles(xn_o, xn)
    _route(xn, wrh_ref[...], wrl_ref[...], br_ref[...], ts_ref[...], cnt_scr, route_o, cnt_o, code_o)


def _out_ab(h_p, h_s, oa_p, oa_s, ob_p, ob_s, gb, og, bd, w16, gf, wrh, wrl, br, tstrict):
    tm = ROUTE_TILE
    n = h_p.shape[0] + h_s.shape[0]
    tiles_p = h_p.shape[0] // tm
    row = lambda w: pl.BlockSpec((tm, w), lambda i: (i, 0))
    return pl.pallas_call(
        functools.partial(_out_ab_body, tiles_p=tiles_p),
        out_shape=[jax.ShapeDtypeStruct((n, D_MODEL), F32), jax.ShapeDtypeStruct((n * ROW_TILE, LANES), F32),
                   jax.ShapeDtypeStruct((n, LANES), F32), jax.ShapeDtypeStruct((1, LANES), F32),
                   jax.ShapeDtypeStruct((2, n), F32)],
        grid=(n // tm,),
        in_specs=_group_specs(D_MODEL, tiles_p, tm) + _group_specs(D_HALF, tiles_p, tm)
        + _group_specs(D_HALF, tiles_p, tm) + [
            row(D_HALF), _const_spec((1, D_HALF)),
            _const_spec((D_HALF, D_HALF)), _const_spec((D_MODEL, D_MODEL)),
            _const_spec((1, D_MODEL)), _const_spec((D_MODEL, LANES)),
            _const_spec((D_MODEL, LANES)), _const_spec((1, LANES)), _const_spec((tm, tm))],
        out_specs=[row(D_MODEL), pl.BlockSpec((tm * ROW_TILE, LANES), lambda i: (i, 0)), row(LANES),
                   _const_spec((1, LANES)), pl.BlockSpec((2, tm), lambda i: (0, i))],
        scratch_shapes=[pltpu.VMEM((1, LANES), F32)],
        compiler_params=_cparams("arbitrary"),
        name="out_ab",
    )(h_p, h_s, oa_p, oa_s, ob_p, ob_s, gb, og, bd, w16, gf, wrh, wrl, br, tstrict)


def _row_copy(src, src_row, dst, dst_row, sem):
    s0 = pl.multiple_of(src_row * ROW_TILE, ROW_TILE)
    d0 = pl.multiple_of(dst_row * ROW_TILE, ROW_TILE)
    return pltpu.make_async_copy(src.at[pl.ds(s0, ROW_TILE)], dst.at[pl.ds(d0, ROW_TILE)], sem)


def _collect_body(pos_ref, ys_hbm, h_ref, r_ref, *rest, tile, n_tok, tiles_p, proj):
    if proj:
        g_ref, w_ref = rest[:2]
        rest = rest[2:]
    outs, (buf, sems) = rest[:-2], rest[-2:]
    i = pl.program_id(0)
    nt = pl.num_programs(0)

    def issue(t):
        slot = t % 2
        base = t * tile

        def one(r, carry):
            tok = base + r
            for k in range(2):
                row = pos_ref[k * n_tok + tok]
                _row_copy(ys_hbm, row, buf.at[slot].at[k], r, sems.at[slot]).start(priority=k)
            return carry

        lax.fori_loop(0, tile, one, 0, unroll=GATHER_UNROLL)

    @pl.when(i == 0)
    def _():
        issue(0)

    @pl.when(i + 1 < nt)
    def _():
        issue(i + 1)

    slot = i % 2
    for _ in range(2 * tile):
        _row_copy(ys_hbm, 0, buf.at[slot].at[0], 0, sems.at[slot]).wait()
    route = r_ref[...]
    ya = _load_row_tiles(buf.at[slot].at[0], tile)
    yb = _load_row_tiles(buf.at[slot].at[1], tile)
    out = h_ref[...] + (route[:, 2:3] * ya + route[:, 3:4] * yb)
    if proj:
        h_o, u_o, bg_o = outs
        h_o[...] = out
        xb = _rms(out, g_ref[...]).astype(BF16)
        bg_o[...] = _dot(xb, w_ref[:, 0:D_MODEL])
        u_o[...] = _dot(xb, w_ref[:, D_MODEL:2 * D_MODEL]) * _dot(xb, w_ref[:, 2 * D_MODEL:])
    else:
        @pl.when(i < tiles_p)
        def _():
            outs[0][...] = out

        @pl.when(i >= tiles_p)
        def _():
            outs[1][...] = out


def _collect(pos, ys, h, route, *, proj=None, n_p=None):
    tm = TOKEN_TILE
    n = h.shape[0]
    row = lambda w: pl.BlockSpec((tm, w), lambda i, pos: (i, 0))
    const = lambda shape: pl.BlockSpec(shape, lambda i, pos: (0,) * len(shape))
    if proj is not None:
        tiles_p = None
        extra_in, extra_specs = list(proj), [const(proj[0].shape), const(proj[1].shape)]
        out_shape = [jax.ShapeDtypeStruct((n, D_MODEL), F32)] * 3
        out_specs = [row(D_MODEL)] * 3
    else:
        tiles_p = n_p // tm
        extra_in, extra_specs = [], []
        out_shape = [jax.ShapeDtypeStruct((n_p, D_MODEL), F32),
                     jax.ShapeDtypeStruct((n - n_p, D_MODEL), F32)]
        out_specs = [pl.BlockSpec((tm, D_MODEL), lambda i, pos: (jnp.minimum(i, tiles_p - 1), 0)),
                     pl.BlockSpec((tm, D_MODEL), lambda i, pos: (jnp.maximum(i - tiles_p, 0), 0))]
    return pl.pallas_call(
        functools.partial(_collect_body, tile=tm, n_tok=n, tiles_p=tiles_p, proj=proj is not None),
        out_shape=out_shape,
        grid_spec=pltpu.PrefetchScalarGridSpec(
            num_scalar_prefetch=1,
            grid=(n // tm,),
            in_specs=[pl.BlockSpec(memory_space=pl.ANY), row(D_MODEL), row(LANES)] + extra_specs,
            out_specs=out_specs,
            scratch_shapes=[pltpu.VMEM((2, 2, tm * ROW_TILE, LANES), F32),
                            pltpu.SemaphoreType.DMA((2,))],
        ),
        compiler_params=pltpu.CompilerParams(dimension_semantics=("arbitrary",),
                                             vmem_limit_bytes=VMEM_LIMIT),
        name="moe_collect",
    )(pos, ys, h, route, *extra_in)


def _scatter_rows_body(pos_ref, pad0_ref, padn_ref, nt_ref, x_ref, xs_hbm,
                       stage, zero_scr, sems, pad_sem, *, tile, n_tok, out_tiles):
    i = pl.program_id(0)
    last = pl.num_programs(0) - 1
    base = i * tile
    slot = i % 2

    def drain(s):
        for _ in range(2 * tile):
            _row_copy(stage.at[s], 0, xs_hbm, 0, sems.at[s]).wait()

    @pl.when(i >= 2)
    def _():
        drain(slot)

    stage[slot] = x_ref[...]

    def issue(r, carry):
        tok = base + r
        src = stage.at[slot]
        for k in range(2):
            _row_copy(src, r, xs_hbm, pos_ref[k * n_tok + tok], sems.at[slot]).start(priority=k)
        return carry

    lax.fori_loop(0, tile, issue, 0, unroll=GATHER_UNROLL)

    @pl.when(i == last)
    def _():
        zero_scr[...] = jnp.zeros_like(zero_scr)

        def fill(lo, count):
            def one(r, carry):
                _row_copy(zero_scr, 0, xs_hbm, lo + r, pad_sem).start()
                return carry

            def done(r, carry):
                _row_copy(zero_scr, 0, xs_hbm, 0, pad_sem).wait()
                return carry

            lax.fori_loop(0, count, one, 0)
            lax.fori_loop(0, count, done, 0)

        for e in range(N_EXPERTS):
            fill(pad0_ref[e], padn_ref[e])

        def tail_copy(t):
            rows = MOE_TILE * ROW_TILE
            return pltpu.make_async_copy(
                zero_scr, xs_hbm.at[pl.ds(pl.multiple_of(t * rows, rows), rows)], pad_sem)

        def tail_start(t, carry):
            tail_copy(t).start()
            return carry

        def tail_done(t, carry):
            tail_copy(t).wait()
            return carry

        lax.fori_loop(nt_ref[0], out_tiles, tail_start, 0)
        lax.fori_loop(nt_ref[0], out_tiles, tail_done, 0)
        drain(slot)

    @pl.when(jnp.logical_and(i == last, i >= 1))
    def _():
        drain(1 - slot)


def _scatter_rows(pos, pad0, padn, n_tiles, xn, n_rows):
    tm = TOKEN_TILE
    n = xn.shape[0] // ROW_TILE
    return pl.pallas_call(
        functools.partial(_scatter_rows_body, tile=tm, n_tok=n, out_tiles=n_rows // MOE_TILE),
        out_shape=jax.ShapeDtypeStruct((n_rows * ROW_TILE, LANES), F32),
        grid_spec=pltpu.PrefetchScalarGridSpec(
            num_scalar_prefetch=4,
            grid=(n // tm,),
            in_specs=[pl.BlockSpec((tm * ROW_TILE, LANES), lambda i, *_: (i, 0))],
            out_specs=pl.BlockSpec(memory_space=pl.ANY),
            scratch_shapes=[pltpu.VMEM((2, tm * ROW_TILE, LANES), F32),
                            pltpu.VMEM((MOE_TILE * ROW_TILE, LANES), F32),
                            pltpu.SemaphoreType.DMA((2,)), pltpu.SemaphoreType.DMA(())],
        ),
        compiler_params=pltpu.CompilerParams(dimension_semantics=("arbitrary",)),
        name="moe_scatter",
    )(pos, pad0, padn, n_tiles, xn)


def _experts_body(te_ref, nt_ref, x_ref, w1_ref, w3_ref, w2_ref, y_ref, w1_s, w3_s, w2_s):
    i = pl.program_id(0)
    prev = te_ref[jnp.maximum(i - 1, 0)]
    fresh = jnp.logical_or(i == 0, te_ref[i] != prev)

    @pl.when(fresh)
    def _():
        w1_s[...] = w1_ref[0].astype(BF16)
        w3_s[...] = w3_ref[0].astype(BF16)
        w2_s[...] = w2_ref[0].astype(BF16)

    @pl.when(i < nt_ref[0])
    def _():
        x = _load_row_tiles(x_ref, MOE_TILE).astype(BF16)
        a = _dot(x, w1_s[...])
        b = _dot(x, w3_s[...])
        _store_row_tiles(y_ref, _dot((jax.nn.silu(a) * b).astype(BF16), w2_s[...]))


def _experts(tile_expert, n_tiles, xs, w1, w3, w2):
    rows = xs.shape[0] // ROW_TILE
    tm = MOE_TILE
    tile = pl.BlockSpec((tm * ROW_TILE, LANES), lambda i, te, nt: (jnp.minimum(i, nt[0] - 1), 0))
    return pl.pallas_call(
        _experts_body,
        out_shape=jax.ShapeDtypeStruct((rows * ROW_TILE, LANES), F32),
        grid_spec=pltpu.PrefetchScalarGridSpec(
            num_scalar_prefetch=2,
            grid=(rows // tm,),
            in_specs=[tile,
                      pl.BlockSpec((1, D_MODEL, D_EXPERT), lambda i, te, nt: (te[i], 0, 0)),
                      pl.BlockSpec((1, D_MODEL, D_EXPERT), lambda i, te, nt: (te[i], 0, 0)),
                      pl.BlockSpec((1, D_EXPERT, D_MODEL), lambda i, te, nt: (te[i], 0, 0))],
            out_specs=tile,
            scratch_shapes=[pltpu.VMEM((D_MODEL, D_EXPERT), BF16), pltpu.VMEM((D_MODEL, D_EXPERT), BF16),
                            pltpu.VMEM((D_EXPERT, D_MODEL), BF16)],
        ),
        input_output_aliases={2: 0},
        compiler_params=_cparams("arbitrary"),
        name="moe_experts",
    )(tile_expert, n_tiles, xs, w1, w3, w2)


def _moe_plan(counts, n):
    counts = counts[0, :N_EXPERTS].astype(I32)
    padded = ((counts + MOE_TILE - 1) // MOE_TILE) * MOE_TILE
    ends = jnp.cumsum(padded)
    offs = ends - padded
    max_tiles = (2 * n + N_EXPERTS * (MOE_TILE - 1) + MOE_TILE - 1) // MOE_TILE
    tile_start = jnp.arange(max_tiles, dtype=I32) * MOE_TILE
    tile_expert = jnp.minimum(jnp.sum((tile_start[:, None] >= ends[None, :]).astype(I32), axis=1),
                              N_EXPERTS - 1)
    n_tiles = (ends[-1] // MOE_TILE).reshape(1)
    return offs, offs + counts, padded - counts, tile_expert, n_tiles, max_tiles * MOE_TILE


def _moe(xn, h, route, code, counts, layer, w1, w3, w2, **collect_mode):
    n = h.shape[0]
    code = code.astype(I32).reshape(2 * n)
    offs, pad0, padn, tile_expert, n_tiles, n_rows = _moe_plan(counts, n)
    pos = jnp.take(offs, code >> RANK_BITS) + (code & (RANK_SPAN - 1))
    xs = _scatter_rows(pos, pad0, padn, n_tiles, xn, n_rows)
    flat = lambda w: w.reshape((-1,) + w.shape[2:])
    ys = _experts(tile_expert + layer * N_EXPERTS, n_tiles, xs, flat(w1), flat(w3), flat(w2))
    return _collect(pos, ys, h, route, **collect_mode)


def _conv_out_body(u_ref, up_ref, stp_ref, sts_ref, bg_ref, h_ref, cw_ref, w_ref, gf_ref,
                   wrh_ref, wrl_ref, br_ref, ts_ref, h_o, xn_o, route_o, cnt_o, code_o, cnt_scr,
                   *, tile, tiles_p, seq_p, seq_s):
    i = pl.program_id(0)

    @pl.when(i == 0)
    def _():
        cnt_scr[...] = jnp.zeros_like(cnt_scr)

    u = u_ref[...]
    rowi = lax.broadcasted_iota(I32, (tile, D_MODEL), 0)
    is_p = i < tiles_p
    at_start = (i * tile) % seq_p == 0
    stp = stp_ref[0]
    prev = up_ref[...]
    m2_p = jnp.where(at_start, stp[0:1, :], prev[6:7, :])
    m1_p = jnp.where(at_start, stp[1:2, :], prev[7:8, :])
    per = tile // seq_s
    sts = sts_ref[...]
    m2_s = jnp.broadcast_to(sts[:, 0:1, :], (per, seq_s, D_MODEL)).reshape(tile, D_MODEL)
    m1_s = jnp.broadcast_to(sts[:, 1:2, :], (per, seq_s, D_MODEL)).reshape(tile, D_MODEL)
    m2 = jnp.where(is_p, m2_p, m2_s)
    m1 = jnp.where(is_p, m1_p, m1_s)
    pos = jnp.where(is_p, rowi, rowi % seq_s)
    u1 = jnp.where(pos == 0, m1, pltpu.roll(u, 1, axis=0))
    u2 = jnp.where(pos == 0, m2, jnp.where(pos == 1, m1, pltpu.roll(u, 2, axis=0)))
    cw = cw_ref[...]
    conv = u2 * cw[0:1, :] + u1 * cw[1:2, :] + u * cw[2:3, :]
    mix = _dot((bg_ref[...] * conv).astype(BF16), w_ref[...])
    h3 = h_ref[...] + mix
    h_o[...] = h3
    xn = _rms(h3, gf_ref[...])
    _store_row_tiles(xn_o, xn)
    _route(xn, wrh_ref[...], wrl_ref[...], br_ref[...], ts_ref[...], cnt_scr, route_o, cnt_o, code_o)


def _conv_out(u, bg, h, state_p, state_s, cw, w16, gf, wrh, wrl, br, tstrict, seq_p, seq_s):
    tm = ROUTE_TILE
    n = u.shape[0]
    tiles_p = state_p.shape[0] * seq_p // tm
    per = tm // seq_s
    row = lambda w: pl.BlockSpec((tm, w), lambda i: (i, 0))
    prev = pl.BlockSpec((8, D_MODEL), lambda i: (jnp.maximum(i * (tm // 8) - 1, 0), 0))
    stp_spec = pl.BlockSpec((1, 2, D_MODEL),
                            lambda i: (jnp.minimum(i, tiles_p - 1) * tm // seq_p, 0, 0))
    sts_spec = pl.BlockSpec((per, 2, D_MODEL), lambda i: (jnp.maximum(i - tiles_p, 0), 0, 0))
    return pl.pallas_call(
        functools.partial(_conv_out_body, tile=tm, tiles_p=tiles_p, seq_p=seq_p, seq_s=seq_s),
        out_shape=[jax.ShapeDtypeStruct((n, D_MODEL), F32),
                   jax.ShapeDtypeStruct((n * ROW_TILE, LANES), F32),
                   jax.ShapeDtypeStruct((n, LANES), F32), jax.ShapeDtypeStruct((1, LANES), F32),
                   jax.ShapeDtypeStruct((2, n), F32)],
        grid=(n // tm,),
        in_specs=[row(D_MODEL), prev, stp_spec, sts_spec, row(D_MODEL), row(D_MODEL),
                  _const_spec((3, D_MODEL)),
                  _const_spec((D_MODEL, D_MODEL)), _const_spec((1, D_MODEL)),
                  _const_spec((D_MODEL, LANES)), _const_spec((D_MODEL, LANES)),
                  _const_spec((1, LANES)), _const_spec((tm, tm))],
        out_specs=[row(D_MODEL), pl.BlockSpec((tm * ROW_TILE, LANES), lambda i: (i, 0)), row(LANES),
                   _const_spec((1, LANES)), pl.BlockSpec((2, tm), lambda i: (0, i))],
        scratch_shapes=[pltpu.VMEM((1, LANES), F32)],
        compiler_params=_cparams("arbitrary"),
        name="conv_out",
    )(u, u, state_p, state_s, bg, h, cw, w16, gf, wrh, wrl, br, tstrict)


def _rel_bias_toeplitz(rel_bias, rows, cols, lead):
    period = rows + cols
    k = jnp.arange(period)
    d = jnp.where(k < cols, k, k - period)
    idx = jnp.clip(lead - d, -REL_CLIP, REL_CLIP) + REL_CLIP
    v = jnp.take(rel_bias.astype(F32), idx, axis=1)
    heads = v.shape[0]
    skew = jnp.tile(v, (1, rows))[:, :rows * (period - 1)].reshape(heads, rows, period - 1)
    return skew[:, :, :cols]


def _bias_prompt(rel_bias):
    r = jnp.arange(ATTN_TILE)[:, None]
    c = jnp.arange(ATTN_WINDOW)[None, :]
    j = c // CHUNK - r // CHUNK
    band = jnp.logical_and(j >= 0, j <= BAND_CHUNKS)
    bias = _rel_bias_toeplitz(rel_bias, ATTN_TILE, ATTN_WINDOW, BAND_CHUNKS * CHUNK)
    return jnp.where(band[None], bias, NEG_INF)


def _bias_sample(rel_bias, la, tn):
    bias = _rel_bias_toeplitz(rel_bias, tn, la + tn, la)
    return bias[:, :, :la], bias[:, :, la:]


def _router_weights(w_group, b_group, w_expert, b_expert):
    pad = LANES - N_EXPERTS - N_GROUPS
    w = jnp.concatenate([w_expert, w_group, jnp.zeros((D_MODEL, pad), F32)], axis=1)
    b = jnp.concatenate([b_expert, b_group, jnp.zeros((pad,), F32)])[None, :].astype(F32)
    hi, lo = _split2(w.astype(F32))
    return hi, lo, b


def kernel(x_prompt, x_sample, cache_a_k, cache_a_v, state_hgrn, state_conv, norm_mix, norm_ffn,
           w_in_ab, w_out_ab, q_norm, k_norm, rel_bias, hgrn_lb_logits, hgrn_out_norm, w_in_c,
           conv_w, w_out_c, w_group, b_group, w_expert, b_expert, w1, w3, w2):
    batch, seq, d = x_prompt.shape
    nseq_s, tn, _ = x_sample.shape
    la = cache_a_k.shape[2]
    n_p = batch * seq
    n_s = nseq_s * tn
    n = n_p + n_s
    keep = min(BAND_CHUNKS * CHUNK, seq)

    x_p = x_prompt.reshape(n_p, d)
    x_s = x_sample.reshape(n_s, d)
    lb_all = jnp.cumsum(jax.nn.softmax(hgrn_lb_logits.astype(F32), axis=0), axis=0)
    head_avg = jnp.kron(jnp.eye(N_HEADS, dtype=F32),
                        jnp.full((HEAD_DIM, HEAD_DIM), 1.0 / HEAD_DIM, F32)).astype(BF16)
    t = jnp.arange(ROUTE_TILE)
    tstrict = (t[None, :] < t[:, None]).astype(BF16)
    row1 = lambda v: v.astype(F32).reshape(1, -1)
    tile8 = lambda v: jnp.tile(v.astype(F32), N_HEADS).reshape(1, -1)

    l = 0
    (q16, kf, vf, k16, v16, lf, kb, vb, qb, gb) = _proj_ab(
        x_p, x_s, row1(norm_mix[0]), w_in_ab[l].astype(BF16), tile8(q_norm[l]), tile8(k_norm[l]),
        row1(lb_all[l]), head_avg, seq, keep)

    oa_p = _attn_prompt(q16, k16, v16, _bias_prompt(rel_bias[l]), batch, seq)
    bias_c, bias_n = _bias_sample(rel_bias[l], la, tn)
    oa_s = _attn_sample(q16, k16, v16, cache_a_k[l].reshape(nseq_s, la * N_HEADS, HEAD_DIM),
                        cache_a_v[l].reshape(nseq_s, la * N_HEADS, HEAD_DIM), bias_c, bias_n,
                        n_p, nseq_s, tn)

    zeros_state = jnp.zeros((batch, N_HEADS, HEAD_DIM, HEAD_DIM), F32)
    ob_p, st_p = _hgrn(qb, kb, vb, lf, zeros_state, 0, batch, seq, 64, 4)
    ob_s, st_s = _hgrn(qb, kb, vb, lf, jnp.swapaxes(state_hgrn[l].astype(F32), -1, -2),
                       n_p, nseq_s, tn, tn, 1)

    wrh, wrl, br = _router_weights(w_group[0], b_group[0], w_expert[0], b_expert[0])
    h1, xn1, route1, cnt1, code1 = _out_ab(x_p, x_s, oa_p, oa_s, ob_p, ob_s, gb, row1(hgrn_out_norm[l]),
                                    head_avg, w_out_ab[l].astype(BF16), row1(norm_ffn[0]),
                                    wrh, wrl, br, tstrict)
    h2, u, bg = _moe(xn1, h1, route1, code1, cnt1, 0, w1, w3, w2,
                     proj=(row1(norm_mix[1]), w_in_c[0].astype(BF16)))

    wrh, wrl, br = _router_weights(w_group[1], b_group[1], w_expert[1], b_expert[1])
    h3, xn2, route2, cnt2, code2 = _conv_out(u, bg, h2, jnp.zeros((batch, 2, d), F32),
                                      state_conv[0].astype(F32), conv_w[0].astype(F32),
                                      w_out_c[0].astype(BF16), row1(norm_ffn[1]), wrh, wrl, br,
                                      tstrict, seq, tn)
    out_p, out_s = _moe(xn2, h3, route2, code2, cnt2, 1, w1, w3, w2, n_p=n_p)

    y_prompt = out_p.reshape(batch, seq, d)
    y_sample = out_s.reshape(nseq_s, tn, d)
    n_kp = batch * keep
    heads = lambda a, b_, t_: a.reshape(b_, t_, N_HEADS, HEAD_DIM)
    nk_p = heads(kf[:n_kp], batch, keep)[None]
    nv_p = heads(vf[:n_kp], batch, keep)[None]
    kf_s = heads(kf[n_kp:n_kp + n_s], nseq_s, tn)
    vf_s = heads(vf[n_kp:n_kp + n_s], nseq_s, tn)
    nk_s = jnp.concatenate([cache_a_k[l].astype(F32), kf_s], axis=1)[:, -la:][None]
    nv_s = jnp.concatenate([cache_a_v[l].astype(F32), vf_s], axis=1)[:, -la:][None]
    nh_p = jnp.swapaxes(st_p, -1, -2)[None]
    nh_s = jnp.swapaxes(st_s, -1, -2)[None]
    nc_p = jnp.stack([u[(b + 1) * seq - 2:(b + 1) * seq] for b in range(batch)])[None]
    u_s = u[n_p:].reshape(nseq_s, tn, d)
    nc_s = jnp.concatenate([state_conv[0].astype(F32), u_s], axis=1)[:, -2:][None]
    return (y_prompt, y_sample, nk_p, nv_p, nk_s, nv_s, nh_p, nh_s, nc_p, nc_s)
```

```python
import functools

import jax
import jax.numpy as jnp
from jax import lax
from jax.experimental import pallas as pl
from jax.experimental.pallas import tpu as pltpu

F32 = jnp.float32
BF16 = jnp.bfloat16
I32 = jnp.int32

D_MODEL = 1024
CHUNK = 64
BAND_CHUNKS = 8
HEAD_DIM = 64
N_HEADS = 8
D_HALF = N_HEADS * HEAD_DIM
REL_CLIP = 128
HGRN_BLOCK = 16
N_GROUPS = 4
EXPERTS_PER_GROUP = 8
N_EXPERTS = N_GROUPS * EXPERTS_PER_GROUP
D_EXPERT = 256
RMS_EPS = 1e-6
NEG_INF = -1e30

LANES = 128
ROW_TILE = D_MODEL // LANES
TOKEN_TILE = 256
PROJ_TILE = 512
ROUTE_TILE = 512
ATTN_TILE = 256
ATTN_WINDOW = ATTN_TILE + BAND_CHUNKS * CHUNK
ATTN_UNIT = 2 * CHUNK
ATTN_SPAN = ATTN_UNIT + BAND_CHUNKS * CHUNK
HGRN_UNROLL = 4
ATTN_HEAD_GROUP = 2
MOE_TILE = 256
RANK_BITS = 16
RANK_SPAN = 1 << RANK_BITS
GATHER_UNROLL = 8
VMEM_LIMIT = 48 * 1024 * 1024


def _cparams(*sem):
    return pltpu.CompilerParams(dimension_semantics=sem, vmem_limit_bytes=VMEM_LIMIT)


def _const_spec(shape):
    nd = len(shape)
    return pl.BlockSpec(shape, lambda *_: (0,) * nd)


def _store_row_tiles(ref, val):
    rows = val.shape[0]
    for c in range(ROW_TILE):
        ref[pl.ds(c, rows, stride=ROW_TILE), :] = val[:, c * LANES:(c + 1) * LANES]


def _load_row_tiles(ref, rows):
    return jnp.concatenate([ref[pl.ds(c, rows, stride=ROW_TILE), :] for c in range(ROW_TILE)], axis=1)


def _rms(x, gain):
    ms = jnp.mean(x * x, axis=-1, keepdims=True)
    return (x * lax.rsqrt(ms + RMS_EPS)) * gain


def _split2(x):
    hi = x.astype(BF16)
    lo = (x - hi.astype(F32)).astype(BF16)
    return hi, lo


def _split3(x):
    p0 = x.astype(BF16)
    r = x - p0.astype(F32)
    p1 = r.astype(BF16)
    p2 = (r - p1.astype(F32)).astype(BF16)
    return p0, p1, p2


def _dot(a, b):
    return jnp.dot(a, b, preferred_element_type=F32)


def _dot_nt(a, b):
    return lax.dot_general(a, b, (((1,), (1,)), ((), ())), preferred_element_type=F32)


def _dot_tn(a, b):
    return lax.dot_general(a, b, (((0,), (0,)), ((), ())), preferred_element_type=F32)


def _head_mean_sq(v, bd):
    hi, lo = _split2(v * v)
    return _dot(hi, bd) + _dot(lo, bd)


def _group_specs(width, tiles_p, tm=TOKEN_TILE):
    return [pl.BlockSpec((tm, width), lambda i: (jnp.minimum(i, tiles_p - 1), 0)),
            pl.BlockSpec((tm, width), lambda i: (jnp.maximum(i - tiles_p, 0), 0))]


def _group_pick(p_ref, s_ref, tiles_p):
    return jnp.where(pl.program_id(0) < tiles_p, p_ref[...], s_ref[...])


def _proj_ab_body(xp_ref, xs_ref, g_ref, w_ref, qg_ref, kg_ref, lb_ref, bd_ref,
                  q_o, kf_o, vf_o, k16_o, v16_o, lf_o, kb_o, vb_o, qb_o, gb_o, *, tiles_p):
    xb = _rms(_group_pick(xp_ref, xs_ref, tiles_p), g_ref[...]).astype(BF16)
    bd = bd_ref[...]

    def seg(j):
        return _dot(xb, w_ref[:, j * D_HALF:(j + 1) * D_HALF])

    qa = seg(0)
    qn = qa * lax.rsqrt(_head_mean_sq(qa, bd) + RMS_EPS) * qg_ref[...]
    q_o[...] = (qn * (HEAD_DIM ** -0.5)).astype(BF16)
    ka = seg(1)
    kn = ka * lax.rsqrt(_head_mean_sq(ka, bd) + RMS_EPS) * kg_ref[...]
    kf_o[...] = kn
    k16_o[...] = kn.astype(BF16)
    va = seg(2)
    vf_o[...] = va
    v16_o[...] = va.astype(BF16)
    lb = lb_ref[...]
    f = lb + (1.0 - lb) * jax.nn.sigmoid(seg(3))
    lf_o[...] = jnp.log(f)
    kb_o[...] = 1.0 - f
    vb_o[...] = seg(4)
    qb_o[...] = jax.nn.silu(seg(5))
    gb_o[...] = seg(6)


def _proj_ab(x_p, x_s, gain, w16, qg, kg, lb, bd, seq, keep):
    tm = PROJ_TILE
    n_p, n_s = x_p.shape[0], x_s.shape[0]
    n = n_p + n_s
    tiles_p, tiles_seq, tiles_keep = n_p // tm, seq // tm, keep // tm
    kept_tiles = (n_p // seq) * tiles_keep + n_s // tm

    def keep_map(i):
        b, j = i // tiles_seq, i % tiles_seq
        prompt_slot = b * tiles_keep + jnp.maximum(j - (tiles_seq - tiles_keep), 0)
        return (jnp.where(i >= tiles_p, kept_tiles - n_s // tm + (i - tiles_p), prompt_slot), 0)

    row = pl.BlockSpec((tm, D_HALF), lambda i: (i, 0))
    kept = pl.BlockSpec((tm, D_HALF), keep_map)
    full = lambda dt: jax.ShapeDtypeStruct((n, D_HALF), dt)
    kept_shape = jax.ShapeDtypeStruct((kept_tiles * tm, D_HALF), F32)
    return pl.pallas_call(
        functools.partial(_proj_ab_body, tiles_p=tiles_p),
        out_shape=[full(BF16), kept_shape, kept_shape, full(BF16), full(BF16),
                   full(F32), full(F32), full(F32), full(F32), full(F32)],
        grid=(n // tm,),
        in_specs=_group_specs(D_MODEL, tiles_p, tm) + [
            _const_spec((1, D_MODEL)), _const_spec(w16.shape),
            _const_spec((1, D_HALF)), _const_spec((1, D_HALF)), _const_spec((1, D_HALF)),
            _const_spec((D_HALF, D_HALF))],
        out_specs=[row, kept, kept] + [row] * 7,
        compiler_params=_cparams("arbitrary"),
        name="proj_ab",
    )(x_p, x_s, gain, w16, qg, kg, lb, bd)


def _attn_prompt_body(q_ref, k0, k1, k2, v0, v1, v2, bias_ref, o_ref):
    i = pl.program_id(1)
    lim = jnp.maximum(2 - i, 0) * ATTN_TILE
    col = lax.broadcasted_iota(I32, (ATTN_UNIT, ATTN_SPAN), 1)
    cut = ATTN_SPAN - 2 * ATTN_TILE

    def span(parts, start):
        if start == 0:
            return (parts[0], parts[1], parts[2][:cut])
        return (parts[0][start:], parts[1], parts[2])

    def head_group(h0):
        units = []
        for h in range(h0, h0 + ATTN_HEAD_GROUP):
            hs = slice(h * HEAD_DIM, (h + 1) * HEAD_DIM)
            kh = (k0[:, hs], k1[:, hs], k2[:, hs])
            vh = (v0[:, hs], v1[:, hs], v2[:, hs])
            for r in range(ATTN_TILE // ATTN_UNIT):
                start = r * ATTN_UNIT
                units.append((h, hs, slice(r * ATTN_UNIT, (r + 1) * ATTN_UNIT), start,
                              span(kh, start), span(vh, start)))
        s = [jnp.concatenate([_dot_nt(q_ref[rows, hs], kp) for kp in ks], axis=1)
             for (_, hs, rows, _, ks, _) in units]
        s = [jnp.where(col + start < lim, NEG_INF,
                       s_u + bias_ref[h, rows, start:start + ATTN_SPAN])
             for s_u, (h, _, rows, start, _, _) in zip(s, units)]
        m = [jnp.max(s_u, axis=-1, keepdims=True) for s_u in s]
        p = [jnp.exp(s_u - m_u) for s_u, m_u in zip(s, m)]
        l = [jnp.sum(p_u, axis=-1, keepdims=True) for p_u in p]
        outs = []
        for p_u, (_, _, _, _, _, vs) in zip(p, units):
            p16 = p_u.astype(BF16)
            o, at = 0.0, 0
            for vp in vs:
                o = o + _dot(p16[:, at:at + vp.shape[0]], vp)
                at += vp.shape[0]
            outs.append(o)
        for o_u, l_u, (_, hs, rows, _, _, _) in zip(outs, l, units):
            o_ref[rows, hs] = (o_u / l_u).astype(BF16)

    for h0 in range(0, N_HEADS, ATTN_HEAD_GROUP):
        head_group(h0)


def _attn_prompt(q16, k16, v16, bias_full, batch, seq):
    tiles = seq // ATTN_TILE
    qspec = pl.BlockSpec((ATTN_TILE, D_HALF), lambda b, i: (b * tiles + i, 0))

    def kv(back):
        return pl.BlockSpec((ATTN_TILE, D_HALF),
                            lambda b, i: (b * tiles + jnp.maximum(i - back, 0), 0))

    return pl.pallas_call(
        _attn_prompt_body,
        out_shape=jax.ShapeDtypeStruct((batch * seq, D_HALF), BF16),
        grid=(batch, tiles),
        in_specs=[qspec, kv(2), kv(1), kv(0), kv(2), kv(1), kv(0),
                  _const_spec(bias_full.shape)],
        out_specs=qspec,
        compiler_params=_cparams("parallel", "parallel"),
        name="attn_prompt",
    )(q16, k16, k16, k16, v16, v16, v16, bias_full)


def _attn_sample_body(q_ref, kc_ref, vc_ref, kn_ref, vn_ref, bc_ref, bn_ref, o_ref):
    la = kc_ref.shape[1] // N_HEADS
    for h in range(N_HEADS):
        hs = slice(h * HEAD_DIM, (h + 1) * HEAD_DIM)
        kc = kc_ref[0, pl.ds(h, la, stride=N_HEADS), :].astype(BF16)
        vc = vc_ref[0, pl.ds(h, la, stride=N_HEADS), :].astype(BF16)
        qh = q_ref[:, hs]
        sc = _dot_nt(qh, kc) + bc_ref[h]
        sn = _dot_nt(qh, kn_ref[:, hs]) + bn_ref[h]
        m = jnp.maximum(jnp.max(sc, axis=-1, keepdims=True), jnp.max(sn, axis=-1, keepdims=True))
        pc = jnp.exp(sc - m)
        pn = jnp.exp(sn - m)
        l = jnp.sum(pc, axis=-1, keepdims=True) + jnp.sum(pn, axis=-1, keepdims=True)
        o = _dot(pc.astype(BF16), vc) + _dot(pn.astype(BF16), vn_ref[:, hs])
        o_ref[:, hs] = (o / l).astype(BF16)


def _attn_sample(q16, k16, v16, cache_k, cache_v, bias_c, bias_n, row0, nseq, tn):
    blk0 = row0 // tn
    la8 = cache_k.shape[1]
    new = pl.BlockSpec((tn, D_HALF), lambda b: (blk0 + b, 0))
    cache = pl.BlockSpec((1, la8, HEAD_DIM), lambda b: (b, 0, 0))
    return pl.pallas_call(
        _attn_sample_body,
        out_shape=jax.ShapeDtypeStruct((nseq * tn, D_HALF), BF16),
        grid=(nseq,),
        in_specs=[new, cache, cache, new, new, _const_spec(bias_c.shape), _const_spec(bias_n.shape)],
        out_specs=pl.BlockSpec((tn, D_HALF), lambda b: (b, 0)),
        compiler_params=_cparams("parallel"),
        name="attn_sample",
    )(q16, cache_k, cache_v, k16, v16, bias_c, bias_n)


def _hgrn_body(q_ref, k_ref, v_ref, lf_ref, s0_ref, tcat_ref, o_ref, sout_ref, st_scr,
               *, chunk, nchunks):
    nblk = chunk // HGRN_BLOCK

    @pl.when(pl.program_id(1) == 0)
    def _():
        st_scr[...] = s0_ref[0]

    row = lax.broadcasted_iota(I32, (chunk, D_HALF), 0)
    r2 = lax.broadcasted_iota(I32, (chunk, chunk), 0)
    c2 = lax.broadcasted_iota(I32, (chunk, chunk), 1)
    same_blk_causal = jnp.logical_and(r2 // HGRN_BLOCK == c2 // HGRN_BLOCK, c2 <= r2)

    def one_chunk(c, carry):
        sl = pl.ds(pl.multiple_of(c * chunk, chunk), chunk)
        q = q_ref[sl, :]
        k = k_ref[sl, :]
        v16 = v_ref[sl, :].astype(BF16)
        l0, l1, l2 = _split3(lf_ref[sl, :])
        tcat = tcat_ref[...]
        both = _dot(tcat, l0) + _dot(tcat, l1) + _dot(tcat, l2)
        b_in = both[:chunk]
        b_ch = both[chunk:]
        ld = (q * jnp.exp(b_in)).astype(BF16)
        rd = (k * jnp.exp(-b_in)).astype(BF16)
        lj, rj = [], []
        for j in range(nblk - 1):
            e_j = b_ch[(j + 1) * HGRN_BLOCK - 1:(j + 1) * HGRN_BLOCK, :]
            later = row >= (j + 1) * HGRN_BLOCK
            inside = jnp.logical_and(row >= j * HGRN_BLOCK, row < (j + 1) * HGRN_BLOCK)
            lj.append(jnp.where(later, q * jnp.exp(jnp.minimum(b_ch - e_j, 0.0)), 0.0).astype(BF16))
            rj.append(jnp.where(inside, k * jnp.exp(jnp.minimum(e_j - b_ch, 0.0)), 0.0).astype(BF16))
        e_end = b_ch[chunk - 1:chunk, :]
        qc = (q * jnp.exp(b_ch)).astype(BF16)
        kc = (k * jnp.exp(e_end - b_ch)).astype(BF16)
        dec = jnp.exp(e_end)
        heads = [slice(h * HEAD_DIM, (h + 1) * HEAD_DIM) for h in range(N_HEADS)]
        sts = [st_scr[h] for h in range(N_HEADS)]
        diag = [_dot_nt(ld[:, hs], rd[:, hs]) for hs in heads]
        off = []
        for hs in heads:
            acc = None
            for j in range(nblk - 1):
                term = _dot_nt(lj[j][:, hs], rj[j][:, hs])
                acc = term if acc is None else acc + term
            off.append(acc)
        inter = [_dot_nt(qc[:, hs], st.astype(BF16)) for hs, st in zip(heads, sts)]
        upd = [_dot_tn(v16[:, hs], kc[:, hs]) for hs in heads]
        sc16 = [(jnp.where(same_blk_causal, d, 0.0) + o_).astype(BF16) for d, o_ in zip(diag, off)]
        intra = [_dot(s_, v16[:, hs]) for s_, hs in zip(sc16, heads)]
        o_ref[sl, :] = jnp.concatenate([a + b for a, b in zip(intra, inter)], axis=1)
        for h, hs in enumerate(heads):
            st_scr[h] = sts[h] * dec[:, hs] + upd[h]
        return carry

    lax.fori_loop(0, nchunks, one_chunk, 0, unroll=min(nchunks, HGRN_UNROLL))
    sout_ref[0] = st_scr[...]


def _hgrn(qb, kb, vb, lf, s0t, row0, nseq, seq, chunk, nchunks):
    tt = chunk * nchunks
    steps = seq // tt
    blk0 = row0 // tt
    t = jnp.arange(chunk)
    lower = t[None, :] <= t[:, None]
    same_blk = (t[None, :] // HGRN_BLOCK) == (t[:, None] // HGRN_BLOCK)
    tcat = jnp.concatenate([jnp.logical_and(lower, same_blk), lower], axis=0).astype(BF16)
    tok = pl.BlockSpec((tt, D_HALF), lambda b, j: (blk0 + b * steps + j, 0))
    state = pl.BlockSpec((1, N_HEADS, HEAD_DIM, HEAD_DIM), lambda b, j: (b, 0, 0, 0))
    return pl.pallas_call(
        functools.partial(_hgrn_body, chunk=chunk, nchunks=nchunks),
        out_shape=[jax.ShapeDtypeStruct((nseq * seq, D_HALF), F32),
                   jax.ShapeDtypeStruct(s0t.shape, F32)],
        grid=(nseq, steps),
        in_specs=[tok, tok, tok, tok, state, _const_spec((2 * chunk, chunk))],
        out_specs=[pl.BlockSpec((tt, D_HALF), lambda b, j: (b * steps + j, 0)), state],
        scratch_shapes=[pltpu.VMEM((N_HEADS, HEAD_DIM, HEAD_DIM), F32)],
        compiler_params=_cparams("parallel", "arbitrary"),
        name="hgrn",
    )(qb, kb, vb, lf, s0t, tcat)


def _route(xn, wr_hi, wr_lo, br, tstrict, cnt_scr, route_o, cnt_o, code_o):
    rows = xn.shape[0]
    x_hi, x_lo = _split2(xn)
    logits = _dot(x_hi, wr_hi) + _dot(x_lo, wr_hi) + _dot(x_hi, wr_lo) + br
    lane = lax.broadcasted_iota(I32, (rows, LANES), 1)
    lane_f = lane.astype(F32)

    def first_argmax(vals):
        m = jnp.max(vals, axis=-1, keepdims=True)
        idx = jnp.min(jnp.where(vals == m, lane_f, float(LANES)), axis=-1, keepdims=True)
        return m, idx.astype(I32)

    is_grp = jnp.logical_and(lane >= N_EXPERTS, lane < N_EXPERTS + N_GROUPS)
    gl = jnp.where(is_grp, logits, -jnp.inf)
    gmax, gidx = first_argmax(gl)
    p_grp = 1.0 / jnp.sum(jnp.exp(gl - gmax), axis=-1, keepdims=True)
    grp = gidx - N_EXPERTS
    in_grp = jnp.logical_and(lane < N_EXPERTS, lane // EXPERTS_PER_GROUP == grp)
    el = jnp.where(in_grp, logits, -jnp.inf)
    v1, e1 = first_argmax(el)
    v2, e2 = first_argmax(jnp.where(lane == e1, -jnp.inf, el))
    t2 = jnp.exp(v2 - v1)
    den = 1.0 + t2
    g1 = (1.0 / den) * p_grp
    g2 = (t2 / den) * p_grp

    oh1 = lane == e1
    oh2 = lane == e2
    oh = jnp.where(jnp.logical_or(oh1, oh2), 1.0, 0.0)
    before = _dot(tstrict, oh.astype(BF16)) + cnt_scr[...]
    rank1 = jnp.sum(jnp.where(oh1, before, 0.0), axis=-1, keepdims=True)
    rank2 = jnp.sum(jnp.where(oh2, before, 0.0), axis=-1, keepdims=True)
    cnt_scr[...] = cnt_scr[...] + jnp.sum(oh, axis=0, keepdims=True)
    cnt_o[...] = cnt_scr[...]

    e1f, e2f = e1.astype(F32), e2.astype(F32)
    fields = (e1f, e2f, g1, g2, rank1, rank2, e1f * RANK_SPAN + rank1, e2f * RANK_SPAN + rank2)
    slab = jnp.zeros((rows, LANES), F32)
    for idx, val in enumerate(fields):
        slab = jnp.where(lane == idx, val, slab)
    route_o[...] = slab
    code_o[...] = slab.T[6:8, :]


def _out_ab_body(hp_ref, hs_ref, oap_ref, oas_ref, obp_ref, obs_ref, gb_ref, og_ref, bd_ref, w_ref,
                 gf_ref, wrh_ref, wrl_ref, br_ref, ts_ref, h_o, xn_o, route_o, cnt_o, code_o, cnt_scr,
                 *, tiles_p):
    @pl.when(pl.program_id(0) == 0)
    def _():
        cnt_scr[...] = jnp.zeros_like(cnt_scr)

    ob = _group_pick(obp_ref, obs_ref, tiles_p)
    obn = ob * lax.rsqrt(_head_mean_sq(ob, bd_ref[...]) + RMS_EPS) * og_ref[...]
    obg = (obn * jax.nn.silu(gb_ref[...])).astype(BF16)
    oa = _group_pick(oap_ref, oas_ref, tiles_p)
    mix = _dot(oa, w_ref[0:D_HALF, :]) + _dot(obg, w_ref[D_HALF:, :])
    h1 = _group_pick(hp_ref, hs_ref, tiles_p) + mix
    h_o[...] = h1
    xn = _rms(h1, gf_ref[...])
    _store_row_tiles(xn_o, xn)
    _route(xn, wrh_ref[...], wrl_ref[...], br_ref[...], ts_ref[...], cnt_scr, route_o, cnt_o, code_o)


def _out_ab(h_p, h_s, oa_p, oa_s, ob_p, ob_s, gb, og, bd, w16, gf, wrh, wrl, br, tstrict):
    tm = ROUTE_TILE
    n = h_p.shape[0] + h_s.shape[0]
    tiles_p = h_p.shape[0] // tm
    row = lambda w: pl.BlockSpec((tm, w), lambda i: (i, 0))
    return pl.pallas_call(
        functools.partial(_out_ab_body, tiles_p=tiles_p),
        out_shape=[jax.ShapeDtypeStruct((n, D_MODEL), F32), jax.ShapeDtypeStruct((n * ROW_TILE, LANES), F32),
                   jax.ShapeDtypeStruct((n, LANES), F32), jax.ShapeDtypeStruct((1, LANES), F32),
                   jax.ShapeDtypeStruct((2, n), F32)],
        grid=(n // tm,),
        in_specs=_group_specs(D_MODEL, tiles_p, tm) + _group_specs(D_HALF, tiles_p, tm)
        + _group_specs(D_HALF, tiles_p, tm) + [
            row(D_HALF), _const_spec((1, D_HALF)),
            _const_spec((D_HALF, D_HALF)), _const_spec((D_MODEL, D_MODEL)),
            _const_spec((1, D_MODEL)), _const_spec((D_MODEL, LANES)),
            _const_spec((D_MODEL, LANES)), _const_spec((1, LANES)), _const_spec((tm, tm))],
        out_specs=[row(D_MODEL), pl.BlockSpec((tm * ROW_TILE, LANES), lambda i: (i, 0)), row(LANES),
                   _const_spec((1, LANES)), pl.BlockSpec((2, tm), lambda i: (0, i))],
        scratch_shapes=[pltpu.VMEM((1, LANES), F32)],
        compiler_params=_cparams("arbitrary"),
        name="out_ab",
    )(h_p, h_s, oa_p, oa_s, ob_p, ob_s, gb, og, bd, w16, gf, wrh, wrl, br, tstrict)


def _row_copy(src, src_row, dst, dst_row, sem):
    s0 = pl.multiple_of(src_row * ROW_TILE, ROW_TILE)
    d0 = pl.multiple_of(dst_row * ROW_TILE, ROW_TILE)
    return pltpu.make_async_copy(src.at[pl.ds(s0, ROW_TILE)], dst.at[pl.ds(d0, ROW_TILE)], sem)


def _collect_body(pos_ref, ys_hbm, h_ref, r_ref, *rest, tile, n_tok, tiles_p, proj):
    if proj:
        g_ref, w_ref = rest[:2]
        rest = rest[2:]
    outs, (buf, sems) = rest[:-2], rest[-2:]
    i = pl.program_id(0)
    nt = pl.num_programs(0)

    def issue(t):
        slot = t % 2
        base = t * tile

        def one(r, carry):
            tok = base + r
            for k in range(2):
                row = pos_ref[k * n_tok + tok]
                _row_copy(ys_hbm, row, buf.at[slot].at[k], r, sems.at[slot]).start(priority=k)
            return carry

        lax.fori_loop(0, tile, one, 0, unroll=GATHER_UNROLL)

    @pl.when(i == 0)
    def _():
        issue(0)

    @pl.when(i + 1 < nt)
    def _():
        issue(i + 1)

    slot = i % 2
    for _ in range(2 * tile):
        _row_copy(ys_hbm, 0, buf.at[slot].at[0], 0, sems.at[slot]).wait()
    route = r_ref[...]
    ya = _load_row_tiles(buf.at[slot].at[0], tile)
    yb = _load_row_tiles(buf.at[slot].at[1], tile)
    out = h_ref[...] + (route[:, 2:3] * ya + route[:, 3:4] * yb)
    if proj:
        h_o, u_o, bg_o = outs
        h_o[...] = out
        xb = _rms(out, g_ref[...]).astype(BF16)
        bg_o[...] = _dot(xb, w_ref[:, 0:D_MODEL])
        u_o[...] = _dot(xb, w_ref[:, D_MODEL:2 * D_MODEL]) * _dot(xb, w_ref[:, 2 * D_MODEL:])
    else:
        @pl.when(i < tiles_p)
        def _():
            outs[0][...] = out

        @pl.when(i >= tiles_p)
        def _():
            outs[1][...] = out


def _collect(pos, ys, h, route, *, proj=None, n_p=None):
    tm = TOKEN_TILE
    n = h.shape[0]
    row = lambda w: pl.BlockSpec((tm, w), lambda i, pos: (i, 0))
    const = lambda shape: pl.BlockSpec(shape, lambda i, pos: (0,) * len(shape))
    if proj is not None:
        tiles_p = None
        extra_in, extra_specs = list(proj), [const(proj[0].shape), const(proj[1].shape)]
        out_shape = [jax.ShapeDtypeStruct((n, D_MODEL), F32)] * 3
        out_specs = [row(D_MODEL)] * 3
    else:
        tiles_p = n_p // tm
        extra_in, extra_specs = [], []
        out_shape = [jax.ShapeDtypeStruct((n_p, D_MODEL), F32),
                     jax.ShapeDtypeStruct((n - n_p, D_MODEL), F32)]
        out_specs = [pl.BlockSpec((tm, D_MODEL), lambda i, pos: (jnp.minimum(i, tiles_p - 1), 0)),
                     pl.BlockSpec((tm, D_MODEL), lambda i, pos: (jnp.maximum(i - tiles_p, 0), 0))]
    return pl.pallas_call(
        functools.partial(_collect_body, tile=tm, n_tok=n, tiles_p=tiles_p, proj=proj is not None),
        out_shape=out_shape,
        grid_spec=pltpu.PrefetchScalarGridSpec(
            num_scalar_prefetch=1,
            grid=(n // tm,),
            in_specs=[pl.BlockSpec(memory_space=pl.ANY), row(D_MODEL), row(LANES)] + extra_specs,
            out_specs=out_specs,
            scratch_shapes=[pltpu.VMEM((2, 2, tm * ROW_TILE, LANES), F32),
                            pltpu.SemaphoreType.DMA((2,))],
        ),
        compiler_params=pltpu.CompilerParams(dimension_semantics=("arbitrary",),
                                             vmem_limit_bytes=VMEM_LIMIT),
        name="moe_collect",
    )(pos, ys, h, route, *extra_in)


def _scatter_rows_body(pos_ref, pad0_ref, padn_ref, nt_ref, x_ref, xs_hbm,
                       stage, zero_scr, sems, pad_sem, *, tile, n_tok, out_tiles):
    i = pl.program_id(0)
    last = pl.num_programs(0) - 1
    base = i * tile
    slot = i % 2

    def drain(s):
        for _ in range(2 * tile):
            _row_copy(stage.at[s], 0, xs_hbm, 0, sems.at[s]).wait()

    @pl.when(i >= 2)
    def _():
        drain(slot)

    stage[slot] = x_ref[...]

    def issue(r, carry):
        tok = base + r
        src = stage.at[slot]
        for k in range(2):
            _row_copy(src, r, xs_hbm, pos_ref[k * n_tok + tok], sems.at[slot]).start(priority=k)
        return carry

    lax.fori_loop(0, tile, issue, 0, unroll=GATHER_UNROLL)

    @pl.when(i == last)
    def _():
        zero_scr[...] = jnp.zeros_like(zero_scr)

        def fill(lo, count):
            def one(r, carry):
                _row_copy(zero_scr, 0, xs_hbm, lo + r, pad_sem).start()
                return carry

            def done(r, carry):
                _row_copy(zero_scr, 0, xs_hbm, 0, pad_sem).wait()
                return carry

            lax.fori_loop(0, count, one, 0)
            lax.fori_loop(0, count, done, 0)

        for e in range(N_EXPERTS):
            fill(pad0_ref[e], padn_ref[e])

        def tail_copy(t):
            rows = MOE_TILE * ROW_TILE
            return pltpu.make_async_copy(
                zero_scr, xs_hbm.at[pl.ds(pl.multiple_of(t * rows, rows), rows)], pad_sem)

        def tail_start(t, carry):
            tail_copy(t).start()
            return carry

        def tail_done(t, carry):
            tail_copy(t).wait()
            return carry

        lax.fori_loop(nt_ref[0], out_tiles, tail_start, 0)
        lax.fori_loop(nt_ref[0], out_tiles, tail_done, 0)
        drain(slot)

    @pl.when(jnp.logical_and(i == last, i >= 1))
    def _():
        drain(1 - slot)


def _scatter_rows(pos, pad0, padn, n_tiles, xn, n_rows):
    tm = TOKEN_TILE
    n = xn.shape[0] // ROW_TILE
    return pl.pallas_call(
        functools.partial(_scatter_rows_body, tile=tm, n_tok=n, out_tiles=n_rows // MOE_TILE),
        out_shape=jax.ShapeDtypeStruct((n_rows * ROW_TILE, LANES), F32),
        grid_spec=pltpu.PrefetchScalarGridSpec(
            num_scalar_prefetch=4,
            grid=(n // tm,),
            in_specs=[pl.BlockSpec((tm * ROW_TILE, LANES), lambda i, *_: (i, 0))],
            out_specs=pl.BlockSpec(memory_space=pl.ANY),
            scratch_shapes=[pltpu.VMEM((2, tm * ROW_TILE, LANES), F32),
                            pltpu.VMEM((MOE_TILE * ROW_TILE, LANES), F32),
                            pltpu.SemaphoreType.DMA((2,)), pltpu.SemaphoreType.DMA(())],
        ),
        compiler_params=pltpu.CompilerParams(dimension_semantics=("arbitrary",)),
        name="moe_scatter",
    )(pos, pad0, padn, n_tiles, xn)


def _experts_body(te_ref, nt_ref, x_ref, w1_ref, w3_ref, w2_ref, y_ref, w1_s, w3_s, w2_s):
    i = pl.program_id(0)
    prev = te_ref[jnp.maximum(i - 1, 0)]
    fresh = jnp.logical_or(i == 0, te_ref[i] != prev)

    @pl.when(fresh)
    def _():
        w1_s[...] = w1_ref[0].astype(BF16)
        w3_s[...] = w3_ref[0].astype(BF16)
        w2_s[...] = w2_ref[0].astype(BF16)

    @pl.when(i < nt_ref[0])
    def _():
        x = _load_row_tiles(x_ref, MOE_TILE).astype(BF16)
        a = _dot(x, w1_s[...])
        b = _dot(x, w3_s[...])
        _store_row_tiles(y_ref, _dot((jax.nn.silu(a) * b).astype(BF16), w2_s[...]))


def _experts(tile_expert, n_tiles, xs, w1, w3, w2):
    rows = xs.shape[0] // ROW_TILE
    tm = MOE_TILE
    tile = pl.BlockSpec((tm * ROW_TILE, LANES), lambda i, te, nt: (jnp.minimum(i, nt[0] - 1), 0))
    return pl.pallas_call(
        _experts_body,
        out_shape=jax.ShapeDtypeStruct((rows * ROW_TILE, LANES), F32),
        grid_spec=pltpu.PrefetchScalarGridSpec(
            num_scalar_prefetch=2,
            grid=(rows // tm,),
            in_specs=[tile,
                      pl.BlockSpec((1, D_MODEL, D_EXPERT), lambda i, te, nt: (te[i], 0, 0)),
                      pl.BlockSpec((1, D_MODEL, D_EXPERT), lambda i, te, nt: (te[i], 0, 0)),
                      pl.BlockSpec((1, D_EXPERT, D_MODEL), lambda i, te, nt: (te[i], 0, 0))],
            out_specs=tile,
            scratch_shapes=[pltpu.VMEM((D_MODEL, D_EXPERT), BF16), pltpu.VMEM((D_MODEL, D_EXPERT), BF16),
                            pltpu.VMEM((D_EXPERT, D_MODEL), BF16)],
        ),
        input_output_aliases={2: 0},
        compiler_params=_cparams("arbitrary"),
        name="moe_experts",
    )(tile_expert, n_tiles, xs, w1, w3, w2)


def _moe_plan(counts, n):
    counts = counts[0, :N_EXPERTS].astype(I32)
    padded = ((counts + MOE_TILE - 1) // MOE_TILE) * MOE_TILE
    ends = jnp.cumsum(padded)
    offs = ends - padded
    max_tiles = (2 * n + N_EXPERTS * (MOE_TILE - 1) + MOE_TILE - 1) // MOE_TILE
    tile_start = jnp.arange(max_tiles, dtype=I32) * MOE_TILE
    tile_expert = jnp.minimum(jnp.sum((tile_start[:, None] >= ends[None, :]).astype(I32), axis=1),
                              N_EXPERTS - 1)
    n_tiles = (ends[-1] // MOE_TILE).reshape(1)
    return offs, offs + counts, padded - counts, tile_expert, n_tiles, max_tiles * MOE_TILE


def _moe(xn, h, route, code, counts, layer, w1, w3, w2, **collect_mode):
    n = h.shape[0]
    code = code.astype(I32).reshape(2 * n)
    offs, pad0, padn, tile_expert, n_tiles, n_rows = _moe_plan(counts, n)
    hit = (code >> RANK_BITS)[None, :] == jnp.arange(N_EXPERTS, dtype=I32)[:, None]
    pos = jnp.sum(jnp.where(hit, offs[:, None], 0), axis=0) + (code & (RANK_SPAN - 1))
    xs = _scatter_rows(pos, pad0, padn, n_tiles, xn, n_rows)
    flat = lambda w: w.reshape((-1,) + w.shape[2:])
    ys = _experts(tile_expert + layer * N_EXPERTS, n_tiles, xs, flat(w1), flat(w3), flat(w2))
    return _collect(pos, ys, h, route, **collect_mode)


def _conv_out_body(u_ref, up_ref, stp_ref, sts_ref, bg_ref, h_ref, cw_ref, w_ref, gf_ref,
                   wrh_ref, wrl_ref, br_ref, ts_ref, h_o, xn_o, route_o, cnt_o, code_o, cnt_scr,
                   *, tile, tiles_p, seq_p, seq_s):
    i = pl.program_id(0)

    @pl.when(i == 0)
    def _():
        cnt_scr[...] = jnp.zeros_like(cnt_scr)

    u = u_ref[...]
    rowi = lax.broadcasted_iota(I32, (tile, D_MODEL), 0)
    is_p = i < tiles_p
    at_start = (i * tile) % seq_p == 0
    stp = stp_ref[0]
    prev = up_ref[...]
    m2_p = jnp.where(at_start, stp[0:1, :], prev[6:7, :])
    m1_p = jnp.where(at_start, stp[1:2, :], prev[7:8, :])
    per = tile // seq_s
    sts = sts_ref[...]
    m2_s = jnp.broadcast_to(sts[:, 0:1, :], (per, seq_s, D_MODEL)).reshape(tile, D_MODEL)
    m1_s = jnp.broadcast_to(sts[:, 1:2, :], (per, seq_s, D_MODEL)).reshape(tile, D_MODEL)
    m2 = jnp.where(is_p, m2_p, m2_s)
    m1 = jnp.where(is_p, m1_p, m1_s)
    pos = jnp.where(is_p, rowi, rowi % seq_s)
    u1 = jnp.where(pos == 0, m1, pltpu.roll(u, 1, axis=0))
    u2 = jnp.where(pos == 0, m2, jnp.where(pos == 1, m1, pltpu.roll(u, 2, axis=0)))
    cw = cw_ref[...]
    conv = u2 * cw[0:1, :] + u1 * cw[1:2, :] + u * cw[2:3, :]
    mix = _dot((bg_ref[...] * conv).astype(BF16), w_ref[...])
    h3 = h_ref[...] + mix
    h_o[...] = h3
    xn = _rms(h3, gf_ref[...])
    _store_row_tiles(xn_o, xn)
    _route(xn, wrh_ref[...], wrl_ref[...], br_ref[...], ts_ref[...], cnt_scr, route_o, cnt_o, code_o)


def _conv_out(u, bg, h, state_p, state_s, cw, w16, gf, wrh, wrl, br, tstrict, seq_p, seq_s):
    tm = ROUTE_TILE
    n = u.shape[0]
    tiles_p = state_p.shape[0] * seq_p // tm
    per = tm // seq_s
    row = lambda w: pl.BlockSpec((tm, w), lambda i: (i, 0))
    prev = pl.BlockSpec((8, D_MODEL), lambda i: (jnp.maximum(i * (tm // 8) - 1, 0), 0))
    stp_spec = pl.BlockSpec((1, 2, D_MODEL),
                            lambda i: (jnp.minimum(i, tiles_p - 1) * tm // seq_p, 0, 0))
    sts_spec = pl.BlockSpec((per, 2, D_MODEL), lambda i: (jnp.maximum(i - tiles_p, 0), 0, 0))
    return pl.pallas_call(
        functools.partial(_conv_out_body, tile=tm, tiles_p=tiles_p, seq_p=seq_p, seq_s=seq_s),
        out_shape=[jax.ShapeDtypeStruct((n, D_MODEL), F32),
                   jax.ShapeDtypeStruct((n * ROW_TILE, LANES), F32),
                   jax.ShapeDtypeStruct((n, LANES), F32), jax.ShapeDtypeStruct((1, LANES), F32),
                   jax.ShapeDtypeStruct((2, n), F32)],
        grid=(n // tm,),
        in_specs=[row(D_MODEL), prev, stp_spec, sts_spec, row(D_MODEL), row(D_MODEL),
                  _const_spec((3, D_MODEL)),
                  _const_spec((D_MODEL, D_MODEL)), _const_spec((1, D_MODEL)),
                  _const_spec((D_MODEL, LANES)), _const_spec((D_MODEL, LANES)),
                  _const_spec((1, LANES)), _const_spec((tm, tm))],
        out_specs=[row(D_MODEL), pl.BlockSpec((tm * ROW_TILE, LANES), lambda i: (i, 0)), row(LANES),
                   _const_spec((1, LANES)), pl.BlockSpec((2, tm), lambda i: (0, i))],
        scratch_shapes=[pltpu.VMEM((1, LANES), F32)],
        compiler_params=_cparams("arbitrary"),
        name="conv_out",
    )(u, u, state_p, state_s, bg, h, cw, w16, gf, wrh, wrl, br, tstrict)


def _rel_bias_toeplitz(rel_bias, rows, cols, lead):
    period = rows + cols
    k = jnp.arange(period)
    d = jnp.where(k < cols, k, k - period)
    idx = jnp.clip(lead - d, -REL_CLIP, REL_CLIP) + REL_CLIP
    v = jnp.take(rel_bias.astype(F32), idx, axis=1)
    heads = v.shape[0]
    skew = jnp.tile(v, (1, rows))[:, :rows * (period - 1)].reshape(heads, rows, period - 1)
    return skew[:, :, :cols]


def _bias_prompt(rel_bias):
    r = jnp.arange(ATTN_TILE)[:, None]
    c = jnp.arange(ATTN_WINDOW)[None, :]
    j = c // CHUNK - r // CHUNK
    band = jnp.logical_and(j >= 0, j <= BAND_CHUNKS)
    bias = _rel_bias_toeplitz(rel_bias, ATTN_TILE, ATTN_WINDOW, BAND_CHUNKS * CHUNK)
    return jnp.where(band[None], bias, NEG_INF)


def _bias_sample(rel_bias, la, tn):
    bias = _rel_bias_toeplitz(rel_bias, tn, la + tn, la)
    return bias[:, :, :la], bias[:, :, la:]


def _router_weights(w_group, b_group, w_expert, b_expert):
    pad = LANES - N_EXPERTS - N_GROUPS
    w = jnp.concatenate([w_expert, w_group, jnp.zeros((D_MODEL, pad), F32)], axis=1)
    b = jnp.concatenate([b_expert, b_group, jnp.zeros((pad,), F32)])[None, :].astype(F32)
    hi, lo = _split2(w.astype(F32))
    return hi, lo, b


def kernel(x_prompt, x_sample, cache_a_k, cache_a_v, state_hgrn, state_conv, norm_mix, norm_ffn,
           w_in_ab, w_out_ab, q_norm, k_norm, rel_bias, hgrn_lb_logits, hgrn_out_norm, w_in_c,
           conv_w, w_out_c, w_group, b_group, w_expert, b_expert, w1, w3, w2):
    batch, seq, d = x_prompt.shape
    nseq_s, tn, _ = x_sample.shape
    la = cache_a_k.shape[2]
    n_p = batch * seq
    n_s = nseq_s * tn
    n = n_p + n_s
    keep = min(BAND_CHUNKS * CHUNK, seq)

    x_p = x_prompt.reshape(n_p, d)
    x_s = x_sample.reshape(n_s, d)
    lb_all = jnp.cumsum(jax.nn.softmax(hgrn_lb_logits.astype(F32), axis=0), axis=0)
    head_avg = jnp.kron(jnp.eye(N_HEADS, dtype=F32),
                        jnp.full((HEAD_DIM, HEAD_DIM), 1.0 / HEAD_DIM, F32)).astype(BF16)
    t = jnp.arange(ROUTE_TILE)
    tstrict = (t[None, :] < t[:, None]).astype(BF16)
    row1 = lambda v: v.astype(F32).reshape(1, -1)
    tile8 = lambda v: jnp.tile(v.astype(F32), N_HEADS).reshape(1, -1)

    l = 0
    (q16, kf, vf, k16, v16, lf, kb, vb, qb, gb) = _proj_ab(
        x_p, x_s, row1(norm_mix[0]), w_in_ab[l].astype(BF16), tile8(q_norm[l]), tile8(k_norm[l]),
        row1(lb_all[l]), head_avg, seq, keep)

    oa_p = _attn_prompt(q16, k16, v16, _bias_prompt(rel_bias[l]), batch, seq)
    bias_c, bias_n = _bias_sample(rel_bias[l], la, tn)
    oa_s = _attn_sample(q16, k16, v16, cache_a_k[l].reshape(nseq_s, la * N_HEADS, HEAD_DIM),
                        cache_a_v[l].reshape(nseq_s, la * N_HEADS, HEAD_DIM), bias_c, bias_n,
                        n_p, nseq_s, tn)

    zeros_state = jnp.zeros((batch, N_HEADS, HEAD_DIM, HEAD_DIM), F32)
    ob_p, st_p = _hgrn(qb, kb, vb, lf, zeros_state, 0, batch, seq, 64, 4)
    ob_s, st_s = _hgrn(qb, kb, vb, lf, jnp.swapaxes(state_hgrn[l].astype(F32), -1, -2),
                       n_p, nseq_s, tn, tn, 1)

    wrh, wrl, br = _router_weights(w_group[0], b_group[0], w_expert[0], b_expert[0])
    h1, xn1, route1, cnt1, code1 = _out_ab(x_p, x_s, oa_p, oa_s, ob_p, ob_s, gb, row1(hgrn_out_norm[l]),
                                    head_avg, w_out_ab[l].astype(BF16), row1(norm_ffn[0]),
                                    wrh, wrl, br, tstrict)
    h2, u, bg = _moe(xn1, h1, route1, code1, cnt1, 0, w1, w3, w2,
                     proj=(row1(norm_mix[1]), w_in_c[0].astype(BF16)))

    wrh, wrl, br = _router_weights(w_group[1], b_group[1], w_expert[1], b_expert[1])
    h3, xn2, route2, cnt2, code2 = _conv_out(u, bg, h2, jnp.zeros((batch, 2, d), F32),
                                      state_conv[0].astype(F32), conv_w[0].astype(F32),
                                      w_out_c[0].astype(BF16), row1(norm_ffn[1]), wrh, wrl, br,
                                      tstrict, seq, tn)
    out_p, out_s = _moe(xn2, h3, route2, code2, cnt2, 1, w1, w3, w2, n_p=n_p)

    y_prompt = out_p.reshape(batch, seq, d)
    y_sample = out_s.reshape(nseq_s, tn, d)
    n_kp = batch * keep
    heads = lambda a, b_, t_: a.reshape(b_, t_, N_HEADS, HEAD_DIM)
    nk_p = heads(kf[:n_kp], batch, keep)[None]
    nv_p = heads(vf[:n_kp], batch, keep)[None]
    kf_s = heads(kf[n_kp:n_kp + n_s], nseq_s, tn)
    vf_s = heads(vf[n_kp:n_kp + n_s], nseq_s, tn)
    nk_s = jnp.concatenate([cache_a_k[l].astype(F32), kf_s], axis=1)[:, -la:][None]
    nv_s = jnp.concatenate([cache_a_v[l].astype(F32), vf_s], axis=1)[:, -la:][None]
    nh_p = jnp.swapaxes(st_p, -1, -2)[None]
    nh_s = jnp.swapaxes(st_s, -1, -2)[None]
    nc_p = jnp.stack([u[(b + 1) * seq - 2:(b + 1) * seq] for b in range(batch)])[None]
    u_s = u[n_p:].reshape(nseq_s, tn, d)
    nc_s = jnp.concatenate([state_conv[0].astype(F32), u_s], axis=1)[:, -2:][None]
    return (y_prompt, y_sample, nk_p, nv_p, nk_s, nv_s, nh_p, nh_s, nc_p, nc_s)
```

```python
import functools

import jax
import jax.numpy as jnp
from jax import lax
from jax.experimental import pallas as pl
from jax.experimental.pallas import tpu as pltpu

F32 = jnp.float32
BF16 = jnp.bfloat16
I32 = jnp.int32

D_MODEL = 1024
CHUNK = 64
BAND_CHUNKS = 8
HEAD_DIM = 64
N_HEADS = 8
D_HALF = N_HEADS * HEAD_DIM
REL_CLIP = 128
HGRN_BLOCK = 16
N_GROUPS = 4
EXPERTS_PER_GROUP = 8
N_EXPERTS = N_GROUPS * EXPERTS_PER_GROUP
D_EXPERT = 256
RMS_EPS = 1e-6
NEG_INF = -1e30

LANES = 128
ROW_TILE = D_MODEL // LANES
TOKEN_TILE = 256
PROJ_TILE = 512
ROUTE_TILE = 512
ATTN_TILE = 256
ATTN_WINDOW = ATTN_TILE + BAND_CHUNKS * CHUNK
ATTN_UNIT = 2 * CHUNK
ATTN_SPAN = ATTN_UNIT + BAND_CHUNKS * CHUNK
HGRN_UNROLL = 4
ATTN_HEAD_GROUP = 2
MOE_TILE = 256
RANK_BITS = 16
RANK_SPAN = 1 << RANK_BITS
GATHER_UNROLL = 8
VMEM_LIMIT = 48 * 1024 * 1024


def _cparams(*sem):
    return pltpu.CompilerParams(dimension_semantics=sem, vmem_limit_bytes=VMEM_LIMIT)


def _const_spec(shape):
    nd = len(shape)
    return pl.BlockSpec(shape, lambda *_: (0,) * nd)


def _store_row_tiles(ref, val):
    rows = val.shape[0]
    for c in range(ROW_TILE):
        ref[pl.ds(c, rows, stride=ROW_TILE), :] = val[:, c * LANES:(c + 1) * LANES]


def _load_row_tiles(ref, rows):
    return jnp.concatenate([ref[pl.ds(c, rows, stride=ROW_TILE), :] for c in range(ROW_TILE)], axis=1)


def _rms(x, gain):
    ms = jnp.mean(x * x, axis=-1, keepdims=True)
    return (x * lax.rsqrt(ms + RMS_EPS)) * gain


def _split2(x):
    hi = x.astype(BF16)
    lo = (x - hi.astype(F32)).astype(BF16)
    return hi, lo


def _split3(x):
    p0 = x.astype(BF16)
    r = x - p0.astype(F32)
    p1 = r.astype(BF16)
    p2 = (r - p1.astype(F32)).astype(BF16)
    return p0, p1, p2


def _dot(a, b):
    return jnp.dot(a, b, preferred_element_type=F32)


def _dot_nt(a, b):
    return lax.dot_general(a, b, (((1,), (1,)), ((), ())), preferred_element_type=F32)


def _dot_tn(a, b):
    return lax.dot_general(a, b, (((0,), (0,)), ((), ())), preferred_element_type=F32)


def _head_mean_sq(v, bd):
    hi, lo = _split2(v * v)
    return _dot(hi, bd) + _dot(lo, bd)


def _group_specs(width, tiles_p, tm=TOKEN_TILE):
    return [pl.BlockSpec((tm, width), lambda i: (jnp.minimum(i, tiles_p - 1), 0)),
            pl.BlockSpec((tm, width), lambda i: (jnp.maximum(i - tiles_p, 0), 0))]


def _group_pick(p_ref, s_ref, tiles_p):
    return jnp.where(pl.program_id(0) < tiles_p, p_ref[...], s_ref[...])


def _proj_ab_body(xp_ref, xs_ref, g_ref, w_ref, qg_ref, kg_ref, lb_ref, bd_ref,
                  q_o, kf_o, vf_o, k16_o, v16_o, lf_o, kb_o, vb_o, qb_o, gb_o, *, tiles_p):
    xb = _rms(_group_pick(xp_ref, xs_ref, tiles_p), g_ref[...]).astype(BF16)
    bd = bd_ref[...]

    def seg(j):
        return _dot(xb, w_ref[:, j * D_HALF:(j + 1) * D_HALF])

    qa = seg(0)
    qn = qa * lax.rsqrt(_head_mean_sq(qa, bd) + RMS_EPS) * qg_ref[...]
    q_o[...] = (qn * (HEAD_DIM ** -0.5)).astype(BF16)
    ka = seg(1)
    kn = ka * lax.rsqrt(_head_mean_sq(ka, bd) + RMS_EPS) * kg_ref[...]
    kf_o[...] = kn
    k16_o[...] = kn.astype(BF16)
    va = seg(2)
    vf_o[...] = va
    v16_o[...] = va.astype(BF16)
    lb = lb_ref[...]
    f = lb + (1.0 - lb) * jax.nn.sigmoid(seg(3))
    lf_o[...] = jnp.log(f)
    kb_o[...] = 1.0 - f
    vb_o[...] = seg(4)
    qb_o[...] = jax.nn.silu(seg(5))
    gb_o[...] = seg(6)


def _proj_ab(x_p, x_s, gain, w16, qg, kg, lb, bd, seq, keep):
    tm = PROJ_TILE
    n_p, n_s = x_p.shape[0], x_s.shape[0]
    n = n_p + n_s
    tiles_p, tiles_seq, tiles_keep = n_p // tm, seq // tm, keep // tm
    kept_tiles = (n_p // seq) * tiles_keep + n_s // tm

    def keep_map(i):
        b, j = i // tiles_seq, i % tiles_seq
        prompt_slot = b * tiles_keep + jnp.maximum(j - (tiles_seq - tiles_keep), 0)
        return (jnp.where(i >= tiles_p, kept_tiles - n_s // tm + (i - tiles_p), prompt_slot), 0)

    row = pl.BlockSpec((tm, D_HALF), lambda i: (i, 0))
    kept = pl.BlockSpec((tm, D_HALF), keep_map)
    full = lambda dt: jax.ShapeDtypeStruct((n, D_HALF), dt)
    kept_shape = jax.ShapeDtypeStruct((kept_tiles * tm, D_HALF), F32)
    return pl.pallas_call(
        functools.partial(_proj_ab_body, tiles_p=tiles_p),
        out_shape=[full(BF16), kept_shape, kept_shape, full(BF16), full(BF16),
                   full(F32), full(F32), full(F32), full(F32), full(F32)],
        grid=(n // tm,),
        in_specs=_group_specs(D_MODEL, tiles_p, tm) + [
            _const_spec((1, D_MODEL)), _const_spec(w16.shape),
            _const_spec((1, D_HALF)), _const_spec((1, D_HALF)), _const_spec((1, D_HALF)),
            _const_spec((D_HALF, D_HALF))],
        out_specs=[row, kept, kept] + [row] * 7,
        compiler_params=_cparams("arbitrary"),
        name="proj_ab",
    )(x_p, x_s, gain, w16, qg, kg, lb, bd)


def _attn_prompt_body(q_ref, k0, k1, k2, v0, v1, v2, bias_ref, o_ref):
    i = pl.program_id(1)
    lim = jnp.maximum(2 - i, 0) * ATTN_TILE
    col = lax.broadcasted_iota(I32, (ATTN_UNIT, ATTN_SPAN), 1)
    cut = ATTN_SPAN - 2 * ATTN_TILE

    def span(parts, start):
        if start == 0:
            return (parts[0], parts[1], parts[2][:cut])
        return (parts[0][start:], parts[1], parts[2])

    def head_group(h0):
        units = []
        for h in range(h0, h0 + ATTN_HEAD_GROUP):
            hs = slice(h * HEAD_DIM, (h + 1) * HEAD_DIM)
            kh = (k0[:, hs], k1[:, hs], k2[:, hs])
            vh = (v0[:, hs], v1[:, hs], v2[:, hs])
            for r in range(ATTN_TILE // ATTN_UNIT):
                start = r * ATTN_UNIT
                units.append((h, hs, slice(r * ATTN_UNIT, (r + 1) * ATTN_UNIT), start,
                              span(kh, start), span(vh, start)))
        s = [jnp.concatenate([_dot_nt(q_ref[rows, hs], kp) for kp in ks], axis=1)
             for (_, hs, rows, _, ks, _) in units]
        s = [jnp.where(col + start < lim, NEG_INF,
                       s_u + bias_ref[h, rows, start:start + ATTN_SPAN])
             for s_u, (h, _, rows, start, _, _) in zip(s, units)]
        m = [jnp.max(s_u, axis=-1, keepdims=True) for s_u in s]
        p = [jnp.exp(s_u - m_u) for s_u, m_u in zip(s, m)]
        l = [jnp.sum(p_u, axis=-1, keepdims=True) for p_u in p]
        outs = []
        for p_u, (_, _, _, _, _, vs) in zip(p, units):
            p16 = p_u.astype(BF16)
            o, at = 0.0, 0
            for vp in vs:
                o = o + _dot(p16[:, at:at + vp.shape[0]], vp)
                at += vp.shape[0]
            outs.append(o)
        for o_u, l_u, (_, hs, rows, _, _, _) in zip(outs, l, units):
            o_ref[rows, hs] = (o_u / l_u).astype(BF16)

    for h0 in range(0, N_HEADS, ATTN_HEAD_GROUP):
        head_group(h0)


def _attn_prompt(q16, k16, v16, bias_full, batch, seq):
    tiles = seq // ATTN_TILE
    qspec = pl.BlockSpec((ATTN_TILE, D_HALF), lambda b, i: (b * tiles + i, 0))

    def kv(back):
        return pl.BlockSpec((ATTN_TILE, D_HALF),
                            lambda b, i: (b * tiles + jnp.maximum(i - back, 0), 0))

    return pl.pallas_call(
        _attn_prompt_body,
        out_shape=jax.ShapeDtypeStruct((batch * seq, D_HALF), BF16),
        grid=(batch, tiles),
        in_specs=[qspec, kv(2), kv(1), kv(0), kv(2), kv(1), kv(0),
                  _const_spec(bias_full.shape)],
        out_specs=qspec,
        compiler_params=_cparams("parallel", "parallel"),
        name="attn_prompt",
    )(q16, k16, k16, k16, v16, v16, v16, bias_full)


def _attn_sample_body(q_ref, kc_ref, vc_ref, kn_ref, vn_ref, bc_ref, bn_ref, o_ref):
    la = kc_ref.shape[1] // N_HEADS
    hr = range(N_HEADS)
    heads = [slice(h * HEAD_DIM, (h + 1) * HEAD_DIM) for h in hr]
    kc = [kc_ref[0, pl.ds(h, la, stride=N_HEADS), :].astype(BF16) for h in hr]
    vc = [vc_ref[0, pl.ds(h, la, stride=N_HEADS), :].astype(BF16) for h in hr]
    sc = [_dot_nt(q_ref[:, heads[h]], kc[h]) + bc_ref[h] for h in hr]
    sn = [_dot_nt(q_ref[:, heads[h]], kn_ref[:, heads[h]]) + bn_ref[h] for h in hr]
    m = [jnp.maximum(jnp.max(sc[h], axis=-1, keepdims=True), jnp.max(sn[h], axis=-1, keepdims=True))
         for h in hr]
    pc = [jnp.exp(sc[h] - m[h]) for h in hr]
    pn = [jnp.exp(sn[h] - m[h]) for h in hr]
    l = [jnp.sum(pc[h], axis=-1, keepdims=True) + jnp.sum(pn[h], axis=-1, keepdims=True) for h in hr]
    o = [_dot(pc[h].astype(BF16), vc[h]) + _dot(pn[h].astype(BF16), vn_ref[:, heads[h]]) for h in hr]
    o_ref[...] = jnp.concatenate([(o[h] / l[h]).astype(BF16) for h in hr], axis=1)


def _attn_sample(q16, k16, v16, cache_k, cache_v, bias_c, bias_n, row0, nseq, tn):
    blk0 = row0 // tn
    la8 = cache_k.shape[1]
    new = pl.BlockSpec((tn, D_HALF), lambda b: (blk0 + b, 0))
    cache = pl.BlockSpec((1, la8, HEAD_DIM), lambda b: (b, 0, 0))
    return pl.pallas_call(
        _attn_sample_body,
        out_shape=jax.ShapeDtypeStruct((nseq * tn, D_HALF), BF16),
        grid=(nseq,),
        in_specs=[new, cache, cache, new, new, _const_spec(bias_c.shape), _const_spec(bias_n.shape)],
        out_specs=pl.BlockSpec((tn, D_HALF), lambda b: (b, 0)),
        compiler_params=_cparams("parallel"),
        name="attn_sample",
    )(q16, cache_k, cache_v, k16, v16, bias_c, bias_n)


def _hgrn_body(q_ref, k_ref, v_ref, lf_ref, s0_ref, tcat_ref, o_ref, sout_ref, st_scr,
               *, chunk, nchunks):
    nblk = chunk // HGRN_BLOCK

    @pl.when(pl.program_id(1) == 0)
    def _():
        st_scr[...] = s0_ref[0]

    row = lax.broadcasted_iota(I32, (chunk, D_HALF), 0)
    r2 = lax.broadcasted_iota(I32, (chunk, chunk), 0)
    c2 = lax.broadcasted_iota(I32, (chunk, chunk), 1)
    same_blk_causal = jnp.logical_and(r2 // HGRN_BLOCK == c2 // HGRN_BLOCK, c2 <= r2)

    def one_chunk(c, carry):
        sl = pl.ds(pl.multiple_of(c * chunk, chunk), chunk)
        q = q_ref[sl, :]
        k = k_ref[sl, :]
        v16 = v_ref[sl, :].astype(BF16)
        l0, l1, l2 = _split3(lf_ref[sl, :])
        tcat = tcat_ref[...]
        both = _dot(tcat, l0) + _dot(tcat, l1) + _dot(tcat, l2)
        b_in = both[:chunk]
        b_ch = both[chunk:]
        ld = (q * jnp.exp(b_in)).astype(BF16)
        rd = (k * jnp.exp(-b_in)).astype(BF16)
        lj, rj = [], []
        for j in range(nblk - 1):
            e_j = b_ch[(j + 1) * HGRN_BLOCK - 1:(j + 1) * HGRN_BLOCK, :]
            later = row >= (j + 1) * HGRN_BLOCK
            inside = jnp.logical_and(row >= j * HGRN_BLOCK, row < (j + 1) * HGRN_BLOCK)
            lj.append(jnp.where(later, q * jnp.exp(jnp.minimum(b_ch - e_j, 0.0)), 0.0).astype(BF16))
            rj.append(jnp.where(inside, k * jnp.exp(jnp.minimum(e_j - b_ch, 0.0)), 0.0).astype(BF16))
        e_end = b_ch[chunk - 1:chunk, :]
        qc = (q * jnp.exp(b_ch)).astype(BF16)
        kc = (k * jnp.exp(e_end - b_ch)).astype(BF16)
        dec = jnp.exp(e_end)
        heads = [slice(h * HEAD_DIM, (h + 1) * HEAD_DIM) for h in range(N_HEADS)]
        sts = [st_scr[h] for h in range(N_HEADS)]
        diag = [_dot_nt(ld[:, hs], rd[:, hs]) for hs in heads]
        off = []
        for hs in heads:
            acc = None
            for j in range(nblk - 1):
                term = _dot_nt(lj[j][:, hs], rj[j][:, hs])
                acc = term if acc is None else acc + term
            off.append(acc)
        inter = [_dot_nt(qc[:, hs], st.astype(BF16)) for hs, st in zip(heads, sts)]
        upd = [_dot_tn(v16[:, hs], kc[:, hs]) for hs in heads]
        sc16 = [(jnp.where(same_blk_causal, d, 0.0) + o_).astype(BF16) for d, o_ in zip(diag, off)]
        intra = [_dot(s_, v16[:, hs]) for s_, hs in zip(sc16, heads)]
        o_ref[sl, :] = jnp.concatenate([a + b for a, b in zip(intra, inter)], axis=1)
        for h, hs in enumerate(heads):
            st_scr[h] = sts[h] * dec[:, hs] + upd[h]
        return carry

    lax.fori_loop(0, nchunks, one_chunk, 0, unroll=min(nchunks, HGRN_UNROLL))
    sout_ref[0] = st_scr[...]


def _hgrn(qb, kb, vb, lf, s0t, row0, nseq, seq, chunk, nchunks):
    tt = chunk * nchunks
    steps = seq // tt
    blk0 = row0 // tt
    t = jnp.arange(chunk)
    lower = t[None, :] <= t[:, None]
    same_blk = (t[None, :] // HGRN_BLOCK) == (t[:, None] // HGRN_BLOCK)
    tcat = jnp.concatenate([jnp.logical_and(lower, same_blk), lower], axis=0).astype(BF16)
    tok = pl.BlockSpec((tt, D_HALF), lambda b, j: (blk0 + b * steps + j, 0))
    state = pl.BlockSpec((1, N_HEADS, HEAD_DIM, HEAD_DIM), lambda b, j: (b, 0, 0, 0))
    return pl.pallas_call(
        functools.partial(_hgrn_body, chunk=chunk, nchunks=nchunks),
        out_shape=[jax.ShapeDtypeStruct((nseq * seq, D_HALF), F32),
                   jax.ShapeDtypeStruct(s0t.shape, F32)],
        grid=(nseq, steps),
        in_specs=[tok, tok, tok, tok, state, _const_spec((2 * chunk, chunk))],
        out_specs=[pl.BlockSpec((tt, D_HALF), lambda b, j: (b * steps + j, 0)), state],
        scratch_shapes=[pltpu.VMEM((N_HEADS, HEAD_DIM, HEAD_DIM), F32)],
        compiler_params=_cparams("parallel", "arbitrary"),
        name="hgrn",
    )(qb, kb, vb, lf, s0t, tcat)


def _route(xn, wr_hi, wr_lo, br, tstrict, cnt_scr, route_o, cnt_o, code_o):
    rows = xn.shape[0]
    x_hi, x_lo = _split2(xn)
    logits = _dot(x_hi, wr_hi) + _dot(x_lo, wr_hi) + _dot(x_hi, wr_lo) + br
    lane = lax.broadcasted_iota(I32, (rows, LANES), 1)
    lane_f = lane.astype(F32)

    def first_argmax(vals):
        m = jnp.max(vals, axis=-1, keepdims=True)
        idx = jnp.min(jnp.where(vals == m, lane_f, float(LANES)), axis=-1, keepdims=True)
        return m, idx.astype(I32)

    is_grp = jnp.logical_and(lane >= N_EXPERTS, lane < N_EXPERTS + N_GROUPS)
    gl = jnp.where(is_grp, logits, -jnp.inf)
    gmax, gidx = first_argmax(gl)
    p_grp = 1.0 / jnp.sum(jnp.exp(gl - gmax), axis=-1, keepdims=True)
    grp = gidx - N_EXPERTS
    in_grp = jnp.logical_and(lane < N_EXPERTS, lane // EXPERTS_PER_GROUP == grp)
    el = jnp.where(in_grp, logits, -jnp.inf)
    v1, e1 = first_argmax(el)
    v2, e2 = first_argmax(jnp.where(lane == e1, -jnp.inf, el))
    t2 = jnp.exp(v2 - v1)
    den = 1.0 + t2
    g1 = (1.0 / den) * p_grp
    g2 = (t2 / den) * p_grp

    oh1 = lane == e1
    oh2 = lane == e2
    oh = jnp.where(jnp.logical_or(oh1, oh2), 1.0, 0.0)
    before = _dot(tstrict, oh.astype(BF16)) + cnt_scr[...]
    rank1 = jnp.sum(jnp.where(oh1, before, 0.0), axis=-1, keepdims=True)
    rank2 = jnp.sum(jnp.where(oh2, before, 0.0), axis=-1, keepdims=True)
    cnt_scr[...] = cnt_scr[...] + jnp.sum(oh, axis=0, keepdims=True)
    cnt_o[...] = cnt_scr[...]

    e1f, e2f = e1.astype(F32), e2.astype(F32)
    fields = (e1f, e2f, g1, g2, rank1, rank2, e1f * RANK_SPAN + rank1, e2f * RANK_SPAN + rank2)
    slab = jnp.zeros((rows, LANES), F32)
    for idx, val in enumerate(fields):
        slab = jnp.where(lane == idx, val, slab)
    route_o[...] = slab
    code_o[...] = slab.T[6:8, :]


def _out_ab_body(hp_ref, hs_ref, oap_ref, oas_ref, obp_ref, obs_ref, gb_ref, og_ref, bd_ref, w_ref,
                 gf_ref, wrh_ref, wrl_ref, br_ref, ts_ref, h_o, xn_o, route_o, cnt_o, code_o, cnt_scr,
                 *, tiles_p):
    @pl.when(pl.program_id(0) == 0)
    def _():
        cnt_scr[...] = jnp.zeros_like(cnt_scr)

    ob = _group_pick(obp_ref, obs_ref, tiles_p)
    obn = ob * lax.rsqrt(_head_mean_sq(ob, bd_ref[...]) + RMS_EPS) * og_ref[...]
    obg = (obn * jax.nn.silu(gb_ref[...])).astype(BF16)
    oa = _group_pick(oap_ref, oas_ref, tiles_p)
    mix = _dot(oa, w_ref[0:D_HALF, :]) + _dot(obg, w_ref[D_HALF:, :])
    h1 = _group_pick(hp_ref, hs_ref, tiles_p) + mix
    h_o[...] = h1
    xn = _rms(h1, gf_ref[...])
    _store_row_tiles(xn_o, xn)
    _route(xn, wrh_ref[...], wrl_ref[...], br_ref[...], ts_ref[...], cnt_scr, route_o, cnt_o, code_o)


def _out_ab(h_p, h_s, oa_p, oa_s, ob_p, ob_s, gb, og, bd, w16, gf, wrh, wrl, br, tstrict):
    tm = ROUTE_TILE
    n = h_p.shape[0] + h_s.shape[0]
    tiles_p = h_p.shape[0] // tm
    row = lambda w: pl.BlockSpec((tm, w), lambda i: (i, 0))
    return pl.pallas_call(
        functools.partial(_out_ab_body, tiles_p=tiles_p),
        out_shape=[jax.ShapeDtypeStruct((n, D_MODEL), F32), jax.ShapeDtypeStruct((n * ROW_TILE, LANES), F32),
                   jax.ShapeDtypeStruct((n, LANES), F32), jax.ShapeDtypeStruct((1, LANES), F32),
                   jax.ShapeDtypeStruct((2, n), F32)],
        grid=(n // tm,),
        in_specs=_group_specs(D_MODEL, tiles_p, tm) + _group_specs(D_HALF, tiles_p, tm)
        + _group_specs(D_HALF, tiles_p, tm) + [
            row(D_HALF), _const_spec((1, D_HALF)),
            _const_spec((D_HALF, D_HALF)), _const_spec((D_MODEL, D_MODEL)),
            _const_spec((1, D_MODEL)), _const_spec((D_MODEL, LANES)),
            _const_spec((D_MODEL, LANES)), _const_spec((1, LANES)), _const_spec((tm, tm))],
        out_specs=[row(D_MODEL), pl.BlockSpec((tm * ROW_TILE, LANES), lambda i: (i, 0)), row(LANES),
                   _const_spec((1, LANES)), pl.BlockSpec((2, tm), lambda i: (0, i))],
        scratch_shapes=[pltpu.VMEM((1, LANES), F32)],
        compiler_params=_cparams("arbitrary"),
        name="out_ab",
    )(h_p, h_s, oa_p, oa_s, ob_p, ob_s, gb, og, bd, w16, gf, wrh, wrl, br, tstrict)


def _row_copy(src, src_row, dst, dst_row, sem):
    s0 = pl.multiple_of(src_row * ROW_TILE, ROW_TILE)
    d0 = pl.multiple_of(dst_row * ROW_TILE, ROW_TILE)
    return pltpu.make_async_copy(src.at[pl.ds(s0, ROW_TILE)], dst.at[pl.ds(d0, ROW_TILE)], sem)


def _collect_body(pos_ref, ys_hbm, h_ref, r_ref, *rest, tile, n_tok, tiles_p, proj):
    if proj:
        g_ref, w_ref = rest[:2]
        rest = rest[2:]
    outs, (buf, sems) = rest[:-2], rest[-2:]
    i = pl.program_id(0)
    nt = pl.num_programs(0)

    def issue(t):
        slot = t % 2
        base = t * tile

        def one(r, carry):
            tok = base + r
            for k in range(2):
                row = pos_ref[k * n_tok + tok]
                _row_copy(ys_hbm, row, buf.at[slot].at[k], r, sems.at[slot]).start(priority=k)
            return carry

        lax.fori_loop(0, tile, one, 0, unroll=GATHER_UNROLL)

    @pl.when(i == 0)
    def _():
        issue(0)

    @pl.when(i + 1 < nt)
    def _():
        issue(i + 1)

    slot = i % 2
    for _ in range(2 * tile):
        _row_copy(ys_hbm, 0, buf.at[slot].at[0], 0, sems.at[slot]).wait()
    route = r_ref[...]
    ya = _load_row_tiles(buf.at[slot].at[0], tile)
    yb = _load_row_tiles(buf.at[slot].at[1], tile)
    out = h_ref[...] + (route[:, 2:3] * ya + route[:, 3:4] * yb)
    if proj:
        h_o, u_o, bg_o = outs
        h_o[...] = out
        xb = _rms(out, g_ref[...]).astype(BF16)
        bg_o[...] = _dot(xb, w_ref[:, 0:D_MODEL])
        u_o[...] = _dot(xb, w_ref[:, D_MODEL:2 * D_MODEL]) * _dot(xb, w_ref[:, 2 * D_MODEL:])
    else:
        @pl.when(i < tiles_p)
        def _():
            outs[0][...] = out

        @pl.when(i >= tiles_p)
        def _():
            outs[1][...] = out


def _collect(pos, ys, h, route, *, proj=None, n_p=None):
    tm = TOKEN_TILE
    n = h.shape[0]
    row = lambda w: pl.BlockSpec((tm, w), lambda i, pos: (i, 0))
    const = lambda shape: pl.BlockSpec(shape, lambda i, pos: (0,) * len(shape))
    if proj is not None:
        tiles_p = None
        extra_in, extra_specs = list(proj), [const(proj[0].shape), const(proj[1].shape)]
        out_shape = [jax.ShapeDtypeStruct((n, D_MODEL), F32)] * 3
        out_specs = [row(D_MODEL)] * 3
    else:
        tiles_p = n_p // tm
        extra_in, extra_specs = [], []
        out_shape = [jax.ShapeDtypeStruct((n_p, D_MODEL), F32),
                     jax.ShapeDtypeStruct((n - n_p, D_MODEL), F32)]
        out_specs = [pl.BlockSpec((tm, D_MODEL), lambda i, pos: (jnp.minimum(i, tiles_p - 1), 0)),
                     pl.BlockSpec((tm, D_MODEL), lambda i, pos: (jnp.maximum(i - tiles_p, 0), 0))]
    return pl.pallas_call(
        functools.partial(_collect_body, tile=tm, n_tok=n, tiles_p=tiles_p, proj=proj is not None),
        out_shape=out_shape,
        grid_spec=pltpu.PrefetchScalarGridSpec(
            num_scalar_prefetch=1,
            grid=(n // tm,),
            in_specs=[pl.BlockSpec(memory_space=pl.ANY), row(D_MODEL), row(LANES)] + extra_specs,
            out_specs=out_specs,
            scratch_shapes=[pltpu.VMEM((2, 2, tm * ROW_TILE, LANES), F32),
                            pltpu.SemaphoreType.DMA((2,))],
        ),
        compiler_params=pltpu.CompilerParams(dimension_semantics=("arbitrary",),
                                             vmem_limit_bytes=VMEM_LIMIT),
        name="moe_collect",
    )(pos, ys, h, route, *extra_in)


def _scatter_rows_body(pos_ref, pad0_ref, padn_ref, nt_ref, x_ref, xs_hbm,
                       stage, zero_scr, sems, pad_sem, *, tile, n_tok, out_tiles):
    i = pl.program_id(0)
    last = pl.num_programs(0) - 1
    base = i * tile
    slot = i % 2

    def drain(s):
        for _ in range(2 * tile):
            _row_copy(stage.at[s], 0, xs_hbm, 0, sems.at[s]).wait()

    @pl.when(i >= 2)
    def _():
        drain(slot)

    stage[slot] = x_ref[...]

    def issue(r, carry):
        tok = base + r
        src = stage.at[slot]
        for k in range(2):
            _row_copy(src, r, xs_hbm, pos_ref[k * n_tok + tok], sems.at[slot]).start(priority=k)
        return carry

    lax.fori_loop(0, tile, issue, 0, unroll=GATHER_UNROLL)

    @pl.when(i == last)
    def _():
        zero_scr[...] = jnp.zeros_like(zero_scr)

        def fill(lo, count):
            def one(r, carry):
                _row_copy(zero_scr, 0, xs_hbm, lo + r, pad_sem).start()
                return carry

            def done(r, carry):
                _row_copy(zero_scr, 0, xs_hbm, 0, pad_sem).wait()
                return carry

            lax.fori_loop(0, count, one, 0)
            lax.fori_loop(0, count, done, 0)

        for e in range(N_EXPERTS):
            fill(pad0_ref[e], padn_ref[e])

        def tail_copy(t):
            rows = MOE_TILE * ROW_TILE
            return pltpu.make_async_copy(
                zero_scr, xs_hbm.at[pl.ds(pl.multiple_of(t * rows, rows), rows)], pad_sem)

        def tail_start(t, carry):
            tail_copy(t).start()
            return carry

        def tail_done(t, carry):
            tail_copy(t).wait()
            return carry

        lax.fori_loop(nt_ref[0], out_tiles, tail_start, 0)
        lax.fori_loop(nt_ref[0], out_tiles, tail_done, 0)
        drain(slot)

    @pl.when(jnp.logical_and(i == last, i >= 1))
    def _():
        drain(1 - slot)


def _scatter_rows(pos, pad0, padn, n_tiles, xn, n_rows):
    tm = TOKEN_TILE
    n = xn.shape[0] // ROW_TILE
    return pl.pallas_call(
        functools.partial(_scatter_rows_body, tile=tm, n_tok=n, out_tiles=n_rows // MOE_TILE),
        out_shape=jax.ShapeDtypeStruct((n_rows * ROW_TILE, LANES), F32),
        grid_spec=pltpu.PrefetchScalarGridSpec(
            num_scalar_prefetch=4,
            grid=(n // tm,),
            in_specs=[pl.BlockSpec((tm * ROW_TILE, LANES), lambda i, *_: (i, 0))],
            out_specs=pl.BlockSpec(memory_space=pl.ANY),
            scratch_shapes=[pltpu.VMEM((2, tm * ROW_TILE, LANES), F32),
                            pltpu.VMEM((MOE_TILE * ROW_TILE, LANES), F32),
                            pltpu.SemaphoreType.DMA((2,)), pltpu.SemaphoreType.DMA(())],
        ),
        compiler_params=pltpu.CompilerParams(dimension_semantics=("arbitrary",)),
        name="moe_scatter",
    )(pos, pad0, padn, n_tiles, xn)


def _experts_body(te_ref, nt_ref, x_ref, w1_ref, w3_ref, w2_ref, y_ref, w1_s, w3_s, w2_s):
    i = pl.program_id(0)
    prev = te_ref[jnp.maximum(i - 1, 0)]
    fresh = jnp.logical_or(i == 0, te_ref[i] != prev)

    @pl.when(fresh)
    def _():
        w1_s[...] = w1_ref[0].astype(BF16)
        w3_s[...] = w3_ref[0].astype(BF16)
        w2_s[...] = w2_ref[0].astype(BF16)

    @pl.when(i < nt_ref[0])
    def _():
        x = _load_row_tiles(x_ref, MOE_TILE).astype(BF16)
        a = _dot(x, w1_s[...])
        b = _dot(x, w3_s[...])
        _store_row_tiles(y_ref, _dot((jax.nn.silu(a) * b).astype(BF16), w2_s[...]))


def _experts(tile_expert, n_tiles, xs, w1, w3, w2):
    rows = xs.shape[0] // ROW_TILE
    tm = MOE_TILE
    tile = pl.BlockSpec((tm * ROW_TILE, LANES), lambda i, te, nt: (jnp.minimum(i, nt[0] - 1), 0))
    return pl.pallas_call(
        _experts_body,
        out_shape=jax.ShapeDtypeStruct((rows * ROW_TILE, LANES), F32),
        grid_spec=pltpu.PrefetchScalarGridSpec(
            num_scalar_prefetch=2,
            grid=(rows // tm,),
            in_specs=[tile,
                      pl.BlockSpec((1, D_MODEL, D_EXPERT), lambda i, te, nt: (te[i], 0, 0)),
                      pl.BlockSpec((1, D_MODEL, D_EXPERT), lambda i, te, nt: (te[i], 0, 0)),
                      pl.BlockSpec((1, D_EXPERT, D_MODEL), lambda i, te, nt: (te[i], 0, 0))],
            out_specs=tile,
            scratch_shapes=[pltpu.VMEM((D_MODEL, D_EXPERT), BF16), pltpu.VMEM((D_MODEL, D_EXPERT), BF16),
                            pltpu.VMEM((D_EXPERT, D_MODEL), BF16)],
        ),
        input_output_aliases={2: 0},
        compiler_params=_cparams("arbitrary"),
        name="moe_experts",
    )(tile_expert, n_tiles, xs, w1, w3, w2)


def _moe_plan(counts, n):
    counts = counts[0, :N_EXPERTS].astype(I32)
    padded = ((counts + MOE_TILE - 1) // MOE_TILE) * MOE_TILE
    ends = jnp.cumsum(padded)
    offs = ends - padded
    max_tiles = (2 * n + N_EXPERTS * (MOE_TILE - 1) + MOE_TILE - 1) // MOE_TILE
    tile_start = jnp.arange(max_tiles, dtype=I32) * MOE_TILE
    tile_expert = jnp.minimum(jnp.sum((tile_start[:, None] >= ends[None, :]).astype(I32), axis=1),
                              N_EXPERTS - 1)
    n_tiles = (ends[-1] // MOE_TILE).reshape(1)
    return offs, offs + counts, padded - counts, tile_expert, n_tiles, max_tiles * MOE_TILE


def _moe(xn, h, route, code, counts, layer, w1, w3, w2, **collect_mode):
    n = h.shape[0]
    code = code.astype(I32).reshape(2 * n)
    offs, pad0, padn, tile_expert, n_tiles, n_rows = _moe_plan(counts, n)
    hit = (code >> RANK_BITS)[None, :] == jnp.arange(N_EXPERTS, dtype=I32)[:, None]
    pos = jnp.sum(jnp.where(hit, offs[:, None], 0), axis=0) + (code & (RANK_SPAN - 1))
    xs = _scatter_rows(pos, pad0, padn, n_tiles, xn, n_rows)
    flat = lambda w: w.reshape((-1,) + w.shape[2:])
    ys = _experts(tile_expert + layer * N_EXPERTS, n_tiles, xs, flat(w1), flat(w3), flat(w2))
    return _collect(pos, ys, h, route, **collect_mode)


def _conv_out_body(u_ref, up_ref, stp_ref, sts_ref, bg_ref, h_ref, cw_ref, w_ref, gf_ref,
                   wrh_ref, wrl_ref, br_ref, ts_ref, h_o, xn_o, route_o, cnt_o, code_o, cnt_scr,
                   *, tile, tiles_p, seq_p, seq_s):
    i = pl.program_id(0)

    @pl.when(i == 0)
    def _():
        cnt_scr[...] = jnp.zeros_like(cnt_scr)

    u = u_ref[...]
    rowi = lax.broadcasted_iota(I32, (tile, D_MODEL), 0)
    is_p = i < tiles_p
    at_start = (i * tile) % seq_p == 0
    stp = stp_ref[0]
    prev = up_ref[...]
    m2_p = jnp.where(at_start, stp[0:1, :], prev[6:7, :])
    m1_p = jnp.where(at_start, stp[1:2, :], prev[7:8, :])
    per = tile // seq_s
    sts = sts_ref[...]
    m2_s = jnp.broadcast_to(sts[:, 0:1, :], (per, seq_s, D_MODEL)).reshape(tile, D_MODEL)
    m1_s = jnp.broadcast_to(sts[:, 1:2, :], (per, seq_s, D_MODEL)).reshape(tile, D_MODEL)
    m2 = jnp.where(is_p, m2_p, m2_s)
    m1 = jnp.where(is_p, m1_p, m1_s)
    pos = jnp.where(is_p, rowi, rowi % seq_s)
    u1 = jnp.where(pos == 0, m1, pltpu.roll(u, 1, axis=0))
    u2 = jnp.where(pos == 0, m2, jnp.where(pos == 1, m1, pltpu.roll(u, 2, axis=0)))
    cw = cw_ref[...]
    conv = u2 * cw[0:1, :] + u1 * cw[1:2, :] + u * cw[2:3, :]
    mix = _dot((bg_ref[...] * conv).astype(BF16), w_ref[...])
    h3 = h_ref[...] + mix
    h_o[...] = h3
    xn = _rms(h3, gf_ref[...])
    _store_row_tiles(xn_o, xn)
    _route(xn, wrh_ref[...], wrl_ref[...], br_ref[...], ts_ref[...], cnt_scr, route_o, cnt_o, code_o)


def _conv_out(u, bg, h, state_p, state_s, cw, w16, gf, wrh, wrl, br, tstrict, seq_p, seq_s):
    tm = ROUTE_TILE
    n = u.shape[0]
    tiles_p = state_p.shape[0] * seq_p // tm
    per = tm // seq_s
    row = lambda w: pl.BlockSpec((tm, w), lambda i: (i, 0))
    prev = pl.BlockSpec((8, D_MODEL), lambda i: (jnp.maximum(i * (tm // 8) - 1, 0), 0))
    stp_spec = pl.BlockSpec((1, 2, D_MODEL),
                            lambda i: (jnp.minimum(i, tiles_p - 1) * tm // seq_p, 0, 0))
    sts_spec = pl.BlockSpec((per, 2, D_MODEL), lambda i: (jnp.maximum(i - tiles_p, 0), 0, 0))
    return pl.pallas_call(
        functools.partial(_conv_out_body, tile=tm, tiles_p=tiles_p, seq_p=seq_p, seq_s=seq_s),
        out_shape=[jax.ShapeDtypeStruct((n, D_MODEL), F32),
                   jax.ShapeDtypeStruct((n * ROW_TILE, LANES), F32),
                   jax.ShapeDtypeStruct((n, LANES), F32), jax.ShapeDtypeStruct((1, LANES), F32),
                   jax.ShapeDtypeStruct((2, n), F32)],
        grid=(n // tm,),
        in_specs=[row(D_MODEL), prev, stp_spec, sts_spec, row(D_MODEL), row(D_MODEL),
                  _const_spec((3, D_MODEL)),
                  _const_spec((D_MODEL, D_MODEL)), _const_spec((1, D_MODEL)),
                  _const_spec((D_MODEL, LANES)), _const_spec((D_MODEL, LANES)),
                  _const_spec((1, LANES)), _const_spec((tm, tm))],
        out_specs=[row(D_MODEL), pl.BlockSpec((tm * ROW_TILE, LANES), lambda i: (i, 0)), row(LANES),
                   _const_spec((1, LANES)), pl.BlockSpec((2, tm), lambda i: (0, i))],
        scratch_shapes=[pltpu.VMEM((1, LANES), F32)],
        compiler_params=_cparams("arbitrary"),
        name="conv_out",
    )(u, u, state_p, state_s, bg, h, cw, w16, gf, wrh, wrl, br, tstrict)


def _rel_bias_toeplitz(rel_bias, rows, cols, lead):
    period = rows + cols
    k = jnp.arange(period)
    d = jnp.where(k < cols, k, k - period)
    idx = jnp.clip(lead - d, -REL_CLIP, REL_CLIP) + REL_CLIP
    v = jnp.take(rel_bias.astype(F32), idx, axis=1)
    heads = v.shape[0]
    skew = jnp.tile(v, (1, rows))[:, :rows * (period - 1)].reshape(heads, rows, period - 1)
    return skew[:, :, :cols]


def _bias_prompt(rel_bias):
    r = jnp.arange(ATTN_TILE)[:, None]
    c = jnp.arange(ATTN_WINDOW)[None, :]
    j = c // CHUNK - r // CHUNK
    band = jnp.logical_and(j >= 0, j <= BAND_CHUNKS)
    bias = _rel_bias_toeplitz(rel_bias, ATTN_TILE, ATTN_WINDOW, BAND_CHUNKS * CHUNK)
    return jnp.where(band[None], bias, NEG_INF)


def _bias_sample(rel_bias, la, tn):
    bias = _rel_bias_toeplitz(rel_bias, tn, la + tn, la)
    return bias[:, :, :la], bias[:, :, la:]


def _router_weights(w_group, b_group, w_expert, b_expert):
    pad = LANES - N_EXPERTS - N_GROUPS
    w = jnp.concatenate([w_expert, w_group, jnp.zeros((D_MODEL, pad), F32)], axis=1)
    b = jnp.concatenate([b_expert, b_group, jnp.zeros((pad,), F32)])[None, :].astype(F32)
    hi, lo = _split2(w.astype(F32))
    return hi, lo, b


def kernel(x_prompt, x_sample, cache_a_k, cache_a_v, state_hgrn, state_conv, norm_mix, norm_ffn,
           w_in_ab, w_out_ab, q_norm, k_norm, rel_bias, hgrn_lb_logits, hgrn_out_norm, w_in_c,
           conv_w, w_out_c, w_group, b_group, w_expert, b_expert, w1, w3, w2):
    batch, seq, d = x_prompt.shape
    nseq_s, tn, _ = x_sample.shape
    la = cache_a_k.shape[2]
    n_p = batch * seq
    n_s = nseq_s * tn
    n = n_p + n_s
    keep = min(BAND_CHUNKS * CHUNK, seq)

    x_p = x_prompt.reshape(n_p, d)
    x_s = x_sample.reshape(n_s, d)
    lb_all = jnp.cumsum(jax.nn.softmax(hgrn_lb_logits.astype(F32), axis=0), axis=0)
    head_avg = jnp.kron(jnp.eye(N_HEADS, dtype=F32),
                        jnp.full((HEAD_DIM, HEAD_DIM), 1.0 / HEAD_DIM, F32)).astype(BF16)
    t = jnp.arange(ROUTE_TILE)
    tstrict = (t[None, :] < t[:, None]).astype(BF16)
    row1 = lambda v: v.astype(F32).reshape(1, -1)
    tile8 = lambda v: jnp.tile(v.astype(F32), N_HEADS).reshape(1, -1)

    l = 0
    (q16, kf, vf, k16, v16, lf, kb, vb, qb, gb) = _proj_ab(
        x_p, x_s, row1(norm_mix[0]), w_in_ab[l].astype(BF16), tile8(q_norm[l]), tile8(k_norm[l]),
        row1(lb_all[l]), head_avg, seq, keep)

    oa_p = _attn_prompt(q16, k16, v16, _bias_prompt(rel_bias[l]), batch, seq)
    bias_c, bias_n = _bias_sample(rel_bias[l], la, tn)
    oa_s = _attn_sample(q16, k16, v16, cache_a_k[l].reshape(nseq_s, la * N_HEADS, HEAD_DIM),
                        cache_a_v[l].reshape(nseq_s, la * N_HEADS, HEAD_DIM), bias_c, bias_n,
                        n_p, nseq_s, tn)

    zeros_state = jnp.zeros((batch, N_HEADS, HEAD_DIM, HEAD_DIM), F32)
    ob_p, st_p = _hgrn(qb, kb, vb, lf, zeros_state, 0, batch, seq, 64, 4)
    ob_s, st_s = _hgrn(qb, kb, vb, lf, jnp.swapaxes(state_hgrn[l].astype(F32), -1, -2),
                       n_p, nseq_s, tn, tn, 1)

    wrh, wrl, br = _router_weights(w_group[0], b_group[0], w_expert[0], b_expert[0])
    h1, xn1, route1, cnt1, code1 = _out_ab(x_p, x_s, oa_p, oa_s, ob_p, ob_s, gb, row1(hgrn_out_norm[l]),
                                    head_avg, w_out_ab[l].astype(BF16), row1(norm_ffn[0]),
                                    wrh, wrl, br, tstrict)
    h2, u, bg = _moe(xn1, h1, route1, code1, cnt1, 0, w1, w3, w2,
                     proj=(row1(norm_mix[1]), w_in_c[0].astype(BF16)))

    wrh, wrl, br = _router_weights(w_group[1], b_group[1], w_expert[1], b_expert[1])
    h3, xn2, route2, cnt2, code2 = _conv_out(u, bg, h2, jnp.zeros((batch, 2, d), F32),
                                      state_conv[0].astype(F32), conv_w[0].astype(F32),
                                      w_out_c[0].astype(BF16), row1(norm_ffn[1]), wrh, wrl, br,
                                      tstrict, seq, tn)
    out_p, out_s = _moe(xn2, h3, route2, code2, cnt2, 1, w1, w3, w2, n_p=n_p)

    y_prompt = out_p.reshape(batch, seq, d)
    y_sample = out_s.reshape(nseq_s, tn, d)
    n_kp = batch * keep
    heads = lambda a, b_, t_: a.reshape(b_, t_, N_HEADS, HEAD_DIM)
    nk_p = heads(kf[:n_kp], batch, keep)[None]
    nv_p = heads(vf[:n_kp], batch, keep)[None]
    kf_s = heads(kf[n_kp:n_kp + n_s], nseq_s, tn)
    vf_s = heads(vf[n_kp:n_kp + n_s], nseq_s, tn)
    nk_s = jnp.concatenate([cache_a_k[l].astype(F32), kf_s], axis=1)[:, -la:][None]
    nv_s = jnp.concatenate([cache_a_v[l].astype(F32), vf_s], axis=1)[:, -la:][None]
    nh_p = jnp.swapaxes(st_p, -1, -2)[None]
    nh_s = jnp.swapaxes(st_s, -1, -2)[None]
    nc_p = jnp.stack([u[(b + 1) * seq - 2:(b + 1) * seq] for b in range(batch)])[None]
    u_s = u[n_p:].reshape(nseq_s, tn, d)
    nc_s = jnp.concatenate([state_conv[0].astype(F32), u_s], axis=1)[:, -2:][None]
    return (y_prompt, y_sample, nk_p, nv_p, nk_s, nv_s, nh_p, nh_s, nc_p, nc_s)
```

```python
import functools

import jax
import jax.numpy as jnp
from jax import lax
from jax.experimental import pallas as pl
from jax.experimental.pallas import tpu as pltpu

F32 = jnp.float32
BF16 = jnp.bfloat16
I32 = jnp.int32

D_MODEL = 1024
CHUNK = 64
BAND_CHUNKS = 8
HEAD_DIM = 64
N_HEADS = 8
D_HALF = N_HEADS * HEAD_DIM
REL_CLIP = 128
HGRN_BLOCK = 16
N_GROUPS = 4
EXPERTS_PER_GROUP = 8
N_EXPERTS = N_GROUPS * EXPERTS_PER_GROUP
D_EXPERT = 256
RMS_EPS = 1e-6
NEG_INF = -1e30

LANES = 128
ROW_TILE = D_MODEL // LANES
TOKEN_TILE = 512
PROJ_TILE = 512
ROUTE_TILE = 512
ATTN_TILE = 256
ATTN_WINDOW = ATTN_TILE + BAND_CHUNKS * CHUNK
ATTN_UNIT = 2 * CHUNK
ATTN_SPAN = ATTN_UNIT + BAND_CHUNKS * CHUNK
HGRN_UNROLL = 4
ATTN_HEAD_GROUP = 2
MOE_TILE = 256
RANK_BITS = 16
RANK_SPAN = 1 << RANK_BITS
GATHER_UNROLL = 8
VMEM_LIMIT = 48 * 1024 * 1024


def _cparams(*sem):
    return pltpu.CompilerParams(dimension_semantics=sem, vmem_limit_bytes=VMEM_LIMIT)


def _const_spec(shape):
    nd = len(shape)
    return pl.BlockSpec(shape, lambda *_: (0,) * nd)


def _store_row_tiles(ref, val):
    rows = val.shape[0]
    for c in range(ROW_TILE):
        ref[pl.ds(c, rows, stride=ROW_TILE), :] = val[:, c * LANES:(c + 1) * LANES]


def _load_row_tiles(ref, rows):
    return jnp.concatenate([ref[pl.ds(c, rows, stride=ROW_TILE), :] for c in range(ROW_TILE)], axis=1)


def _rms(x, gain):
    ms = jnp.mean(x * x, axis=-1, keepdims=True)
    return (x * lax.rsqrt(ms + RMS_EPS)) * gain


def _split2(x):
    hi = x.astype(BF16)
    lo = (x - hi.astype(F32)).astype(BF16)
    return hi, lo


def _split3(x):
    p0 = x.astype(BF16)
    r = x - p0.astype(F32)
    p1 = r.astype(BF16)
    p2 = (r - p1.astype(F32)).astype(BF16)
    return p0, p1, p2


def _dot(a, b):
    return jnp.dot(a, b, preferred_element_type=F32)


def _dot_nt(a, b):
    return lax.dot_general(a, b, (((1,), (1,)), ((), ())), preferred_element_type=F32)


def _dot_tn(a, b):
    return lax.dot_general(a, b, (((0,), (0,)), ((), ())), preferred_element_type=F32)


def _head_mean_sq(v, bd):
    hi, lo = _split2(v * v)
    return _dot(hi, bd) + _dot(lo, bd)


def _group_specs(width, tiles_p, tm=TOKEN_TILE):
    return [pl.BlockSpec((tm, width), lambda i: (jnp.minimum(i, tiles_p - 1), 0)),
            pl.BlockSpec((tm, width), lambda i: (jnp.maximum(i - tiles_p, 0), 0))]


def _group_pick(p_ref, s_ref, tiles_p):
    return jnp.where(pl.program_id(0) < tiles_p, p_ref[...], s_ref[...])


def _proj_ab_body(xp_ref, xs_ref, g_ref, w_ref, qg_ref, kg_ref, lb_ref, bd_ref,
                  q_o, kf_o, vf_o, k16_o, v16_o, lf_o, kb_o, vb_o, qb_o, gb_o, *, tiles_p):
    xb = _rms(_group_pick(xp_ref, xs_ref, tiles_p), g_ref[...]).astype(BF16)
    bd = bd_ref[...]

    def seg(j):
        return _dot(xb, w_ref[:, j * D_HALF:(j + 1) * D_HALF])

    qa = seg(0)
    qn = qa * lax.rsqrt(_head_mean_sq(qa, bd) + RMS_EPS) * qg_ref[...]
    q_o[...] = (qn * (HEAD_DIM ** -0.5)).astype(BF16)
    ka = seg(1)
    kn = ka * lax.rsqrt(_head_mean_sq(ka, bd) + RMS_EPS) * kg_ref[...]
    kf_o[...] = kn
    k16_o[...] = kn.astype(BF16)
    va = seg(2)
    vf_o[...] = va
    v16_o[...] = va.astype(BF16)
    lb = lb_ref[...]
    f = lb + (1.0 - lb) * jax.nn.sigmoid(seg(3))
    lf_o[...] = jnp.log(f)
    kb_o[...] = 1.0 - f
    vb_o[...] = seg(4)
    qb_o[...] = jax.nn.silu(seg(5))
    gb_o[...] = seg(6)


def _proj_ab(x_p, x_s, gain, w16, qg, kg, lb, bd, seq, keep):
    tm = PROJ_TILE
    n_p, n_s = x_p.shape[0], x_s.shape[0]
    n = n_p + n_s
    tiles_p, tiles_seq, tiles_keep = n_p // tm, seq // tm, keep // tm
    kept_tiles = (n_p // seq) * tiles_keep + n_s // tm

    def keep_map(i):
        b, j = i // tiles_seq, i % tiles_seq
        prompt_slot = b * tiles_keep + jnp.maximum(j - (tiles_seq - tiles_keep), 0)
        return (jnp.where(i >= tiles_p, kept_tiles - n_s // tm + (i - tiles_p), prompt_slot), 0)

    row = pl.BlockSpec((tm, D_HALF), lambda i: (i, 0))
    kept = pl.BlockSpec((tm, D_HALF), keep_map)
    full = lambda dt: jax.ShapeDtypeStruct((n, D_HALF), dt)
    kept_shape = jax.ShapeDtypeStruct((kept_tiles * tm, D_HALF), F32)
    return pl.pallas_call(
        functools.partial(_proj_ab_body, tiles_p=tiles_p),
        out_shape=[full(BF16), kept_shape, kept_shape, full(BF16), full(BF16),
                   full(F32), full(F32), full(F32), full(F32), full(F32)],
        grid=(n // tm,),
        in_specs=_group_specs(D_MODEL, tiles_p, tm) + [
            _const_spec((1, D_MODEL)), _const_spec(w16.shape),
            _const_spec((1, D_HALF)), _const_spec((1, D_HALF)), _const_spec((1, D_HALF)),
            _const_spec((D_HALF, D_HALF))],
        out_specs=[row, kept, kept] + [row] * 7,
        compiler_params=_cparams("arbitrary"),
        name="proj_ab",
    )(x_p, x_s, gain, w16, qg, kg, lb, bd)


def _attn_prompt_body(q_ref, k0, k1, k2, v0, v1, v2, bias_ref, o_ref):
    i = pl.program_id(1)
    lim = jnp.maximum(2 - i, 0) * ATTN_TILE
    col = lax.broadcasted_iota(I32, (ATTN_UNIT, ATTN_SPAN), 1)
    cut = ATTN_SPAN - 2 * ATTN_TILE

    def span(parts, start):
        if start == 0:
            return (parts[0], parts[1], parts[2][:cut])
        return (parts[0][start:], parts[1], parts[2])

    def head_group(h0):
        units = []
        for h in range(h0, h0 + ATTN_HEAD_GROUP):
            hs = slice(h * HEAD_DIM, (h + 1) * HEAD_DIM)
            kh = (k0[:, hs], k1[:, hs], k2[:, hs])
            vh = (v0[:, hs], v1[:, hs], v2[:, hs])
            for r in range(ATTN_TILE // ATTN_UNIT):
                start = r * ATTN_UNIT
                units.append((h, hs, slice(r * ATTN_UNIT, (r + 1) * ATTN_UNIT), start,
                              span(kh, start), span(vh, start)))
        s = [jnp.concatenate([_dot_nt(q_ref[rows, hs], kp) for kp in ks], axis=1)
             for (_, hs, rows, _, ks, _) in units]
        s = [jnp.where(col + start < lim, NEG_INF,
                       s_u + bias_ref[h, rows, start:start + ATTN_SPAN])
             for s_u, (h, _, rows, start, _, _) in zip(s, units)]
        m = [jnp.max(s_u, axis=-1, keepdims=True) for s_u in s]
        p = [jnp.exp(s_u - m_u) for s_u, m_u in zip(s, m)]
        l = [jnp.sum(p_u, axis=-1, keepdims=True) for p_u in p]
        outs = []
        for p_u, (_, _, _, _, _, vs) in zip(p, units):
            p16 = p_u.astype(BF16)
            o, at = 0.0, 0
            for vp in vs:
                o = o + _dot(p16[:, at:at + vp.shape[0]], vp)
                at += vp.shape[0]
            outs.append(o)
        for o_u, l_u, (_, hs, rows, _, _, _) in zip(outs, l, units):
            o_ref[rows, hs] = (o_u / l_u).astype(BF16)

    for h0 in range(0, N_HEADS, ATTN_HEAD_GROUP):
        head_group(h0)


def _attn_prompt(q16, k16, v16, bias_full, batch, seq):
    tiles = seq // ATTN_TILE
    qspec = pl.BlockSpec((ATTN_TILE, D_HALF), lambda b, i: (b * tiles + i, 0))

    def kv(back):
        return pl.BlockSpec((ATTN_TILE, D_HALF),
                            lambda b, i: (b * tiles + jnp.maximum(i - back, 0), 0))

    return pl.pallas_call(
        _attn_prompt_body,
        out_shape=jax.ShapeDtypeStruct((batch * seq, D_HALF), BF16),
        grid=(batch, tiles),
        in_specs=[qspec, kv(2), kv(1), kv(0), kv(2), kv(1), kv(0),
                  _const_spec(bias_full.shape)],
        out_specs=qspec,
        compiler_params=_cparams("parallel", "parallel"),
        name="attn_prompt",
    )(q16, k16, k16, k16, v16, v16, v16, bias_full)


def _attn_sample_body(q_ref, kc_ref, vc_ref, kn_ref, vn_ref, bc_ref, bn_ref, o_ref):
    la = kc_ref.shape[1] // N_HEADS
    hr = range(N_HEADS)
    heads = [slice(h * HEAD_DIM, (h + 1) * HEAD_DIM) for h in hr]
    kc = [kc_ref[0, pl.ds(h, la, stride=N_HEADS), :].astype(BF16) for h in hr]
    vc = [vc_ref[0, pl.ds(h, la, stride=N_HEADS), :].astype(BF16) for h in hr]
    sc = [_dot_nt(q_ref[:, heads[h]], kc[h]) + bc_ref[h] for h in hr]
    sn = [_dot_nt(q_ref[:, heads[h]], kn_ref[:, heads[h]]) + bn_ref[h] for h in hr]
    m = [jnp.maximum(jnp.max(sc[h], axis=-1, keepdims=True), jnp.max(sn[h], axis=-1, keepdims=True))
         for h in hr]
    pc = [jnp.exp(sc[h] - m[h]) for h in hr]
    pn = [jnp.exp(sn[h] - m[h]) for h in hr]
    l = [jnp.sum(pc[h], axis=-1, keepdims=True) + jnp.sum(pn[h], axis=-1, keepdims=True) for h in hr]
    o = [_dot(pc[h].astype(BF16), vc[h]) + _dot(pn[h].astype(BF16), vn_ref[:, heads[h]]) for h in hr]
    o_ref[...] = jnp.concatenate([(o[h] / l[h]).astype(BF16) for h in hr], axis=1)


def _attn_sample(q16, k16, v16, cache_k, cache_v, bias_c, bias_n, row0, nseq, tn):
    blk0 = row0 // tn
    la8 = cache_k.shape[1]
    new = pl.BlockSpec((tn, D_HALF), lambda b: (blk0 + b, 0))
    cache = pl.BlockSpec((1, la8, HEAD_DIM), lambda b: (b, 0, 0))
    return pl.pallas_call(
        _attn_sample_body,
        out_shape=jax.ShapeDtypeStruct((nseq * tn, D_HALF), BF16),
        grid=(nseq,),
        in_specs=[new, cache, cache, new, new, _const_spec(bias_c.shape), _const_spec(bias_n.shape)],
        out_specs=pl.BlockSpec((tn, D_HALF), lambda b: (b, 0)),
        compiler_params=_cparams("parallel"),
        name="attn_sample",
    )(q16, cache_k, cache_v, k16, v16, bias_c, bias_n)


def _hgrn_body(q_ref, k_ref, v_ref, lf_ref, s0_ref, tcat_ref, o_ref, sout_ref, st_scr,
               *, chunk, nchunks):
    nblk = chunk // HGRN_BLOCK

    @pl.when(pl.program_id(1) == 0)
    def _():
        st_scr[...] = s0_ref[0]

    row = lax.broadcasted_iota(I32, (chunk, D_HALF), 0)
    r2 = lax.broadcasted_iota(I32, (chunk, chunk), 0)
    c2 = lax.broadcasted_iota(I32, (chunk, chunk), 1)
    same_blk_causal = jnp.logical_and(r2 // HGRN_BLOCK == c2 // HGRN_BLOCK, c2 <= r2)

    def one_chunk(c, carry):
        sl = pl.ds(pl.multiple_of(c * chunk, chunk), chunk)
        q = q_ref[sl, :]
        k = k_ref[sl, :]
        v16 = v_ref[sl, :].astype(BF16)
        l0, l1, l2 = _split3(lf_ref[sl, :])
        tcat = tcat_ref[...]
        both = _dot(tcat, l0) + _dot(tcat, l1) + _dot(tcat, l2)
        b_in = both[:chunk]
        b_ch = both[chunk:]
        ld = (q * jnp.exp(b_in)).astype(BF16)
        rd = (k * jnp.exp(-b_in)).astype(BF16)
        lj, rj = [], []
        for j in range(nblk - 1):
            e_j = b_ch[(j + 1) * HGRN_BLOCK - 1:(j + 1) * HGRN_BLOCK, :]
            later = row >= (j + 1) * HGRN_BLOCK
            inside = jnp.logical_and(row >= j * HGRN_BLOCK, row < (j + 1) * HGRN_BLOCK)
            lj.append(jnp.where(later, q * jnp.exp(jnp.minimum(b_ch - e_j, 0.0)), 0.0).astype(BF16))
            rj.append(jnp.where(inside, k * jnp.exp(jnp.minimum(e_j - b_ch, 0.0)), 0.0).astype(BF16))
        e_end = b_ch[chunk - 1:chunk, :]
        qc = (q * jnp.exp(b_ch)).astype(BF16)
        kc = (k * jnp.exp(e_end - b_ch)).astype(BF16)
        dec = jnp.exp(e_end)
        heads = [slice(h * HEAD_DIM, (h + 1) * HEAD_DIM) for h in range(N_HEADS)]
        sts = [st_scr[h] for h in range(N_HEADS)]
        diag = [_dot_nt(ld[:, hs], rd[:, hs]) for hs in heads]
        off = []
        for hs in heads:
            acc = None
            for j in range(nblk - 1):
                term = _dot_nt(lj[j][:, hs], rj[j][:, hs])
                acc = term if acc is None else acc + term
            off.append(acc)
        inter = [_dot_nt(qc[:, hs], st.astype(BF16)) for hs, st in zip(heads, sts)]
        upd = [_dot_tn(v16[:, hs], kc[:, hs]) for hs in heads]
        sc16 = [(jnp.where(same_blk_causal, d, 0.0) + o_).astype(BF16) for d, o_ in zip(diag, off)]
        intra = [_dot(s_, v16[:, hs]) for s_, hs in zip(sc16, heads)]
        o_ref[sl, :] = jnp.concatenate([a + b for a, b in zip(intra, inter)], axis=1)
        for h, hs in enumerate(heads):
            st_scr[h] = sts[h] * dec[:, hs] + upd[h]
        return carry

    lax.fori_loop(0, nchunks, one_chunk, 0, unroll=min(nchunks, HGRN_UNROLL))
    sout_ref[0] = st_scr[...]


def _hgrn(qb, kb, vb, lf, s0t, row0, nseq, seq, chunk, nchunks):
    tt = chunk * nchunks
    steps = seq // tt
    blk0 = row0 // tt
    t = jnp.arange(chunk)
    lower = t[None, :] <= t[:, None]
    same_blk = (t[None, :] // HGRN_BLOCK) == (t[:, None] // HGRN_BLOCK)
    tcat = jnp.concatenate([jnp.logical_and(lower, same_blk), lower], axis=0).astype(BF16)
    tok = pl.BlockSpec((tt, D_HALF), lambda b, j: (blk0 + b * steps + j, 0))
    state = pl.BlockSpec((1, N_HEADS, HEAD_DIM, HEAD_DIM), lambda b, j: (b, 0, 0, 0))
    return pl.pallas_call(
        functools.partial(_hgrn_body, chunk=chunk, nchunks=nchunks),
        out_shape=[jax.ShapeDtypeStruct((nseq * seq, D_HALF), F32),
                   jax.ShapeDtypeStruct(s0t.shape, F32)],
        grid=(nseq, steps),
        in_specs=[tok, tok, tok, tok, state, _const_spec((2 * chunk, chunk))],
        out_specs=[pl.BlockSpec((tt, D_HALF), lambda b, j: (b * steps + j, 0)), state],
        scratch_shapes=[pltpu.VMEM((N_HEADS, HEAD_DIM, HEAD_DIM), F32)],
        compiler_params=_cparams("parallel", "arbitrary"),
        name="hgrn",
    )(qb, kb, vb, lf, s0t, tcat)


def _route(xn, wr_hi, wr_lo, br, tstrict, cnt_scr, route_o, cnt_o, code_o):
    rows = xn.shape[0]
    x_hi, x_lo = _split2(xn)
    logits = _dot(x_hi, wr_hi) + _dot(x_lo, wr_hi) + _dot(x_hi, wr_lo) + br
    lane = lax.broadcasted_iota(I32, (rows, LANES), 1)
    lane_f = lane.astype(F32)

    def first_argmax(vals):
        m = jnp.max(vals, axis=-1, keepdims=True)
        idx = jnp.min(jnp.where(vals == m, lane_f, float(LANES)), axis=-1, keepdims=True)
        return m, idx.astype(I32)

    is_grp = jnp.logical_and(lane >= N_EXPERTS, lane < N_EXPERTS + N_GROUPS)
    gl = jnp.where(is_grp, logits, -jnp.inf)
    gmax, gidx = first_argmax(gl)
    p_grp = 1.0 / jnp.sum(jnp.exp(gl - gmax), axis=-1, keepdims=True)
    grp = gidx - N_EXPERTS
    in_grp = jnp.logical_and(lane < N_EXPERTS, lane // EXPERTS_PER_GROUP == grp)
    el = jnp.where(in_grp, logits, -jnp.inf)
    v1, e1 = first_argmax(el)
    v2, e2 = first_argmax(jnp.where(lane == e1, -jnp.inf, el))
    t2 = jnp.exp(v2 - v1)
    den = 1.0 + t2
    g1 = (1.0 / den) * p_grp
    g2 = (t2 / den) * p_grp

    oh1 = lane == e1
    oh2 = lane == e2
    oh = jnp.where(jnp.logical_or(oh1, oh2), 1.0, 0.0)
    before = _dot(tstrict, oh.astype(BF16)) + cnt_scr[...]
    rank1 = jnp.sum(jnp.where(oh1, before, 0.0), axis=-1, keepdims=True)
    rank2 = jnp.sum(jnp.where(oh2, before, 0.0), axis=-1, keepdims=True)
    cnt_scr[...] = cnt_scr[...] + jnp.sum(oh, axis=0, keepdims=True)
    cnt_o[...] = cnt_scr[...]

    e1f, e2f = e1.astype(F32), e2.astype(F32)
    fields = (e1f, e2f, g1, g2, rank1, rank2, e1f * RANK_SPAN + rank1, e2f * RANK_SPAN + rank2)
    slab = jnp.zeros((rows, LANES), F32)
    for idx, val in enumerate(fields):
        slab = jnp.where(lane == idx, val, slab)
    route_o[...] = slab
    code_o[...] = slab.T[6:8, :]


def _out_ab_body(hp_ref, hs_ref, oap_ref, oas_ref, obp_ref, obs_ref, gb_ref, og_ref, bd_ref, w_ref,
                 gf_ref, wrh_ref, wrl_ref, br_ref, ts_ref, h_o, xn_o, route_o, cnt_o, code_o, cnt_scr,
                 *, tiles_p):
    @pl.when(pl.program_id(0) == 0)
    def _():
        cnt_scr[...] = jnp.zeros_like(cnt_scr)

    ob = _group_pick(obp_ref, obs_ref, tiles_p)
    obn = ob * lax.rsqrt(_head_mean_sq(ob, bd_ref[...]) + RMS_EPS) * og_ref[...]
    obg = (obn * jax.nn.silu(gb_ref[...])).astype(BF16)
    oa = _group_pick(oap_ref, oas_ref, tiles_p)
    mix = _dot(oa, w_ref[0:D_HALF, :]) + _dot(obg, w_ref[D_HALF:, :])
    h1 = _group_pick(hp_ref, hs_ref, tiles_p) + mix
    h_o[...] = h1
    xn = _rms(h1, gf_ref[...])
    _store_row_tiles(xn_o, xn)
    _route(xn, wrh_ref[...], wrl_ref[...], br_ref[...], ts_ref[...], cnt_scr, route_o, cnt_o, code_o)


def _out_ab(h_p, h_s, oa_p, oa_s, ob_p, ob_s, gb, og, bd, w16, gf, wrh, wrl, br, tstrict):
    tm = ROUTE_TILE
    n = h_p.shape[0] + h_s.shape[0]
    tiles_p = h_p.shape[0] // tm
    row = lambda w: pl.BlockSpec((tm, w), lambda i: (i, 0))
    return pl.pallas_call(
        functools.partial(_out_ab_body, tiles_p=tiles_p),
        out_shape=[jax.ShapeDtypeStruct((n, D_MODEL), F32), jax.ShapeDtypeStruct((n * ROW_TILE, LANES), F32),
                   jax.ShapeDtypeStruct((n, LANES), F32), jax.ShapeDtypeStruct((1, LANES), F32),
                   jax.ShapeDtypeStruct((2, n), F32)],
        grid=(n // tm,),
        in_specs=_group_specs(D_MODEL, tiles_p, tm) + _group_specs(D_HALF, tiles_p, tm)
        + _group_specs(D_HALF, tiles_p, tm) + [
            row(D_HALF), _const_spec((1, D_HALF)),
            _const_spec((D_HALF, D_HALF)), _const_spec((D_MODEL, D_MODEL)),
            _const_spec((1, D_MODEL)), _const_spec((D_MODEL, LANES)),
            _const_spec((D_MODEL, LANES)), _const_spec((1, LANES)), _const_spec((tm, tm))],
        out_specs=[row(D_MODEL), pl.BlockSpec((tm * ROW_TILE, LANES), lambda i: (i, 0)), row(LANES),
                   _const_spec((1, LANES)), pl.BlockSpec((2, tm), lambda i: (0, i))],
        scratch_shapes=[pltpu.VMEM((1, LANES), F32)],
        compiler_params=_cparams("arbitrary"),
        name="out_ab",
    )(h_p, h_s, oa_p, oa_s, ob_p, ob_s, gb, og, bd, w16, gf, wrh, wrl, br, tstrict)


def _row_copy(src, src_row, dst, dst_row, sem):
    s0 = pl.multiple_of(src_row * ROW_TILE, ROW_TILE)
    d0 = pl.multiple_of(dst_row * ROW_TILE, ROW_TILE)
    return pltpu.make_async_copy(src.at[pl.ds(s0, ROW_TILE)], dst.at[pl.ds(d0, ROW_TILE)], sem)


def _collect_body(pos_ref, ys_hbm, h_ref, r_ref, *rest, tile, n_tok, tiles_p, proj):
    if proj:
        g_ref, w_ref = rest[:2]
        rest = rest[2:]
    outs, (buf, sems) = rest[:-2], rest[-2:]
    i = pl.program_id(0)
    nt = pl.num_programs(0)

    def issue(t):
        slot = t % 2
        base = t * tile

        def one(r, carry):
            tok = base + r
            for k in range(2):
                row = pos_ref[k * n_tok + tok]
                _row_copy(ys_hbm, row, buf.at[slot].at[k], r, sems.at[slot]).start(priority=k)
            return carry

        lax.fori_loop(0, tile, one, 0, unroll=GATHER_UNROLL)

    @pl.when(i == 0)
    def _():
        issue(0)

    @pl.when(i + 1 < nt)
    def _():
        issue(i + 1)

    slot = i % 2
    for _ in range(2 * tile):
        _row_copy(ys_hbm, 0, buf.at[slot].at[0], 0, sems.at[slot]).wait()
    route = r_ref[...]
    ya = _load_row_tiles(buf.at[slot].at[0], tile)
    yb = _load_row_tiles(buf.at[slot].at[1], tile)
    out = h_ref[...] + (route[:, 2:3] * ya + route[:, 3:4] * yb)
    if proj:
        h_o, u_o, bg_o = outs
        h_o[...] = out
        xb = _rms(out, g_ref[...]).astype(BF16)
        bg_o[...] = _dot(xb, w_ref[:, 0:D_MODEL])
        u_o[...] = _dot(xb, w_ref[:, D_MODEL:2 * D_MODEL]) * _dot(xb, w_ref[:, 2 * D_MODEL:])
    else:
        @pl.when(i < tiles_p)
        def _():
            outs[0][...] = out

        @pl.when(i >= tiles_p)
        def _():
            outs[1][...] = out


def _collect(pos, ys, h, route, *, proj=None, n_p=None):
    tm = TOKEN_TILE
    n = h.shape[0]
    row = lambda w: pl.BlockSpec((tm, w), lambda i, pos: (i, 0))
    const = lambda shape: pl.BlockSpec(shape, lambda i, pos: (0,) * len(shape))
    if proj is not None:
        tiles_p = None
        extra_in, extra_specs = list(proj), [const(proj[0].shape), const(proj[1].shape)]
        out_shape = [jax.ShapeDtypeStruct((n, D_MODEL), F32)] * 3
        out_specs = [row(D_MODEL)] * 3
    else:
        tiles_p = n_p // tm
        extra_in, extra_specs = [], []
        out_shape = [jax.ShapeDtypeStruct((n_p, D_MODEL), F32),
                     jax.ShapeDtypeStruct((n - n_p, D_MODEL), F32)]
        out_specs = [pl.BlockSpec((tm, D_MODEL), lambda i, pos: (jnp.minimum(i, tiles_p - 1), 0)),
                     pl.BlockSpec((tm, D_MODEL), lambda i, pos: (jnp.maximum(i - tiles_p, 0), 0))]
    return pl.pallas_call(
        functools.partial(_collect_body, tile=tm, n_tok=n, tiles_p=tiles_p, proj=proj is not None),
        out_shape=out_shape,
        grid_spec=pltpu.PrefetchScalarGridSpec(
            num_scalar_prefetch=1,
            grid=(n // tm,),
            in_specs=[pl.BlockSpec(memory_space=pl.ANY), row(D_MODEL), row(LANES)] + extra_specs,
            out_specs=out_specs,
            scratch_shapes=[pltpu.VMEM((2, 2, tm * ROW_TILE, LANES), F32),
                            pltpu.SemaphoreType.DMA((2,))],
        ),
        compiler_params=pltpu.CompilerParams(dimension_semantics=("arbitrary",),
                                             vmem_limit_bytes=VMEM_LIMIT),
        name="moe_collect",
    )(pos, ys, h, route, *extra_in)


def _scatter_rows_body(pos_ref, pad0_ref, padn_ref, nt_ref, x_ref, xs_hbm,
                       stage, zero_scr, sems, pad_sem, *, tile, n_tok, out_tiles):
    i = pl.program_id(0)
    last = pl.num_programs(0) - 1
    base = i * tile
    slot = i % 2

    def drain(s):
        for _ in range(2 * tile):
            _row_copy(stage.at[s], 0, xs_hbm, 0, sems.at[s]).wait()

    @pl.when(i >= 2)
    def _():
        drain(slot)

    stage[slot] = x_ref[...]

    def issue(r, carry):
        tok = base + r
        src = stage.at[slot]
        for k in range(2):
            _row_copy(src, r, xs_hbm, pos_ref[k * n_tok + tok], sems.at[slot]).start(priority=k)
        return carry

    lax.fori_loop(0, tile, issue, 0, unroll=GATHER_UNROLL)

    @pl.when(i == last)
    def _():
        zero_scr[...] = jnp.zeros_like(zero_scr)

        def fill(lo, count):
            def one(r, carry):
                _row_copy(zero_scr, 0, xs_hbm, lo + r, pad_sem).start()
                return carry

            def done(r, carry):
                _row_copy(zero_scr, 0, xs_hbm, 0, pad_sem).wait()
                return carry

            lax.fori_loop(0, count, one, 0)
            lax.fori_loop(0, count, done, 0)

        for e in range(N_EXPERTS):
            fill(pad0_ref[e], padn_ref[e])

        def tail_copy(t):
            rows = MOE_TILE * ROW_TILE
            return pltpu.make_async_copy(
                zero_scr, xs_hbm.at[pl.ds(pl.multiple_of(t * rows, rows), rows)], pad_sem)

        def tail_start(t, carry):
            tail_copy(t).start()
            return carry

        def tail_done(t, carry):
            tail_copy(t).wait()
            return carry

        lax.fori_loop(nt_ref[0], out_tiles, tail_start, 0)
        lax.fori_loop(nt_ref[0], out_tiles, tail_done, 0)
        drain(slot)

    @pl.when(jnp.logical_and(i == last, i >= 1))
    def _():
        drain(1 - slot)


def _scatter_rows(pos, pad0, padn, n_tiles, xn, n_rows):
    tm = TOKEN_TILE
    n = xn.shape[0] // ROW_TILE
    return pl.pallas_call(
        functools.partial(_scatter_rows_body, tile=tm, n_tok=n, out_tiles=n_rows // MOE_TILE),
        out_shape=jax.ShapeDtypeStruct((n_rows * ROW_TILE, LANES), F32),
        grid_spec=pltpu.PrefetchScalarGridSpec(
            num_scalar_prefetch=4,
            grid=(n // tm,),
            in_specs=[pl.BlockSpec((tm * ROW_TILE, LANES), lambda i, *_: (i, 0))],
            out_specs=pl.BlockSpec(memory_space=pl.ANY),
            scratch_shapes=[pltpu.VMEM((2, tm * ROW_TILE, LANES), F32),
                            pltpu.VMEM((MOE_TILE * ROW_TILE, LANES), F32),
                            pltpu.SemaphoreType.DMA((2,)), pltpu.SemaphoreType.DMA(())],
        ),
        compiler_params=pltpu.CompilerParams(dimension_semantics=("arbitrary",)),
        name="moe_scatter",
    )(pos, pad0, padn, n_tiles, xn)


def _experts_body(te_ref, nt_ref, x_ref, w1_ref, w3_ref, w2_ref, y_ref, w1_s, w3_s, w2_s):
    i = pl.program_id(0)
    prev = te_ref[jnp.maximum(i - 1, 0)]
    fresh = jnp.logical_or(i == 0, te_ref[i] != prev)

    @pl.when(fresh)
    def _():
        w1_s[...] = w1_ref[0].astype(BF16)
        w3_s[...] = w3_ref[0].astype(BF16)
        w2_s[...] = w2_ref[0].astype(BF16)

    @pl.when(i < nt_ref[0])
    def _():
        x = _load_row_tiles(x_ref, MOE_TILE).astype(BF16)
        a = _dot(x, w1_s[...])
        b = _dot(x, w3_s[...])
        _store_row_tiles(y_ref, _dot((jax.nn.silu(a) * b).astype(BF16), w2_s[...]))


def _experts(tile_expert, n_tiles, xs, w1, w3, w2):
    rows = xs.shape[0] // ROW_TILE
    tm = MOE_TILE
    tile = pl.BlockSpec((tm * ROW_TILE, LANES), lambda i, te, nt: (jnp.minimum(i, nt[0] - 1), 0))
    return pl.pallas_call(
        _experts_body,
        out_shape=jax.ShapeDtypeStruct((rows * ROW_TILE, LANES), F32),
        grid_spec=pltpu.PrefetchScalarGridSpec(
            num_scalar_prefetch=2,
            grid=(rows // tm,),
            in_specs=[tile,
                      pl.BlockSpec((1, D_MODEL, D_EXPERT), lambda i, te, nt: (te[i], 0, 0)),
                      pl.BlockSpec((1, D_MODEL, D_EXPERT), lambda i, te, nt: (te[i], 0, 0)),
                      pl.BlockSpec((1, D_EXPERT, D_MODEL), lambda i, te, nt: (te[i], 0, 0))],
            out_specs=tile,
            scratch_shapes=[pltpu.VMEM((D_MODEL, D_EXPERT), BF16), pltpu.VMEM((D_MODEL, D_EXPERT), BF16),
                            pltpu.VMEM((D_EXPERT, D_MODEL), BF16)],
        ),
        input_output_aliases={2: 0},
        compiler_params=_cparams("arbitrary"),
        name="moe_experts",
    )(tile_expert, n_tiles, xs, w1, w3, w2)


def _moe_plan(counts, n):
    counts = counts[0, :N_EXPERTS].astype(I32)
    padded = ((counts + MOE_TILE - 1) // MOE_TILE) * MOE_TILE
    ends = jnp.cumsum(padded)
    offs = ends - padded
    max_tiles = (2 * n + N_EXPERTS * (MOE_TILE - 1) + MOE_TILE - 1) // MOE_TILE
    tile_start = jnp.arange(max_tiles, dtype=I32) * MOE_TILE
    tile_expert = jnp.minimum(jnp.sum((tile_start[:, None] >= ends[None, :]).astype(I32), axis=1),
                              N_EXPERTS - 1)
    n_tiles = (ends[-1] // MOE_TILE).reshape(1)
    return offs, offs + counts, padded - counts, tile_expert, n_tiles, max_tiles * MOE_TILE


def _moe(xn, h, route, code, counts, layer, w1, w3, w2, **collect_mode):
    n = h.shape[0]
    code = code.astype(I32).reshape(2 * n)
    offs, pad0, padn, tile_expert, n_tiles, n_rows = _moe_plan(counts, n)
    hit = (code >> RANK_BITS)[None, :] == jnp.arange(N_EXPERTS, dtype=I32)[:, None]
    pos = jnp.sum(jnp.where(hit, offs[:, None], 0), axis=0) + (code & (RANK_SPAN - 1))
    xs = _scatter_rows(pos, pad0, padn, n_tiles, xn, n_rows)
    flat = lambda w: w.reshape((-1,) + w.shape[2:])
    ys = _experts(tile_expert + layer * N_EXPERTS, n_tiles, xs, flat(w1), flat(w3), flat(w2))
    return _collect(pos, ys, h, route, **collect_mode)


def _conv_out_body(u_ref, up_ref, stp_ref, sts_ref, bg_ref, h_ref, cw_ref, w_ref, gf_ref,
                   wrh_ref, wrl_ref, br_ref, ts_ref, h_o, xn_o, route_o, cnt_o, code_o, cnt_scr,
                   *, tile, tiles_p, seq_p, seq_s):
    i = pl.program_id(0)

    @pl.when(i == 0)
    def _():
        cnt_scr[...] = jnp.zeros_like(cnt_scr)

    u = u_ref[...]
    rowi = lax.broadcasted_iota(I32, (tile, D_MODEL), 0)
    is_p = i < tiles_p
    at_start = (i * tile) % seq_p == 0
    stp = stp_ref[0]
    prev = up_ref[...]
    m2_p = jnp.where(at_start, stp[0:1, :], prev[6:7, :])
    m1_p = jnp.where(at_start, stp[1:2, :], prev[7:8, :])
    per = tile // seq_s
    sts = sts_ref[...]
    m2_s = jnp.broadcast_to(sts[:, 0:1, :], (per, seq_s, D_MODEL)).reshape(tile, D_MODEL)
    m1_s = jnp.broadcast_to(sts[:, 1:2, :], (per, seq_s, D_MODEL)).reshape(tile, D_MODEL)
    m2 = jnp.where(is_p, m2_p, m2_s)
    m1 = jnp.where(is_p, m1_p, m1_s)
    pos = jnp.where(is_p, rowi, rowi % seq_s)
    u1 = jnp.where(pos == 0, m1, pltpu.roll(u, 1, axis=0))
    u2 = jnp.where(pos == 0, m2, jnp.where(pos == 1, m1, pltpu.roll(u, 2, axis=0)))
    cw = cw_ref[...]
    conv = u2 * cw[0:1, :] + u1 * cw[1:2, :] + u * cw[2:3, :]
    mix = _dot((bg_ref[...] * conv).astype(BF16), w_ref[...])
    h3 = h_ref[...] + mix
    h_o[...] = h3
    xn = _rms(h3, gf_ref[...])
    _store_row_tiles(xn_o, xn)
    _route(xn, wrh_ref[...], wrl_ref[...], br_ref[...], ts_ref[...], cnt_scr, route_o, cnt_o, code_o)


def _conv_out(u, bg, h, state_p, state_s, cw, w16, gf, wrh, wrl, br, tstrict, seq_p, seq_s):
    tm = ROUTE_TILE
    n = u.shape[0]
    tiles_p = state_p.shape[0] * seq_p // tm
    per = tm // seq_s
    row = lambda w: pl.BlockSpec((tm, w), lambda i: (i, 0))
    prev = pl.BlockSpec((8, D_MODEL), lambda i: (jnp.maximum(i * (tm // 8) - 1, 0), 0))
    stp_spec = pl.BlockSpec((1, 2, D_MODEL),
                            lambda i: (jnp.minimum(i, tiles_p - 1) * tm // seq_p, 0, 0))
    sts_spec = pl.BlockSpec((per, 2, D_MODEL), lambda i: (jnp.maximum(i - tiles_p, 0), 0, 0))
    return pl.pallas_call(
        functools.partial(_conv_out_body, tile=tm, tiles_p=tiles_p, seq_p=seq_p, seq_s=seq_s),
        out_shape=[jax.ShapeDtypeStruct((n, D_MODEL), F32),
                   jax.ShapeDtypeStruct((n * ROW_TILE, LANES), F32),
                   jax.ShapeDtypeStruct((n, LANES), F32), jax.ShapeDtypeStruct((1, LANES), F32),
                   jax.ShapeDtypeStruct((2, n), F32)],
        grid=(n // tm,),
        in_specs=[row(D_MODEL), prev, stp_spec, sts_spec, row(D_MODEL), row(D_MODEL),
                  _const_spec((3, D_MODEL)),
                  _const_spec((D_MODEL, D_MODEL)), _const_spec((1, D_MODEL)),
                  _const_spec((D_MODEL, LANES)), _const_spec((D_MODEL, LANES)),
                  _const_spec((1, LANES)), _const_spec((tm, tm))],
        out_specs=[row(D_MODEL), pl.BlockSpec((tm * ROW_TILE, LANES), lambda i: (i, 0)), row(LANES),
                   _const_spec((1, LANES)), pl.BlockSpec((2, tm), lambda i: (0, i))],
        scratch_shapes=[pltpu.VMEM((1, LANES), F32)],
        compiler_params=_cparams("arbitrary"),
        name="conv_out",
    )(u, u, state_p, state_s, bg, h, cw, w16, gf, wrh, wrl, br, tstrict)


def _rel_bias_toeplitz(rel_bias, rows, cols, lead):
    period = rows + cols
    k = jnp.arange(period)
    d = jnp.where(k < cols, k, k - period)
    idx = jnp.clip(lead - d, -REL_CLIP, REL_CLIP) + REL_CLIP
    v = jnp.take(rel_bias.astype(F32), idx, axis=1)
    heads = v.shape[0]
    skew = jnp.tile(v, (1, rows))[:, :rows * (period - 1)].reshape(heads, rows, period - 1)
    return skew[:, :, :cols]


def _bias_prompt(rel_bias):
    r = jnp.arange(ATTN_TILE)[:, None]
    c = jnp.arange(ATTN_WINDOW)[None, :]
    j = c // CHUNK - r // CHUNK
    band = jnp.logical_and(j >= 0, j <= BAND_CHUNKS)
    bias = _rel_bias_toeplitz(rel_bias, ATTN_TILE, ATTN_WINDOW, BAND_CHUNKS * CHUNK)
    return jnp.where(band[None], bias, NEG_INF)


def _bias_sample(rel_bias, la, tn):
    bias = _rel_bias_toeplitz(rel_bias, tn, la + tn, la)
    return bias[:, :, :la], bias[:, :, la:]


def _router_weights(w_group, b_group, w_expert, b_expert):
    pad = LANES - N_EXPERTS - N_GROUPS
    w = jnp.concatenate([w_expert, w_group, jnp.zeros((D_MODEL, pad), F32)], axis=1)
    b = jnp.concatenate([b_expert, b_group, jnp.zeros((pad,), F32)])[None, :].astype(F32)
    hi, lo = _split2(w.astype(F32))
    return hi, lo, b


def kernel(x_prompt, x_sample, cache_a_k, cache_a_v, state_hgrn, state_conv, norm_mix, norm_ffn,
           w_in_ab, w_out_ab, q_norm, k_norm, rel_bias, hgrn_lb_logits, hgrn_out_norm, w_in_c,
           conv_w, w_out_c, w_group, b_group, w_expert, b_expert, w1, w3, w2):
    batch, seq, d = x_prompt.shape
    nseq_s, tn, _ = x_sample.shape
    la = cache_a_k.shape[2]
    n_p = batch * seq
    n_s = nseq_s * tn
    n = n_p + n_s
    keep = min(BAND_CHUNKS * CHUNK, seq)

    x_p = x_prompt.reshape(n_p, d)
    x_s = x_sample.reshape(n_s, d)
    lb_all = jnp.cumsum(jax.nn.softmax(hgrn_lb_logits.astype(F32), axis=0), axis=0)
    head_avg = jnp.kron(jnp.eye(N_HEADS, dtype=F32),
                        jnp.full((HEAD_DIM, HEAD_DIM), 1.0 / HEAD_DIM, F32)).astype(BF16)
    t = jnp.arange(ROUTE_TILE)
    tstrict = (t[None, :] < t[:, None]).astype(BF16)
    row1 = lambda v: v.astype(F32).reshape(1, -1)
    tile8 = lambda v: jnp.tile(v.astype(F32), N_HEADS).reshape(1, -1)

    l = 0
    (q16, kf, vf, k16, v16, lf, kb, vb, qb, gb) = _proj_ab(
        x_p, x_s, row1(norm_mix[0]), w_in_ab[l].astype(BF16), tile8(q_norm[l]), tile8(k_norm[l]),
        row1(lb_all[l]), head_avg, seq, keep)

    oa_p = _attn_prompt(q16, k16, v16, _bias_prompt(rel_bias[l]), batch, seq)
    bias_c, bias_n = _bias_sample(rel_bias[l], la, tn)
    oa_s = _attn_sample(q16, k16, v16, cache_a_k[l].reshape(nseq_s, la * N_HEADS, HEAD_DIM),
                        cache_a_v[l].reshape(nseq_s, la * N_HEADS, HEAD_DIM), bias_c, bias_n,
                        n_p, nseq_s, tn)

    zeros_state = jnp.zeros((batch, N_HEADS, HEAD_DIM, HEAD_DIM), F32)
    ob_p, st_p = _hgrn(qb, kb, vb, lf, zeros_state, 0, batch, seq, 64, 4)
    ob_s, st_s = _hgrn(qb, kb, vb, lf, jnp.swapaxes(state_hgrn[l].astype(F32), -1, -2),
                       n_p, nseq_s, tn, tn, 1)

    wrh, wrl, br = _router_weights(w_group[0], b_group[0], w_expert[0], b_expert[0])
    h1, xn1, route1, cnt1, code1 = _out_ab(x_p, x_s, oa_p, oa_s, ob_p, ob_s, gb, row1(hgrn_out_norm[l]),
                                    head_avg, w_out_ab[l].astype(BF16), row1(norm_ffn[0]),
                                    wrh, wrl, br, tstrict)
    h2, u, bg = _moe(xn1, h1, route1, code1, cnt1, 0, w1, w3, w2,
                     proj=(row1(norm_mix[1]), w_in_c[0].astype(BF16)))

    wrh, wrl, br = _router_weights(w_group[1], b_group[1], w_expert[1], b_expert[1])
    h3, xn2, route2, cnt2, code2 = _conv_out(u, bg, h2, jnp.zeros((batch, 2, d), F32),
                                      state_conv[0].astype(F32), conv_w[0].astype(F32),
                                      w_out_c[0].astype(BF16), row1(norm_ffn[1]), wrh, wrl, br,
                                      tstrict, seq, tn)
    out_p, out_s = _moe(xn2, h3, route2, code2, cnt2, 1, w1, w3, w2, n_p=n_p)

    y_prompt = out_p.reshape(batch, seq, d)
    y_sample = out_s.reshape(nseq_s, tn, d)
    n_kp = batch * keep
    heads = lambda a, b_, t_: a.reshape(b_, t_, N_HEADS, HEAD_DIM)
    nk_p = heads(kf[:n_kp], batch, keep)[None]
    nv_p = heads(vf[:n_kp], batch, keep)[None]
    kf_s = heads(kf[n_kp:n_kp + n_s], nseq_s, tn)
    vf_s = heads(vf[n_kp:n_kp + n_s], nseq_s, tn)
    nk_s = jnp.concatenate([cache_a_k[l].astype(F32), kf_s], axis=1)[:, -la:][None]
    nv_s = jnp.concatenate([cache_a_v[l].astype(F32), vf_s], axis=1)[:, -la:][None]
    nh_p = jnp.swapaxes(st_p, -1, -2)[None]
    nh_s = jnp.swapaxes(st_s, -1, -2)[None]
    nc_p = jnp.stack([u[(b + 1) * seq - 2:(b + 1) * seq] for b in range(batch)])[None]
    u_s = u[n_p:].reshape(nseq_s, tn, d)
    nc_s = jnp.concatenate([state_conv[0].astype(F32), u_s], axis=1)[:, -2:][None]
    return (y_prompt, y_sample, nk_p, nv_p, nk_s, nv_s, nh_p, nh_s, nc_p, nc_s)
```

```python
import functools

import jax
import jax.numpy as jnp
from jax import lax
from jax.experimental import pallas as pl
from jax.experimental.pallas import tpu as pltpu

F32 = jnp.float32
BF16 = jnp.bfloat16
I32 = jnp.int32

D_MODEL = 1024
CHUNK = 64
BAND_CHUNKS = 8
HEAD_DIM = 64
N_HEADS = 8
D_HALF = N_HEADS * HEAD_DIM
REL_CLIP = 128
HGRN_BLOCK = 16
N_GROUPS = 4
EXPERTS_PER_GROUP = 8
N_EXPERTS = N_GROUPS * EXPERTS_PER_GROUP
D_EXPERT = 256
RMS_EPS = 1e-6
NEG_INF = -1e30

LANES = 128
ROW_TILE = D_MODEL // LANES
TOKEN_TILE = 512
COLLECT_TILE = 256
PROJ_TILE = 512
ROUTE_TILE = 512
ATTN_TILE = 256
ATTN_WINDOW = ATTN_TILE + BAND_CHUNKS * CHUNK
ATTN_UNIT = 2 * CHUNK
ATTN_SPAN = ATTN_UNIT + BAND_CHUNKS * CHUNK
HGRN_UNROLL = 4
ATTN_HEAD_GROUP = 2
MOE_TILE = 256
RANK_BITS = 16
RANK_SPAN = 1 << RANK_BITS
GATHER_UNROLL = 8
VMEM_LIMIT = 48 * 1024 * 1024


def _cparams(*sem):
    return pltpu.CompilerParams(dimension_semantics=sem, vmem_limit_bytes=VMEM_LIMIT)


def _const_spec(shape):
    nd = len(shape)
    return pl.BlockSpec(shape, lambda *_: (0,) * nd)


def _store_row_tiles(ref, val):
    rows = val.shape[0]
    for c in range(ROW_TILE):
        ref[pl.ds(c, rows, stride=ROW_TILE), :] = val[:, c * LANES:(c + 1) * LANES]


def _load_row_tiles(ref, rows):
    return jnp.concatenate([ref[pl.ds(c, rows, stride=ROW_TILE), :] for c in range(ROW_TILE)], axis=1)


def _rms(x, gain):
    ms = jnp.mean(x * x, axis=-1, keepdims=True)
    return (x * lax.rsqrt(ms + RMS_EPS)) * gain


def _split2(x):
    hi = x.astype(BF16)
    lo = (x - hi.astype(F32)).astype(BF16)
    return hi, lo


def _split3(x):
    p0 = x.astype(BF16)
    r = x - p0.astype(F32)
    p1 = r.astype(BF16)
    p2 = (r - p1.astype(F32)).astype(BF16)
    return p0, p1, p2


def _dot(a, b):
    return jnp.dot(a, b, preferred_element_type=F32)


def _dot_nt(a, b):
    return lax.dot_general(a, b, (((1,), (1,)), ((), ())), preferred_element_type=F32)


def _dot_tn(a, b):
    return lax.dot_general(a, b, (((0,), (0,)), ((), ())), preferred_element_type=F32)


def _head_mean_sq(v, bd):
    hi, lo = _split2(v * v)
    return _dot(hi, bd) + _dot(lo, bd)


def _group_specs(width, tiles_p, tm=TOKEN_TILE):
    return [pl.BlockSpec((tm, width), lambda i: (jnp.minimum(i, tiles_p - 1), 0)),
            pl.BlockSpec((tm, width), lambda i: (jnp.maximum(i - tiles_p, 0), 0))]


def _group_pick(p_ref, s_ref, tiles_p):
    return jnp.where(pl.program_id(0) < tiles_p, p_ref[...], s_ref[...])


def _proj_ab_body(xp_ref, xs_ref, g_ref, w_ref, qg_ref, kg_ref, lb_ref, bd_ref,
                  q_o, kf_o, vf_o, k16_o, v16_o, lf_o, kb_o, vb_o, qb_o, gb_o, *, tiles_p):
    xb = _rms(_group_pick(xp_ref, xs_ref, tiles_p), g_ref[...]).astype(BF16)
    bd = bd_ref[...]

    def seg(j):
        return _dot(xb, w_ref[:, j * D_HALF:(j + 1) * D_HALF])

    qa = seg(0)
    qn = qa * lax.rsqrt(_head_mean_sq(qa, bd) + RMS_EPS) * qg_ref[...]
    q_o[...] = (qn * (HEAD_DIM ** -0.5)).astype(BF16)
    ka = seg(1)
    kn = ka * lax.rsqrt(_head_mean_sq(ka, bd) + RMS_EPS) * kg_ref[...]
    kf_o[...] = kn
    k16_o[...] = kn.astype(BF16)
    va = seg(2)
    vf_o[...] = va
    v16_o[...] = va.astype(BF16)
    lb = lb_ref[...]
    f = lb + (1.0 - lb) * jax.nn.sigmoid(seg(3))
    lf_o[...] = jnp.log(f)
    kb_o[...] = 1.0 - f
    vb_o[...] = seg(4)
    qb_o[...] = jax.nn.silu(seg(5))
    gb_o[...] = seg(6)


def _proj_ab(x_p, x_s, gain, w16, qg, kg, lb, bd, seq, keep):
    tm = PROJ_TILE
    n_p, n_s = x_p.shape[0], x_s.shape[0]
    n = n_p + n_s
    tiles_p, tiles_seq, tiles_keep = n_p // tm, seq // tm, keep // tm
    kept_tiles = (n_p // seq) * tiles_keep + n_s // tm

    def keep_map(i):
        b, j = i // tiles_seq, i % tiles_seq
        prompt_slot = b * tiles_keep + jnp.maximum(j - (tiles_seq - tiles_keep), 0)
        return (jnp.where(i >= tiles_p, kept_tiles - n_s // tm + (i - tiles_p), prompt_slot), 0)

    row = pl.BlockSpec((tm, D_HALF), lambda i: (i, 0))
    kept = pl.BlockSpec((tm, D_HALF), keep_map)
    full = lambda dt: jax.ShapeDtypeStruct((n, D_HALF), dt)
    kept_shape = jax.ShapeDtypeStruct((kept_tiles * tm, D_HALF), F32)
    return pl.pallas_call(
        functools.partial(_proj_ab_body, tiles_p=tiles_p),
        out_shape=[full(BF16), kept_shape, kept_shape, full(BF16), full(BF16),
                   full(F32), full(F32), full(F32), full(F32), full(F32)],
        grid=(n // tm,),
        in_specs=_group_specs(D_MODEL, tiles_p, tm) + [
            _const_spec((1, D_MODEL)), _const_spec(w16.shape),
            _const_spec((1, D_HALF)), _const_spec((1, D_HALF)), _const_spec((1, D_HALF)),
            _const_spec((D_HALF, D_HALF))],
        out_specs=[row, kept, kept] + [row] * 7,
        compiler_params=_cparams("arbitrary"),
        name="proj_ab",
    )(x_p, x_s, gain, w16, qg, kg, lb, bd)


def _attn_prompt_body(q_ref, k0, k1, k2, v0, v1, v2, bias_ref, o_ref):
    i = pl.program_id(1)
    lim = jnp.maximum(2 - i, 0) * ATTN_TILE
    col = lax.broadcasted_iota(I32, (ATTN_UNIT, ATTN_SPAN), 1)
    cut = ATTN_SPAN - 2 * ATTN_TILE

    def span(parts, start):
        if start == 0:
            return (parts[0], parts[1], parts[2][:cut])
        return (parts[0][start:], parts[1], parts[2])

    def head_group(h0):
        units = []
        for h in range(h0, h0 + ATTN_HEAD_GROUP):
            hs = slice(h * HEAD_DIM, (h + 1) * HEAD_DIM)
            kh = (k0[:, hs], k1[:, hs], k2[:, hs])
            vh = (v0[:, hs], v1[:, hs], v2[:, hs])
            for r in range(ATTN_TILE // ATTN_UNIT):
                start = r * ATTN_UNIT
                units.append((h, hs, slice(r * ATTN_UNIT, (r + 1) * ATTN_UNIT), start,
                              span(kh, start), span(vh, start)))
        s = [jnp.concatenate([_dot_nt(q_ref[rows, hs], kp) for kp in ks], axis=1)
             for (_, hs, rows, _, ks, _) in units]
        s = [jnp.where(col + start < lim, NEG_INF,
                       s_u + bias_ref[h, rows, start:start + ATTN_SPAN])
             for s_u, (h, _, rows, start, _, _) in zip(s, units)]
        m = [jnp.max(s_u, axis=-1, keepdims=True) for s_u in s]
        p = [jnp.exp(s_u - m_u) for s_u, m_u in zip(s, m)]
        l = [jnp.sum(p_u, axis=-1, keepdims=True) for p_u in p]
        outs = []
        for p_u, (_, _, _, _, _, vs) in zip(p, units):
            p16 = p_u.astype(BF16)
            o, at = 0.0, 0
            for vp in vs:
                o = o + _dot(p16[:, at:at + vp.shape[0]], vp)
                at += vp.shape[0]
            outs.append(o)
        for o_u, l_u, (_, hs, rows, _, _, _) in zip(outs, l, units):
            o_ref[rows, hs] = (o_u / l_u).astype(BF16)

    for h0 in range(0, N_HEADS, ATTN_HEAD_GROUP):
        head_group(h0)


def _attn_prompt(q16, k16, v16, bias_full, batch, seq):
    tiles = seq // ATTN_TILE
    qspec = pl.BlockSpec((ATTN_TILE, D_HALF), lambda b, i: (b * tiles + i, 0))

    def kv(back):
        return pl.BlockSpec((ATTN_TILE, D_HALF),
                            lambda b, i: (b * tiles + jnp.maximum(i - back, 0), 0))

    return pl.pallas_call(
        _attn_prompt_body,
        out_shape=jax.ShapeDtypeStruct((batch * seq, D_HALF), BF16),
        grid=(batch, tiles),
        in_specs=[qspec, kv(2), kv(1), kv(0), kv(2), kv(1), kv(0),
                  _const_spec(bias_full.shape)],
        out_specs=qspec,
        compiler_params=_cparams("parallel", "parallel"),
        name="attn_prompt",
    )(q16, k16, k16, k16, v16, v16, v16, bias_full)


def _attn_sample_body(q_ref, kc_ref, vc_ref, kn_ref, vn_ref, bc_ref, bn_ref, o_ref):
    la = kc_ref.shape[1] // N_HEADS
    hr = range(N_HEADS)
    heads = [slice(h * HEAD_DIM, (h + 1) * HEAD_DIM) for h in hr]
    kc = [kc_ref[0, pl.ds(h, la, stride=N_HEADS), :].astype(BF16) for h in hr]
    vc = [vc_ref[0, pl.ds(h, la, stride=N_HEADS), :].astype(BF16) for h in hr]
    sc = [_dot_nt(q_ref[:, heads[h]], kc[h]) + bc_ref[h] for h in hr]
    sn = [_dot_nt(q_ref[:, heads[h]], kn_ref[:, heads[h]]) + bn_ref[h] for h in hr]
    m = [jnp.maximum(jnp.max(sc[h], axis=-1, keepdims=True), jnp.max(sn[h], axis=-1, keepdims=True))
         for h in hr]
    pc = [jnp.exp(sc[h] - m[h]) for h in hr]
    pn = [jnp.exp(sn[h] - m[h]) for h in hr]
    l = [jnp.sum(pc[h], axis=-1, keepdims=True) + jnp.sum(pn[h], axis=-1, keepdims=True) for h in hr]
    o = [_dot(pc[h].astype(BF16), vc[h]) + _dot(pn[h].astype(BF16), vn_ref[:, heads[h]]) for h in hr]
    o_ref[...] = jnp.concatenate([(o[h] / l[h]).astype(BF16) for h in hr], axis=1)


def _attn_sample(q16, k16, v16, cache_k, cache_v, bias_c, bias_n, row0, nseq, tn):
    blk0 = row0 // tn
    la8 = cache_k.shape[1]
    new = pl.BlockSpec((tn, D_HALF), lambda b: (blk0 + b, 0))
    cache = pl.BlockSpec((1, la8, HEAD_DIM), lambda b: (b, 0, 0))
    return pl.pallas_call(
        _attn_sample_body,
        out_shape=jax.ShapeDtypeStruct((nseq * tn, D_HALF), BF16),
        grid=(nseq,),
        in_specs=[new, cache, cache, new, new, _const_spec(bias_c.shape), _const_spec(bias_n.shape)],
        out_specs=pl.BlockSpec((tn, D_HALF), lambda b: (b, 0)),
        compiler_params=_cparams("parallel"),
        name="attn_sample",
    )(q16, cache_k, cache_v, k16, v16, bias_c, bias_n)


def _hgrn_body(q_ref, k_ref, v_ref, lf_ref, s0_ref, tcat_ref, o_ref, sout_ref, st_scr,
               *, chunk, nchunks):
    nblk = chunk // HGRN_BLOCK

    @pl.when(pl.program_id(1) == 0)
    def _():
        st_scr[...] = s0_ref[0]

    row = lax.broadcasted_iota(I32, (chunk, D_HALF), 0)
    r2 = lax.broadcasted_iota(I32, (chunk, chunk), 0)
    c2 = lax.broadcasted_iota(I32, (chunk, chunk), 1)
    same_blk_causal = jnp.logical_and(r2 // HGRN_BLOCK == c2 // HGRN_BLOCK, c2 <= r2)

    def one_chunk(c, carry):
        sl = pl.ds(pl.multiple_of(c * chunk, chunk), chunk)
        q = q_ref[sl, :]
        k = k_ref[sl, :]
        v16 = v_ref[sl, :].astype(BF16)
        l0, l1, l2 = _split3(lf_ref[sl, :])
        tcat = tcat_ref[...]
        both = _dot(tcat, l0) + _dot(tcat, l1) + _dot(tcat, l2)
        b_in = both[:chunk]
        b_ch = both[chunk:]
        ld = (q * jnp.exp(b_in)).astype(BF16)
        rd = (k * jnp.exp(-b_in)).astype(BF16)
        lj, rj = [], []
        for j in range(nblk - 1):
            e_j = b_ch[(j + 1) * HGRN_BLOCK - 1:(j + 1) * HGRN_BLOCK, :]
            later = row >= (j + 1) * HGRN_BLOCK
            inside = jnp.logical_and(row >= j * HGRN_BLOCK, row < (j + 1) * HGRN_BLOCK)
            lj.append(jnp.where(later, q * jnp.exp(jnp.minimum(b_ch - e_j, 0.0)), 0.0).astype(BF16))
            rj.append(jnp.where(inside, k * jnp.exp(jnp.minimum(e_j - b_ch, 0.0)), 0.0).astype(BF16))
        e_end = b_ch[chunk - 1:chunk, :]
        qc = (q * jnp.exp(b_ch)).astype(BF16)
        kc = (k * jnp.exp(e_end - b_ch)).astype(BF16)
        dec = jnp.exp(e_end)
        heads = [slice(h * HEAD_DIM, (h + 1) * HEAD_DIM) for h in range(N_HEADS)]
        sts = [st_scr[h] for h in range(N_HEADS)]
        diag = [_dot_nt(ld[:, hs], rd[:, hs]) for hs in heads]
        off = []
        for hs in heads:
            acc = None
            for j in range(nblk - 1):
                term = _dot_nt(lj[j][:, hs], rj[j][:, hs])
                acc = term if acc is None else acc + term
            off.append(acc)
        inter = [_dot_nt(qc[:, hs], st.astype(BF16)) for hs, st in zip(heads, sts)]
        upd = [_dot_tn(v16[:, hs], kc[:, hs]) for hs in heads]
        sc16 = [(jnp.where(same_blk_causal, d, 0.0) + o_).astype(BF16) for d, o_ in zip(diag, off)]
        intra = [_dot(s_, v16[:, hs]) for s_, hs in zip(sc16, heads)]
        o_ref[sl, :] = jnp.concatenate([a + b for a, b in zip(intra, inter)], axis=1)
        for h, hs in enumerate(heads):
            st_scr[h] = sts[h] * dec[:, hs] + upd[h]
        return carry

    lax.fori_loop(0, nchunks, one_chunk, 0, unroll=min(nchunks, HGRN_UNROLL))
    sout_ref[0] = st_scr[...]


def _hgrn(qb, kb, vb, lf, s0t, row0, nseq, seq, chunk, nchunks):
    tt = chunk * nchunks
    steps = seq // tt
    blk0 = row0 // tt
    t = jnp.arange(chunk)
    lower = t[None, :] <= t[:, None]
    same_blk = (t[None, :] // HGRN_BLOCK) == (t[:, None] // HGRN_BLOCK)
    tcat = jnp.concatenate([jnp.logical_and(lower, same_blk), lower], axis=0).astype(BF16)
    tok = pl.BlockSpec((tt, D_HALF), lambda b, j: (blk0 + b * steps + j, 0))
    state = pl.BlockSpec((1, N_HEADS, HEAD_DIM, HEAD_DIM), lambda b, j: (b, 0, 0, 0))
    return pl.pallas_call(
        functools.partial(_hgrn_body, chunk=chunk, nchunks=nchunks),
        out_shape=[jax.ShapeDtypeStruct((nseq * seq, D_HALF), F32),
                   jax.ShapeDtypeStruct(s0t.shape, F32)],
        grid=(nseq, steps),
        in_specs=[tok, tok, tok, tok, state, _const_spec((2 * chunk, chunk))],
        out_specs=[pl.BlockSpec((tt, D_HALF), lambda b, j: (b * steps + j, 0)), state],
        scratch_shapes=[pltpu.VMEM((N_HEADS, HEAD_DIM, HEAD_DIM), F32)],
        compiler_params=_cparams("parallel", "arbitrary"),
        name="hgrn",
    )(qb, kb, vb, lf, s0t, tcat)


def _route(xn, wr_hi, wr_lo, br, tstrict, cnt_scr, route_o, cnt_o, code_o):
    rows = xn.shape[0]
    x_hi, x_lo = _split2(xn)
    logits = _dot(x_hi, wr_hi) + _dot(x_lo, wr_hi) + _dot(x_hi, wr_lo) + br
    lane = lax.broadcasted_iota(I32, (rows, LANES), 1)
    lane_f = lane.astype(F32)

    def first_argmax(vals):
        m = jnp.max(vals, axis=-1, keepdims=True)
        idx = jnp.min(jnp.where(vals == m, lane_f, float(LANES)), axis=-1, keepdims=True)
        return m, idx.astype(I32)

    is_grp = jnp.logical_and(lane >= N_EXPERTS, lane < N_EXPERTS + N_GROUPS)
    gl = jnp.where(is_grp, logits, -jnp.inf)
    gmax, gidx = first_argmax(gl)
    p_grp = 1.0 / jnp.sum(jnp.exp(gl - gmax), axis=-1, keepdims=True)
    grp = gidx - N_EXPERTS
    in_grp = jnp.logical_and(lane < N_EXPERTS, lane // EXPERTS_PER_GROUP == grp)
    el = jnp.where(in_grp, logits, -jnp.inf)
    v1, e1 = first_argmax(el)
    v2, e2 = first_argmax(jnp.where(lane == e1, -jnp.inf, el))
    t2 = jnp.exp(v2 - v1)
    den = 1.0 + t2
    g1 = (1.0 / den) * p_grp
    g2 = (t2 / den) * p_grp

    oh1 = lane == e1
    oh2 = lane == e2
    oh = jnp.where(jnp.logical_or(oh1, oh2), 1.0, 0.0)
    before = _dot(tstrict, oh.astype(BF16)) + cnt_scr[...]
    rank1 = jnp.sum(jnp.where(oh1, before, 0.0), axis=-1, keepdims=True)
    rank2 = jnp.sum(jnp.where(oh2, before, 0.0), axis=-1, keepdims=True)
    cnt_scr[...] = cnt_scr[...] + jnp.sum(oh, axis=0, keepdims=True)
    cnt_o[...] = cnt_scr[...]

    e1f, e2f = e1.astype(F32), e2.astype(F32)
    fields = (e1f, e2f, g1, g2, rank1, rank2, e1f * RANK_SPAN + rank1, e2f * RANK_SPAN + rank2)
    slab = jnp.zeros((rows, LANES), F32)
    for idx, val in enumerate(fields):
        slab = jnp.where(lane == idx, val, slab)
    route_o[...] = slab
    code_o[...] = slab.T[6:8, :]


def _out_ab_body(hp_ref, hs_ref, oap_ref, oas_ref, obp_ref, obs_ref, gb_ref, og_ref, bd_ref, w_ref,
                 gf_ref, wrh_ref, wrl_ref, br_ref, ts_ref, h_o, xn_o, route_o, cnt_o, code_o, cnt_scr,
                 *, tiles_p):
    @pl.when(pl.program_id(0) == 0)
    def _():
        cnt_scr[...] = jnp.zeros_like(cnt_scr)

    ob = _group_pick(obp_ref, obs_ref, tiles_p)
    obn = ob * lax.rsqrt(_head_mean_sq(ob, bd_ref[...]) + RMS_EPS) * og_ref[...]
    obg = (obn * jax.nn.silu(gb_ref[...])).astype(BF16)
    oa = _group_pick(oap_ref, oas_ref, tiles_p)
    mix = _dot(oa, w_ref[0:D_HALF, :]) + _dot(obg, w_ref[D_HALF:, :])
    h1 = _group_pick(hp_ref, hs_ref, tiles_p) + mix
    h_o[...] = h1
    xn = _rms(h1, gf_ref[...])
    _store_row_tiles(xn_o, xn)
    _route(xn, wrh_ref[...], wrl_ref[...], br_ref[...], ts_ref[...], cnt_scr, route_o, cnt_o, code_o)


def _out_ab(h_p, h_s, oa_p, oa_s, ob_p, ob_s, gb, og, bd, w16, gf, wrh, wrl, br, tstrict):
    tm = ROUTE_TILE
    n = h_p.shape[0] + h_s.shape[0]
    tiles_p = h_p.shape[0] // tm
    row = lambda w: pl.BlockSpec((tm, w), lambda i: (i, 0))
    return pl.pallas_call(
        functools.partial(_out_ab_body, tiles_p=tiles_p),
        out_shape=[jax.ShapeDtypeStruct((n, D_MODEL), F32), jax.ShapeDtypeStruct((n * ROW_TILE, LANES), F32),
                   jax.ShapeDtypeStruct((n, LANES), F32), jax.ShapeDtypeStruct((1, LANES), F32),
                   jax.ShapeDtypeStruct((2, n), F32)],
        grid=(n // tm,),
        in_specs=_group_specs(D_MODEL, tiles_p, tm) + _group_specs(D_HALF, tiles_p, tm)
        + _group_specs(D_HALF, tiles_p, tm) + [
            row(D_HALF), _const_spec((1, D_HALF)),
            _const_spec((D_HALF, D_HALF)), _const_spec((D_MODEL, D_MODEL)),
            _const_spec((1, D_MODEL)), _const_spec((D_MODEL, LANES)),
            _const_spec((D_MODEL, LANES)), _const_spec((1, LANES)), _const_spec((tm, tm))],
        out_specs=[row(D_MODEL), pl.BlockSpec((tm * ROW_TILE, LANES), lambda i: (i, 0)), row(LANES),
                   _const_spec((1, LANES)), pl.BlockSpec((2, tm), lambda i: (0, i))],
        scratch_shapes=[pltpu.VMEM((1, LANES), F32)],
        compiler_params=_cparams("arbitrary"),
        name="out_ab",
    )(h_p, h_s, oa_p, oa_s, ob_p, ob_s, gb, og, bd, w16, gf, wrh, wrl, br, tstrict)


def _row_copy(src, src_row, dst, dst_row, sem):
    s0 = pl.multiple_of(src_row * ROW_TILE, ROW_TILE)
    d0 = pl.multiple_of(dst_row * ROW_TILE, ROW_TILE)
    return pltpu.make_async_copy(src.at[pl.ds(s0, ROW_TILE)], dst.at[pl.ds(d0, ROW_TILE)], sem)


def _collect_body(pos_ref, ys_hbm, h_ref, r_ref, *rest, tile, n_tok, tiles_p, proj):
    if proj:
        g_ref, w_ref = rest[:2]
        rest = rest[2:]
    outs, (buf, sems) = rest[:-2], rest[-2:]
    i = pl.program_id(0)
    nt = pl.num_programs(0)

    def issue(t):
        slot = t % 2
        base = t * tile

        def one(r, carry):
            tok = base + r
            for k in range(2):
                row = pos_ref[k * n_tok + tok]
                _row_copy(ys_hbm, row, buf.at[slot].at[k], r, sems.at[slot]).start(priority=k)
            return carry

        lax.fori_loop(0, tile, one, 0, unroll=GATHER_UNROLL)

    @pl.when(i == 0)
    def _():
        issue(0)

    @pl.when(i + 1 < nt)
    def _():
        issue(i + 1)

    slot = i % 2
    for _ in range(2 * tile):
        _row_copy(ys_hbm, 0, buf.at[slot].at[0], 0, sems.at[slot]).wait()
    route = r_ref[...]
    ya = _load_row_tiles(buf.at[slot].at[0], tile)
    yb = _load_row_tiles(buf.at[slot].at[1], tile)
    out = h_ref[...] + (route[:, 2:3] * ya + route[:, 3:4] * yb)
    if proj:
        h_o, u_o, bg_o = outs
        h_o[...] = out
        xb = _rms(out, g_ref[...]).astype(BF16)
        bg_o[...] = _dot(xb, w_ref[:, 0:D_MODEL])
        u_o[...] = _dot(xb, w_ref[:, D_MODEL:2 * D_MODEL]) * _dot(xb, w_ref[:, 2 * D_MODEL:])
    else:
        @pl.when(i < tiles_p)
        def _():
            outs[0][...] = out

        @pl.when(i >= tiles_p)
        def _():
            outs[1][...] = out


def _collect(pos, ys, h, route, *, proj=None, n_p=None):
    tm = COLLECT_TILE
    n = h.shape[0]
    row = lambda w: pl.BlockSpec((tm, w), lambda i, pos: (i, 0))
    const = lambda shape: pl.BlockSpec(shape, lambda i, pos: (0,) * len(shape))
    if proj is not None:
        tiles_p = None
        extra_in, extra_specs = list(proj), [const(proj[0].shape), const(proj[1].shape)]
        out_shape = [jax.ShapeDtypeStruct((n, D_MODEL), F32)] * 3
        out_specs = [row(D_MODEL)] * 3
    else:
        tiles_p = n_p // tm
        extra_in, extra_specs = [], []
        out_shape = [jax.ShapeDtypeStruct((n_p, D_MODEL), F32),
                     jax.ShapeDtypeStruct((n - n_p, D_MODEL), F32)]
        out_specs = [pl.BlockSpec((tm, D_MODEL), lambda i, pos: (jnp.minimum(i, tiles_p - 1), 0)),
                     pl.BlockSpec((tm, D_MODEL), lambda i, pos: (jnp.maximum(i - tiles_p, 0), 0))]
    return pl.pallas_call(
        functools.partial(_collect_body, tile=tm, n_tok=n, tiles_p=tiles_p, proj=proj is not None),
        out_shape=out_shape,
        grid_spec=pltpu.PrefetchScalarGridSpec(
            num_scalar_prefetch=1,
            grid=(n // tm,),
            in_specs=[pl.BlockSpec(memory_space=pl.ANY), row(D_MODEL), row(LANES)] + extra_specs,
            out_specs=out_specs,
            scratch_shapes=[pltpu.VMEM((2, 2, tm * ROW_TILE, LANES), F32),
                            pltpu.SemaphoreType.DMA((2,))],
        ),
        compiler_params=pltpu.CompilerParams(dimension_semantics=("arbitrary",),
                                             vmem_limit_bytes=VMEM_LIMIT),
        name="moe_collect",
    )(pos, ys, h, route, *extra_in)


def _scatter_rows_body(pos_ref, pad0_ref, padn_ref, nt_ref, x_ref, xs_hbm,
                       stage, zero_scr, sems, pad_sem, *, tile, n_tok, out_tiles):
    i = pl.program_id(0)
    last = pl.num_programs(0) - 1
    base = i * tile
    slot = i % 2

    def drain(s):
        for _ in range(2 * tile):
            _row_copy(stage.at[s], 0, xs_hbm, 0, sems.at[s]).wait()

    @pl.when(i >= 2)
    def _():
        drain(slot)

    stage[slot] = x_ref[...]

    def issue(r, carry):
        tok = base + r
        src = stage.at[slot]
        for k in range(2):
            _row_copy(src, r, xs_hbm, pos_ref[k * n_tok + tok], sems.at[slot]).start(priority=k)
        return carry

    lax.fori_loop(0, tile, issue, 0, unroll=GATHER_UNROLL)

    @pl.when(i == last)
    def _():
        zero_scr[...] = jnp.zeros_like(zero_scr)

        def fill(lo, count):
            def one(r, carry):
                _row_copy(zero_scr, 0, xs_hbm, lo + r, pad_sem).start()
                return carry

            def done(r, carry):
                _row_copy(zero_scr, 0, xs_hbm, 0, pad_sem).wait()
                return carry

            lax.fori_loop(0, count, one, 0)
            lax.fori_loop(0, count, done, 0)

        for e in range(N_EXPERTS):
            fill(pad0_ref[e], padn_ref[e])

        def tail_copy(t):
            rows = MOE_TILE * ROW_TILE
            return pltpu.make_async_copy(
                zero_scr, xs_hbm.at[pl.ds(pl.multiple_of(t * rows, rows), rows)], pad_sem)

        def tail_start(t, carry):
            tail_copy(t).start()
            return carry

        def tail_done(t, carry):
            tail_copy(t).wait()
            return carry

        lax.fori_loop(nt_ref[0], out_tiles, tail_start, 0)
        lax.fori_loop(nt_ref[0], out_tiles, tail_done, 0)
        drain(slot)

    @pl.when(jnp.logical_and(i == last, i >= 1))
    def _():
        drain(1 - slot)


def _scatter_rows(pos, pad0, padn, n_tiles, xn, n_rows):
    tm = TOKEN_TILE
    n = xn.shape[0] // ROW_TILE
    return pl.pallas_call(
        functools.partial(_scatter_rows_body, tile=tm, n_tok=n, out_tiles=n_rows // MOE_TILE),
        out_shape=jax.ShapeDtypeStruct((n_rows * ROW_TILE, LANES), F32),
        grid_spec=pltpu.PrefetchScalarGridSpec(
            num_scalar_prefetch=4,
            grid=(n // tm,),
            in_specs=[pl.BlockSpec((tm * ROW_TILE, LANES), lambda i, *_: (i, 0))],
            out_specs=pl.BlockSpec(memory_space=pl.ANY),
            scratch_shapes=[pltpu.VMEM((2, tm * ROW_TILE, LANES), F32),
                            pltpu.VMEM((MOE_TILE * ROW_TILE, LANES), F32),
                            pltpu.SemaphoreType.DMA((2,)), pltpu.SemaphoreType.DMA(())],
        ),
        compiler_params=pltpu.CompilerParams(dimension_semantics=("arbitrary",)),
        name="moe_scatter",
    )(pos, pad0, padn, n_tiles, xn)


def _experts_body(te_ref, nt_ref, x_ref, w1_ref, w3_ref, w2_ref, y_ref, w1_s, w3_s, w2_s):
    i = pl.program_id(0)
    prev = te_ref[jnp.maximum(i - 1, 0)]
    fresh = jnp.logical_or(i == 0, te_ref[i] != prev)

    @pl.when(fresh)
    def _():
        w1_s[...] = w1_ref[0].astype(BF16)
        w3_s[...] = w3_ref[0].astype(BF16)
        w2_s[...] = w2_ref[0].astype(BF16)

    @pl.when(i < nt_ref[0])
    def _():
        x = _load_row_tiles(x_ref, MOE_TILE).astype(BF16)
        a = _dot(x, w1_s[...])
        b = _dot(x, w3_s[...])
        _store_row_tiles(y_ref, _dot((jax.nn.silu(a) * b).astype(BF16), w2_s[...]))


def _experts(tile_expert, n_tiles, xs, w1, w3, w2):
    rows = xs.shape[0] // ROW_TILE
    tm = MOE_TILE
    tile = pl.BlockSpec((tm * ROW_TILE, LANES), lambda i, te, nt: (jnp.minimum(i, nt[0] - 1), 0))
    return pl.pallas_call(
        _experts_body,
        out_shape=jax.ShapeDtypeStruct((rows * ROW_TILE, LANES), F32),
        grid_spec=pltpu.PrefetchScalarGridSpec(
            num_scalar_prefetch=2,
            grid=(rows // tm,),
            in_specs=[tile,
                      pl.BlockSpec((1, D_MODEL, D_EXPERT), lambda i, te, nt: (te[i], 0, 0)),
                      pl.BlockSpec((1, D_MODEL, D_EXPERT), lambda i, te, nt: (te[i], 0, 0)),
                      pl.BlockSpec((1, D_EXPERT, D_MODEL), lambda i, te, nt: (te[i], 0, 0))],
            out_specs=tile,
            scratch_shapes=[pltpu.VMEM((D_MODEL, D_EXPERT), BF16), pltpu.VMEM((D_MODEL, D_EXPERT), BF16),
                            pltpu.VMEM((D_EXPERT, D_MODEL), BF16)],
        ),
        input_output_aliases={2: 0},
        compiler_params=_cparams("arbitrary"),
        name="moe_experts",
    )(tile_expert, n_tiles, xs, w1, w3, w2)


def _moe_plan(counts, n):
    counts = counts[0, :N_EXPERTS].astype(I32)
    padded = ((counts + MOE_TILE - 1) // MOE_TILE) * MOE_TILE
    ends = jnp.cumsum(padded)
    offs = ends - padded
    max_tiles = (2 * n + N_EXPERTS * (MOE_TILE - 1) + MOE_TILE - 1) // MOE_TILE
    tile_start = jnp.arange(max_tiles, dtype=I32) * MOE_TILE
    tile_expert = jnp.minimum(jnp.sum((tile_start[:, None] >= ends[None, :]).astype(I32), axis=1),
                              N_EXPERTS - 1)
    n_tiles = (ends[-1] // MOE_TILE).reshape(1)
    return offs, offs + counts, padded - counts, tile_expert, n_tiles, max_tiles * MOE_TILE


def _moe(xn, h, route, code, counts, layer, w1, w3, w2, **collect_mode):
    n = h.shape[0]
    code = code.astype(I32).reshape(2 * n)
    offs, pad0, padn, tile_expert, n_tiles, n_rows = _moe_plan(counts, n)
    hit = (code >> RANK_BITS)[None, :] == jnp.arange(N_EXPERTS, dtype=I32)[:, None]
    pos = jnp.sum(jnp.where(hit, offs[:, None], 0), axis=0) + (code & (RANK_SPAN - 1))
    xs = _scatter_rows(pos, pad0, padn, n_tiles, xn, n_rows)
    flat = lambda w: w.reshape((-1,) + w.shape[2:])
    ys = _experts(tile_expert + layer * N_EXPERTS, n_tiles, xs, flat(w1), flat(w3), flat(w2))
    return _collect(pos, ys, h, route, **collect_mode)


def _conv_out_body(u_ref, up_ref, stp_ref, sts_ref, bg_ref, h_ref, cw_ref, w_ref, gf_ref,
                   wrh_ref, wrl_ref, br_ref, ts_ref, h_o, xn_o, route_o, cnt_o, code_o, cnt_scr,
                   *, tile, tiles_p, seq_p, seq_s):
    i = pl.program_id(0)

    @pl.when(i == 0)
    def _():
        cnt_scr[...] = jnp.zeros_like(cnt_scr)

    u = u_ref[...]
    rowi = lax.broadcasted_iota(I32, (tile, D_MODEL), 0)
    is_p = i < tiles_p
    at_start = (i * tile) % seq_p == 0
    stp = stp_ref[0]
    prev = up_ref[...]
    m2_p = jnp.where(at_start, stp[0:1, :], prev[6:7, :])
    m1_p = jnp.where(at_start, stp[1:2, :], prev[7:8, :])
    per = tile // seq_s
    sts = sts_ref[...]
    m2_s = jnp.broadcast_to(sts[:, 0:1, :], (per, seq_s, D_MODEL)).reshape(tile, D_MODEL)
    m1_s = jnp.broadcast_to(sts[:, 1:2, :], (per, seq_s, D_MODEL)).reshape(tile, D_MODEL)
    m2 = jnp.where(is_p, m2_p, m2_s)
    m1 = jnp.where(is_p, m1_p, m1_s)
    pos = jnp.where(is_p, rowi, rowi % seq_s)
    u1 = jnp.where(pos == 0, m1, pltpu.roll(u, 1, axis=0))
    u2 = jnp.where(pos == 0, m2, jnp.where(pos == 1, m1, pltpu.roll(u, 2, axis=0)))
    cw = cw_ref[...]
    conv = u2 * cw[0:1, :] + u1 * cw[1:2, :] + u * cw[2:3, :]
    mix = _dot((bg_ref[...] * conv).astype(BF16), w_ref[...])
    h3 = h_ref[...] + mix
    h_o[...] = h3
    xn = _rms(h3, gf_ref[...])
    _store_row_tiles(xn_o, xn)
    _route(xn, wrh_ref[...], wrl_ref[...], br_ref[...], ts_ref[...], cnt_scr, route_o, cnt_o, code_o)


def _conv_out(u, bg, h, state_p, state_s, cw, w16, gf, wrh, wrl, br, tstrict, seq_p, seq_s):
    tm = ROUTE_TILE
    n = u.shape[0]
    tiles_p = state_p.shape[0] * seq_p // tm
    per = tm // seq_s
    row = lambda w: pl.BlockSpec((tm, w), lambda i: (i, 0))
    prev = pl.BlockSpec((8, D_MODEL), lambda i: (jnp.maximum(i * (tm // 8) - 1, 0), 0))
    stp_spec = pl.BlockSpec((1, 2, D_MODEL),
                            lambda i: (jnp.minimum(i, tiles_p - 1) * tm // seq_p, 0, 0))
    sts_spec = pl.BlockSpec((per, 2, D_MODEL), lambda i: (jnp.maximum(i - tiles_p, 0), 0, 0))
    return pl.pallas_call(
        functools.partial(_conv_out_body, tile=tm, tiles_p=tiles_p, seq_p=seq_p, seq_s=seq_s),
        out_shape=[jax.ShapeDtypeStruct((n, D_MODEL), F32),
                   jax.ShapeDtypeStruct((n * ROW_TILE, LANES), F32),
                   jax.ShapeDtypeStruct((n, LANES), F32), jax.ShapeDtypeStruct((1, LANES), F32),
                   jax.ShapeDtypeStruct((2, n), F32)],
        grid=(n // tm,),
        in_specs=[row(D_MODEL), prev, stp_spec, sts_spec, row(D_MODEL), row(D_MODEL),
                  _const_spec((3, D_MODEL)),
                  _const_spec((D_MODEL, D_MODEL)), _const_spec((1, D_MODEL)),
                  _const_spec((D_MODEL, LANES)), _const_spec((D_MODEL, LANES)),
                  _const_spec((1, LANES)), _const_spec((tm, tm))],
        out_specs=[row(D_MODEL), pl.BlockSpec((tm * ROW_TILE, LANES), lambda i: (i, 0)), row(LANES),
                   _const_spec((1, LANES)), pl.BlockSpec((2, tm), lambda i: (0, i))],
        scratch_shapes=[pltpu.VMEM((1, LANES), F32)],
        compiler_params=_cparams("arbitrary"),
        name="conv_out",
    )(u, u, state_p, state_s, bg, h, cw, w16, gf, wrh, wrl, br, tstrict)


def _rel_bias_toeplitz(rel_bias, rows, cols, lead):
    period = rows + cols
    k = jnp.arange(period)
    d = jnp.where(k < cols, k, k - period)
    idx = jnp.clip(lead - d, -REL_CLIP, REL_CLIP) + REL_CLIP
    v = jnp.take(rel_bias.astype(F32), idx, axis=1)
    heads = v.shape[0]
    skew = jnp.tile(v, (1, rows))[:, :rows * (period - 1)].reshape(heads, rows, period - 1)
    return skew[:, :, :cols]


def _bias_prompt(rel_bias):
    r = jnp.arange(ATTN_TILE)[:, None]
    c = jnp.arange(ATTN_WINDOW)[None, :]
    j = c // CHUNK - r // CHUNK
    band = jnp.logical_and(j >= 0, j <= BAND_CHUNKS)
    bias = _rel_bias_toeplitz(rel_bias, ATTN_TILE, ATTN_WINDOW, BAND_CHUNKS * CHUNK)
    return jnp.where(band[None], bias, NEG_INF)


def _bias_sample(rel_bias, la, tn):
    bias = _rel_bias_toeplitz(rel_bias, tn, la + tn, la)
    return bias[:, :, :la], bias[:, :, la:]


def _router_weights(w_group, b_group, w_expert, b_expert):
    pad = LANES - N_EXPERTS - N_GROUPS
    w = jnp.concatenate([w_expert, w_group, jnp.zeros((D_MODEL, pad), F32)], axis=1)
    b = jnp.concatenate([b_expert, b_group, jnp.zeros((pad,), F32)])[None, :].astype(F32)
    hi, lo = _split2(w.astype(F32))
    return hi, lo, b


def kernel(x_prompt, x_sample, cache_a_k, cache_a_v, state_hgrn, state_conv, norm_mix, norm_ffn,
           w_in_ab, w_out_ab, q_norm, k_norm, rel_bias, hgrn_lb_logits, hgrn_out_norm, w_in_c,
           conv_w, w_out_c, w_group, b_group, w_expert, b_expert, w1, w3, w2):
    batch, seq, d = x_prompt.shape
    nseq_s, tn, _ = x_sample.shape
    la = cache_a_k.shape[2]
    n_p = batch * seq
    n_s = nseq_s * tn
    n = n_p + n_s
    keep = min(BAND_CHUNKS * CHUNK, seq)

    x_p = x_prompt.reshape(n_p, d)
    x_s = x_sample.reshape(n_s, d)
    lb_all = jnp.cumsum(jax.nn.softmax(hgrn_lb_logits.astype(F32), axis=0), axis=0)
    head_avg = jnp.kron(jnp.eye(N_HEADS, dtype=F32),
                        jnp.full((HEAD_DIM, HEAD_DIM), 1.0 / HEAD_DIM, F32)).astype(BF16)
    t = jnp.arange(ROUTE_TILE)
    tstrict = (t[None, :] < t[:, None]).astype(BF16)
    row1 = lambda v: v.astype(F32).reshape(1, -1)
    tile8 = lambda v: jnp.tile(v.astype(F32), N_HEADS).reshape(1, -1)

    l = 0
    (q16, kf, vf, k16, v16, lf, kb, vb, qb, gb) = _proj_ab(
        x_p, x_s, row1(norm_mix[0]), w_in_ab[l].astype(BF16), tile8(q_norm[l]), tile8(k_norm[l]),
        row1(lb_all[l]), head_avg, seq, keep)

    oa_p = _attn_prompt(q16, k16, v16, _bias_prompt(rel_bias[l]), batch, seq)
    bias_c, bias_n = _bias_sample(rel_bias[l], la, tn)
    oa_s = _attn_sample(q16, k16, v16, cache_a_k[l].reshape(nseq_s, la * N_HEADS, HEAD_DIM),
                        cache_a_v[l].reshape(nseq_s, la * N_HEADS, HEAD_DIM), bias_c, bias_n,
                        n_p, nseq_s, tn)

    zeros_state = jnp.zeros((batch, N_HEADS, HEAD_DIM, HEAD_DIM), F32)
    ob_p, st_p = _hgrn(qb, kb, vb, lf, zeros_state, 0, batch, seq, 64, 4)
    ob_s, st_s = _hgrn(qb, kb, vb, lf, jnp.swapaxes(state_hgrn[l].astype(F32), -1, -2),
                       n_p, nseq_s, tn, tn, 1)

    wrh, wrl, br = _router_weights(w_group[0], b_group[0], w_expert[0], b_expert[0])
    h1, xn1, route1, cnt1, code1 = _out_ab(x_p, x_s, oa_p, oa_s, ob_p, ob_s, gb, row1(hgrn_out_norm[l]),
                                    head_avg, w_out_ab[l].astype(BF16), row1(norm_ffn[0]),
                                    wrh, wrl, br, tstrict)
    h2, u, bg = _moe(xn1, h1, route1, code1, cnt1, 0, w1, w3, w2,
                     proj=(row1(norm_mix[1]), w_in_c[0].astype(BF16)))

    wrh, wrl, br = _router_weights(w_group[1], b_group[1], w_expert[1], b_expert[1])
    h3, xn2, route2, cnt2, code2 = _conv_out(u, bg, h2, jnp.zeros((batch, 2, d), F32),
                                      state_conv[0].astype(F32), conv_w[0].astype(F32),
                                      w_out_c[0].astype(BF16), row1(norm_ffn[1]), wrh, wrl, br,
                                      tstrict, seq, tn)
    out_p, out_s = _moe(xn2, h3, route2, code2, cnt2, 1, w1, w3, w2, n_p=n_p)

    y_prompt = out_p.reshape(batch, seq, d)
    y_sample = out_s.reshape(nseq_s, tn, d)
    n_kp = batch * keep
    heads = lambda a, b_, t_: a.reshape(b_, t_, N_HEADS, HEAD_DIM)
    nk_p = heads(kf[:n_kp], batch, keep)[None]
    nv_p = heads(vf[:n_kp], batch, keep)[None]
    kf_s = heads(kf[n_kp:n_kp + n_s], nseq_s, tn)
    vf_s = heads(vf[n_kp:n_kp + n_s], nseq_s, tn)
    nk_s = jnp.concatenate([cache_a_k[l].astype(F32), kf_s], axis=1)[:, -la:][None]
    nv_s = jnp.concatenate([cache_a_v[l].astype(F32), vf_s], axis=1)[:, -la:][None]
    nh_p = jnp.swapaxes(st_p, -1, -2)[None]
    nh_s = jnp.swapaxes(st_s, -1, -2)[None]
    nc_p = jnp.stack([u[(b + 1) * seq - 2:(b + 1) * seq] for b in range(batch)])[None]
    u_s = u[n_p:].reshape(nseq_s, tn, d)
    nc_s = jnp.concatenate([state_conv[0].astype(F32), u_s], axis=1)[:, -2:][None]
    return (y_prompt, y_sample, nk_p, nv_p, nk_s, nv_s, nh_p, nh_s, nc_p, nc_s)
```
